```python
import math
import jax
import jax.numpy as jnp
from jax import lax
import numpy as np

D_MODEL = 2048
BATCH = 16
SEQ = 2048
DEPTH = 4

CTX_LEN = 256
GRID_W = 64
HEAD_DIM = 128
N_HEADS_A = 8
N_KV_A = 2
N_HEADS_B = 8
N_KV_B = 2
BLOCK_Q = 128
WINDOW = 128
ROPE_THETA = 10000.0
D_FF = 5632
N_MOD = 9
HYENA_ORDER = 2
HYENA_SHORT_K = 3
HYENA_HIDDEN = 64
HYENA_BANDS = 16
HYENA_POS_DIM = 1 + 2 * HYENA_BANDS
HYENA_TARGET = 1e-2
HYENA_FAST_PCT = 0.3
HYENA_SLOW_PCT = 1.5
EPS = 1e-6
NEG_INF = -1e30

Q_A = N_HEADS_A * HEAD_DIM
Q_B = N_HEADS_B * HEAD_DIM
KV_A = N_KV_A * HEAD_DIM
KV_B = N_KV_B * HEAD_DIM
KV_OFF = Q_A + Q_B
ATTN_IN = KV_OFF + 2 * KV_A + 2 * KV_B
ATTN_OUT = Q_A + Q_B

kernel_name = "hybrid_dit_gqa_window_hyena_macaron"


def rms_norm(x, g):
    xf = x.astype(jnp.float32)
    y = xf * lax.rsqrt(jnp.mean(xf * xf, axis=-1, keepdims=True) + EPS)
    return (y * g.astype(jnp.float32)).astype(x.dtype)


def modulate(x, g, m, k):
    shift = m[:, 3 * k][:, None, :]
    scale = m[:, 3 * k + 1][:, None, :]
    return rms_norm(x, g) * (1.0 + scale) + shift


def gate_of(m, k):
    return m[:, 3 * k + 2][:, None, :]


def swiglu(h, w_in, w_out):
    g, u = jnp.split(h @ w_in, 2, axis=-1)
    return (jax.nn.silu(g) * u) @ w_out


def axial_rope(n_tok):
    rows = n_tok // GRID_W
    r, col = jnp.meshgrid(jnp.arange(rows), jnp.arange(GRID_W), indexing="ij")
    r = r.reshape(-1).astype(jnp.float32)
    col = col.reshape(-1).astype(jnp.float32)
    half = HEAD_DIM // 2
    inv = ROPE_THETA ** (-jnp.arange(0, half, 2, dtype=jnp.float32) / half)
    ang = jnp.concatenate([r[:, None] * inv, col[:, None] * inv], axis=-1)
    return jnp.cos(ang), jnp.sin(ang)


def apply_rope(t, rope):
    cos, sin = rope
    tf = t.astype(jnp.float32).reshape(t.shape[:-1] + (HEAD_DIM // 2, 2))
    t0, t1 = tf[..., 0], tf[..., 1]
    c = cos[None, :, None, :]
    s = sin[None, :, None, :]
    out = jnp.stack([t0 * c - t1 * s, t0 * s + t1 * c], axis=-1)
    return out.reshape(t.shape).astype(t.dtype)


def heads(t, n):
    return t.reshape(t.shape[:2] + (n, HEAD_DIM))


def groups(q, n_kv):
    return q.reshape(q.shape[:2] + (n_kv, q.shape[2] // n_kv, HEAD_DIM))


def merge_heads(o):
    return o.reshape(o.shape[:2] + (-1,))


def gqa_attend(q, k, v, mask, sink):
    s = jnp.einsum("bqhgd,bkhd->bhgqk", q.astype(jnp.float32), k.astype(jnp.float32)) * (HEAD_DIM ** -0.5)
    if mask is not None:
        s = jnp.where(mask, s, NEG_INF)
    if sink is None:
        p = jax.nn.softmax(s, axis=-1)
    else:
        sk = jnp.broadcast_to(sink.astype(jnp.float32)[None, :, :, None, None], s.shape[:-1] + (1,))
        p = jax.nn.softmax(jnp.concatenate([s, sk], axis=-1), axis=-1)[..., :-1]
    o = jnp.einsum("bhgqk,bkhd->bqhgd", p, v.astype(jnp.float32))
    return o.astype(q.dtype)


def global_attention(q, k, v):
    b, n = q.shape[:2]
    nb = n // BLOCK_Q
    qb = jnp.moveaxis(q.reshape((b, nb, BLOCK_Q) + q.shape[2:]), 1, 0)
    ob = lax.map(lambda qq: gqa_attend(qq, k, v, None, None), qb)
    return jnp.moveaxis(ob, 0, 1).reshape(q.shape)


def window_attention(q, k, v, k_ctx, v_ctx, sink):
    b, n = q.shape[:2]
    nb = n // BLOCK_Q
    n_ctx = k_ctx.shape[1]
    pad = ((0, 0), (BLOCK_Q, BLOCK_Q), (0, 0), (0, 0))

    def band(t):
        tp = jnp.pad(t, pad).reshape((b, nb + 2, BLOCK_Q) + t.shape[2:])
        tw = jnp.concatenate([tp[:, :-2], tp[:, 1:-1], tp[:, 2:]], axis=2)
        return jnp.moveaxis(tw, 1, 0)

    kw, vw = band(k), band(v)
    blk = jnp.arange(nb)[:, None, None]
    q_pos = blk * BLOCK_Q + jnp.arange(BLOCK_Q)[None, :, None]
    k_pos = (blk - 1) * BLOCK_Q + jnp.arange(3 * BLOCK_Q)[None, None, :]
    lat_mask = (jnp.abs(k_pos - q_pos) <= WINDOW) & (k_pos >= 0) & (k_pos < n)
    mask = jnp.concatenate([jnp.ones((nb, BLOCK_Q, n_ctx), dtype=bool), lat_mask], axis=-1)
    qb = jnp.moveaxis(q.reshape((b, nb, BLOCK_Q) + q.shape[2:]), 1, 0)

    def one(args):
        qq, kk, vv, mm = args
        return gqa_attend(qq, jnp.concatenate([k_ctx, kk], axis=1), jnp.concatenate([v_ctx, vv], axis=1), mm, sink)

    ob = lax.map(one, (qb, kw, vw, mask))
    return jnp.moveaxis(ob, 0, 1).reshape(q.shape)


def attn_project(h, w_in, g_q, g_k, rope, with_q):
    if with_q:
        p = h @ w_in
        q_a, q_b, rest = p[..., :Q_A], p[..., Q_A:KV_OFF], p[..., KV_OFF:]
    else:
        rest = h @ w_in[:, KV_OFF:]
    k_a, v_a, k_b, v_b = jnp.split(rest, [KV_A, 2 * KV_A, 2 * KV_A + KV_B], axis=-1)
    k_a = rms_norm(heads(k_a, N_KV_A), g_k[0])
    k_b = rms_norm(heads(k_b, N_KV_B), g_k[1])
    v_a = heads(v_a, N_KV_A)
    v_b = heads(v_b, N_KV_B)
    if rope is not None:
        k_a = apply_rope(k_a, rope)
        k_b = apply_rope(k_b, rope)
    if not with_q:
        return None, k_a, v_a, None, k_b, v_b
    q_a = rms_norm(heads(q_a, N_HEADS_A), g_q[0])
    q_b = rms_norm(heads(q_b, N_HEADS_B), g_q[1])
    if rope is not None:
        q_a = apply_rope(q_a, rope)
        q_b = apply_rope(q_b, rope)
    return groups(q_a, N_KV_A), k_a, v_a, groups(q_b, N_KV_B), k_b, v_b


def attn_mixer(h, hc, w_in, w_out, g_q, g_k, sink, rope, ctx_out):
    q_a, k_a, v_a, q_b, k_b, v_b = attn_project(h, w_in, g_q, g_k, rope, True)
    cq_a, ck_a, cv_a, cq_b, ck_b, cv_b = attn_project(hc, w_in, g_q, g_k, None, ctx_out)
    sink_b = sink.reshape(N_KV_B, N_HEADS_B // N_KV_B)
    o_a = global_attention(q_a, jnp.concatenate([ck_a, k_a], axis=1), jnp.concatenate([cv_a, v_a], axis=1))
    o_b = window_attention(q_b, k_b, v_b, ck_b, cv_b, sink_b)
    y = jnp.concatenate([merge_heads(o_a), merge_heads(o_b)], axis=-1) @ w_out
    if not ctx_out:
        return y, None
    co_a = gqa_attend(cq_a, ck_a, cv_a, None, None)
    co_b = gqa_attend(cq_b, ck_b, cv_b, None, sink_b)
    yc = jnp.concatenate([merge_heads(co_a), merge_heads(co_b)], axis=-1) @ w_out
    return y, yc


def hyena_filter(n, w1, b1, fr1, w2, b2, fr2, w3):
    f32 = jnp.float32
    t = jnp.linspace(0.0, 1.0, n, dtype=f32)[:, None]
    w = (2.0 * math.pi / n) * jnp.arange(n, dtype=f32)[:, None]
    bands = jnp.linspace(1e-4, HYENA_BANDS - 1, HYENA_BANDS, dtype=f32)
    ang = w * bands[None, :]
    feats = jnp.concatenate([t, jnp.cos(ang), -jnp.sin(ang)], axis=-1)
    hdn = jnp.sin(fr1.astype(f32) * (feats @ w1.astype(f32) + b1.astype(f32)))
    hdn = jnp.sin(fr2.astype(f32) * (hdn @ w2.astype(f32) + b2.astype(f32)))
    k = (hdn @ w3.astype(f32)).reshape(n, HYENA_ORDER, 2, D_MODEL)
    max_decay = math.log(HYENA_TARGET) / HYENA_FAST_PCT
    min_decay = math.log(HYENA_TARGET) / HYENA_SLOW_PCT
    deltas = jnp.linspace(min_decay, max_decay, D_MODEL, dtype=f32)
    k = k * jnp.exp(-t * jnp.abs(deltas)[None, :])[:, None, None, :]
    fwd, bwd = k[:, :, 0], k[:, :, 1]
    two = jnp.concatenate([fwd, jnp.zeros_like(fwd[:1]), bwd[:0:-1]], axis=0)
    two = two * lax.rsqrt(jnp.sum(two * two, axis=0, keepdims=True) + EPS)
    return jnp.fft.rfft(two, axis=0)


def long_conv(u, kf, bias):
    n = u.shape[1]
    uf32 = u.astype(jnp.float32)
    uf = jnp.fft.rfft(uf32, n=2 * n, axis=1)
    y = jnp.fft.irfft(uf * kf[None], n=2 * n, axis=1)[:, :n]
    return (y + uf32 * bias.astype(jnp.float32)).astype(u.dtype)


def hyena_mixer(h, w_in, b_in, w_conv, b_conv, hf_w1, hf_b1, hf_freq1, hf_w2, hf_b2, hf_freq2, hf_w3, bias, w_out, b_out):
    n = h.shape[1]
    kf = hyena_filter(n, hf_w1, hf_b1, hf_freq1, hf_w2, hf_b2, hf_freq2, hf_w3)
    p = h @ w_in + b_in
    half = HYENA_SHORT_K // 2
    pp = jnp.pad(p, ((0, 0), (half, half), (0, 0)))
    pc = sum((pp[:, j:j + n] * w_conv[j] for j in range(HYENA_SHORT_K)), b_conv)
    v, x1, x2 = jnp.split(pc, 3, axis=-1)
    z = v
    for o, g in enumerate((x1, x2)):
        z = g * long_conv(z, kf[:, o], bias[o])
    return z @ w_out + b_out


def setup_inputs(seed: int = 0) -> dict:
    key = jax.random.key(seed)
    ks = iter(jax.random.split(key, 32))

    def nrm(shape, scale):
        return jax.random.normal(next(ks), shape, jnp.float32) * scale

    D = D_MODEL
    n_even = (DEPTH + 1) // 2
    n_odd = DEPTH // 2
    return {
        "x": nrm((BATCH, SEQ, D), 1.0),
        "c": nrm((BATCH, D), 1.0),
        "ctx": nrm((BATCH, CTX_LEN, D), 1.0),
        "c_ctx": nrm((D,), 1.0),
        "w_mod": nrm((DEPTH, D, N_MOD * D), 0.5 * D ** -0.5),
        "b_mod": nrm((DEPTH, N_MOD * D), 0.02),
        "g_norm": 1.0 + nrm((DEPTH, 3, D), 0.05),
        "w_ffn_in": nrm((DEPTH, 2, D, 2 * D_FF), D ** -0.5),
        "w_ffn_out": nrm((DEPTH, 2, D_FF, D), D_FF ** -0.5),
        "w_attn_in": nrm((n_even, D, ATTN_IN), D ** -0.5),
        "w_attn_out": nrm((n_even, ATTN_OUT, D), ATTN_OUT ** -0.5),
        "g_q": 1.0 + nrm((n_even, 2, HEAD_DIM), 0.05),
        "g_k": 1.0 + nrm((n_even, 2, HEAD_DIM), 0.05),
        "sink": nrm((n_even, N_HEADS_B), 0.5),
        "w_hy_in": nrm((n_odd, D, 3 * D), D ** -0.5),
        "b_hy_in": nrm((n_odd, 3 * D), 0.02),
        "w_hy_conv": nrm((n_odd, HYENA_SHORT_K, 3 * D), HYENA_SHORT_K ** -0.5),
        "b_hy_conv": nrm((n_odd, 3 * D), 0.02),
        "hf_w1": nrm((n_odd, HYENA_POS_DIM, HYENA_HIDDEN), HYENA_POS_DIM ** -0.5),
        "hf_b1": nrm((n_odd, HYENA_HIDDEN), 0.1),
        "hf_freq1": 1.0 + nrm((n_odd, HYENA_HIDDEN), 0.1),
        "hf_w2": nrm((n_odd, HYENA_HIDDEN, HYENA_HIDDEN), HYENA_HIDDEN ** -0.5),
        "hf_b2": nrm((n_odd, HYENA_HIDDEN), 0.1),
        "hf_freq2": 1.0 + nrm((n_odd, HYENA_HIDDEN), 0.1),
        "hf_w3": nrm((n_odd, HYENA_HIDDEN, HYENA_ORDER * 2 * D), HYENA_HIDDEN ** -0.5),
        "hy_bias": nrm((n_odd, HYENA_ORDER, D), 0.1),
        "w_hy_out": nrm((n_odd, D, D), D ** -0.5),
        "b_hy_out": nrm((n_odd, D), 0.02),
    }


def reference(x, c, ctx, c_ctx, w_mod, b_mod, g_norm, w_ffn_in, w_ffn_out, w_attn_in, w_attn_out, g_q, g_k, sink,
              w_hy_in, b_hy_in, w_hy_conv, b_hy_conv, hf_w1, hf_b1, hf_freq1, hf_w2, hf_b2, hf_freq2, hf_w3,
              hy_bias, w_hy_out, b_hy_out):
    n_tok = x.shape[1]
    rope = axial_rope(n_tok)
    s_lat = jax.nn.silu(c)
    s_ctx = jax.nn.silu(c_ctx)[None, :]
    last_ctx = max(l for l in range(DEPTH) if l % 2 == 0)
    xc = ctx
    for l in range(DEPTH):
        i = l // 2
        ctx_live = l <= last_ctx
        ctx_full = l < last_ctx
        m = (s_lat @ w_mod[l] + b_mod[l]).reshape(-1, N_MOD, D_MODEL)
        if ctx_live:
            mc = (s_ctx @ w_mod[l] + b_mod[l]).reshape(-1, N_MOD, D_MODEL)
        x = x + 0.5 * gate_of(m, 0) * swiglu(modulate(x, g_norm[l, 0], m, 0), w_ffn_in[l, 0], w_ffn_out[l, 0])
        if ctx_live:
            xc = xc + 0.5 * gate_of(mc, 0) * swiglu(modulate(xc, g_norm[l, 0], mc, 0), w_ffn_in[l, 0], w_ffn_out[l, 0])
        h = modulate(x, g_norm[l, 1], m, 1)
        if l % 2 == 0:
            hc = modulate(xc, g_norm[l, 1], mc, 1)
            y, yc = attn_mixer(h, hc, w_attn_in[i], w_attn_out[i], g_q[i], g_k[i], sink[i], rope, ctx_full)
        else:
            y = hyena_mixer(h, w_hy_in[i], b_hy_in[i], w_hy_conv[i], b_hy_conv[i], hf_w1[i], hf_b1[i], hf_freq1[i],
                            hf_w2[i], hf_b2[i], hf_freq2[i], hf_w3[i], hy_bias[i], w_hy_out[i], b_hy_out[i])
            if ctx_full:
                hc = modulate(xc, g_norm[l, 1], mc, 1)
                yc = hyena_mixer(hc, w_hy_in[i], b_hy_in[i], w_hy_conv[i], b_hy_conv[i], hf_w1[i], hf_b1[i],
                                 hf_freq1[i], hf_w2[i], hf_b2[i], hf_freq2[i], hf_w3[i], hy_bias[i], w_hy_out[i],
                                 b_hy_out[i])
        x = x + gate_of(m, 1) * y
        if ctx_full:
            xc = xc + gate_of(mc, 1) * yc
        x = x + 0.5 * gate_of(m, 2) * swiglu(modulate(x, g_norm[l, 2], m, 2), w_ffn_in[l, 1], w_ffn_out[l, 1])
        if ctx_full:
            xc = xc + 0.5 * gate_of(mc, 2) * swiglu(modulate(xc, g_norm[l, 2], mc, 2), w_ffn_in[l, 1], w_ffn_out[l, 1])
    return x
```

```python
import functools
import math

import jax
import jax.numpy as jnp
from jax import lax
from jax.experimental import pallas as pl
from jax.experimental.pallas import tpu as pltpu

HEAD_DIM = 128
N_HEADS = 8
N_KV = 2
GROUP = N_HEADS // N_KV
Q_COLS = N_HEADS * HEAD_DIM
KV_COLS = N_KV * HEAD_DIM
GRID_W = 64
WINDOW = 128
ROPE_THETA = 10000.0
N_MOD = 9
HYENA_ORDER = 2
HYENA_SHORT_K = 3
HYENA_BANDS = 16
HYENA_TARGET = 1e-2
HYENA_FAST_PCT = 0.3
HYENA_SLOW_PCT = 1.5
EPS = 1e-6
NEG_INF = -1e30

LANES = 128
SUBLANES = 8
VMEM_LIMIT_BYTES = 56 * 1024 * 1024

F32 = jnp.float32
BF16 = jnp.bfloat16
HIGHEST = lax.Precision.HIGHEST


def _params(*sem):
    return pltpu.CompilerParams(dimension_semantics=sem, vmem_limit_bytes=VMEM_LIMIT_BYTES)


def _tile(dim, pref):
    t = min(dim, pref)
    while dim % t:
        t //= 2
    return t


def _dot(a, b):
    return jnp.dot(a, b, preferred_element_type=F32)


def _dot_nt(a, b):
    return lax.dot_general(a, b, (((1,), (1,)), ((), ())), preferred_element_type=F32)


def _silu(v):
    return v * (1.0 / (1.0 + jnp.exp(-v)))


def _modulated(xf, g, shift, scale):
    ms = jnp.mean(xf * xf, axis=-1, keepdims=True)
    y = xf * lax.rsqrt(ms + EPS)
    return (y * g) * (1.0 + scale) + shift


def _mod_body(c_ref, w_ref, b_ref, o_ref):
    s = _silu(c_ref[...])
    o_ref[0] = jnp.dot(s, w_ref[0], preferred_element_type=F32, precision=HIGHEST) + b_ref[0]


def _mod_all(c_all, w_mod, b_mod):
    depth, d, nd = w_mod.shape
    r = c_all.shape[0]
    tn = _tile(nd, 1024)
    return pl.pallas_call(
        _mod_body,
        out_shape=jax.ShapeDtypeStruct((depth, r, nd), F32),
        grid=(depth, nd // tn),
        in_specs=[
            pl.BlockSpec((r, d), lambda l, j: (0, 0)),
            pl.BlockSpec((1, d, tn), lambda l, j: (l, 0, j)),
            pl.BlockSpec((1, 1, tn), lambda l, j: (l, 0, j)),
        ],
        out_specs=pl.BlockSpec((1, r, tn), lambda l, j: (l, 0, j)),
        compiler_params=_params("parallel", "parallel"),
        name="mod_vectors",
    )(c_all, w_mod, b_mod.reshape(depth, 1, nd))


def _ffn_body(x_ref, mod_ref, g_ref, wg_ref, wu_ref, wo_ref, o_ref, h_ref, acc_ref):
    j = pl.program_id(1)

    @pl.when(j == 0)
    def _():
        h = _modulated(x_ref[...], g_ref[...], mod_ref[0, 0:1, :], mod_ref[0, 1:2, :])
        h_ref[...] = h.astype(BF16)
        acc_ref[...] = jnp.zeros_like(acc_ref)

    h = h_ref[...]
    a = _dot(h, wg_ref[...])
    u = _dot(h, wu_ref[...])
    act = (_silu(a) * u).astype(BF16)
    acc_ref[...] += _dot(act, wo_ref[...])

    @pl.when(j == pl.num_programs(1) - 1)
    def _():
        o_ref[...] = x_ref[...] + (0.5 * mod_ref[0, 2:3, :]) * acc_ref[...]


def _ffn(x, mod, g, w_in, w_out, rows_per_mod):
    rows, d = x.shape
    f = w_out.shape[0]
    tm = _tile(min(rows, rows_per_mod), 512)
    tf = _tile(f, 512)
    nf = f // tf
    return pl.pallas_call(
        _ffn_body,
        out_shape=jax.ShapeDtypeStruct((rows, d), F32),
        grid=(rows // tm, nf),
        in_specs=[
            pl.BlockSpec((tm, d), lambda i, j: (i, 0)),
            pl.BlockSpec((1, 3, d), lambda i, j: (i * tm // rows_per_mod, 0, 0)),
            pl.BlockSpec((1, d), lambda i, j: (0, 0)),
            pl.BlockSpec((d, tf), lambda i, j: (0, j)),
            pl.BlockSpec((d, tf), lambda i, j: (0, nf + j)),
            pl.BlockSpec((tf, d), lambda i, j: (j, 0)),
        ],
        out_specs=pl.BlockSpec((tm, d), lambda i, j: (i, 0)),
        scratch_shapes=[pltpu.VMEM((tm, d), BF16), pltpu.VMEM((tm, d), F32)],
        compiler_params=_params("parallel", "arbitrary"),
        name="ffn_half_step",
    )(x, mod, g.reshape(1, d), w_in, w_in, w_out)


def _proj_body(x_ref, mod_ref, g_ref, w_ref, b_ref, o_ref, h_ref):
    @pl.when(pl.program_id(1) == 0)
    def _():
        h = _modulated(x_ref[...], g_ref[...], mod_ref[0, 0:1, :], mod_ref[0, 1:2, :])
        h_ref[...] = h.astype(BF16)

    o_ref[...] = (_dot(h_ref[...], w_ref[...]) + b_ref[...]).astype(o_ref.dtype)


def _qkv_body(x_ref, mod_ref, g_ref, w_ref, gamma_ref, qs_ref, isv_ref, cos_ref, sin_ref, o_ref, h_ref):
    @pl.when(pl.program_id(1) == 0)
    def _():
        h = _modulated(x_ref[...], g_ref[...], mod_ref[0, 0:1, :], mod_ref[0, 1:2, :])
        h_ref[...] = h.astype(BF16)

    p = _dot(h_ref[...], w_ref[...])
    cos = cos_ref[...]
    sin = sin_ref[...]
    even = (lax.broadcasted_iota(jnp.int32, (1, HEAD_DIM), 1) % 2) == 0
    for hh in range(p.shape[1] // HEAD_DIM):
        sl = slice(hh * HEAD_DIM, (hh + 1) * HEAD_DIM)
        ph = p[:, sl]
        ms = jnp.mean(ph * ph, axis=-1, keepdims=True)
        nh = (ph * lax.rsqrt(ms + EPS)) * gamma_ref[:, sl]
        partner = jnp.where(even, pltpu.roll(nh, HEAD_DIM - 1, 1), pltpu.roll(nh, 1, 1))
        r = (nh * cos + partner * sin) * qs_ref[:, sl]
        o_ref[:, sl] = jnp.where(isv_ref[:, sl] > 0.0, ph, r).astype(o_ref.dtype)


def _proj(x, mod, g, w, b, rows_per_mod, qkv_tables=None, rope=None, seq=None):
    rows, d = x.shape
    n = w.shape[1]
    tm = _tile(min(rows, rows_per_mod), 512)
    tn = _tile(n, 512)
    common = [
        pl.BlockSpec((tm, d), lambda i, j: (i, 0)),
        pl.BlockSpec((1, 3, d), lambda i, j: (i * tm // rows_per_mod, 0, 0)),
        pl.BlockSpec((1, d), lambda i, j: (0, 0)),
        pl.BlockSpec((d, tn), lambda i, j: (0, j)),
    ]
    col = pl.BlockSpec((1, tn), lambda i, j: (0, j))
    if qkv_tables is None:
        body, specs, args = _proj_body, common + [col], (x, mod, g.reshape(1, d), w, b.reshape(1, n))
    else:
        gamma, qs, isv = qkv_tables
        cos, sin = rope
        if seq is None:
            rope_spec = pl.BlockSpec((tm, HEAD_DIM), lambda i, j: (0, 0))
        else:
            rope_spec = pl.BlockSpec((tm, HEAD_DIM), lambda i, j: (i % (seq // tm), 0))
        body = _qkv_body
        specs = common + [col, col, col, rope_spec, rope_spec]
        args = (x, mod, g.reshape(1, d), w, gamma, qs, isv, cos, sin)
    return pl.pallas_call(
        body,
        out_shape=jax.ShapeDtypeStruct((rows, n), BF16),
        grid=(rows // tm, n // tn),
        in_specs=specs,
        out_specs=pl.BlockSpec((tm, tn), lambda i, j: (i, j)),
        scratch_shapes=[pltpu.VMEM((tm, d), BF16)],
        compiler_params=_params("parallel", "arbitrary"),
        name="modulate_project",
    )(*args)


def _attn_body(*refs, has_lat, has_mask, has_sink, tq):
    sink_ref, q_ref, kc_ref, vc_ref = refs[:4]
    if has_lat:
        k_ref, v_ref, o_ref = refs[4:7]
    else:
        o_ref = refs[4]
    g = pl.program_id(1)
    qi = pl.program_id(2)
    for hh in range(GROUP):
        sl = slice(hh * HEAD_DIM, (hh + 1) * HEAD_DIM)
        q = q_ref[:, sl]
        s1 = _dot_nt(q, kc_ref[...])
        m = jnp.max(s1, axis=-1, keepdims=True)
        if has_lat:
            s2 = _dot_nt(q, k_ref[...])
            if has_mask:
                qpos = qi * tq + lax.broadcasted_iota(jnp.int32, (tq, 1), 0)
                kpos = lax.broadcasted_iota(jnp.int32, (1, s2.shape[1]), 1)
                s2 = jnp.where(jnp.abs(kpos - qpos) <= WINDOW, s2, NEG_INF)
            m = jnp.maximum(m, jnp.max(s2, axis=-1, keepdims=True))
        if has_sink:
            sk = sink_ref[g * GROUP + hh]
            m = jnp.maximum(m, sk)
        p1 = jnp.exp(s1 - m)
        l = jnp.sum(p1, axis=-1, keepdims=True)
        o = _dot(p1.astype(BF16), vc_ref[...])
        if has_lat:
            p2 = jnp.exp(s2 - m)
            l = l + jnp.sum(p2, axis=-1, keepdims=True)
            o = o + _dot(p2.astype(BF16), v_ref[...])
        if has_sink:
            l = l + jnp.exp(sk - m)
        o_ref[:, sl] = (o / l).astype(o_ref.dtype)


def _attention(sink, q_src, ctx_src, lat_src, *, batch, q_rows, q_col, k_col, v_col, has_mask, has_sink):
    tq = _tile(q_rows, 256)
    c_rows = ctx_src.shape[0] // batch
    gw = GROUP * HEAD_DIM
    in_specs = [
        pl.BlockSpec(memory_space=pltpu.SMEM),
        pl.BlockSpec((tq, gw), lambda b, g, i: (b * (q_rows // tq) + i, q_col + g)),
        pl.BlockSpec((c_rows, HEAD_DIM), lambda b, g, i: (b, k_col + g)),
        pl.BlockSpec((c_rows, HEAD_DIM), lambda b, g, i: (b, v_col + g)),
    ]
    args = [sink, q_src, ctx_src, ctx_src]
    has_lat = lat_src is not None
    if has_lat:
        l_rows = lat_src.shape[0] // batch
        in_specs += [
            pl.BlockSpec((l_rows, HEAD_DIM), lambda b, g, i: (b, k_col + g)),
            pl.BlockSpec((l_rows, HEAD_DIM), lambda b, g, i: (b, v_col + g)),
        ]
        args += [lat_src, lat_src]
    return pl.pallas_call(
        functools.partial(_attn_body, has_lat=has_lat, has_mask=has_mask, has_sink=has_sink, tq=tq),
        out_shape=jax.ShapeDtypeStruct((batch * q_rows, Q_COLS), BF16),
        grid=(batch, N_KV, q_rows // tq),
        in_specs=in_specs,
        out_specs=pl.BlockSpec((tq, gw), lambda b, g, i: (b * (q_rows // tq) + i, g)),
        compiler_params=_params("parallel", "parallel", "parallel"),
        name="gqa_attention",
    )(*args)


def _outproj_body(*refs, n_in):
    a_refs, w_refs = refs[:n_in], refs[n_in:2 * n_in]
    b_ref, x_ref, mod_ref, o_ref = refs[2 * n_in:]
    y = b_ref[...] + _dot(a_refs[0][...], w_refs[0][...])
    for a_ref, w_ref in zip(a_refs[1:], w_refs[1:]):
        y = y + _dot(a_ref[...], w_ref[...])
    o_ref[...] = x_ref[...] + mod_ref[0, 2:3, :] * y


def _outproj(acts, weights, bias, x, mod, rows_per_mod):
    rows, d = x.shape
    tm = _tile(min(rows, rows_per_mod), 512)
    n_in = len(acts)
    in_specs = [pl.BlockSpec((tm, a.shape[1]), lambda i: (i, 0)) for a in acts]
    in_specs += [pl.BlockSpec(w.shape, lambda i: (0, 0)) for w in weights]
    in_specs += [
        pl.BlockSpec((1, d), lambda i: (0, 0)),
        pl.BlockSpec((tm, d), lambda i: (i, 0)),
        pl.BlockSpec((1, 3, d), lambda i: (i * tm // rows_per_mod, 0, 0)),
    ]
    return pl.pallas_call(
        functools.partial(_outproj_body, n_in=n_in),
        out_shape=jax.ShapeDtypeStruct((rows, d), F32),
        grid=(rows // tm,),
        in_specs=in_specs,
        out_specs=pl.BlockSpec((tm, d), lambda i: (i, 0)),
        compiler_params=_params("parallel"),
        name="outproj_residual",
    )(*acts, *weights, bias.reshape(1, d), x, mod)


def _filter_body(feats_ref, w1_ref, b1_ref, fr1_ref, w2_ref, b2_ref, fr2_ref, w3f_ref, w3b_ref, dl_ref, o_ref):
    feats = feats_ref[...]
    h1 = jnp.sin(fr1_ref[...] * (jnp.dot(feats, w1_ref[...], preferred_element_type=F32, precision=HIGHEST)
                                 + b1_ref[...]))
    h2 = jnp.sin(fr2_ref[...] * (jnp.dot(h1, w2_ref[...], preferred_element_type=F32, precision=HIGHEST)
                                 + b2_ref[...]))
    decay = jnp.exp(-feats[:, 0:1] * dl_ref[...])
    fwd = jnp.dot(h2, w3f_ref[...], preferred_element_type=F32, precision=HIGHEST) * decay
    bwd = jnp.dot(h2, w3b_ref[...], preferred_element_type=F32, precision=HIGHEST) * decay
    row = lax.broadcasted_iota(jnp.int32, (fwd.shape[0], 1), 0)
    bwd = jnp.where(row == 0, 0.0, bwd)
    nrm = lax.rsqrt(jnp.sum(fwd * fwd + bwd * bwd, axis=0, keepdims=True) + EPS)
    o_ref[0] = ((fwd + bwd) * nrm).astype(o_ref.dtype)
    o_ref[1] = ((fwd - bwd) * nrm).astype(o_ref.dtype)


def _hyena_filter_taps(feats, w1, b1, fr1, w2, b2, fr2, w3, absdelta, d):
    n, fp = feats.shape
    hp = w1.shape[1]
    tn = _tile(d, 512)
    nd = d // tn
    vec = pl.BlockSpec((1, hp), lambda o, j: (0, 0))
    return pl.pallas_call(
        _filter_body,
        out_shape=jax.ShapeDtypeStruct((2, n, HYENA_ORDER * d), BF16),
        grid=(HYENA_ORDER, nd),
        in_specs=[
            pl.BlockSpec((n, fp), lambda o, j: (0, 0)),
            pl.BlockSpec((fp, hp), lambda o, j: (0, 0)), vec, vec,
            pl.BlockSpec((hp, hp), lambda o, j: (0, 0)), vec, vec,
            pl.BlockSpec((hp, tn), lambda o, j: (0, (2 * o) * nd + j)),
            pl.BlockSpec((hp, tn), lambda o, j: (0, (2 * o + 1) * nd + j)),
            pl.BlockSpec((1, tn), lambda o, j: (0, j)),
        ],
        out_specs=pl.BlockSpec((2, n, tn), lambda o, j: (0, 0, o * nd + j)),
        compiler_params=_params("parallel", "parallel"),
        name="hyena_filter_taps",
    )(feats, w1, b1, fr1, w2, b2, fr2, w3, w3, absdelta)


def _spectrum_body(a_ref, kk_ref, ks_ref, o_ref, *, inv_n):
    part = pl.program_id(0)
    i = pl.program_id(1)
    tk = a_ref.shape[0]
    z = _dot(a_ref[...], kk_ref[0])
    row = i * tk + lax.broadcasted_iota(jnp.int32, (tk, 1), 0)
    o_ref[0] = z * jnp.where(row == 0, inv_n, 2.0 * inv_n)

    @pl.when(jnp.logical_and(part == 1, i == 0))
    def _():
        nyq = _dot(a_ref[0:2 * SUBLANES, :], ks_ref[0])
        o_ref[0, 0:1, :] = nyq[0:1, :] * inv_n


def _hyena_spectrum(dft, taps):
    n = taps.shape[1]
    cols = taps.shape[2]
    tk = _tile(n, 1024)
    tn = _tile(cols, 512)
    nr = n // tk
    return pl.pallas_call(
        functools.partial(_spectrum_body, inv_n=1.0 / (2 * n)),
        out_shape=jax.ShapeDtypeStruct((2, n, cols), F32),
        grid=(2, nr, cols // tn),
        in_specs=[
            pl.BlockSpec((tk, n), lambda p, i, j: (p * nr + i, 0)),
            pl.BlockSpec((1, n, tn), lambda p, i, j: (p, 0, j)),
            pl.BlockSpec((1, n, tn), lambda p, i, j: (0, 0, j)),
        ],
        out_specs=pl.BlockSpec((1, tk, tn), lambda p, i, j: (p, i, j)),
        compiler_params=_params("parallel", "parallel", "parallel"),
        name="hyena_filter_spectrum",
    )(dft, taps, taps)


def _shortconv_body(p_ref, w_ref, b_ref, o_ref):
    x = p_ref[...].astype(F32)
    n = x.shape[0]
    row = lax.broadcasted_iota(jnp.int32, (n, 1), 0)
    prev = jnp.where(row == 0, 0.0, pltpu.roll(x, 1, 0))
    nxt = jnp.where(row == n - 1, 0.0, pltpu.roll(x, n - 1, 0))
    y = ((b_ref[...] + prev * w_ref[0:1, :]) + x * w_ref[1:2, :]) + nxt * w_ref[2:3, :]
    o_ref[...] = y.astype(o_ref.dtype)


def _shortconv(p, w, b, batch):
    rows, n3 = p.shape
    n = rows // batch
    tn = _tile(n3, 512)
    return pl.pallas_call(
        _shortconv_body,
        out_shape=jax.ShapeDtypeStruct((rows, n3), BF16),
        grid=(batch, n3 // tn),
        in_specs=[
            pl.BlockSpec((n, tn), lambda b_, j: (b_, j)),
            pl.BlockSpec((HYENA_SHORT_K, tn), lambda b_, j: (0, j)),
            pl.BlockSpec((1, tn), lambda b_, j: (0, j)),
        ],
        out_specs=pl.BlockSpec((n, tn), lambda b_, j: (b_, j)),
        compiler_params=_params("parallel", "parallel"),
        name="hyena_short_conv",
    )(p, w, b.reshape(1, n3))


def _dft_fwd_body(are_ref, aim_ref, z_ref, kr_ref, ki_ref, yr_ref, yi_ref):
    i = pl.program_id(1)
    tk = are_ref.shape[0]
    z = z_ref[...]
    zr = _dot(are_ref[...], z)
    zi = _dot(aim_ref[...], z)
    kr = kr_ref[0]
    ki = ki_ref[0]
    row0 = (i * tk + lax.broadcasted_iota(jnp.int32, (tk, 1), 0)) == 0
    zi_ki = zi * ki
    yr_ref[...] = (zr * kr - jnp.where(row0, 0.0, zi_ki)).astype(yr_ref.dtype)
    yi_ref[...] = jnp.where(row0, zi_ki, zr * ki + zi * kr).astype(yi_ref.dtype)


def _dft_fwd(dft, z_src, z_col, spec, order, batch, d):
    n = z_src.shape[0] // batch
    tk = _tile(n, 1024)
    tn = _tile(d, 512)
    nr, nd = n // tk, d // tn
    out = jax.ShapeDtypeStruct((batch * n, d), BF16)
    return pl.pallas_call(
        _dft_fwd_body,
        out_shape=(out, out),
        grid=(nd, nr, batch),
        in_specs=[
            pl.BlockSpec((tk, n), lambda j, i, b: (i, 0)),
            pl.BlockSpec((tk, n), lambda j, i, b: (nr + i, 0)),
            pl.BlockSpec((n, tn), lambda j, i, b: (b, z_col * nd + j)),
            pl.BlockSpec((1, tk, tn), lambda j, i, b: (0, i, order * nd + j)),
            pl.BlockSpec((1, tk, tn), lambda j, i, b: (1, i, order * nd + j)),
        ],
        out_specs=(
            pl.BlockSpec((tk, tn), lambda j, i, b: (b * nr + i, j)),
            pl.BlockSpec((tk, tn), lambda j, i, b: (b * nr + i, j)),
        ),
        compiler_params=_params("parallel", "parallel", "parallel"),
        name="hyena_dft_forward",
    )(dft, dft, z_src, spec, spec)


def _dft_inv_body(atre_ref, atim_ref, yr_ref, yi_ref, g_ref, zp_ref, bias_ref, o_ref):
    y = _dot(atre_ref[...], yr_ref[...]) + _dot(atim_ref[...], yi_ref[...])
    zp = zp_ref[...].astype(F32)
    o_ref[...] = (g_ref[...].astype(F32) * (y + zp * bias_ref[...])).astype(o_ref.dtype)


def _dft_inv(dft_t, yr, yi, gate_src, gate_col, z_src, z_col, bias, batch, d):
    n = yr.shape[0] // batch
    tt = _tile(n, 1024)
    tn = _tile(d, 512)
    nr, nd = n // tt, d // tn
    return pl.pallas_call(
        _dft_inv_body,
        out_shape=jax.ShapeDtypeStruct((batch * n, d), BF16),
        grid=(nd, nr, batch),
        in_specs=[
            pl.BlockSpec((tt, n), lambda j, i, b: (i, 0)),
            pl.BlockSpec((tt, n), lambda j, i, b: (i, 1)),
            pl.BlockSpec((n, tn), lambda j, i, b: (b, j)),
            pl.BlockSpec((n, tn), lambda j, i, b: (b, j)),
            pl.BlockSpec((tt, tn), lambda j, i, b: (b * nr + i, gate_col * nd + j)),
            pl.BlockSpec((tt, tn), lambda j, i, b: (b * nr + i, z_col * nd + j)),
            pl.BlockSpec((1, tn), lambda j, i, b: (0, j)),
        ],
        out_specs=pl.BlockSpec((tt, tn), lambda j, i, b: (b * nr + i, j)),
        compiler_params=_params("parallel", "parallel", "parallel"),
        name="hyena_dft_inverse",
    )(dft_t, dft_t, yr, yi, gate_src, z_src, bias.reshape(1, d))


def _rope_tables(n_tok):
    rows = n_tok // GRID_W
    r = jnp.repeat(jnp.arange(rows), GRID_W).astype(F32)
    col = jnp.tile(jnp.arange(GRID_W), rows).astype(F32)
    half = HEAD_DIM // 2
    inv = ROPE_THETA ** (-jnp.arange(0, half, 2, dtype=F32) / half)
    ang = jnp.concatenate([r[:, None] * inv, col[:, None] * inv], axis=-1)
    cos = jnp.repeat(jnp.cos(ang), 2, axis=-1)
    sin = jnp.repeat(jnp.sin(ang), 2, axis=-1)
    sign = jnp.where(jnp.arange(HEAD_DIM) % 2 == 0, -1.0, 1.0).astype(F32)
    return cos, sin * sign


def _dft_tables(n):
    big = 2 * n
    k = jnp.arange(n, dtype=jnp.int32)[:, None]
    t = jnp.arange(n, dtype=jnp.int32)[None, :]
    ang = ((k * t) % big).astype(F32) * (2.0 * math.pi / big)
    c = jnp.cos(ang)
    s = -jnp.sin(ang)
    nyq = jnp.where(t % 2 == 0, 1.0, -1.0).astype(F32)
    s = jnp.where(k == 0, nyq, s)
    a = jnp.concatenate([c, s], axis=0).astype(BF16)
    return a, a.T


def _filter_features(n, width):
    t = jnp.linspace(0.0, 1.0, n, dtype=F32)[:, None]
    w = (2.0 * math.pi / n) * jnp.arange(n, dtype=F32)[:, None]
    bands = jnp.linspace(1e-4, HYENA_BANDS - 1, HYENA_BANDS, dtype=F32)
    ang = w * bands[None, :]
    feats = jnp.concatenate([t, jnp.cos(ang), -jnp.sin(ang)], axis=-1)
    return jnp.pad(feats, ((0, 0), (0, width - feats.shape[1])))


def _pad_to(a, shape):
    return jnp.pad(a, [(0, s - d) for d, s in zip(a.shape, shape)])


def _attn_layer(x, xc, mod_l, mod_c, g, w_in, w_out, g_q, g_k, sink, rope, batch, seq, n_ctx, ctx_out):
    d = x.shape[1]
    ones = jnp.ones((HEAD_DIM,), F32)
    gamma = jnp.concatenate([jnp.tile(g_q[0], N_HEADS), jnp.tile(g_q[1], N_HEADS),
                             jnp.tile(g_k[0], N_KV), jnp.tile(ones, N_KV),
                             jnp.tile(g_k[1], N_KV), jnp.tile(ones, N_KV)])[None, :]
    qs = jnp.concatenate([jnp.full((2 * Q_COLS,), HEAD_DIM ** -0.5, F32), jnp.ones((4 * KV_COLS,), F32)])[None, :]
    zk, ok = jnp.zeros((KV_COLS,), F32), jnp.ones((KV_COLS,), F32)
    isv = jnp.concatenate([jnp.zeros((2 * Q_COLS,), F32), zk, ok, zk, ok])[None, :]
    tables = (gamma, qs, isv)
    tm = _tile(batch * n_ctx, 512)
    ident = (jnp.ones((tm, HEAD_DIM), F32), jnp.zeros((tm, HEAD_DIM), F32))
    w_in_b = w_in.astype(BF16)
    qkv = _proj(x, mod_l, g, w_in_b, None, seq, qkv_tables=tables, rope=rope, seq=seq)
    qkv_c = _proj(xc, mod_c, g, w_in_b, None, batch * n_ctx, qkv_tables=tables, rope=ident)
    qa_col, qb_col = 0, N_KV
    ka_col = 2 * Q_COLS // HEAD_DIM
    va_col, kb_col, vb_col = ka_col + N_KV, ka_col + 2 * N_KV, ka_col + 3 * N_KV
    o_a = _attention(sink, qkv, qkv_c, qkv, batch=batch, q_rows=seq, q_col=qa_col, k_col=ka_col, v_col=va_col,
                     has_mask=False, has_sink=False)
    o_b = _attention(sink, qkv, qkv_c, qkv, batch=batch, q_rows=seq, q_col=qb_col, k_col=kb_col, v_col=vb_col,
                     has_mask=True, has_sink=True)
    w_out_b = w_out.astype(BF16)
    ws = (w_out_b[:Q_COLS], w_out_b[Q_COLS:])
    zero_b = jnp.zeros((d,), F32)
    x = _outproj((o_a, o_b), ws, zero_b, x, mod_l, seq)
    if ctx_out:
        co_a = _attention(sink, qkv_c, qkv_c, None, batch=batch, q_rows=n_ctx, q_col=qa_col, k_col=ka_col,
                          v_col=va_col, has_mask=False, has_sink=False)
        co_b = _attention(sink, qkv_c, qkv_c, None, batch=batch, q_rows=n_ctx, q_col=qb_col, k_col=kb_col,
                          v_col=vb_col, has_mask=False, has_sink=True)
        xc = _outproj((co_a, co_b), ws, zero_b, xc, mod_c, batch * n_ctx)
    return x, xc


def _hyena_layer(x, mod, g, rows_per_mod, batch, w_in_b, b_in, w_conv, b_conv, filt, hy_bias, w_out_b, b_out):
    d = x.shape[1]
    n = x.shape[0] // batch
    w1, b1, fr1, w2, b2, fr2, w3 = filt
    hp = LANES
    feats = _filter_features(n, LANES)
    max_decay = math.log(HYENA_TARGET) / HYENA_FAST_PCT
    min_decay = math.log(HYENA_TARGET) / HYENA_SLOW_PCT
    absdelta = jnp.abs(jnp.linspace(min_decay, max_decay, d, dtype=F32))[None, :]
    taps = _hyena_filter_taps(
        feats, _pad_to(w1, (LANES, hp)), _pad_to(b1[None, :], (1, hp)), _pad_to(fr1[None, :], (1, hp)),
        _pad_to(w2, (hp, hp)), _pad_to(b2[None, :], (1, hp)), _pad_to(fr2[None, :], (1, hp)),
        _pad_to(w3, (hp, w3.shape[1])), absdelta, d)
    dft, dft_t = _dft_tables(n)
    spec = _hyena_spectrum(dft, taps)
    p = _proj(x, mod, g, w_in_b, b_in, rows_per_mod)
    pc = _shortconv(p, w_conv, b_conv, batch)
    yr, yi = _dft_fwd(dft, pc, 0, spec, 0, batch, d)
    z = _dft_inv(dft_t, yr, yi, pc, 1, pc, 0, hy_bias[0], batch, d)
    yr, yi = _dft_fwd(dft, z, 0, spec, 1, batch, d)
    z = _dft_inv(dft_t, yr, yi, pc, 2, z, 0, hy_bias[1], batch, d)
    return _outproj((z,), (w_out_b,), b_out, x, mod, rows_per_mod)


def kernel(x, c, ctx, c_ctx, w_mod, b_mod, g_norm, w_ffn_in, w_ffn_out, w_attn_in, w_attn_out, g_q, g_k, sink, w_hy_in, b_hy_in, w_hy_conv, b_hy_conv, hf_w1, hf_b1, hf_freq1, hf_w2, hf_b2, hf_freq2, hf_w3, hy_bias, w_hy_out, b_hy_out):
    batch, seq, d = x.shape
    n_ctx = ctx.shape[1]
    depth = w_mod.shape[0]
    rows_c = batch * n_ctx
    rope = _rope_tables(seq)
    last_ctx = max(l for l in range(depth) if l % 2 == 0)

    r_pad = -(-(batch + 1) // SUBLANES) * SUBLANES
    c_all = _pad_to(jnp.concatenate([c, c_ctx[None, :]], axis=0), (r_pad, d))
    m_all = _mod_all(c_all, w_mod, b_mod).reshape(depth, r_pad, N_MOD, d)

    w_ffn_in_b = w_ffn_in.astype(BF16)
    w_ffn_out_b = w_ffn_out.astype(BF16)

    xl = x.reshape(batch * seq, d)
    xc = ctx.reshape(rows_c, d)
    for l in range(depth):
        i = l // 2
        ctx_live = l <= last_ctx
        ctx_full = l < last_ctx
        mods_l = [m_all[l, :batch, 3 * k:3 * k + 3] for k in range(3)]
        mods_c = [m_all[l, batch:batch + 1, 3 * k:3 * k + 3] for k in range(3)]
        xl = _ffn(xl, mods_l[0], g_norm[l, 0], w_ffn_in_b[l, 0], w_ffn_out_b[l, 0], seq)
        if ctx_live:
            xc = _ffn(xc, mods_c[0], g_norm[l, 0], w_ffn_in_b[l, 0], w_ffn_out_b[l, 0], rows_c)
        if l % 2 == 0:
            xl, xc = _attn_layer(xl, xc, mods_l[1], mods_c[1], g_norm[l, 1], w_attn_in[i], w_attn_out[i],
                                 g_q[i], g_k[i], sink[i], rope, batch, seq, n_ctx, ctx_full)
        else:
            w_in_b = w_hy_in[i].astype(BF16)
            w_out_b = w_hy_out[i].astype(BF16)
            filt = (hf_w1[i], hf_b1[i], hf_freq1[i], hf_w2[i], hf_b2[i], hf_freq2[i], hf_w3[i])
            xl = _hyena_layer(xl, mods_l[1], g_norm[l, 1], seq, batch, w_in_b, b_hy_in[i], w_hy_conv[i],
                              b_hy_conv[i], filt, hy_bias[i], w_out_b, b_hy_out[i])
            if ctx_full:
                xc = _hyena_layer(xc, mods_c[1], g_norm[l, 1], rows_c, batch, w_in_b, b_hy_in[i], w_hy_conv[i],
                                  b_hy_conv[i], filt, hy_bias[i], w_out_b, b_hy_out[i])
        xl = _ffn(xl, mods_l[2], g_norm[l, 2], w_ffn_in_b[l, 1], w_ffn_out_b[l, 1], seq)
        if ctx_full:
            xc = _ffn(xc, mods_c[2], g_norm[l, 2], w_ffn_in_b[l, 1], w_ffn_out_b[l, 1], rows_c)
    return xl.reshape(batch, seq, d)
```

```python
import functools
import math

import jax
import jax.numpy as jnp
from jax import lax
from jax.experimental import pallas as pl
from jax.experimental.pallas import tpu as pltpu

HEAD_DIM = 128
N_HEADS = 8
N_KV = 2
GROUP = N_HEADS // N_KV
Q_COLS = N_HEADS * HEAD_DIM
KV_COLS = N_KV * HEAD_DIM
GRID_W = 64
WINDOW = 128
ROPE_THETA = 10000.0
N_MOD = 9
HYENA_ORDER = 2
HYENA_SHORT_K = 3
HYENA_BANDS = 16
HYENA_TARGET = 1e-2
HYENA_FAST_PCT = 0.3
HYENA_SLOW_PCT = 1.5
EPS = 1e-6
NEG_INF = -1e30

LANES = 128
SUBLANES = 8
VMEM_LIMIT_BYTES = 56 * 1024 * 1024

F32 = jnp.float32
BF16 = jnp.bfloat16
HIGHEST = lax.Precision.HIGHEST


def _params(*sem):
    return pltpu.CompilerParams(dimension_semantics=sem, vmem_limit_bytes=VMEM_LIMIT_BYTES)


def _tile(dim, pref):
    t = min(dim, pref)
    while dim % t:
        t //= 2
    return t


def _dot(a, b):
    return jnp.dot(a, b, preferred_element_type=F32)


def _dot_nt(a, b):
    return lax.dot_general(a, b, (((1,), (1,)), ((), ())), preferred_element_type=F32)


def _silu(v):
    return v * (1.0 / (1.0 + jnp.exp(-v)))


def _modulated(xf, g, shift, scale):
    ms = jnp.mean(xf * xf, axis=-1, keepdims=True)
    y = xf * lax.rsqrt(ms + EPS)
    return (y * g) * (1.0 + scale) + shift


def _mod_body(c_ref, w_ref, b_ref, o_ref):
    s = _silu(c_ref[...])
    o_ref[0] = jnp.dot(s, w_ref[0], preferred_element_type=F32, precision=HIGHEST) + b_ref[0]


def _mod_all(c_all, w_mod, b_mod):
    depth, d, nd = w_mod.shape
    r = c_all.shape[0]
    tn = _tile(nd, 1024)
    return pl.pallas_call(
        _mod_body,
        out_shape=jax.ShapeDtypeStruct((depth, r, nd), F32),
        grid=(depth, nd // tn),
        in_specs=[
            pl.BlockSpec((r, d), lambda l, j: (0, 0)),
            pl.BlockSpec((1, d, tn), lambda l, j: (l, 0, j)),
            pl.BlockSpec((1, 1, tn), lambda l, j: (l, 0, j)),
        ],
        out_specs=pl.BlockSpec((1, r, tn), lambda l, j: (l, 0, j)),
        compiler_params=_params("parallel", "parallel"),
        name="mod_vectors",
    )(c_all, w_mod, b_mod.reshape(depth, 1, nd))


def _ffn_body(x_ref, xn_ref, mod_ref, modn_ref, g_ref, wg_ref, wu_ref, wo_ref, o_ref, ha_ref, hb_ref, acc_ref, *,
              chunk):
    i = pl.program_id(0)
    j = pl.program_id(1)
    tm = x_ref.shape[0]

    @pl.when(jnp.logical_and(i == 0, j == 0))
    def _():
        h = _modulated(x_ref[...], g_ref[...], mod_ref[0, 0:1, :], mod_ref[0, 1:2, :])
        ha_ref[...] = h.astype(BF16)
        acc_ref[...] = jnp.zeros_like(acc_ref)

    def step(h_cur, h_next):
        start = pl.multiple_of(jnp.minimum(j * chunk, tm - chunk), 2 * SUBLANES)
        rows = pl.ds(start, chunk)
        hn = _modulated(xn_ref[rows, :], g_ref[...], modn_ref[0, 0:1, :], modn_ref[0, 1:2, :])
        h_next[rows, :] = hn.astype(BF16)

        h = h_cur[...]
        a = _dot(h, wg_ref[...])
        u = _dot(h, wu_ref[...])
        act = (_silu(a) * u).astype(BF16)
        prev = jnp.where(j == 0, 0.0, acc_ref[...])
        acc_ref[...] = prev + _dot(act, wo_ref[...])

    @pl.when(i % 2 == 0)
    def _():
        step(ha_ref, hb_ref)

    @pl.when(i % 2 == 1)
    def _():
        step(hb_ref, ha_ref)

    @pl.when(j == pl.num_programs(1) - 1)
    def _():
        o_ref[...] = x_ref[...] + (0.5 * mod_ref[0, 2:3, :]) * acc_ref[...]


def _ffn(x, mod, g, w_in, w_out, layer, half, rows_per_mod):
    rows, d = x.shape
    f = w_out.shape[2]
    tm = _tile(min(rows, rows_per_mod), 512)
    tf = _tile(f, 512)
    nf = f // tf
    ni = rows // tm
    pack = 2 * SUBLANES
    chunk = min(tm, pl.cdiv(pl.cdiv(tm, nf), pack) * pack)

    def nxt(i):
        return jnp.minimum(i + 1, ni - 1)

    return pl.pallas_call(
        functools.partial(_ffn_body, chunk=chunk),
        out_shape=jax.ShapeDtypeStruct((rows, d), F32),
        grid=(ni, nf),
        in_specs=[
            pl.BlockSpec((tm, d), lambda i, j: (i, 0)),
            pl.BlockSpec((tm, d), lambda i, j: (nxt(i), 0)),
            pl.BlockSpec((1, 3, d), lambda i, j: (i * tm // rows_per_mod, 0, 0)),
            pl.BlockSpec((1, 3, d), lambda i, j: (nxt(i) * tm // rows_per_mod, 0, 0)),
            pl.BlockSpec((1, d), lambda i, j: (0, 0)),
            pl.BlockSpec((None, None, d, tf), lambda i, j: (layer, half, 0, j)),
            pl.BlockSpec((None, None, d, tf), lambda i, j: (layer, half, 0, nf + j)),
            pl.BlockSpec((None, None, tf, d), lambda i, j: (layer, half, j, 0)),
        ],
        out_specs=pl.BlockSpec((tm, d), lambda i, j: (i, 0)),
        scratch_shapes=[pltpu.VMEM((tm, d), BF16), pltpu.VMEM((tm, d), BF16), pltpu.VMEM((tm, d), F32)],
        compiler_params=_params("arbitrary", "arbitrary"),
        name="ffn_half_step",
    )(x, x, mod, mod, g.reshape(1, d), w_in, w_in, w_out)


def _proj_body(x_ref, mod_ref, g_ref, w_ref, b_ref, o_ref, h_ref):
    @pl.when(pl.program_id(1) == 0)
    def _():
        h = _modulated(x_ref[...], g_ref[...], mod_ref[0, 0:1, :], mod_ref[0, 1:2, :])
        h_ref[...] = h.astype(BF16)

    o_ref[...] = (_dot(h_ref[...], w_ref[...]) + b_ref[...]).astype(o_ref.dtype)


def _qkv_body(x_ref, mod_ref, g_ref, w_ref, gamma_ref, qs_ref, isv_ref, cos_ref, sin_ref, o_ref, h_ref, p_ref):
    i = pl.program_id(0)
    j = pl.program_id(1)

    @pl.when(j == 0)
    def _():
        h = _modulated(x_ref[...], g_ref[...], mod_ref[0, 0:1, :], mod_ref[0, 1:2, :])
        h_ref[...] = h.astype(BF16)

    @pl.when(jnp.logical_and(i == 0, j == 0))
    def _():
        p_ref[...] = jnp.zeros_like(p_ref)

    cos = cos_ref[...]
    sin = sin_ref[...]
    even = (lax.broadcasted_iota(jnp.int32, (1, HEAD_DIM), 1) % 2) == 0
    for hh in range(p_ref.shape[1] // HEAD_DIM):
        sl = slice(hh * HEAD_DIM, (hh + 1) * HEAD_DIM)
        ph = p_ref[:, sl]
        ms = jnp.mean(ph * ph, axis=-1, keepdims=True)
        nh = (ph * lax.rsqrt(ms + EPS)) * gamma_ref[:, sl]
        partner = jnp.where(even, pltpu.roll(nh, HEAD_DIM - 1, 1), pltpu.roll(nh, 1, 1))
        r = (nh * cos + partner * sin) * qs_ref[:, sl]
        o_ref[:, sl] = jnp.where(isv_ref[:, sl] > 0.0, ph, r).astype(o_ref.dtype)
    p_ref[...] = _dot(h_ref[...], w_ref[...])


def _row_tile(rows, rows_per_mod):
    return _tile(min(rows, rows_per_mod), 512)


def _proj(x, mod, g, w, b, rows_per_mod):
    rows, d = x.shape
    n = w.shape[1]
    tm = _row_tile(rows, rows_per_mod)
    tn = _tile(n, 512)
    return pl.pallas_call(
        _proj_body,
        out_shape=jax.ShapeDtypeStruct((rows, n), BF16),
        grid=(rows // tm, n // tn),
        in_specs=[
            pl.BlockSpec((tm, d), lambda i, j: (i, 0)),
            pl.BlockSpec((1, 3, d), lambda i, j: (i * tm // rows_per_mod, 0, 0)),
            pl.BlockSpec((1, d), lambda i, j: (0, 0)),
            pl.BlockSpec((d, tn), lambda i, j: (0, j)),
            pl.BlockSpec((1, tn), lambda i, j: (0, j)),
        ],
        out_specs=pl.BlockSpec((tm, tn), lambda i, j: (i, j)),
        scratch_shapes=[pltpu.VMEM((tm, d), BF16)],
        compiler_params=_params("parallel", "arbitrary"),
        name="modulate_project",
    )(x, mod, g.reshape(1, d), w, b.reshape(1, n))


def _qkv_proj(x, mod, g, w, rows_per_mod, tables, rope, seq):
    rows, d = x.shape
    n = w.shape[1]
    tm = _row_tile(rows, rows_per_mod)
    tn = _tile(n, 512)
    nj = n // tn
    gamma, qs, isv = tables
    cos, sin = rope

    def prev(j):
        return jnp.maximum(j - 1, 0)

    col = pl.BlockSpec((1, tn), lambda i, j: (0, prev(j)))
    if seq is None:
        rope_spec = pl.BlockSpec((tm, HEAD_DIM), lambda i, j: (0, 0))
    else:
        rope_spec = pl.BlockSpec((tm, HEAD_DIM), lambda i, j: (i % (seq // tm), 0))
    return pl.pallas_call(
        _qkv_body,
        out_shape=jax.ShapeDtypeStruct((rows, n), BF16),
        grid=(rows // tm, nj + 1),
        in_specs=[
            pl.BlockSpec((tm, d), lambda i, j: (i, 0)),
            pl.BlockSpec((1, 3, d), lambda i, j: (i * tm // rows_per_mod, 0, 0)),
            pl.BlockSpec((1, d), lambda i, j: (0, 0)),
            pl.BlockSpec((d, tn), lambda i, j: (0, jnp.minimum(j, nj - 1))),
            col, col, col, rope_spec, rope_spec,
        ],
        out_specs=pl.BlockSpec((tm, tn), lambda i, j: (i, prev(j))),
        scratch_shapes=[pltpu.VMEM((tm, d), BF16), pltpu.VMEM((tm, tn), F32)],
        compiler_params=_params("arbitrary", "arbitrary"),
        name="qkv_project",
    )(x, mod, g.reshape(1, d), w, gamma, qs, isv, cos, sin)


def _attn_body(*refs, has_lat, has_mask, has_sink, tq):
    sink_ref, q_ref, kc_ref, vc_ref = refs[:4]
    if has_lat:
        k_ref, v_ref, o_ref = refs[4:7]
    else:
        o_ref = refs[4]
    g = pl.program_id(1)
    qi = pl.program_id(2)
    if has_lat:
        l_rows = k_ref.shape[0]
        if has_mask:
            kw = min(l_rows, tq + 2 * WINDOW)
            k0 = pl.multiple_of(jnp.clip(qi * tq - WINDOW, 0, l_rows - kw), WINDOW)
            keys = pl.ds(k0, kw)
        else:
            k0, keys = 0, slice(None)
    for hh in range(GROUP):
        sl = slice(hh * HEAD_DIM, (hh + 1) * HEAD_DIM)
        q = q_ref[:, sl]
        s1 = _dot_nt(q, kc_ref[...])
        m = jnp.max(s1, axis=-1, keepdims=True)
        if has_lat:
            s2 = _dot_nt(q, k_ref[keys, :])
            if has_mask:
                qpos = qi * tq + lax.broadcasted_iota(jnp.int32, (tq, 1), 0)
                kpos = k0 + lax.broadcasted_iota(jnp.int32, (1, s2.shape[1]), 1)
                s2 = jnp.where(jnp.abs(kpos - qpos) <= WINDOW, s2, NEG_INF)
            m = jnp.maximum(m, jnp.max(s2, axis=-1, keepdims=True))
        if has_sink:
            sk = sink_ref[g * GROUP + hh]
            m = jnp.maximum(m, sk)
        p1 = jnp.exp(s1 - m)
        l = jnp.sum(p1, axis=-1, keepdims=True)
        o = _dot(p1.astype(BF16), vc_ref[...])
        if has_lat:
            p2 = jnp.exp(s2 - m)
            l = l + jnp.sum(p2, axis=-1, keepdims=True)
            o = o + _dot(p2.astype(BF16), v_ref[keys, :])
        if has_sink:
            l = l + jnp.exp(sk - m)
        o_ref[:, sl] = (o / l).astype(o_ref.dtype)


def _attention(sink, q_src, ctx_src, lat_src, *, batch, q_rows, q_col, k_col, v_col, has_mask, has_sink):
    tq = _tile(q_rows, 256)
    assert not has_mask or tq % WINDOW == 0
    c_rows = ctx_src.shape[0] // batch
    gw = GROUP * HEAD_DIM
    in_specs = [
        pl.BlockSpec(memory_space=pltpu.SMEM),
        pl.BlockSpec((tq, gw), lambda b, g, i: (b * (q_rows // tq) + i, q_col + g)),
        pl.BlockSpec((c_rows, HEAD_DIM), lambda b, g, i: (b, k_col + g)),
        pl.BlockSpec((c_rows, HEAD_DIM), lambda b, g, i: (b, v_col + g)),
    ]
    args = [sink, q_src, ctx_src, ctx_src]
    has_lat = lat_src is not None
    if has_lat:
        l_rows = lat_src.shape[0] // batch
        in_specs += [
            pl.BlockSpec((l_rows, HEAD_DIM), lambda b, g, i: (b, k_col + g)),
            pl.BlockSpec((l_rows, HEAD_DIM), lambda b, g, i: (b, v_col + g)),
        ]
        args += [lat_src, lat_src]
    return pl.pallas_call(
        functools.partial(_attn_body, has_lat=has_lat, has_mask=has_mask, has_sink=has_sink, tq=tq),
        out_shape=jax.ShapeDtypeStruct((batch * q_rows, Q_COLS), BF16),
        grid=(batch, N_KV, q_rows // tq),
        in_specs=in_specs,
        out_specs=pl.BlockSpec((tq, gw), lambda b, g, i: (b * (q_rows // tq) + i, g)),
        compiler_params=_params("parallel", "parallel", "parallel"),
        name="gqa_attention",
    )(*args)


def _outproj_body(*refs, n_in):
    a_refs, w_refs = refs[:n_in], refs[n_in:2 * n_in]
    b_ref, x_ref, mod_ref, o_ref = refs[2 * n_in:]
    y = b_ref[...] + _dot(a_refs[0][...], w_refs[0][...])
    for a_ref, w_ref in zip(a_refs[1:], w_refs[1:]):
        y = y + _dot(a_ref[...], w_ref[...])
    o_ref[...] = x_ref[...] + mod_ref[0, 2:3, :] * y


def _outproj(acts, weight, bias, x, mod, rows_per_mod):
    rows, d = x.shape
    tm = _tile(min(rows, rows_per_mod), 512)
    n_in = len(acts)
    k = acts[0].shape[1]
    assert all(a.shape[1] == k for a in acts) and weight.shape[0] == n_in * k
    in_specs = [pl.BlockSpec((tm, k), lambda i: (i, 0)) for _ in acts]
    in_specs += [pl.BlockSpec((k, d), lambda i, n=n: (n, 0)) for n in range(n_in)]
    in_specs += [
        pl.BlockSpec((1, d), lambda i: (0, 0)),
        pl.BlockSpec((tm, d), lambda i: (i, 0)),
        pl.BlockSpec((1, 3, d), lambda i: (i * tm // rows_per_mod, 0, 0)),
    ]
    return pl.pallas_call(
        functools.partial(_outproj_body, n_in=n_in),
        out_shape=jax.ShapeDtypeStruct((rows, d), F32),
        grid=(rows // tm,),
        in_specs=in_specs,
        out_specs=pl.BlockSpec((tm, d), lambda i: (i, 0)),
        compiler_params=_params("parallel"),
        name="outproj_residual",
    )(*acts, *([weight] * n_in), bias.reshape(1, d), x, mod)


def _filter_body(feats_ref, w1_ref, b1_ref, fr1_ref, w2_ref, b2_ref, fr2_ref, w3f_ref, w3b_ref, dl_ref, o_ref):
    feats = feats_ref[...]
    h1 = jnp.sin(fr1_ref[...] * (jnp.dot(feats, w1_ref[...], preferred_element_type=F32, precision=HIGHEST)
                                 + b1_ref[...]))
    h2 = jnp.sin(fr2_ref[...] * (jnp.dot(h1, w2_ref[...], preferred_element_type=F32, precision=HIGHEST)
                                 + b2_ref[...]))
    decay = jnp.exp(-feats[:, 0:1] * dl_ref[...])
    fwd = jnp.dot(h2, w3f_ref[...], preferred_element_type=F32, precision=HIGHEST) * decay
    bwd = jnp.dot(h2, w3b_ref[...], preferred_element_type=F32, precision=HIGHEST) * decay
    row = lax.broadcasted_iota(jnp.int32, (fwd.shape[0], 1), 0)
    bwd = jnp.where(row == 0, 0.0, bwd)
    nrm = lax.rsqrt(jnp.sum(fwd * fwd + bwd * bwd, axis=0, keepdims=True) + EPS)
    o_ref[0] = ((fwd + bwd) * nrm).astype(o_ref.dtype)
    o_ref[1] = ((fwd - bwd) * nrm).astype(o_ref.dtype)


def _hyena_filter_taps(feats, w1, b1, fr1, w2, b2, fr2, w3, absdelta, d):
    n, fp = feats.shape
    hp = w1.shape[1]
    tn = _tile(d, 512)
    nd = d // tn
    vec = pl.BlockSpec((1, hp), lambda o, j: (0, 0))
    return pl.pallas_call(
        _filter_body,
        out_shape=jax.ShapeDtypeStruct((2, n, HYENA_ORDER * d), BF16),
        grid=(HYENA_ORDER, nd),
        in_specs=[
            pl.BlockSpec((n, fp), lambda o, j: (0, 0)),
            pl.BlockSpec((fp, hp), lambda o, j: (0, 0)), vec, vec,
            pl.BlockSpec((hp, hp), lambda o, j: (0, 0)), vec, vec,
            pl.BlockSpec((hp, tn), lambda o, j: (0, (2 * o) * nd + j)),
            pl.BlockSpec((hp, tn), lambda o, j: (0, (2 * o + 1) * nd + j)),
            pl.BlockSpec((1, tn), lambda o, j: (0, j)),
        ],
        out_specs=pl.BlockSpec((2, n, tn), lambda o, j: (0, 0, o * nd + j)),
        compiler_params=_params("parallel", "parallel"),
        name="hyena_filter_taps",
    )(feats, w1, b1, fr1, w2, b2, fr2, w3, w3, absdelta)


def _spectrum_body(a_ref, kk_ref, ks_ref, o_ref, *, inv_n):
    part = pl.program_id(0)
    i = pl.program_id(1)
    tk = a_ref.shape[0]
    z = _dot(a_ref[...], kk_ref[0])
    row = i * tk + lax.broadcasted_iota(jnp.int32, (tk, 1), 0)
    o_ref[0] = z * jnp.where(row == 0, inv_n, 2.0 * inv_n)

    @pl.when(jnp.logical_and(part == 1, i == 0))
    def _():
        nyq = _dot(a_ref[0:2 * SUBLANES, :], ks_ref[0])
        o_ref[0, 0:1, :] = nyq[0:1, :] * inv_n


def _hyena_spectrum(dft, taps):
    n = taps.shape[1]
    cols = taps.shape[2]
    tk = _tile(n, 1024)
    tn = _tile(cols, 512)
    nr = n // tk
    return pl.pallas_call(
        functools.partial(_spectrum_body, inv_n=1.0 / (2 * n)),
        out_shape=jax.ShapeDtypeStruct((2, n, cols), F32),
        grid=(2, nr, cols // tn),
        in_specs=[
            pl.BlockSpec((tk, n), lambda p, i, j: (p * nr + i, 0)),
            pl.BlockSpec((1, n, tn), lambda p, i, j: (p, 0, j)),
            pl.BlockSpec((1, n, tn), lambda p, i, j: (0, 0, j)),
        ],
        out_specs=pl.BlockSpec((1, tk, tn), lambda p, i, j: (p, i, j)),
        compiler_params=_params("parallel", "parallel", "parallel"),
        name="hyena_filter_spectrum",
    )(dft, taps, taps)


def _shortconv_body(p_ref, w_ref, b_ref, o_ref):
    x = p_ref[...].astype(F32)
    n = x.shape[0]
    row = lax.broadcasted_iota(jnp.int32, (n, 1), 0)
    prev = jnp.where(row == 0, 0.0, pltpu.roll(x, 1, 0))
    nxt = jnp.where(row == n - 1, 0.0, pltpu.roll(x, n - 1, 0))
    y = ((b_ref[...] + prev * w_ref[0:1, :]) + x * w_ref[1:2, :]) + nxt * w_ref[2:3, :]
    o_ref[...] = y.astype(o_ref.dtype)


def _shortconv(p, w, b, batch):
    rows, n3 = p.shape
    n = rows // batch
    tn = _tile(n3, 512)
    return pl.pallas_call(
        _shortconv_body,
        out_shape=jax.ShapeDtypeStruct((rows, n3), BF16),
        grid=(batch, n3 // tn),
        in_specs=[
            pl.BlockSpec((n, tn), lambda b_, j: (b_, j)),
            pl.BlockSpec((HYENA_SHORT_K, tn), lambda b_, j: (0, j)),
            pl.BlockSpec((1, tn), lambda b_, j: (0, j)),
        ],
        out_specs=pl.BlockSpec((n, tn), lambda b_, j: (b_, j)),
        compiler_params=_params("parallel", "parallel"),
        name="hyena_short_conv",
    )(p, w, b.reshape(1, n3))


def _dft_fwd_body(are_ref, aim_ref, z_ref, kr_ref, ki_ref, yr_ref, yi_ref):
    i = pl.program_id(1)
    tk = are_ref.shape[0]
    z = z_ref[...]
    zr = _dot(are_ref[...], z)
    zi = _dot(aim_ref[...], z)
    kr = kr_ref[0]
    ki = ki_ref[0]
    row0 = (i * tk + lax.broadcasted_iota(jnp.int32, (tk, 1), 0)) == 0
    zi_ki = zi * ki
    yr_ref[...] = (zr * kr - jnp.where(row0, 0.0, zi_ki)).astype(yr_ref.dtype)
    yi_ref[...] = jnp.where(row0, zi_ki, zr * ki + zi * kr).astype(yi_ref.dtype)


def _dft_fwd(dft, z_src, z_col, spec, order, batch, d):
    n = z_src.shape[0] // batch
    tk = _tile(n, 1024)
    tn = _tile(d, 512)
    nr, nd = n // tk, d // tn
    out = jax.ShapeDtypeStruct((batch * n, d), BF16)
    return pl.pallas_call(
        _dft_fwd_body,
        out_shape=(out, out),
        grid=(nd, nr, batch),
        in_specs=[
            pl.BlockSpec((tk, n), lambda j, i, b: (i, 0)),
            pl.BlockSpec((tk, n), lambda j, i, b: (nr + i, 0)),
            pl.BlockSpec((n, tn), lambda j, i, b: (b, z_col * nd + j)),
            pl.BlockSpec((1, tk, tn), lambda j, i, b: (0, i, order * nd + j)),
            pl.BlockSpec((1, tk, tn), lambda j, i, b: (1, i, order * nd + j)),
        ],
        out_specs=(
            pl.BlockSpec((tk, tn), lambda j, i, b: (b * nr + i, j)),
            pl.BlockSpec((tk, tn), lambda j, i, b: (b * nr + i, j)),
        ),
        compiler_params=_params("parallel", "parallel", "parallel"),
        name="hyena_dft_forward",
    )(dft, dft, z_src, spec, spec)


def _dft_inv_body(atre_ref, atim_ref, yr_ref, yi_ref, g_ref, zp_ref, bias_ref, o_ref):
    y = _dot(atre_ref[...], yr_ref[...]) + _dot(atim_ref[...], yi_ref[...])
    zp = zp_ref[...].astype(F32)
    o_ref[...] = (g_ref[...].astype(F32) * (y + zp * bias_ref[...])).astype(o_ref.dtype)


def _dft_inv(dft_t, yr, yi, gate_src, gate_col, z_src, z_col, bias, batch, d):
    n = yr.shape[0] // batch
    tt = _tile(n, 1024)
    tn = _tile(d, 512)
    nr, nd = n // tt, d // tn
    return pl.pallas_call(
        _dft_inv_body,
        out_shape=jax.ShapeDtypeStruct((batch * n, d), BF16),
        grid=(nd, nr, batch),
        in_specs=[
            pl.BlockSpec((tt, n), lambda j, i, b: (i, 0)),
            pl.BlockSpec((tt, n), lambda j, i, b: (i, 1)),
            pl.BlockSpec((n, tn), lambda j, i, b: (b, j)),
            pl.BlockSpec((n, tn), lambda j, i, b: (b, j)),
            pl.BlockSpec((tt, tn), lambda j, i, b: (b * nr + i, gate_col * nd + j)),
            pl.BlockSpec((tt, tn), lambda j, i, b: (b * nr + i, z_col * nd + j)),
            pl.BlockSpec((1, tn), lambda j, i, b: (0, j)),
        ],
        out_specs=pl.BlockSpec((tt, tn), lambda j, i, b: (b * nr + i, j)),
        compiler_params=_params("parallel", "parallel", "parallel"),
        name="hyena_dft_inverse",
    )(dft_t, dft_t, yr, yi, gate_src, z_src, bias.reshape(1, d))


def _rope_tables(n_tok):
    rows = n_tok // GRID_W
    r = jnp.repeat(jnp.arange(rows), GRID_W).astype(F32)
    col = jnp.tile(jnp.arange(GRID_W), rows).astype(F32)
    half = HEAD_DIM // 2
    inv = ROPE_THETA ** (-jnp.arange(0, half, 2, dtype=F32) / half)
    ang = jnp.concatenate([r[:, None] * inv, col[:, None] * inv], axis=-1)
    cos = jnp.repeat(jnp.cos(ang), 2, axis=-1)
    sin = jnp.repeat(jnp.sin(ang), 2, axis=-1)
    sign = jnp.where(jnp.arange(HEAD_DIM) % 2 == 0, -1.0, 1.0).astype(F32)
    return cos, sin * sign


def _dft_tables(n):
    big = 2 * n
    k = jnp.arange(n, dtype=jnp.int32)[:, None]
    t = jnp.arange(n, dtype=jnp.int32)[None, :]
    ang = ((k * t) % big).astype(F32) * (2.0 * math.pi / big)
    c = jnp.cos(ang)
    s = -jnp.sin(ang)
    nyq = jnp.where(t % 2 == 0, 1.0, -1.0).astype(F32)
    s = jnp.where(k == 0, nyq, s)
    a = jnp.concatenate([c, s], axis=0).astype(BF16)
    return a, a.T


def _filter_features(n, width):
    t = jnp.linspace(0.0, 1.0, n, dtype=F32)[:, None]
    w = (2.0 * math.pi / n) * jnp.arange(n, dtype=F32)[:, None]
    bands = jnp.linspace(1e-4, HYENA_BANDS - 1, HYENA_BANDS, dtype=F32)
    ang = w * bands[None, :]
    feats = jnp.concatenate([t, jnp.cos(ang), -jnp.sin(ang)], axis=-1)
    return jnp.pad(feats, ((0, 0), (0, width - feats.shape[1])))


def _pad_to(a, shape):
    return jnp.pad(a, [(0, s - d) for d, s in zip(a.shape, shape)])


def _attn_layer(x, xc, mod_l, mod_c, g, w_in, w_out, g_q, g_k, sink, rope, batch, seq, n_ctx, ctx_out):
    d = x.shape[1]
    ones = jnp.ones((HEAD_DIM,), F32)
    gamma = jnp.concatenate([jnp.tile(g_q[0], N_HEADS), jnp.tile(g_q[1], N_HEADS),
                             jnp.tile(g_k[0], N_KV), jnp.tile(ones, N_KV),
                             jnp.tile(g_k[1], N_KV), jnp.tile(ones, N_KV)])[None, :]
    qs = jnp.concatenate([jnp.full((2 * Q_COLS,), HEAD_DIM ** -0.5, F32), jnp.ones((4 * KV_COLS,), F32)])[None, :]
    zk, ok = jnp.zeros((KV_COLS,), F32), jnp.ones((KV_COLS,), F32)
    isv = jnp.concatenate([jnp.zeros((2 * Q_COLS,), F32), zk, ok, zk, ok])[None, :]
    tables = (gamma, qs, isv)
    tm = _row_tile(batch * n_ctx, batch * n_ctx)
    ident = (jnp.ones((tm, HEAD_DIM), F32), jnp.zeros((tm, HEAD_DIM), F32))
    w_in_b = w_in.astype(BF16)
    qkv = _qkv_proj(x, mod_l, g, w_in_b, seq, tables, rope, seq)
    qkv_c = _qkv_proj(xc, mod_c, g, w_in_b, batch * n_ctx, tables, ident, None)
    qa_col, qb_col = 0, N_KV
    ka_col = 2 * Q_COLS // HEAD_DIM
    va_col, kb_col, vb_col = ka_col + N_KV, ka_col + 2 * N_KV, ka_col + 3 * N_KV
    o_a = _attention(sink, qkv, qkv_c, qkv, batch=batch, q_rows=seq, q_col=qa_col, k_col=ka_col, v_col=va_col,
                     has_mask=False, has_sink=False)
    o_b = _attention(sink, qkv, qkv_c, qkv, batch=batch, q_rows=seq, q_col=qb_col, k_col=kb_col, v_col=vb_col,
                     has_mask=True, has_sink=True)
    w_out_b = w_out.astype(BF16)
    zero_b = jnp.zeros((d,), F32)
    x = _outproj((o_a, o_b), w_out_b, zero_b, x, mod_l, seq)
    if ctx_out:
        co_a = _attention(sink, qkv_c, qkv_c, None, batch=batch, q_rows=n_ctx, q_col=qa_col, k_col=ka_col,
                          v_col=va_col, has_mask=False, has_sink=False)
        co_b = _attention(sink, qkv_c, qkv_c, None, batch=batch, q_rows=n_ctx, q_col=qb_col, k_col=kb_col,
                          v_col=vb_col, has_mask=False, has_sink=True)
        xc = _outproj((co_a, co_b), w_out_b, zero_b, xc, mod_c, batch * n_ctx)
    return x, xc


def _hyena_layer(x, mod, g, rows_per_mod, batch, w_in_b, b_in, w_conv, b_conv, filt, hy_bias, w_out_b, b_out):
    d = x.shape[1]
    n = x.shape[0] // batch
    w1, b1, fr1, w2, b2, fr2, w3 = filt
    hp = LANES
    feats = _filter_features(n, LANES)
    max_decay = math.log(HYENA_TARGET) / HYENA_FAST_PCT
    min_decay = math.log(HYENA_TARGET) / HYENA_SLOW_PCT
    absdelta = jnp.abs(jnp.linspace(min_decay, max_decay, d, dtype=F32))[None, :]
    taps = _hyena_filter_taps(
        feats, _pad_to(w1, (LANES, hp)), _pad_to(b1[None, :], (1, hp)), _pad_to(fr1[None, :], (1, hp)),
        _pad_to(w2, (hp, hp)), _pad_to(b2[None, :], (1, hp)), _pad_to(fr2[None, :], (1, hp)),
        _pad_to(w3, (hp, w3.shape[1])), absdelta, d)
    dft, dft_t = _dft_tables(n)
    spec = _hyena_spectrum(dft, taps)
    p = _proj(x, mod, g, w_in_b, b_in, rows_per_mod)
    pc = _shortconv(p, w_conv, b_conv, batch)
    yr, yi = _dft_fwd(dft, pc, 0, spec, 0, batch, d)
    z = _dft_inv(dft_t, yr, yi, pc, 1, pc, 0, hy_bias[0], batch, d)
    yr, yi = _dft_fwd(dft, z, 0, spec, 1, batch, d)
    z = _dft_inv(dft_t, yr, yi, pc, 2, z, 0, hy_bias[1], batch, d)
    return _outproj((z,), w_out_b, b_out, x, mod, rows_per_mod)


def kernel(x, c, ctx, c_ctx, w_mod, b_mod, g_norm, w_ffn_in, w_ffn_out, w_attn_in, w_attn_out, g_q, g_k, sink, w_hy_in, b_hy_in, w_hy_conv, b_hy_conv, hf_w1, hf_b1, hf_freq1, hf_w2, hf_b2, hf_freq2, hf_w3, hy_bias, w_hy_out, b_hy_out):
    batch, seq, d = x.shape
    n_ctx = ctx.shape[1]
    depth = w_mod.shape[0]
    rows_c = batch * n_ctx
    rope = _rope_tables(seq)
    last_ctx = max(l for l in range(depth) if l % 2 == 0)

    r_pad = -(-(batch + 1) // SUBLANES) * SUBLANES
    c_all = _pad_to(jnp.concatenate([c, c_ctx[None, :]], axis=0), (r_pad, d))
    m_all = _mod_all(c_all, w_mod, b_mod).reshape(depth, r_pad, N_MOD, d)

    w_ffn_in_b = w_ffn_in.astype(BF16)
    w_ffn_out_b = w_ffn_out.astype(BF16)

    xl = x.reshape(batch * seq, d)
    xc = ctx.reshape(rows_c, d)
    for l in range(depth):
        i = l // 2
        ctx_live = l <= last_ctx
        ctx_full = l < last_ctx
        mods_l = [m_all[l, :batch, 3 * k:3 * k + 3] for k in range(3)]
        mods_c = [m_all[l, batch:batch + 1, 3 * k:3 * k + 3] for k in range(3)]
        xl = _ffn(xl, mods_l[0], g_norm[l, 0], w_ffn_in_b, w_ffn_out_b, l, 0, seq)
        if ctx_live:
            xc = _ffn(xc, mods_c[0], g_norm[l, 0], w_ffn_in_b, w_ffn_out_b, l, 0, rows_c)
        if l % 2 == 0:
            xl, xc = _attn_layer(xl, xc, mods_l[1], mods_c[1], g_norm[l, 1], w_attn_in[i], w_attn_out[i],
                                 g_q[i], g_k[i], sink[i], rope, batch, seq, n_ctx, ctx_full)
        else:
            w_in_b = w_hy_in[i].astype(BF16)
            w_out_b = w_hy_out[i].astype(BF16)
            filt = (hf_w1[i], hf_b1[i], hf_freq1[i], hf_w2[i], hf_b2[i], hf_freq2[i], hf_w3[i])
            xl = _hyena_layer(xl, mods_l[1], g_norm[l, 1], seq, batch, w_in_b, b_hy_in[i], w_hy_conv[i],
                              b_hy_conv[i], filt, hy_bias[i], w_out_b, b_hy_out[i])
            if ctx_full:
                xc = _hyena_layer(xc, mods_c[1], g_norm[l, 1], rows_c, batch, w_in_b, b_hy_in[i], w_hy_conv[i],
                                  b_hy_conv[i], filt, hy_bias[i], w_out_b, b_hy_out[i])
        xl = _ffn(xl, mods_l[2], g_norm[l, 2], w_ffn_in_b, w_ffn_out_b, l, 1, seq)
        if ctx_full:
            xc = _ffn(xc, mods_c[2], g_norm[l, 2], w_ffn_in_b, w_ffn_out_b, l, 1, rows_c)
    return xl.reshape(batch, seq, d)
```

```python
import functools
import math

import jax
import jax.numpy as jnp
from jax import lax
from jax.experimental import pallas as pl
from jax.experimental.pallas import tpu as pltpu

HEAD_DIM = 128
N_HEADS = 8
N_KV = 2
GROUP = N_HEADS // N_KV
Q_COLS = N_HEADS * HEAD_DIM
KV_COLS = N_KV * HEAD_DIM
GRID_W = 64
WINDOW = 128
ROPE_THETA = 10000.0
N_MOD = 9
HYENA_ORDER = 2
HYENA_SHORT_K = 3
HYENA_BANDS = 16
HYENA_TARGET = 1e-2
HYENA_FAST_PCT = 0.3
HYENA_SLOW_PCT = 1.5
EPS = 1e-6
NEG_INF = -1e30

LANES = 128
SUBLANES = 8
VMEM_LIMIT_BYTES = 56 * 1024 * 1024

F32 = jnp.float32
BF16 = jnp.bfloat16
HIGHEST = lax.Precision.HIGHEST


def _params(*sem):
    return pltpu.CompilerParams(dimension_semantics=sem, vmem_limit_bytes=VMEM_LIMIT_BYTES)


def _tile(dim, pref):
    t = min(dim, pref)
    while dim % t:
        t //= 2
    return t


def _dot(a, b):
    return jnp.dot(a, b, preferred_element_type=F32)


def _dot_nt(a, b):
    return lax.dot_general(a, b, (((1,), (1,)), ((), ())), preferred_element_type=F32)


def _silu(v):
    return v * (1.0 / (1.0 + jnp.exp(-v)))


def _modulated(xf, g, shift, scale):
    ms = jnp.mean(xf * xf, axis=-1, keepdims=True)
    y = xf * lax.rsqrt(ms + EPS)
    return (y * g) * (1.0 + scale) + shift


def _mod_body(c_ref, w_ref, b_ref, o_ref):
    s = _silu(c_ref[...])
    o_ref[0] = jnp.dot(s, w_ref[0], preferred_element_type=F32, precision=HIGHEST) + b_ref[0]


def _mod_all(c_all, w_mod, b_mod):
    depth, d, nd = w_mod.shape
    r = c_all.shape[0]
    tn = _tile(nd, 1024)
    return pl.pallas_call(
        _mod_body,
        out_shape=jax.ShapeDtypeStruct((depth, r, nd), F32),
        grid=(depth, nd // tn),
        in_specs=[
            pl.BlockSpec((r, d), lambda l, j: (0, 0)),
            pl.BlockSpec((1, d, tn), lambda l, j: (l, 0, j)),
            pl.BlockSpec((1, 1, tn), lambda l, j: (l, 0, j)),
        ],
        out_specs=pl.BlockSpec((1, r, tn), lambda l, j: (l, 0, j)),
        compiler_params=_params("parallel", "parallel"),
        name="mod_vectors",
    )(c_all, w_mod, b_mod.reshape(depth, 1, nd))


def _ffn_body(x_ref, xn_ref, mod_ref, modn_ref, g_ref, wgu_ref, wo_ref, o_ref, ha_ref, hb_ref, acc_ref, *, chunk):
    i = pl.program_id(0)
    j = pl.program_id(1)
    tm = x_ref.shape[0]

    @pl.when(jnp.logical_and(i == 0, j == 0))
    def _():
        h = _modulated(x_ref[...], g_ref[...], mod_ref[0, 0:1, :], mod_ref[0, 1:2, :])
        ha_ref[...] = h.astype(BF16)
        acc_ref[...] = jnp.zeros_like(acc_ref)

    def step(h_cur, h_next):
        start = pl.multiple_of(jnp.minimum(j * chunk, tm - chunk), 2 * SUBLANES)
        rows = pl.ds(start, chunk)
        hn = _modulated(xn_ref[rows, :], g_ref[...], modn_ref[0, 0:1, :], modn_ref[0, 1:2, :])
        h_next[rows, :] = hn.astype(BF16)

        gu = _dot(h_cur[...], wgu_ref[...])
        tf = gu.shape[1] // 2
        act = (_silu(gu[:, :tf]) * gu[:, tf:]).astype(BF16)
        prev = jnp.where(j == 0, 0.0, acc_ref[...])
        acc_ref[...] = prev + _dot(act, wo_ref[...])

    @pl.when(i % 2 == 0)
    def _():
        step(ha_ref, hb_ref)

    @pl.when(i % 2 == 1)
    def _():
        step(hb_ref, ha_ref)

    @pl.when(j == pl.num_programs(1) - 1)
    def _():
        o_ref[...] = x_ref[...] + (0.5 * mod_ref[0, 2:3, :]) * acc_ref[...]


def _ffn_tile(f):
    return _tile(f, 512)


def _ffn_weights(w_in, w_out):
    depth, two, d, f2 = w_in.shape
    tf = _ffn_tile(f2 // 2)
    w = w_in.reshape(depth, two, d, 2, f2 // 2 // tf, tf).transpose(0, 1, 4, 2, 3, 5)
    return w.reshape(depth, two, f2 // 2 // tf, d, 2 * tf).astype(BF16), w_out.astype(BF16)


def _ffn(x, mod, g, w_in, w_out, layer, half, rows_per_mod):
    rows, d = x.shape
    nf, tf = w_in.shape[2], w_in.shape[4] // 2
    tm = _tile(min(rows, rows_per_mod), 512)
    ni = rows // tm
    pack = 2 * SUBLANES
    chunk = min(tm, pl.cdiv(pl.cdiv(tm, nf), pack) * pack)

    def nxt(i):
        return jnp.minimum(i + 1, ni - 1)

    return pl.pallas_call(
        functools.partial(_ffn_body, chunk=chunk),
        out_shape=jax.ShapeDtypeStruct((rows, d), F32),
        grid=(ni, nf),
        in_specs=[
            pl.BlockSpec((tm, d), lambda i, j: (i, 0)),
            pl.BlockSpec((tm, d), lambda i, j: (nxt(i), 0)),
            pl.BlockSpec((1, 3, d), lambda i, j: (i * tm // rows_per_mod, 0, 0)),
            pl.BlockSpec((1, 3, d), lambda i, j: (nxt(i) * tm // rows_per_mod, 0, 0)),
            pl.BlockSpec((1, d), lambda i, j: (0, 0)),
            pl.BlockSpec((None, None, None, d, 2 * tf), lambda i, j: (layer, half, j, 0, 0)),
            pl.BlockSpec((None, None, tf, d), lambda i, j: (layer, half, j, 0)),
        ],
        out_specs=pl.BlockSpec((tm, d), lambda i, j: (i, 0)),
        scratch_shapes=[pltpu.VMEM((tm, d), BF16), pltpu.VMEM((tm, d), BF16), pltpu.VMEM((tm, d), F32)],
        compiler_params=_params("arbitrary", "arbitrary"),
        name="ffn_half_step",
    )(x, x, mod, mod, g.reshape(1, d), w_in, w_out)


def _proj_body(x_ref, mod_ref, g_ref, w_ref, b_ref, o_ref, h_ref):
    @pl.when(pl.program_id(1) == 0)
    def _():
        h = _modulated(x_ref[...], g_ref[...], mod_ref[0, 0:1, :], mod_ref[0, 1:2, :])
        h_ref[...] = h.astype(BF16)

    o_ref[...] = (_dot(h_ref[...], w_ref[...]) + b_ref[...]).astype(o_ref.dtype)


def _qkv_body(x_ref, mod_ref, g_ref, w_ref, gamma_ref, qs_ref, isv_ref, cos_ref, sin_ref, o_ref, h_ref, p_ref):
    i = pl.program_id(0)
    j = pl.program_id(1)

    @pl.when(j == 0)
    def _():
        h = _modulated(x_ref[...], g_ref[...], mod_ref[0, 0:1, :], mod_ref[0, 1:2, :])
        h_ref[...] = h.astype(BF16)

    @pl.when(jnp.logical_and(i == 0, j == 0))
    def _():
        p_ref[...] = jnp.zeros_like(p_ref)

    cos = cos_ref[...]
    sin = sin_ref[...]
    even = (lax.broadcasted_iota(jnp.int32, (1, HEAD_DIM), 1) % 2) == 0
    for hh in range(p_ref.shape[1] // HEAD_DIM):
        sl = slice(hh * HEAD_DIM, (hh + 1) * HEAD_DIM)
        ph = p_ref[:, sl]
        ms = jnp.mean(ph * ph, axis=-1, keepdims=True)
        nh = (ph * lax.rsqrt(ms + EPS)) * gamma_ref[:, sl]
        partner = jnp.where(even, pltpu.roll(nh, HEAD_DIM - 1, 1), pltpu.roll(nh, 1, 1))
        r = (nh * cos + partner * sin) * qs_ref[:, sl]
        o_ref[:, sl] = jnp.where(isv_ref[:, sl] > 0.0, ph, r).astype(o_ref.dtype)
    p_ref[...] = _dot(h_ref[...], w_ref[...])


def _row_tile(rows, rows_per_mod):
    return _tile(min(rows, rows_per_mod), 1024)


def _col_tiled(w, tn):
    k, n = w.shape
    return w.reshape(k, n // tn, tn).transpose(1, 0, 2).astype(BF16)


def _proj(x, mod, g, w_t, b, rows_per_mod):
    rows, d = x.shape
    nj, _, tn = w_t.shape
    n = nj * tn
    tm = _row_tile(rows, rows_per_mod)
    return pl.pallas_call(
        _proj_body,
        out_shape=jax.ShapeDtypeStruct((rows, n), BF16),
        grid=(rows // tm, n // tn),
        in_specs=[
            pl.BlockSpec((tm, d), lambda i, j: (i, 0)),
            pl.BlockSpec((1, 3, d), lambda i, j: (i * tm // rows_per_mod, 0, 0)),
            pl.BlockSpec((1, d), lambda i, j: (0, 0)),
            pl.BlockSpec((None, d, tn), lambda i, j: (j, 0, 0)),
            pl.BlockSpec((1, tn), lambda i, j: (0, j)),
        ],
        out_specs=pl.BlockSpec((tm, tn), lambda i, j: (i, j)),
        scratch_shapes=[pltpu.VMEM((tm, d), BF16)],
        compiler_params=_params("parallel", "arbitrary"),
        name="modulate_project",
    )(x, mod, g.reshape(1, d), w_t, b.reshape(1, n))


def _qkv_proj(x, mod, g, w_t, rows_per_mod, tables, rope, seq):
    rows, d = x.shape
    nj, _, tn = w_t.shape
    n = nj * tn
    tm = _row_tile(rows, rows_per_mod)
    gamma, qs, isv = tables
    cos, sin = rope

    def prev(j):
        return jnp.maximum(j - 1, 0)

    col = pl.BlockSpec((1, tn), lambda i, j: (0, prev(j)))
    if seq is None:
        rope_spec = pl.BlockSpec((tm, HEAD_DIM), lambda i, j: (0, 0))
    else:
        rope_spec = pl.BlockSpec((tm, HEAD_DIM), lambda i, j: (i % (seq // tm), 0))
    return pl.pallas_call(
        _qkv_body,
        out_shape=jax.ShapeDtypeStruct((rows, n), BF16),
        grid=(rows // tm, nj + 1),
        in_specs=[
            pl.BlockSpec((tm, d), lambda i, j: (i, 0)),
            pl.BlockSpec((1, 3, d), lambda i, j: (i * tm // rows_per_mod, 0, 0)),
            pl.BlockSpec((1, d), lambda i, j: (0, 0)),
            pl.BlockSpec((None, d, tn), lambda i, j: (jnp.minimum(j, nj - 1), 0, 0)),
            col, col, col, rope_spec, rope_spec,
        ],
        out_specs=pl.BlockSpec((tm, tn), lambda i, j: (i, prev(j))),
        scratch_shapes=[pltpu.VMEM((tm, d), BF16), pltpu.VMEM((tm, tn), F32)],
        compiler_params=_params("arbitrary", "arbitrary"),
        name="qkv_project",
    )(x, mod, g.reshape(1, d), w_t, gamma, qs, isv, cos, sin)


def _attn_body(*refs, has_lat, has_mask, has_sink, tq):
    sink_ref, q_ref, kc_ref, vc_ref = refs[:4]
    if has_lat:
        k_ref, v_ref, o_ref = refs[4:7]
    else:
        o_ref = refs[4]
    g = pl.program_id(1)
    qi = pl.program_id(2)
    if has_lat:
        l_rows = k_ref.shape[0]
        if has_mask:
            kw = min(l_rows, tq + 2 * WINDOW)
            k0 = pl.multiple_of(jnp.clip(qi * tq - WINDOW, 0, l_rows - kw), WINDOW)
            keys = pl.ds(k0, kw)
        else:
            k0, keys = 0, slice(None)
    for hh in range(GROUP):
        sl = slice(hh * HEAD_DIM, (hh + 1) * HEAD_DIM)
        q = q_ref[:, sl]
        s1 = _dot_nt(q, kc_ref[...])
        m = jnp.max(s1, axis=-1, keepdims=True)
        if has_lat:
            s2 = _dot_nt(q, k_ref[keys, :])
            if has_mask:
                qpos = qi * tq + lax.broadcasted_iota(jnp.int32, (tq, 1), 0)
                kpos = k0 + lax.broadcasted_iota(jnp.int32, (1, s2.shape[1]), 1)
                s2 = jnp.where(jnp.abs(kpos - qpos) <= WINDOW, s2, NEG_INF)
            m = jnp.maximum(m, jnp.max(s2, axis=-1, keepdims=True))
        if has_sink:
            sk = sink_ref[g * GROUP + hh]
            m = jnp.maximum(m, sk)
        p1 = jnp.exp(s1 - m)
        l = jnp.sum(p1, axis=-1, keepdims=True)
        o = _dot(p1.astype(BF16), vc_ref[...])
        if has_lat:
            p2 = jnp.exp(s2 - m)
            l = l + jnp.sum(p2, axis=-1, keepdims=True)
            o = o + _dot(p2.astype(BF16), v_ref[keys, :])
        if has_sink:
            l = l + jnp.exp(sk - m)
        o_ref[:, sl] = (o / l).astype(o_ref.dtype)


def _attention(sink, q_src, ctx_src, lat_src, *, batch, q_rows, q_col, k_col, v_col, has_mask, has_sink):
    tq = _tile(q_rows, 256)
    assert not has_mask or tq % WINDOW == 0
    c_rows = ctx_src.shape[0] // batch
    gw = GROUP * HEAD_DIM
    in_specs = [
        pl.BlockSpec(memory_space=pltpu.SMEM),
        pl.BlockSpec((tq, gw), lambda b, g, i: (b * (q_rows // tq) + i, q_col + g)),
        pl.BlockSpec((c_rows, HEAD_DIM), lambda b, g, i: (b, k_col + g)),
        pl.BlockSpec((c_rows, HEAD_DIM), lambda b, g, i: (b, v_col + g)),
    ]
    args = [sink, q_src, ctx_src, ctx_src]
    has_lat = lat_src is not None
    if has_lat:
        l_rows = lat_src.shape[0] // batch
        in_specs += [
            pl.BlockSpec((l_rows, HEAD_DIM), lambda b, g, i: (b, k_col + g)),
            pl.BlockSpec((l_rows, HEAD_DIM), lambda b, g, i: (b, v_col + g)),
        ]
        args += [lat_src, lat_src]
    return pl.pallas_call(
        functools.partial(_attn_body, has_lat=has_lat, has_mask=has_mask, has_sink=has_sink, tq=tq),
        out_shape=jax.ShapeDtypeStruct((batch * q_rows, Q_COLS), BF16),
        grid=(batch, N_KV, q_rows // tq),
        in_specs=in_specs,
        out_specs=pl.BlockSpec((tq, gw), lambda b, g, i: (b * (q_rows // tq) + i, g)),
        compiler_params=_params("parallel", "parallel", "parallel"),
        name="gqa_attention",
    )(*args)


def _outproj_body(*refs, n_in):
    a_refs, w_refs = refs[:n_in], refs[n_in:2 * n_in]
    b_ref, x_ref, mod_ref, o_ref = refs[2 * n_in:]
    y = b_ref[...] + _dot(a_refs[0][...], w_refs[0][...])
    for a_ref, w_ref in zip(a_refs[1:], w_refs[1:]):
        y = y + _dot(a_ref[...], w_ref[...])
    o_ref[...] = x_ref[...] + mod_ref[0, 2:3, :] * y


def _outproj(acts, weight, bias, x, mod, rows_per_mod):
    rows, d = x.shape
    tm = _tile(min(rows, rows_per_mod), 512)
    n_in = len(acts)
    k = acts[0].shape[1]
    assert all(a.shape[1] == k for a in acts) and weight.shape[0] == n_in * k
    in_specs = [pl.BlockSpec((tm, k), lambda i: (i, 0)) for _ in acts]
    in_specs += [pl.BlockSpec((k, d), lambda i, n=n: (n, 0)) for n in range(n_in)]
    in_specs += [
        pl.BlockSpec((1, d), lambda i: (0, 0)),
        pl.BlockSpec((tm, d), lambda i: (i, 0)),
        pl.BlockSpec((1, 3, d), lambda i: (i * tm // rows_per_mod, 0, 0)),
    ]
    return pl.pallas_call(
        functools.partial(_outproj_body, n_in=n_in),
        out_shape=jax.ShapeDtypeStruct((rows, d), F32),
        grid=(rows // tm,),
        in_specs=in_specs,
        out_specs=pl.BlockSpec((tm, d), lambda i: (i, 0)),
        compiler_params=_params("parallel"),
        name="outproj_residual",
    )(*acts, *([weight] * n_in), bias.reshape(1, d), x, mod)


def _filter_body(feats_ref, w1_ref, b1_ref, fr1_ref, w2_ref, b2_ref, fr2_ref, w3f_ref, w3b_ref, dl_ref, o_ref):
    feats = feats_ref[...]
    h1 = jnp.sin(fr1_ref[...] * (jnp.dot(feats, w1_ref[...], preferred_element_type=F32, precision=HIGHEST)
                                 + b1_ref[...]))
    h2 = jnp.sin(fr2_ref[...] * (jnp.dot(h1, w2_ref[...], preferred_element_type=F32, precision=HIGHEST)
                                 + b2_ref[...]))
    decay = jnp.exp(-feats[:, 0:1] * dl_ref[...])
    fwd = jnp.dot(h2, w3f_ref[...], preferred_element_type=F32, precision=HIGHEST) * decay
    bwd = jnp.dot(h2, w3b_ref[...], preferred_element_type=F32, precision=HIGHEST) * decay
    row = lax.broadcasted_iota(jnp.int32, (fwd.shape[0], 1), 0)
    bwd = jnp.where(row == 0, 0.0, bwd)
    nrm = lax.rsqrt(jnp.sum(fwd * fwd + bwd * bwd, axis=0, keepdims=True) + EPS)
    o_ref[0] = ((fwd + bwd) * nrm).astype(o_ref.dtype)
    o_ref[1] = ((fwd - bwd) * nrm).astype(o_ref.dtype)


def _hyena_filter_taps(feats, w1, b1, fr1, w2, b2, fr2, w3, absdelta, d):
    n, fp = feats.shape
    hp = w1.shape[1]
    tn = _tile(d, 512)
    nd = d // tn
    vec = pl.BlockSpec((1, hp), lambda o, j: (0, 0))
    return pl.pallas_call(
        _filter_body,
        out_shape=jax.ShapeDtypeStruct((2, n, HYENA_ORDER * d), BF16),
        grid=(HYENA_ORDER, nd),
        in_specs=[
            pl.BlockSpec((n, fp), lambda o, j: (0, 0)),
            pl.BlockSpec((fp, hp), lambda o, j: (0, 0)), vec, vec,
            pl.BlockSpec((hp, hp), lambda o, j: (0, 0)), vec, vec,
            pl.BlockSpec((hp, tn), lambda o, j: (0, (2 * o) * nd + j)),
            pl.BlockSpec((hp, tn), lambda o, j: (0, (2 * o + 1) * nd + j)),
            pl.BlockSpec((1, tn), lambda o, j: (0, j)),
        ],
        out_specs=pl.BlockSpec((2, n, tn), lambda o, j: (0, 0, o * nd + j)),
        compiler_params=_params("parallel", "parallel"),
        name="hyena_filter_taps",
    )(feats, w1, b1, fr1, w2, b2, fr2, w3, w3, absdelta)


def _spectrum_body(a_ref, kk_ref, ks_ref, o_ref, *, inv_n):
    part = pl.program_id(0)
    i = pl.program_id(1)
    tk = a_ref.shape[0]
    z = _dot(a_ref[...], kk_ref[0])
    row = i * tk + lax.broadcasted_iota(jnp.int32, (tk, 1), 0)
    o_ref[0] = z * jnp.where(row == 0, inv_n, 2.0 * inv_n)

    @pl.when(jnp.logical_and(part == 1, i == 0))
    def _():
        nyq = _dot(a_ref[0:2 * SUBLANES, :], ks_ref[0])
        o_ref[0, 0:1, :] = nyq[0:1, :] * inv_n


def _hyena_spectrum(dft, taps):
    n = taps.shape[1]
    cols = taps.shape[2]
    tk = _tile(n, 1024)
    tn = _tile(cols, 512)
    nr = n // tk
    return pl.pallas_call(
        functools.partial(_spectrum_body, inv_n=1.0 / (2 * n)),
        out_shape=jax.ShapeDtypeStruct((2, n, cols), F32),
        grid=(2, nr, cols // tn),
        in_specs=[
            pl.BlockSpec((tk, n), lambda p, i, j: (p * nr + i, 0)),
            pl.BlockSpec((1, n, tn), lambda p, i, j: (p, 0, j)),
            pl.BlockSpec((1, n, tn), lambda p, i, j: (0, 0, j)),
        ],
        out_specs=pl.BlockSpec((1, tk, tn), lambda p, i, j: (p, i, j)),
        compiler_params=_params("parallel", "parallel", "parallel"),
        name="hyena_filter_spectrum",
    )(dft, taps, taps)


def _shortconv_body(p_ref, w_ref, b_ref, o_ref):
    x = p_ref[...].astype(F32)
    n = x.shape[0]
    row = lax.broadcasted_iota(jnp.int32, (n, 1), 0)
    prev = jnp.where(row == 0, 0.0, pltpu.roll(x, 1, 0))
    nxt = jnp.where(row == n - 1, 0.0, pltpu.roll(x, n - 1, 0))
    y = ((b_ref[...] + prev * w_ref[0:1, :]) + x * w_ref[1:2, :]) + nxt * w_ref[2:3, :]
    o_ref[...] = y.astype(o_ref.dtype)


def _shortconv(p, w, b, batch):
    rows, n3 = p.shape
    n = rows // batch
    tn = _tile(n3, 512)
    return pl.pallas_call(
        _shortconv_body,
        out_shape=jax.ShapeDtypeStruct((rows, n3), BF16),
        grid=(batch, n3 // tn),
        in_specs=[
            pl.BlockSpec((n, tn), lambda b_, j: (b_, j)),
            pl.BlockSpec((HYENA_SHORT_K, tn), lambda b_, j: (0, j)),
            pl.BlockSpec((1, tn), lambda b_, j: (0, j)),
        ],
        out_specs=pl.BlockSpec((n, tn), lambda b_, j: (b_, j)),
        compiler_params=_params("parallel", "parallel"),
        name="hyena_short_conv",
    )(p, w, b.reshape(1, n3))


def _dft_fwd_body(are_ref, aim_ref, z_ref, kr_ref, ki_ref, yr_ref, yi_ref):
    i = pl.program_id(1)
    tk = are_ref.shape[0]
    z = z_ref[...]
    zr = _dot(are_ref[...], z)
    zi = _dot(aim_ref[...], z)
    kr = kr_ref[0]
    ki = ki_ref[0]
    row0 = (i * tk + lax.broadcasted_iota(jnp.int32, (tk, 1), 0)) == 0
    zi_ki = zi * ki
    yr_ref[...] = (zr * kr - jnp.where(row0, 0.0, zi_ki)).astype(yr_ref.dtype)
    yi_ref[...] = jnp.where(row0, zi_ki, zr * ki + zi * kr).astype(yi_ref.dtype)


def _dft_fwd(dft, z_src, z_col, spec, order, batch, d):
    n = z_src.shape[0] // batch
    tk = _tile(n, 1024)
    tn = _tile(d, 512)
    nr, nd = n // tk, d // tn
    out = jax.ShapeDtypeStruct((batch * n, d), BF16)
    return pl.pallas_call(
        _dft_fwd_body,
        out_shape=(out, out),
        grid=(nd, nr, batch),
        in_specs=[
            pl.BlockSpec((tk, n), lambda j, i, b: (i, 0)),
            pl.BlockSpec((tk, n), lambda j, i, b: (nr + i, 0)),
            pl.BlockSpec((n, tn), lambda j, i, b: (b, z_col * nd + j)),
            pl.BlockSpec((1, tk, tn), lambda j, i, b: (0, i, order * nd + j)),
            pl.BlockSpec((1, tk, tn), lambda j, i, b: (1, i, order * nd + j)),
        ],
        out_specs=(
            pl.BlockSpec((tk, tn), lambda j, i, b: (b * nr + i, j)),
            pl.BlockSpec((tk, tn), lambda j, i, b: (b * nr + i, j)),
        ),
        compiler_params=_params("parallel", "parallel", "parallel"),
        name="hyena_dft_forward",
    )(dft, dft, z_src, spec, spec)


def _dft_inv_body(atre_ref, atim_ref, yr_ref, yi_ref, g_ref, zp_ref, bias_ref, o_ref):
    y = _dot(atre_ref[...], yr_ref[...]) + _dot(atim_ref[...], yi_ref[...])
    zp = zp_ref[...].astype(F32)
    o_ref[...] = (g_ref[...].astype(F32) * (y + zp * bias_ref[...])).astype(o_ref.dtype)


def _dft_inv(dft_t, yr, yi, gate_src, gate_col, z_src, z_col, bias, batch, d):
    n = yr.shape[0] // batch
    tt = _tile(n, 1024)
    tn = _tile(d, 512)
    nr, nd = n // tt, d // tn
    return pl.pallas_call(
        _dft_inv_body,
        out_shape=jax.ShapeDtypeStruct((batch * n, d), BF16),
        grid=(nd, nr, batch),
        in_specs=[
            pl.BlockSpec((tt, n), lambda j, i, b: (i, 0)),
            pl.BlockSpec((tt, n), lambda j, i, b: (i, 1)),
            pl.BlockSpec((n, tn), lambda j, i, b: (b, j)),
            pl.BlockSpec((n, tn), lambda j, i, b: (b, j)),
            pl.BlockSpec((tt, tn), lambda j, i, b: (b * nr + i, gate_col * nd + j)),
            pl.BlockSpec((tt, tn), lambda j, i, b: (b * nr + i, z_col * nd + j)),
            pl.BlockSpec((1, tn), lambda j, i, b: (0, j)),
        ],
        out_specs=pl.BlockSpec((tt, tn), lambda j, i, b: (b * nr + i, j)),
        compiler_params=_params("parallel", "parallel", "parallel"),
        name="hyena_dft_inverse",
    )(dft_t, dft_t, yr, yi, gate_src, z_src, bias.reshape(1, d))


def _rope_tables(n_tok):
    rows = n_tok // GRID_W
    r = jnp.repeat(jnp.arange(rows), GRID_W).astype(F32)
    col = jnp.tile(jnp.arange(GRID_W), rows).astype(F32)
    half = HEAD_DIM // 2
    inv = ROPE_THETA ** (-jnp.arange(0, half, 2, dtype=F32) / half)
    ang = jnp.concatenate([r[:, None] * inv, col[:, None] * inv], axis=-1)
    cos = jnp.repeat(jnp.cos(ang), 2, axis=-1)
    sin = jnp.repeat(jnp.sin(ang), 2, axis=-1)
    sign = jnp.where(jnp.arange(HEAD_DIM) % 2 == 0, -1.0, 1.0).astype(F32)
    return cos, sin * sign


def _dft_tables(n):
    big = 2 * n
    k = jnp.arange(n, dtype=jnp.int32)[:, None]
    t = jnp.arange(n, dtype=jnp.int32)[None, :]
    ang = ((k * t) % big).astype(F32) * (2.0 * math.pi / big)
    c = jnp.cos(ang)
    s = -jnp.sin(ang)
    nyq = jnp.where(t % 2 == 0, 1.0, -1.0).astype(F32)
    s = jnp.where(k == 0, nyq, s)
    a = jnp.concatenate([c, s], axis=0).astype(BF16)
    return a, a.T


def _filter_features(n, width):
    t = jnp.linspace(0.0, 1.0, n, dtype=F32)[:, None]
    w = (2.0 * math.pi / n) * jnp.arange(n, dtype=F32)[:, None]
    bands = jnp.linspace(1e-4, HYENA_BANDS - 1, HYENA_BANDS, dtype=F32)
    ang = w * bands[None, :]
    feats = jnp.concatenate([t, jnp.cos(ang), -jnp.sin(ang)], axis=-1)
    return jnp.pad(feats, ((0, 0), (0, width - feats.shape[1])))


def _pad_to(a, shape):
    return jnp.pad(a, [(0, s - d) for d, s in zip(a.shape, shape)])


def _attn_layer(x, xc, mod_l, mod_c, g, w_in, w_out, g_q, g_k, sink, rope, batch, seq, n_ctx, ctx_out):
    d = x.shape[1]
    ones = jnp.ones((HEAD_DIM,), F32)
    gamma = jnp.concatenate([jnp.tile(g_q[0], N_HEADS), jnp.tile(g_q[1], N_HEADS),
                             jnp.tile(g_k[0], N_KV), jnp.tile(ones, N_KV),
                             jnp.tile(g_k[1], N_KV), jnp.tile(ones, N_KV)])[None, :]
    qs = jnp.concatenate([jnp.full((2 * Q_COLS,), HEAD_DIM ** -0.5, F32), jnp.ones((4 * KV_COLS,), F32)])[None, :]
    zk, ok = jnp.zeros((KV_COLS,), F32), jnp.ones((KV_COLS,), F32)
    isv = jnp.concatenate([jnp.zeros((2 * Q_COLS,), F32), zk, ok, zk, ok])[None, :]
    tables = (gamma, qs, isv)
    tm = _row_tile(batch * n_ctx, batch * n_ctx)
    ident = (jnp.ones((tm, HEAD_DIM), F32), jnp.zeros((tm, HEAD_DIM), F32))
    w_in_b = _col_tiled(w_in, _tile(w_in.shape[1], 512))
    qkv = _qkv_proj(x, mod_l, g, w_in_b, seq, tables, rope, seq)
    qkv_c = _qkv_proj(xc, mod_c, g, w_in_b, batch * n_ctx, tables, ident, None)
    qa_col, qb_col = 0, N_KV
    ka_col = 2 * Q_COLS // HEAD_DIM
    va_col, kb_col, vb_col = ka_col + N_KV, ka_col + 2 * N_KV, ka_col + 3 * N_KV
    o_a = _attention(sink, qkv, qkv_c, qkv, batch=batch, q_rows=seq, q_col=qa_col, k_col=ka_col, v_col=va_col,
                     has_mask=False, has_sink=False)
    o_b = _attention(sink, qkv, qkv_c, qkv, batch=batch, q_rows=seq, q_col=qb_col, k_col=kb_col, v_col=vb_col,
                     has_mask=True, has_sink=True)
    w_out_b = w_out.astype(BF16)
    zero_b = jnp.zeros((d,), F32)
    x = _outproj((o_a, o_b), w_out_b, zero_b, x, mod_l, seq)
    if ctx_out:
        co_a = _attention(sink, qkv_c, qkv_c, None, batch=batch, q_rows=n_ctx, q_col=qa_col, k_col=ka_col,
                          v_col=va_col, has_mask=False, has_sink=False)
        co_b = _attention(sink, qkv_c, qkv_c, None, batch=batch, q_rows=n_ctx, q_col=qb_col, k_col=kb_col,
                          v_col=vb_col, has_mask=False, has_sink=True)
        xc = _outproj((co_a, co_b), w_out_b, zero_b, xc, mod_c, batch * n_ctx)
    return x, xc


def _hyena_layer(x, mod, g, rows_per_mod, batch, w_in_b, b_in, w_conv, b_conv, filt, hy_bias, w_out_b, b_out):
    d = x.shape[1]
    n = x.shape[0] // batch
    w1, b1, fr1, w2, b2, fr2, w3 = filt
    hp = LANES
    feats = _filter_features(n, LANES)
    max_decay = math.log(HYENA_TARGET) / HYENA_FAST_PCT
    min_decay = math.log(HYENA_TARGET) / HYENA_SLOW_PCT
    absdelta = jnp.abs(jnp.linspace(min_decay, max_decay, d, dtype=F32))[None, :]
    taps = _hyena_filter_taps(
        feats, _pad_to(w1, (LANES, hp)), _pad_to(b1[None, :], (1, hp)), _pad_to(fr1[None, :], (1, hp)),
        _pad_to(w2, (hp, hp)), _pad_to(b2[None, :], (1, hp)), _pad_to(fr2[None, :], (1, hp)),
        _pad_to(w3, (hp, w3.shape[1])), absdelta, d)
    dft, dft_t = _dft_tables(n)
    spec = _hyena_spectrum(dft, taps)
    p = _proj(x, mod, g, w_in_b, b_in, rows_per_mod)
    pc = _shortconv(p, w_conv, b_conv, batch)
    yr, yi = _dft_fwd(dft, pc, 0, spec, 0, batch, d)
    z = _dft_inv(dft_t, yr, yi, pc, 1, pc, 0, hy_bias[0], batch, d)
    yr, yi = _dft_fwd(dft, z, 0, spec, 1, batch, d)
    z = _dft_inv(dft_t, yr, yi, pc, 2, z, 0, hy_bias[1], batch, d)
    return _outproj((z,), w_out_b, b_out, x, mod, rows_per_mod)


def kernel(x, c, ctx, c_ctx, w_mod, b_mod, g_norm, w_ffn_in, w_ffn_out, w_attn_in, w_attn_out, g_q, g_k, sink, w_hy_in, b_hy_in, w_hy_conv, b_hy_conv, hf_w1, hf_b1, hf_freq1, hf_w2, hf_b2, hf_freq2, hf_w3, hy_bias, w_hy_out, b_hy_out):
    batch, seq, d = x.shape
    n_ctx = ctx.shape[1]
    depth = w_mod.shape[0]
    rows_c = batch * n_ctx
    rope = _rope_tables(seq)
    last_ctx = max(l for l in range(depth) if l % 2 == 0)

    r_pad = -(-(batch + 1) // SUBLANES) * SUBLANES
    c_all = _pad_to(jnp.concatenate([c, c_ctx[None, :]], axis=0), (r_pad, d))
    m_all = _mod_all(c_all, w_mod, b_mod).reshape(depth, r_pad, N_MOD, d)

    w_ffn_in_b, w_ffn_out_b = _ffn_weights(w_ffn_in, w_ffn_out)

    xl = x.reshape(batch * seq, d)
    xc = ctx.reshape(rows_c, d)
    for l in range(depth):
        i = l // 2
        ctx_live = l <= last_ctx
        ctx_full = l < last_ctx
        mods_l = [m_all[l, :batch, 3 * k:3 * k + 3] for k in range(3)]
        mods_c = [m_all[l, batch:batch + 1, 3 * k:3 * k + 3] for k in range(3)]
        xl = _ffn(xl, mods_l[0], g_norm[l, 0], w_ffn_in_b, w_ffn_out_b, l, 0, seq)
        if ctx_live:
            xc = _ffn(xc, mods_c[0], g_norm[l, 0], w_ffn_in_b, w_ffn_out_b, l, 0, rows_c)
        if l % 2 == 0:
            xl, xc = _attn_layer(xl, xc, mods_l[1], mods_c[1], g_norm[l, 1], w_attn_in[i], w_attn_out[i],
                                 g_q[i], g_k[i], sink[i], rope, batch, seq, n_ctx, ctx_full)
        else:
            w_in_b = _col_tiled(w_hy_in[i], _tile(w_hy_in.shape[2], 512))
            w_out_b = w_hy_out[i].astype(BF16)
            filt = (hf_w1[i], hf_b1[i], hf_freq1[i], hf_w2[i], hf_b2[i], hf_freq2[i], hf_w3[i])
            xl = _hyena_layer(xl, mods_l[1], g_norm[l, 1], seq, batch, w_in_b, b_hy_in[i], w_hy_conv[i],
                              b_hy_conv[i], filt, hy_bias[i], w_out_b, b_hy_out[i])
            if ctx_full:
                xc = _hyena_layer(xc, mods_c[1], g_norm[l, 1], rows_c, batch, w_in_b, b_hy_in[i], w_hy_conv[i],
                                  b_hy_conv[i], filt, hy_bias[i], w_out_b, b_hy_out[i])
        xl = _ffn(xl, mods_l[2], g_norm[l, 2], w_ffn_in_b, w_ffn_out_b, l, 1, seq)
        if ctx_full:
            xc = _ffn(xc, mods_c[2], g_norm[l, 2], w_ffn_in_b, w_ffn_out_b, l, 1, rows_c)
    return xl.reshape(batch, seq, d)
```

```python
import functools
import math

import jax
import jax.numpy as jnp
from jax import lax
from jax.experimental import pallas as pl
from jax.experimental.pallas import tpu as pltpu

HEAD_DIM = 128
N_HEADS = 8
N_KV = 2
GROUP = N_HEADS // N_KV
Q_COLS = N_HEADS * HEAD_DIM
KV_COLS = N_KV * HEAD_DIM
GRID_W = 64
WINDOW = 128
ROPE_THETA = 10000.0
N_MOD = 9
HYENA_ORDER = 2
HYENA_SHORT_K = 3
HYENA_BANDS = 16
HYENA_TARGET = 1e-2
HYENA_FAST_PCT = 0.3
HYENA_SLOW_PCT = 1.5
EPS = 1e-6
NEG_INF = -1e30

LANES = 128
SUBLANES = 8
VMEM_LIMIT_BYTES = 56 * 1024 * 1024

F32 = jnp.float32
BF16 = jnp.bfloat16
HIGHEST = lax.Precision.HIGHEST


def _params(*sem):
    return pltpu.CompilerParams(dimension_semantics=sem, vmem_limit_bytes=VMEM_LIMIT_BYTES)


def _tile(dim, pref):
    t = min(dim, pref)
    while dim % t:
        t //= 2
    return t


def _dot(a, b):
    return jnp.dot(a, b, preferred_element_type=F32)


def _dot_nt(a, b):
    return lax.dot_general(a, b, (((1,), (1,)), ((), ())), preferred_element_type=F32)


def _silu(v):
    return v * (1.0 / (1.0 + jnp.exp(-v)))


def _modulated(xf, g, shift, scale):
    ms = jnp.mean(xf * xf, axis=-1, keepdims=True)
    y = xf * lax.rsqrt(ms + EPS)
    return (y * g) * (1.0 + scale) + shift


def _mod_body(c_ref, w_ref, b_ref, o_ref):
    s = _silu(c_ref[...])
    o_ref[0] = jnp.dot(s, w_ref[0], preferred_element_type=F32, precision=HIGHEST) + b_ref[0]


def _mod_all(c_all, w_mod, b_mod):
    depth, d, nd = w_mod.shape
    r = c_all.shape[0]
    tn = _tile(nd, 1024)
    return pl.pallas_call(
        _mod_body,
        out_shape=jax.ShapeDtypeStruct((depth, r, nd), F32),
        grid=(depth, nd // tn),
        in_specs=[
            pl.BlockSpec((r, d), lambda l, j: (0, 0)),
            pl.BlockSpec((1, d, tn), lambda l, j: (l, 0, j)),
            pl.BlockSpec((1, 1, tn), lambda l, j: (l, 0, j)),
        ],
        out_specs=pl.BlockSpec((1, r, tn), lambda l, j: (l, 0, j)),
        compiler_params=_params("parallel", "parallel"),
        name="mod_vectors",
    )(c_all, w_mod, b_mod.reshape(depth, 1, nd))


def _ffn_body(x_ref, mod_ref, g_ref, wg_ref, wu_ref, wo_ref, o_ref, h_ref, acc_ref):
    j = pl.program_id(1)

    @pl.when(j == 0)
    def _():
        h = _modulated(x_ref[...], g_ref[...], mod_ref[0, 0:1, :], mod_ref[0, 1:2, :])
        h_ref[...] = h.astype(BF16)
        acc_ref[...] = jnp.zeros_like(acc_ref)

    h = h_ref[...]
    a = _dot(h, wg_ref[...])
    u = _dot(h, wu_ref[...])
    act = (_silu(a) * u).astype(BF16)
    acc_ref[...] += _dot(act, wo_ref[...])

    @pl.when(j == pl.num_programs(1) - 1)
    def _():
        o_ref[...] = x_ref[...] + (0.5 * mod_ref[0, 2:3, :]) * acc_ref[...]


def _ffn(x, mod, g, w_in, w_out, layer, half, rows_per_mod):
    rows, d = x.shape
    f = w_out.shape[2]
    tm = _tile(min(rows, rows_per_mod), 512)
    tf = _tile(f, 512)
    nf = f // tf
    return pl.pallas_call(
        _ffn_body,
        out_shape=jax.ShapeDtypeStruct((rows, d), F32),
        grid=(rows // tm, nf),
        in_specs=[
            pl.BlockSpec((tm, d), lambda i, j: (i, 0)),
            pl.BlockSpec((1, 3, d), lambda i, j: (i * tm // rows_per_mod, 0, 0)),
            pl.BlockSpec((1, d), lambda i, j: (0, 0)),
            pl.BlockSpec((None, None, d, tf), lambda i, j: (layer, half, 0, j)),
            pl.BlockSpec((None, None, d, tf), lambda i, j: (layer, half, 0, nf + j)),
            pl.BlockSpec((None, None, tf, d), lambda i, j: (layer, half, j, 0)),
        ],
        out_specs=pl.BlockSpec((tm, d), lambda i, j: (i, 0)),
        scratch_shapes=[pltpu.VMEM((tm, d), BF16), pltpu.VMEM((tm, d), F32)],
        compiler_params=_params("parallel", "arbitrary"),
        name="ffn_half_step",
    )(x, mod, g.reshape(1, d), w_in, w_in, w_out)


def _proj_body(x_ref, mod_ref, g_ref, perm_ref, w_ref, b_ref, o_ref, h_ref):
    half = x_ref.shape[0] // 2

    @pl.when(pl.program_id(1) == 0)
    def _():
        h = _modulated(x_ref[...], g_ref[...], mod_ref[0, 0:1, :], mod_ref[0, 1:2, :]).astype(BF16)
        h_ref[...] = _dot(perm_ref[...], h).astype(BF16)

    p = (_dot(h_ref[...], w_ref[...]) + b_ref[...]).astype(o_ref.dtype)
    o_ref[0] = p[:half]
    o_ref[1] = p[half:]


def _qkv_body(x_ref, mod_ref, g_ref, w_ref, gamma_ref, qs_ref, isv_ref, cos_ref, sin_ref, o_ref, h_ref, p_ref):
    i = pl.program_id(0)
    j = pl.program_id(1)

    @pl.when(j == 0)
    def _():
        h = _modulated(x_ref[...], g_ref[...], mod_ref[0, 0:1, :], mod_ref[0, 1:2, :])
        h_ref[...] = h.astype(BF16)

    @pl.when(jnp.logical_and(i == 0, j == 0))
    def _():
        p_ref[...] = jnp.zeros_like(p_ref)

    cos = cos_ref[...]
    sin = sin_ref[...]
    even = (lax.broadcasted_iota(jnp.int32, (1, HEAD_DIM), 1) % 2) == 0
    for hh in range(p_ref.shape[1] // HEAD_DIM):
        sl = slice(hh * HEAD_DIM, (hh + 1) * HEAD_DIM)
        ph = p_ref[:, sl]
        ms = jnp.mean(ph * ph, axis=-1, keepdims=True)
        nh = (ph * lax.rsqrt(ms + EPS)) * gamma_ref[:, sl]
        partner = jnp.where(even, pltpu.roll(nh, HEAD_DIM - 1, 1), pltpu.roll(nh, 1, 1))
        r = (nh * cos + partner * sin) * qs_ref[:, sl]
        o_ref[:, sl] = jnp.where(isv_ref[:, sl] > 0.0, ph, r).astype(o_ref.dtype)
    p_ref[...] = _dot(h_ref[...], w_ref[...])


def _row_tile(rows, rows_per_mod):
    return _tile(min(rows, rows_per_mod), 1024)


def _parity_perm(tm):
    r = jnp.arange(tm, dtype=jnp.int32)[:, None]
    c = jnp.arange(tm, dtype=jnp.int32)[None, :]
    src = jnp.where(r < tm // 2, 2 * r, 2 * (r - tm // 2) + 1)
    return (c == src).astype(BF16)


def _col_tiled(w, tn):
    k, n = w.shape
    return w.reshape(k, n // tn, tn).transpose(1, 0, 2).astype(BF16)


def _proj(x, mod, g, w_t, b, rows_per_mod, seq):
    rows, d = x.shape
    nj, _, tn = w_t.shape
    n = nj * tn
    tm = _tile(min(rows, rows_per_mod, seq), 1024)
    return pl.pallas_call(
        _proj_body,
        out_shape=jax.ShapeDtypeStruct((2, rows // 2, n), BF16),
        grid=(rows // tm, n // tn),
        in_specs=[
            pl.BlockSpec((tm, d), lambda i, j: (i, 0)),
            pl.BlockSpec((1, 3, d), lambda i, j: (i * tm // rows_per_mod, 0, 0)),
            pl.BlockSpec((1, d), lambda i, j: (0, 0)),
            pl.BlockSpec((tm, tm), lambda i, j: (0, 0), pipeline_mode=pl.Buffered(1)),
            pl.BlockSpec((None, d, tn), lambda i, j: (j, 0, 0)),
            pl.BlockSpec((1, tn), lambda i, j: (0, j)),
        ],
        out_specs=pl.BlockSpec((2, tm // 2, tn), lambda i, j: (0, i, j)),
        scratch_shapes=[pltpu.VMEM((tm, d), BF16)],
        compiler_params=_params("parallel", "arbitrary"),
        name="modulate_project",
    )(x, mod, g.reshape(1, d), _parity_perm(tm), w_t, b.reshape(1, n))


def _qkv_proj(x, mod, g, w_t, rows_per_mod, tables, rope, seq):
    rows, d = x.shape
    nj, _, tn = w_t.shape
    n = nj * tn
    tm = _row_tile(rows, rows_per_mod)
    gamma, qs, isv = tables
    cos, sin = rope

    def prev(j):
        return jnp.maximum(j - 1, 0)

    col = pl.BlockSpec((1, tn), lambda i, j: (0, prev(j)))
    if seq is None:
        rope_spec = pl.BlockSpec((tm, HEAD_DIM), lambda i, j: (0, 0))
    else:
        rope_spec = pl.BlockSpec((tm, HEAD_DIM), lambda i, j: (i % (seq // tm), 0))
    return pl.pallas_call(
        _qkv_body,
        out_shape=jax.ShapeDtypeStruct((rows, n), BF16),
        grid=(rows // tm, nj + 1),
        in_specs=[
            pl.BlockSpec((tm, d), lambda i, j: (i, 0)),
            pl.BlockSpec((1, 3, d), lambda i, j: (i * tm // rows_per_mod, 0, 0)),
            pl.BlockSpec((1, d), lambda i, j: (0, 0)),
            pl.BlockSpec((None, d, tn), lambda i, j: (jnp.minimum(j, nj - 1), 0, 0)),
            col, col, col, rope_spec, rope_spec,
        ],
        out_specs=pl.BlockSpec((tm, tn), lambda i, j: (i, prev(j))),
        scratch_shapes=[pltpu.VMEM((tm, d), BF16), pltpu.VMEM((tm, tn), F32)],
        compiler_params=_params("arbitrary", "arbitrary"),
        name="qkv_project",
    )(x, mod, g.reshape(1, d), w_t, gamma, qs, isv, cos, sin)


def _attn_body(*refs, has_lat, has_mask, has_sink, tq):
    sink_ref, q_ref, kc_ref, vc_ref = refs[:4]
    if has_lat:
        k_ref, v_ref, o_ref = refs[4:7]
    else:
        o_ref = refs[4]
    g = pl.program_id(1)
    qi = pl.program_id(2)
    if has_lat:
        l_rows = k_ref.shape[0]
        if has_mask:
            kw = min(l_rows, tq + 2 * WINDOW)
            k0 = pl.multiple_of(jnp.clip(qi * tq - WINDOW, 0, l_rows - kw), WINDOW)
            keys = pl.ds(k0, kw)
        else:
            k0, keys = 0, slice(None)
    for hh in range(GROUP):
        sl = slice(hh * HEAD_DIM, (hh + 1) * HEAD_DIM)
        q = q_ref[:, sl]
        s1 = _dot_nt(q, kc_ref[...])
        m = jnp.max(s1, axis=-1, keepdims=True)
        if has_lat:
            s2 = _dot_nt(q, k_ref[keys, :])
            if has_mask:
                qpos = qi * tq + lax.broadcasted_iota(jnp.int32, (tq, 1), 0)
                kpos = k0 + lax.broadcasted_iota(jnp.int32, (1, s2.shape[1]), 1)
                s2 = jnp.where(jnp.abs(kpos - qpos) <= WINDOW, s2, NEG_INF)
            m = jnp.maximum(m, jnp.max(s2, axis=-1, keepdims=True))
        if has_sink:
            sk = sink_ref[g * GROUP + hh]
            m = jnp.maximum(m, sk)
        p1 = jnp.exp(s1 - m)
        l = jnp.sum(p1, axis=-1, keepdims=True)
        o = _dot(p1.astype(BF16), vc_ref[...])
        if has_lat:
            p2 = jnp.exp(s2 - m)
            l = l + jnp.sum(p2, axis=-1, keepdims=True)
            o = o + _dot(p2.astype(BF16), v_ref[keys, :])
        if has_sink:
            l = l + jnp.exp(sk - m)
        o_ref[:, sl] = (o / l).astype(o_ref.dtype)


def _attention(sink, q_src, ctx_src, lat_src, *, batch, q_rows, q_col, k_col, v_col, has_mask, has_sink):
    tq = _tile(q_rows, 256)
    assert not has_mask or tq % WINDOW == 0
    c_rows = ctx_src.shape[0] // batch
    gw = GROUP * HEAD_DIM
    in_specs = [
        pl.BlockSpec(memory_space=pltpu.SMEM),
        pl.BlockSpec((tq, gw), lambda b, g, i: (b * (q_rows // tq) + i, q_col + g)),
        pl.BlockSpec((c_rows, HEAD_DIM), lambda b, g, i: (b, k_col + g)),
        pl.BlockSpec((c_rows, HEAD_DIM), lambda b, g, i: (b, v_col + g)),
    ]
    args = [sink, q_src, ctx_src, ctx_src]
    has_lat = lat_src is not None
    if has_lat:
        l_rows = lat_src.shape[0] // batch
        in_specs += [
            pl.BlockSpec((l_rows, HEAD_DIM), lambda b, g, i: (b, k_col + g)),
            pl.BlockSpec((l_rows, HEAD_DIM), lambda b, g, i: (b, v_col + g)),
        ]
        args += [lat_src, lat_src]
    return pl.pallas_call(
        functools.partial(_attn_body, has_lat=has_lat, has_mask=has_mask, has_sink=has_sink, tq=tq),
        out_shape=jax.ShapeDtypeStruct((batch * q_rows, Q_COLS), BF16),
        grid=(batch, N_KV, q_rows // tq),
        in_specs=in_specs,
        out_specs=pl.BlockSpec((tq, gw), lambda b, g, i: (b * (q_rows // tq) + i, g)),
        compiler_params=_params("parallel", "parallel", "parallel"),
        name="gqa_attention",
    )(*args)


def _outproj_body(*refs, n_in):
    a_refs, w_refs = refs[:n_in], refs[n_in:2 * n_in]
    b_ref, x_ref, mod_ref, o_ref = refs[2 * n_in:]
    y = b_ref[...] + _dot(a_refs[0][...], w_refs[0][...])
    for a_ref, w_ref in zip(a_refs[1:], w_refs[1:]):
        y = y + _dot(a_ref[...], w_ref[...])
    o_ref[...] = x_ref[...] + mod_ref[0, 2:3, :] * y


def _outproj(acts, weight, bias, x, mod, rows_per_mod):
    rows, d = x.shape
    tm = _tile(min(rows, rows_per_mod), 512)
    n_in = len(acts)
    k = acts[0].shape[1]
    assert all(a.shape[1] == k for a in acts) and weight.shape[0] == n_in * k
    in_specs = [pl.BlockSpec((tm, k), lambda i: (i, 0)) for _ in acts]
    in_specs += [pl.BlockSpec((k, d), lambda i, n=n: (n, 0)) for n in range(n_in)]
    in_specs += [
        pl.BlockSpec((1, d), lambda i: (0, 0)),
        pl.BlockSpec((tm, d), lambda i: (i, 0)),
        pl.BlockSpec((1, 3, d), lambda i: (i * tm // rows_per_mod, 0, 0)),
    ]
    return pl.pallas_call(
        functools.partial(_outproj_body, n_in=n_in),
        out_shape=jax.ShapeDtypeStruct((rows, d), F32),
        grid=(rows // tm,),
        in_specs=in_specs,
        out_specs=pl.BlockSpec((tm, d), lambda i: (i, 0)),
        compiler_params=_params("parallel"),
        name="outproj_residual",
    )(*acts, *([weight] * n_in), bias.reshape(1, d), x, mod)


def _outproj_parity_body(z_ref, unperm_ref, w_ref, b_ref, x_ref, mod_ref, o_ref):
    z = jnp.concatenate([z_ref[0], z_ref[1]], axis=0)
    z = _dot(unperm_ref[...], z).astype(BF16)
    y = b_ref[...] + _dot(z, w_ref[...])
    o_ref[...] = x_ref[...] + mod_ref[0, 2:3, :] * y


def _outproj_parity(z, weight, bias, x, mod, rows_per_mod, seq):
    rows, d = x.shape
    k = z.shape[2]
    tm = _tile(min(rows, rows_per_mod, seq), 512)
    return pl.pallas_call(
        _outproj_parity_body,
        out_shape=jax.ShapeDtypeStruct((rows, d), F32),
        grid=(rows // tm,),
        in_specs=[
            pl.BlockSpec((2, tm // 2, k), lambda i: (0, i, 0)),
            pl.BlockSpec((tm, tm), lambda i: (0, 0)),
            pl.BlockSpec((k, d), lambda i: (0, 0)),
            pl.BlockSpec((1, d), lambda i: (0, 0)),
            pl.BlockSpec((tm, d), lambda i: (i, 0)),
            pl.BlockSpec((1, 3, d), lambda i: (i * tm // rows_per_mod, 0, 0)),
        ],
        out_specs=pl.BlockSpec((tm, d), lambda i: (i, 0)),
        compiler_params=_params("parallel"),
        name="outproj_residual_parity",
    )(z, _parity_perm(tm).T, weight, bias.reshape(1, d), x, mod)


def _filter_body(feats_ref, w1_ref, b1_ref, fr1_ref, w2_ref, b2_ref, fr2_ref, w3f_ref, w3b_ref, dl_ref, o_ref):
    feats = feats_ref[...]
    h1 = jnp.sin(fr1_ref[...] * (jnp.dot(feats, w1_ref[...], preferred_element_type=F32, precision=HIGHEST)
                                 + b1_ref[...]))
    h2 = jnp.sin(fr2_ref[...] * (jnp.dot(h1, w2_ref[...], preferred_element_type=F32, precision=HIGHEST)
                                 + b2_ref[...]))
    decay = jnp.exp(-feats[:, 0:1] * dl_ref[...])
    fwd = jnp.dot(h2, w3f_ref[...], preferred_element_type=F32, precision=HIGHEST) * decay
    bwd = jnp.dot(h2, w3b_ref[...], preferred_element_type=F32, precision=HIGHEST) * decay
    row = lax.broadcasted_iota(jnp.int32, (fwd.shape[0], 1), 0)
    bwd = jnp.where(row == 0, 0.0, bwd)
    nrm = lax.rsqrt(jnp.sum(fwd * fwd + bwd * bwd, axis=0, keepdims=True) + EPS)
    o_ref[0] = ((fwd + bwd) * nrm).astype(o_ref.dtype)
    o_ref[1] = ((fwd - bwd) * nrm).astype(o_ref.dtype)


def _hyena_filter_taps(feats, w1, b1, fr1, w2, b2, fr2, w3, absdelta, d):
    n, fp = feats.shape
    hp = w1.shape[1]
    tn = _tile(d, 512)
    nd = d // tn
    vec = pl.BlockSpec((1, hp), lambda o, j: (0, 0))
    return pl.pallas_call(
        _filter_body,
        out_shape=jax.ShapeDtypeStruct((2, n, HYENA_ORDER * d), BF16),
        grid=(HYENA_ORDER, nd),
        in_specs=[
            pl.BlockSpec((n, fp), lambda o, j: (0, 0)),
            pl.BlockSpec((fp, hp), lambda o, j: (0, 0)), vec, vec,
            pl.BlockSpec((hp, hp), lambda o, j: (0, 0)), vec, vec,
            pl.BlockSpec((hp, tn), lambda o, j: (0, (2 * o) * nd + j)),
            pl.BlockSpec((hp, tn), lambda o, j: (0, (2 * o + 1) * nd + j)),
            pl.BlockSpec((1, tn), lambda o, j: (0, j)),
        ],
        out_specs=pl.BlockSpec((2, n, tn), lambda o, j: (0, 0, o * nd + j)),
        compiler_params=_params("parallel", "parallel"),
        name="hyena_filter_taps",
    )(feats, w1, b1, fr1, w2, b2, fr2, w3, w3, absdelta)


def _butterfly(ce_ref, se_ref, co_ref, so_ref, xe, xo):
    pr = _dot(ce_ref[...], xe)
    pi = _dot(se_ref[...], xe)
    qr = _dot(co_ref[...], xo)
    qi = _dot(so_ref[...], xo)
    return pr, pi, qr, qi


def _spectrum_body(ce_ref, se_ref, co_ref, so_ref, se_, so_, de_, do_, o_ref, *, inv_n):
    i = pl.program_id(0)
    tk = ce_ref.shape[0]
    pr = _dot(ce_ref[...], se_[...])
    qr = _dot(co_ref[...], so_[...])
    pi = _dot(se_ref[...], de_[...])
    qi = _dot(so_ref[...], do_[...])
    row0 = (i * tk + lax.broadcasted_iota(jnp.int32, (tk, 1), 0)) == 0
    w_re = jnp.where(row0, inv_n, 2.0 * inv_n)
    o_ref[0] = (pr + qr) * w_re
    o_ref[1] = (pr - qr) * w_re
    o_ref[2] = (pi + qi) * (2.0 * inv_n)
    o_ref[3] = (qi - pi) * (2.0 * inv_n)

    @pl.when(i == 0)
    def _():
        rows = 2 * SUBLANES
        mid_r = _dot(se_ref[0:rows, :], se_[...])
        o_ref[2, 0:1, :] = mid_r[0:1, :] * (2.0 * inv_n)
        o_ref[3, 0:1, :] = qi[0:1, :] * (2.0 * inv_n)


def _hyena_spectrum(dft, taps):
    n = taps.shape[1]
    cols = taps.shape[2]
    half = n // 2
    tk = _tile(half, 1024)
    tn = _tile(cols, 512)
    nr = half // tk
    mat = lambda par, part: pl.BlockSpec((None, tk, half), lambda i, j: (par, part * nr + i, 0))
    tap = lambda which, par: pl.BlockSpec((None, half, tn), lambda i, j: (which, par, j))
    return pl.pallas_call(
        functools.partial(_spectrum_body, inv_n=1.0 / (2 * n)),
        out_shape=jax.ShapeDtypeStruct((4, half, cols), F32),
        grid=(nr, cols // tn),
        in_specs=[mat(0, 0), mat(0, 1), mat(1, 0), mat(1, 1), tap(0, 0), tap(0, 1), tap(1, 0), tap(1, 1)],
        out_specs=pl.BlockSpec((4, tk, tn), lambda i, j: (0, i, j)),
        compiler_params=_params("parallel", "parallel"),
        name="hyena_filter_spectrum",
    )(dft, dft, dft, dft, taps, taps, taps, taps)


def _shortconv_body(p_ref, w_ref, b_ref, o_ref):
    pe = p_ref[0].astype(F32)
    po = p_ref[1].astype(F32)
    half = pe.shape[0]
    row = lax.broadcasted_iota(jnp.int32, (half, 1), 0)
    po_prev = jnp.where(row == 0, 0.0, pltpu.roll(po, 1, 0))
    pe_next = jnp.where(row == half - 1, 0.0, pltpu.roll(pe, half - 1, 0))
    w0, w1, w2 = w_ref[0:1, :], w_ref[1:2, :], w_ref[2:3, :]
    o_ref[0] = (((b_ref[...] + po_prev * w0) + pe * w1) + po * w2).astype(o_ref.dtype)
    o_ref[1] = (((b_ref[...] + pe * w0) + po * w1) + pe_next * w2).astype(o_ref.dtype)


def _shortconv(p, w, b, batch):
    _, rows, n3 = p.shape
    half = rows // batch
    tn = _tile(n3, 512)
    return pl.pallas_call(
        _shortconv_body,
        out_shape=jax.ShapeDtypeStruct(p.shape, BF16),
        grid=(batch, n3 // tn),
        in_specs=[
            pl.BlockSpec((2, half, tn), lambda b_, j: (0, b_, j)),
            pl.BlockSpec((HYENA_SHORT_K, tn), lambda b_, j: (0, j)),
            pl.BlockSpec((1, tn), lambda b_, j: (0, j)),
        ],
        out_specs=pl.BlockSpec((2, half, tn), lambda b_, j: (0, b_, j)),
        compiler_params=_params("parallel", "parallel"),
        name="hyena_short_conv",
    )(p, w, b.reshape(1, n3))


def _dft_fwd_body(ce_ref, se_ref, co_ref, so_ref, ze_ref, zo_ref, k_ref, yr_ref, yi_ref):
    i = pl.program_id(1)
    tk = ce_ref.shape[0]
    pr, pi, qr, qi = _butterfly(ce_ref, se_ref, co_ref, so_ref, ze_ref[...], zo_ref[...])
    kr_lo, kr_hi, ki_lo, ki_hi = k_ref[0], k_ref[1], k_ref[2], k_ref[3]
    row0 = (i * tk + lax.broadcasted_iota(jnp.int32, (tk, 1), 0)) == 0
    zr_lo, zr_hi, zi_lo, zi_hi = pr + qr, pr - qr, pi + qi, qi - pi
    yr_lo = zr_lo * kr_lo - jnp.where(row0, 0.0, zi_lo * ki_lo)
    yr_hi = zr_hi * kr_hi - jnp.where(row0, 0.0, zi_hi * ki_hi)
    yi_lo = zr_lo * ki_lo + zi_lo * kr_lo
    yi_hi = zr_hi * ki_hi + zi_hi * kr_hi
    mid_r = pi * ki_lo - qi * ki_hi
    mid_i = pi * ki_hi + qi * ki_lo
    yr_ref[0] = (yr_lo + yr_hi).astype(yr_ref.dtype)
    yr_ref[1] = (yr_lo - yr_hi).astype(yr_ref.dtype)
    yi_ref[0] = jnp.where(row0, mid_r, yi_lo - yi_hi).astype(yi_ref.dtype)
    yi_ref[1] = jnp.where(row0, mid_i, yi_lo + yi_hi).astype(yi_ref.dtype)


def _dft_fwd(dft, z_src, z_col, spec, order, batch, d):
    half = z_src.shape[1] // batch
    tk = _tile(half, 1024)
    tn = _tile(d, 512)
    nr, nd = half // tk, d // tn
    out = jax.ShapeDtypeStruct((2, batch * half, d), BF16)
    once = pl.Buffered(1)
    mat = lambda par, part: pl.BlockSpec((None, tk, half), lambda j, i, b: (par, part * nr + i, 0),
                                         pipeline_mode=once)
    zin = lambda par: pl.BlockSpec((None, half, tn), lambda j, i, b: (par, b, z_col * nd + j))
    res = pl.BlockSpec((2, tk, tn), lambda j, i, b: (0, b * nr + i, j))
    return pl.pallas_call(
        _dft_fwd_body,
        out_shape=(out, out),
        grid=(nd, nr, batch),
        in_specs=[
            mat(0, 0), mat(0, 1), mat(1, 0), mat(1, 1), zin(0), zin(1),
            pl.BlockSpec((4, tk, tn), lambda j, i, b: (0, i, order * nd + j), pipeline_mode=once),
        ],
        out_specs=(res, res),
        compiler_params=_params("parallel", "parallel", "parallel"),
        name="hyena_dft_forward",
    )(dft, dft, dft, dft, z_src, z_src, spec)


def _dft_inv_body(atr_ref, ati_ref, yr_ref, yi_ref, g_ref, zp_ref, bias_ref, o_ref):
    y = _dot(atr_ref[...], yr_ref[...]) + _dot(ati_ref[...], yi_ref[...])
    zp = zp_ref[...].astype(F32)
    o_ref[...] = (g_ref[...].astype(F32) * (y + zp * bias_ref[...])).astype(o_ref.dtype)


def _dft_inv(dft_t, yr, yi, gate_src, gate_col, z_src, z_col, bias, batch, d):
    half = yr.shape[1] // batch
    tt = _tile(half, 1024)
    tn = _tile(d, 1024)
    nr, nd = half // tt, d // tn
    row = lambda cols: pl.BlockSpec((None, tt, tn), lambda j, p, i, b: (p, b * nr + i, cols * nd + j))
    return pl.pallas_call(
        _dft_inv_body,
        out_shape=jax.ShapeDtypeStruct((2, batch * half, d), BF16),
        grid=(nd, 2, nr, batch),
        in_specs=[
            pl.BlockSpec((None, tt, half), lambda j, p, i, b: (p, i, 0)),
            pl.BlockSpec((None, tt, half), lambda j, p, i, b: (p, i, 1)),
            pl.BlockSpec((None, half, tn), lambda j, p, i, b: (p, b, j)),
            pl.BlockSpec((None, half, tn), lambda j, p, i, b: (p, b, j)),
            row(gate_col), row(z_col),
            pl.BlockSpec((1, tn), lambda j, p, i, b: (0, j)),
        ],
        out_specs=row(0),
        compiler_params=_params("parallel", "parallel", "parallel", "parallel"),
        name="hyena_dft_inverse",
    )(dft_t, dft_t, yr, yi, gate_src, z_src, bias.reshape(1, d))


def _rope_tables(n_tok):
    rows = n_tok // GRID_W
    r = jnp.repeat(jnp.arange(rows), GRID_W).astype(F32)
    col = jnp.tile(jnp.arange(GRID_W), rows).astype(F32)
    half = HEAD_DIM // 2
    inv = ROPE_THETA ** (-jnp.arange(0, half, 2, dtype=F32) / half)
    ang = jnp.concatenate([r[:, None] * inv, col[:, None] * inv], axis=-1)
    cos = jnp.repeat(jnp.cos(ang), 2, axis=-1)
    sin = jnp.repeat(jnp.sin(ang), 2, axis=-1)
    sign = jnp.where(jnp.arange(HEAD_DIM) % 2 == 0, -1.0, 1.0).astype(F32)
    return cos, sin * sign


def _dft_tables(n):
    big = 2 * n
    half = n // 2
    k = jnp.arange(half, dtype=jnp.int32)[:, None]
    m = jnp.arange(half, dtype=jnp.int32)[None, :]
    alt = jnp.where(m % 2 == 0, 1.0, -1.0).astype(F32)
    mats = []
    for par in range(2):
        ang = ((k * (2 * m + par)) % big).astype(F32) * (2.0 * math.pi / big)
        s = jnp.where(k == 0, alt if par == 0 else -alt, -jnp.sin(ang))
        mats.append(jnp.concatenate([jnp.cos(ang), s], axis=0))
    a = jnp.stack(mats).astype(BF16)
    return a, a.transpose(0, 2, 1)


def _filter_features(n, width):
    t = jnp.linspace(0.0, 1.0, n, dtype=F32)[:, None]
    w = (2.0 * math.pi / n) * jnp.arange(n, dtype=F32)[:, None]
    bands = jnp.linspace(1e-4, HYENA_BANDS - 1, HYENA_BANDS, dtype=F32)
    ang = w * bands[None, :]
    feats = jnp.concatenate([t, jnp.cos(ang), -jnp.sin(ang)], axis=-1)
    feats = jnp.concatenate([feats[0::2], feats[1::2]], axis=0)
    return jnp.pad(feats, ((0, 0), (0, width - feats.shape[1])))


def _pad_to(a, shape):
    return jnp.pad(a, [(0, s - d) for d, s in zip(a.shape, shape)])


def _attn_layer(x, xc, mod_l, mod_c, g, w_in, w_out, g_q, g_k, sink, rope, batch, seq, n_ctx, ctx_out):
    d = x.shape[1]
    ones = jnp.ones((HEAD_DIM,), F32)
    gamma = jnp.concatenate([jnp.tile(g_q[0], N_HEADS), jnp.tile(g_q[1], N_HEADS),
                             jnp.tile(g_k[0], N_KV), jnp.tile(ones, N_KV),
                             jnp.tile(g_k[1], N_KV), jnp.tile(ones, N_KV)])[None, :]
    qs = jnp.concatenate([jnp.full((2 * Q_COLS,), HEAD_DIM ** -0.5, F32), jnp.ones((4 * KV_COLS,), F32)])[None, :]
    zk, ok = jnp.zeros((KV_COLS,), F32), jnp.ones((KV_COLS,), F32)
    isv = jnp.concatenate([jnp.zeros((2 * Q_COLS,), F32), zk, ok, zk, ok])[None, :]
    tables = (gamma, qs, isv)
    tm = _row_tile(batch * n_ctx, batch * n_ctx)
    ident = (jnp.ones((tm, HEAD_DIM), F32), jnp.zeros((tm, HEAD_DIM), F32))
    w_in_b = _col_tiled(w_in, _tile(w_in.shape[1], 512))
    qkv = _qkv_proj(x, mod_l, g, w_in_b, seq, tables, rope, seq)
    qkv_c = _qkv_proj(xc, mod_c, g, w_in_b, batch * n_ctx, tables, ident, None)
    qa_col, qb_col = 0, N_KV
    ka_col = 2 * Q_COLS // HEAD_DIM
    va_col, kb_col, vb_col = ka_col + N_KV, ka_col + 2 * N_KV, ka_col + 3 * N_KV
    o_a = _attention(sink, qkv, qkv_c, qkv, batch=batch, q_rows=seq, q_col=qa_col, k_col=ka_col, v_col=va_col,
                     has_mask=False, has_sink=False)
    o_b = _attention(sink, qkv, qkv_c, qkv, batch=batch, q_rows=seq, q_col=qb_col, k_col=kb_col, v_col=vb_col,
                     has_mask=True, has_sink=True)
    w_out_b = w_out.astype(BF16)
    zero_b = jnp.zeros((d,), F32)
    x = _outproj((o_a, o_b), w_out_b, zero_b, x, mod_l, seq)
    if ctx_out:
        co_a = _attention(sink, qkv_c, qkv_c, None, batch=batch, q_rows=n_ctx, q_col=qa_col, k_col=ka_col,
                          v_col=va_col, has_mask=False, has_sink=False)
        co_b = _attention(sink, qkv_c, qkv_c, None, batch=batch, q_rows=n_ctx, q_col=qb_col, k_col=kb_col,
                          v_col=vb_col, has_mask=False, has_sink=True)
        xc = _outproj((co_a, co_b), w_out_b, zero_b, xc, mod_c, batch * n_ctx)
    return x, xc


def _hyena_layer(x, mod, g, rows_per_mod, batch, w_in_b, b_in, w_conv, b_conv, filt, hy_bias, w_out_b, b_out):
    d = x.shape[1]
    n = x.shape[0] // batch
    w1, b1, fr1, w2, b2, fr2, w3 = filt
    hp = LANES
    feats = _filter_features(n, LANES)
    max_decay = math.log(HYENA_TARGET) / HYENA_FAST_PCT
    min_decay = math.log(HYENA_TARGET) / HYENA_SLOW_PCT
    absdelta = jnp.abs(jnp.linspace(min_decay, max_decay, d, dtype=F32))[None, :]
    taps = _hyena_filter_taps(
        feats, _pad_to(w1, (LANES, hp)), _pad_to(b1[None, :], (1, hp)), _pad_to(fr1[None, :], (1, hp)),
        _pad_to(w2, (hp, hp)), _pad_to(b2[None, :], (1, hp)), _pad_to(fr2[None, :], (1, hp)),
        _pad_to(w3, (hp, w3.shape[1])), absdelta, d)
    dft, dft_t = _dft_tables(n)
    spec = _hyena_spectrum(dft, taps)
    p = _proj(x, mod, g, w_in_b, b_in, rows_per_mod, n)
    pc = _shortconv(p, w_conv, b_conv, batch)
    yr, yi = _dft_fwd(dft, pc, 0, spec, 0, batch, d)
    z = _dft_inv(dft_t, yr, yi, pc, 1, pc, 0, hy_bias[0], batch, d)
    yr, yi = _dft_fwd(dft, z, 0, spec, 1, batch, d)
    z = _dft_inv(dft_t, yr, yi, pc, 2, z, 0, hy_bias[1], batch, d)
    return _outproj_parity(z, w_out_b, b_out, x, mod, rows_per_mod, n)


def kernel(x, c, ctx, c_ctx, w_mod, b_mod, g_norm, w_ffn_in, w_ffn_out, w_attn_in, w_attn_out, g_q, g_k, sink, w_hy_in, b_hy_in, w_hy_conv, b_hy_conv, hf_w1, hf_b1, hf_freq1, hf_w2, hf_b2, hf_freq2, hf_w3, hy_bias, w_hy_out, b_hy_out):
    batch, seq, d = x.shape
    n_ctx = ctx.shape[1]
    depth = w_mod.shape[0]
    rows_c = batch * n_ctx
    rope = _rope_tables(seq)
    last_ctx = max(l for l in range(depth) if l % 2 == 0)

    r_pad = -(-(batch + 1) // SUBLANES) * SUBLANES
    c_all = _pad_to(jnp.concatenate([c, c_ctx[None, :]], axis=0), (r_pad, d))
    m_all = _mod_all(c_all, w_mod, b_mod).reshape(depth, r_pad, N_MOD, d)

    w_ffn_in_b = w_ffn_in.astype(BF16)
    w_ffn_out_b = w_ffn_out.astype(BF16)

    xl = x.reshape(batch * seq, d)
    xc = ctx.reshape(rows_c, d)
    for l in range(depth):
        i = l // 2
        ctx_live = l <= last_ctx
        ctx_full = l < last_ctx
        mods_l = [m_all[l, :batch, 3 * k:3 * k + 3] for k in range(3)]
        mods_c = [m_all[l, batch:batch + 1, 3 * k:3 * k + 3] for k in range(3)]
        xl = _ffn(xl, mods_l[0], g_norm[l, 0], w_ffn_in_b, w_ffn_out_b, l, 0, seq)
        if ctx_live:
            xc = _ffn(xc, mods_c[0], g_norm[l, 0], w_ffn_in_b, w_ffn_out_b, l, 0, rows_c)
        if l % 2 == 0:
            xl, xc = _attn_layer(xl, xc, mods_l[1], mods_c[1], g_norm[l, 1], w_attn_in[i], w_attn_out[i],
                                 g_q[i], g_k[i], sink[i], rope, batch, seq, n_ctx, ctx_full)
        else:
            w_in_b = _col_tiled(w_hy_in[i], _tile(w_hy_in.shape[2], 512))
            w_out_b = w_hy_out[i].astype(BF16)
            filt = (hf_w1[i], hf_b1[i], hf_freq1[i], hf_w2[i], hf_b2[i], hf_freq2[i], hf_w3[i])
            xl = _hyena_layer(xl, mods_l[1], g_norm[l, 1], seq, batch, w_in_b, b_hy_in[i], w_hy_conv[i],
                              b_hy_conv[i], filt, hy_bias[i], w_out_b, b_hy_out[i])
            if ctx_full:
                xc = _hyena_layer(xc, mods_c[1], g_norm[l, 1], rows_c, batch, w_in_b, b_hy_in[i], w_hy_conv[i],
                                  b_hy_conv[i], filt, hy_bias[i], w_out_b, b_hy_out[i])
        xl = _ffn(xl, mods_l[2], g_norm[l, 2], w_ffn_in_b, w_ffn_out_b, l, 1, seq)
        if ctx_full:
            xc = _ffn(xc, mods_c[2], g_norm[l, 2], w_ffn_in_b, w_ffn_out_b, l, 1, rows_c)
    return xl.reshape(batch, seq, d)
```

```python
import functools
import math

import jax
import jax.numpy as jnp
from jax import lax
from jax.experimental import pallas as pl
from jax.experimental.pallas import tpu as pltpu

HEAD_DIM = 128
N_HEADS = 8
N_KV = 2
GROUP = N_HEADS // N_KV
Q_COLS = N_HEADS * HEAD_DIM
KV_COLS = N_KV * HEAD_DIM
GRID_W = 64
WINDOW = 128
ROPE_THETA = 10000.0
N_MOD = 9
HYENA_ORDER = 2
HYENA_SHORT_K = 3
HYENA_BANDS = 16
HYENA_TARGET = 1e-2
HYENA_FAST_PCT = 0.3
HYENA_SLOW_PCT = 1.5
EPS = 1e-6
NEG_INF = -1e30

LANES = 128
SUBLANES = 8
VMEM_LIMIT_BYTES = 56 * 1024 * 1024

F32 = jnp.float32
BF16 = jnp.bfloat16
HIGHEST = lax.Precision.HIGHEST


def _params(*sem):
    return pltpu.CompilerParams(dimension_semantics=sem, vmem_limit_bytes=VMEM_LIMIT_BYTES)


def _tile(dim, pref):
    t = min(dim, pref)
    while dim % t:
        t //= 2
    return t


def _dot(a, b):
    return jnp.dot(a, b, preferred_element_type=F32)


def _dot_nt(a, b):
    return lax.dot_general(a, b, (((1,), (1,)), ((), ())), preferred_element_type=F32)


def _silu(v):
    return v * (1.0 / (1.0 + jnp.exp(-v)))


def _modulated(xf, g, shift, scale):
    ms = jnp.mean(xf * xf, axis=-1, keepdims=True)
    y = xf * lax.rsqrt(ms + EPS)
    return (y * g) * (1.0 + scale) + shift


def _mod_body(c_ref, w_ref, b_ref, o_ref):
    s = _silu(c_ref[...])
    o_ref[0] = jnp.dot(s, w_ref[0], preferred_element_type=F32, precision=HIGHEST) + b_ref[0]


def _mod_all(c_all, w_mod, b_mod):
    depth, d, nd = w_mod.shape
    r = c_all.shape[0]
    tn = _tile(nd, 1024)
    return pl.pallas_call(
        _mod_body,
        out_shape=jax.ShapeDtypeStruct((depth, r, nd), F32),
        grid=(depth, nd // tn),
        in_specs=[
            pl.BlockSpec((r, d), lambda l, j: (0, 0)),
            pl.BlockSpec((1, d, tn), lambda l, j: (l, 0, j)),
            pl.BlockSpec((1, 1, tn), lambda l, j: (l, 0, j)),
        ],
        out_specs=pl.BlockSpec((1, r, tn), lambda l, j: (l, 0, j)),
        compiler_params=_params("parallel", "parallel"),
        name="mod_vectors",
    )(c_all, w_mod, b_mod.reshape(depth, 1, nd))


def _ffn_body(x_ref, mod_ref, g_ref, wg_ref, wu_ref, wo_ref, o_ref, h_ref, acta_ref, actb_ref, acc_ref, *, nf):
    j = pl.program_id(1)

    def expand(act_ref):
        h = h_ref[...]
        a = _dot(h, wg_ref[...])
        u = _dot(h, wu_ref[...])
        act_ref[...] = (_silu(a) * u).astype(BF16)

    def contract(act_ref):
        return _dot(act_ref[...], wo_ref[...])

    @pl.when(j == 0)
    def _():
        h = _modulated(x_ref[...], g_ref[...], mod_ref[0, 0:1, :], mod_ref[0, 1:2, :])
        h_ref[...] = h.astype(BF16)
        acc_ref[...] = jnp.zeros_like(acc_ref)
        expand(acta_ref)

    for parity, (new, old) in enumerate(((acta_ref, actb_ref), (actb_ref, acta_ref))):
        @pl.when(jnp.logical_and(jnp.logical_and(j > 0, j < nf), j % 2 == parity))
        def _(new=new, old=old):
            acc_ref[...] += contract(old)
            expand(new)

    @pl.when(j == nf)
    def _():
        last = acta_ref if (nf - 1) % 2 == 0 else actb_ref
        y = acc_ref[...] + contract(last)
        o_ref[...] = x_ref[...] + (0.5 * mod_ref[0, 2:3, :]) * y


def _ffn(x, mod, g, w_in, w_out, layer, half, rows_per_mod):
    rows, d = x.shape
    f = w_out.shape[2]
    tm = _tile(min(rows, rows_per_mod), 512)
    tf = _tile(f, 512)
    nf = f // tf

    def exp_chunk(j):
        return jnp.minimum(j, nf - 1)

    def con_chunk(j):
        return jnp.maximum(j - 1, 0)

    return pl.pallas_call(
        functools.partial(_ffn_body, nf=nf),
        out_shape=jax.ShapeDtypeStruct((rows, d), F32),
        grid=(rows // tm, nf + 1),
        in_specs=[
            pl.BlockSpec((tm, d), lambda i, j: (i, 0)),
            pl.BlockSpec((1, 3, d), lambda i, j: (i * tm // rows_per_mod, 0, 0)),
            pl.BlockSpec((1, d), lambda i, j: (0, 0)),
            pl.BlockSpec((None, None, d, tf), lambda i, j: (layer, half, 0, exp_chunk(j))),
            pl.BlockSpec((None, None, d, tf), lambda i, j: (layer, half, 0, nf + exp_chunk(j))),
            pl.BlockSpec((None, None, tf, d), lambda i, j: (layer, half, con_chunk(j), 0)),
        ],
        out_specs=pl.BlockSpec((tm, d), lambda i, j: (i, 0)),
        scratch_shapes=[pltpu.VMEM((tm, d), BF16), pltpu.VMEM((tm, tf), BF16), pltpu.VMEM((tm, tf), BF16),
                        pltpu.VMEM((tm, d), F32)],
        compiler_params=_params("parallel", "arbitrary"),
        name="ffn_half_step",
    )(x, mod, g.reshape(1, d), w_in, w_in, w_out)


def _proj_body(x_ref, mod_ref, g_ref, perm_ref, w_ref, b_ref, o_ref, h_ref):
    half = x_ref.shape[0] // 2

    @pl.when(pl.program_id(1) == 0)
    def _():
        h = _modulated(x_ref[...], g_ref[...], mod_ref[0, 0:1, :], mod_ref[0, 1:2, :]).astype(BF16)
        h_ref[...] = _dot(perm_ref[...], h).astype(BF16)

    p = (_dot(h_ref[...], w_ref[...]) + b_ref[...]).astype(o_ref.dtype)
    o_ref[0] = p[:half]
    o_ref[1] = p[half:]


def _qkv_body(x_ref, mod_ref, g_ref, w_ref, gamma_ref, qs_ref, isv_ref, cos_ref, sin_ref, o_ref, h_ref, p_ref):
    i = pl.program_id(0)
    j = pl.program_id(1)

    @pl.when(j == 0)
    def _():
        h = _modulated(x_ref[...], g_ref[...], mod_ref[0, 0:1, :], mod_ref[0, 1:2, :])
        h_ref[...] = h.astype(BF16)

    @pl.when(jnp.logical_and(i == 0, j == 0))
    def _():
        p_ref[...] = jnp.zeros_like(p_ref)

    cos = cos_ref[...]
    sin = sin_ref[...]
    even = (lax.broadcasted_iota(jnp.int32, (1, HEAD_DIM), 1) % 2) == 0
    for hh in range(p_ref.shape[1] // HEAD_DIM):
        sl = slice(hh * HEAD_DIM, (hh + 1) * HEAD_DIM)
        ph = p_ref[:, sl]
        ms = jnp.mean(ph * ph, axis=-1, keepdims=True)
        nh = (ph * lax.rsqrt(ms + EPS)) * gamma_ref[:, sl]
        partner = jnp.where(even, pltpu.roll(nh, HEAD_DIM - 1, 1), pltpu.roll(nh, 1, 1))
        r = (nh * cos + partner * sin) * qs_ref[:, sl]
        o_ref[:, sl] = jnp.where(isv_ref[:, sl] > 0.0, ph, r).astype(o_ref.dtype)
    p_ref[...] = _dot(h_ref[...], w_ref[...])


def _row_tile(rows, rows_per_mod):
    return _tile(min(rows, rows_per_mod), 1024)


def _parity_perm(tm):
    r = jnp.arange(tm, dtype=jnp.int32)[:, None]
    c = jnp.arange(tm, dtype=jnp.int32)[None, :]
    src = jnp.where(r < tm // 2, 2 * r, 2 * (r - tm // 2) + 1)
    return (c == src).astype(BF16)


def _col_tiled(w, tn):
    k, n = w.shape
    return w.reshape(k, n // tn, tn).transpose(1, 0, 2).astype(BF16)


def _proj(x, mod, g, w_t, b, rows_per_mod, seq):
    rows, d = x.shape
    nj, _, tn = w_t.shape
    n = nj * tn
    tm = _tile(min(rows, rows_per_mod), 1024)
    assert tm % 2 == 0 and (tm % seq == 0 or seq % tm == 0)
    return pl.pallas_call(
        _proj_body,
        out_shape=jax.ShapeDtypeStruct((2, rows // 2, n), BF16),
        grid=(rows // tm, n // tn),
        in_specs=[
            pl.BlockSpec((tm, d), lambda i, j: (i, 0)),
            pl.BlockSpec((1, 3, d), lambda i, j: (i * tm // rows_per_mod, 0, 0)),
            pl.BlockSpec((1, d), lambda i, j: (0, 0)),
            pl.BlockSpec((tm, tm), lambda i, j: (0, 0), pipeline_mode=pl.Buffered(1)),
            pl.BlockSpec((None, d, tn), lambda i, j: (j, 0, 0)),
            pl.BlockSpec((1, tn), lambda i, j: (0, j)),
        ],
        out_specs=pl.BlockSpec((2, tm // 2, tn), lambda i, j: (0, i, j)),
        scratch_shapes=[pltpu.VMEM((tm, d), BF16)],
        compiler_params=_params("parallel", "arbitrary"),
        name="modulate_project",
    )(x, mod, g.reshape(1, d), _parity_perm(tm), w_t, b.reshape(1, n))


def _qkv_proj(x, mod, g, w_t, rows_per_mod, tables, rope, seq):
    rows, d = x.shape
    nj, _, tn = w_t.shape
    n = nj * tn
    tm = _row_tile(rows, rows_per_mod)
    gamma, qs, isv = tables
    cos, sin = rope

    def prev(j):
        return jnp.maximum(j - 1, 0)

    col = pl.BlockSpec((1, tn), lambda i, j: (0, prev(j)))
    if seq is None:
        rope_spec = pl.BlockSpec((tm, HEAD_DIM), lambda i, j: (0, 0))
    else:
        rope_spec = pl.BlockSpec((tm, HEAD_DIM), lambda i, j: (i % (seq // tm), 0))
    return pl.pallas_call(
        _qkv_body,
        out_shape=jax.ShapeDtypeStruct((rows, n), BF16),
        grid=(rows // tm, nj + 1),
        in_specs=[
            pl.BlockSpec((tm, d), lambda i, j: (i, 0)),
            pl.BlockSpec((1, 3, d), lambda i, j: (i * tm // rows_per_mod, 0, 0)),
            pl.BlockSpec((1, d), lambda i, j: (0, 0)),
            pl.BlockSpec((None, d, tn), lambda i, j: (jnp.minimum(j, nj - 1), 0, 0)),
            col, col, col, rope_spec, rope_spec,
        ],
        out_specs=pl.BlockSpec((tm, tn), lambda i, j: (i, prev(j))),
        scratch_shapes=[pltpu.VMEM((tm, d), BF16), pltpu.VMEM((tm, tn), F32)],
        compiler_params=_params("arbitrary", "arbitrary"),
        name="qkv_project",
    )(x, mod, g.reshape(1, d), w_t, gamma, qs, isv, cos, sin)


def _attn_body(*refs, has_lat, has_mask, has_sink, tq):
    sink_ref, q_ref, kc_ref, vc_ref = refs[:4]
    if has_lat:
        k_ref, v_ref, o_ref = refs[4:7]
    else:
        o_ref = refs[4]
    g = pl.program_id(1)
    qi = pl.program_id(2)
    if has_lat:
        l_rows = k_ref.shape[0]
        if has_mask:
            kw = min(l_rows, tq + 2 * WINDOW)
            k0 = pl.multiple_of(jnp.clip(qi * tq - WINDOW, 0, l_rows - kw), WINDOW)
            keys = pl.ds(k0, kw)
        else:
            k0, keys = 0, slice(None)
    for hh in range(GROUP):
        sl = slice(hh * HEAD_DIM, (hh + 1) * HEAD_DIM)
        q = q_ref[:, sl]
        s1 = _dot_nt(q, kc_ref[...])
        m = jnp.max(s1, axis=-1, keepdims=True)
        if has_lat:
            s2 = _dot_nt(q, k_ref[keys, :])
            if has_mask:
                qpos = qi * tq + lax.broadcasted_iota(jnp.int32, (tq, 1), 0)
                kpos = k0 + lax.broadcasted_iota(jnp.int32, (1, s2.shape[1]), 1)
                s2 = jnp.where(jnp.abs(kpos - qpos) <= WINDOW, s2, NEG_INF)
            m = jnp.maximum(m, jnp.max(s2, axis=-1, keepdims=True))
        if has_sink:
            sk = sink_ref[g * GROUP + hh]
            m = jnp.maximum(m, sk)
        p1 = jnp.exp(s1 - m)
        l = jnp.sum(p1, axis=-1, keepdims=True)
        o = _dot(p1.astype(BF16), vc_ref[...])
        if has_lat:
            p2 = jnp.exp(s2 - m)
            l = l + jnp.sum(p2, axis=-1, keepdims=True)
            o = o + _dot(p2.astype(BF16), v_ref[keys, :])
        if has_sink:
            l = l + jnp.exp(sk - m)
        o_ref[:, sl] = (o / l).astype(o_ref.dtype)


def _attention(sink, q_src, ctx_src, lat_src, *, batch, q_rows, q_col, k_col, v_col, has_mask, has_sink):
    tq = _tile(q_rows, 256)
    assert not has_mask or tq % WINDOW == 0
    c_rows = ctx_src.shape[0] // batch
    gw = GROUP * HEAD_DIM
    in_specs = [
        pl.BlockSpec(memory_space=pltpu.SMEM),
        pl.BlockSpec((tq, gw), lambda b, g, i: (b * (q_rows // tq) + i, q_col + g)),
        pl.BlockSpec((c_rows, HEAD_DIM), lambda b, g, i: (b, k_col + g)),
        pl.BlockSpec((c_rows, HEAD_DIM), lambda b, g, i: (b, v_col + g)),
    ]
    args = [sink, q_src, ctx_src, ctx_src]
    has_lat = lat_src is not None
    if has_lat:
        l_rows = lat_src.shape[0] // batch
        in_specs += [
            pl.BlockSpec((l_rows, HEAD_DIM), lambda b, g, i: (b, k_col + g)),
            pl.BlockSpec((l_rows, HEAD_DIM), lambda b, g, i: (b, v_col + g)),
        ]
        args += [lat_src, lat_src]
    return pl.pallas_call(
        functools.partial(_attn_body, has_lat=has_lat, has_mask=has_mask, has_sink=has_sink, tq=tq),
        out_shape=jax.ShapeDtypeStruct((batch * q_rows, Q_COLS), BF16),
        grid=(batch, N_KV, q_rows // tq),
        in_specs=in_specs,
        out_specs=pl.BlockSpec((tq, gw), lambda b, g, i: (b * (q_rows // tq) + i, g)),
        compiler_params=_params("parallel", "parallel", "parallel"),
        name="gqa_attention",
    )(*args)


def _outproj_body(*refs, n_in):
    a_refs, w_refs = refs[:n_in], refs[n_in:2 * n_in]
    b_ref, x_ref, mod_ref, o_ref = refs[2 * n_in:]
    y = b_ref[...] + _dot(a_refs[0][...], w_refs[0][...])
    for a_ref, w_ref in zip(a_refs[1:], w_refs[1:]):
        y = y + _dot(a_ref[...], w_ref[...])
    o_ref[...] = x_ref[...] + mod_ref[0, 2:3, :] * y


def _outproj(acts, weight, bias, x, mod, rows_per_mod):
    rows, d = x.shape
    tm = _tile(min(rows, rows_per_mod), 512)
    n_in = len(acts)
    k = acts[0].shape[1]
    assert all(a.shape[1] == k for a in acts) and weight.shape[0] == n_in * k
    in_specs = [pl.BlockSpec((tm, k), lambda i: (i, 0)) for _ in acts]
    in_specs += [pl.BlockSpec((k, d), lambda i, n=n: (n, 0)) for n in range(n_in)]
    in_specs += [
        pl.BlockSpec((1, d), lambda i: (0, 0)),
        pl.BlockSpec((tm, d), lambda i: (i, 0)),
        pl.BlockSpec((1, 3, d), lambda i: (i * tm // rows_per_mod, 0, 0)),
    ]
    return pl.pallas_call(
        functools.partial(_outproj_body, n_in=n_in),
        out_shape=jax.ShapeDtypeStruct((rows, d), F32),
        grid=(rows // tm,),
        in_specs=in_specs,
        out_specs=pl.BlockSpec((tm, d), lambda i: (i, 0)),
        compiler_params=_params("parallel"),
        name="outproj_residual",
    )(*acts, *([weight] * n_in), bias.reshape(1, d), x, mod)


def _outproj_parity_body(z_ref, unperm_ref, w_ref, b_ref, x_ref, mod_ref, o_ref):
    z = jnp.concatenate([z_ref[0], z_ref[1]], axis=0)
    z = _dot(unperm_ref[...], z).astype(BF16)
    y = b_ref[...] + _dot(z, w_ref[...])
    o_ref[...] = x_ref[...] + mod_ref[0, 2:3, :] * y


def _outproj_parity(z, weight, bias, x, mod, rows_per_mod, seq):
    rows, d = x.shape
    k = z.shape[2]
    tm = _tile(min(rows, rows_per_mod), 512)
    assert tm % 2 == 0 and (tm % seq == 0 or seq % tm == 0)
    return pl.pallas_call(
        _outproj_parity_body,
        out_shape=jax.ShapeDtypeStruct((rows, d), F32),
        grid=(rows // tm,),
        in_specs=[
            pl.BlockSpec((2, tm // 2, k), lambda i: (0, i, 0)),
            pl.BlockSpec((tm, tm), lambda i: (0, 0)),
            pl.BlockSpec((k, d), lambda i: (0, 0)),
            pl.BlockSpec((1, d), lambda i: (0, 0)),
            pl.BlockSpec((tm, d), lambda i: (i, 0)),
            pl.BlockSpec((1, 3, d), lambda i: (i * tm // rows_per_mod, 0, 0)),
        ],
        out_specs=pl.BlockSpec((tm, d), lambda i: (i, 0)),
        compiler_params=_params("parallel"),
        name="outproj_residual_parity",
    )(z, _parity_perm(tm).T, weight, bias.reshape(1, d), x, mod)


def _filter_body(feats_ref, w1_ref, b1_ref, fr1_ref, w2_ref, b2_ref, fr2_ref, w3f_ref, w3b_ref, dl_ref, o_ref):
    feats = feats_ref[...]
    h1 = jnp.sin(fr1_ref[...] * (jnp.dot(feats, w1_ref[...], preferred_element_type=F32, precision=HIGHEST)
                                 + b1_ref[...]))
    h2 = jnp.sin(fr2_ref[...] * (jnp.dot(h1, w2_ref[...], preferred_element_type=F32, precision=HIGHEST)
                                 + b2_ref[...]))
    decay = jnp.exp(-feats[:, 0:1] * dl_ref[...])
    fwd = jnp.dot(h2, w3f_ref[...], preferred_element_type=F32, precision=HIGHEST) * decay
    bwd = jnp.dot(h2, w3b_ref[...], preferred_element_type=F32, precision=HIGHEST) * decay
    row = lax.broadcasted_iota(jnp.int32, (fwd.shape[0], 1), 0)
    bwd = jnp.where(row == 0, 0.0, bwd)
    nrm = lax.rsqrt(jnp.sum(fwd * fwd + bwd * bwd, axis=0, keepdims=True) + EPS)
    o_ref[0] = ((fwd + bwd) * nrm).astype(o_ref.dtype)
    o_ref[1] = ((fwd - bwd) * nrm).astype(o_ref.dtype)


def _hyena_filter_taps(feats, w1, b1, fr1, w2, b2, fr2, w3, absdelta, d):
    n, fp = feats.shape
    hp = w1.shape[1]
    tn = _tile(d, 512)
    nd = d // tn
    vec = pl.BlockSpec((1, hp), lambda o, j: (0, 0))
    return pl.pallas_call(
        _filter_body,
        out_shape=jax.ShapeDtypeStruct((2, n, HYENA_ORDER * d), BF16),
        grid=(HYENA_ORDER, nd),
        in_specs=[
            pl.BlockSpec((n, fp), lambda o, j: (0, 0)),
            pl.BlockSpec((fp, hp), lambda o, j: (0, 0)), vec, vec,
            pl.BlockSpec((hp, hp), lambda o, j: (0, 0)), vec, vec,
            pl.BlockSpec((hp, tn), lambda o, j: (0, (2 * o) * nd + j)),
            pl.BlockSpec((hp, tn), lambda o, j: (0, (2 * o + 1) * nd + j)),
            pl.BlockSpec((1, tn), lambda o, j: (0, j)),
        ],
        out_specs=pl.BlockSpec((2, n, tn), lambda o, j: (0, 0, o * nd + j)),
        compiler_params=_params("parallel", "parallel"),
        name="hyena_filter_taps",
    )(feats, w1, b1, fr1, w2, b2, fr2, w3, w3, absdelta)


def _butterfly(ce_ref, se_ref, co_ref, so_ref, xe, xo):
    pr = _dot(ce_ref[...], xe)
    pi = _dot(se_ref[...], xe)
    qr = _dot(co_ref[...], xo)
    qi = _dot(so_ref[...], xo)
    return pr, pi, qr, qi


def _spectrum_body(ce_ref, se_ref, co_ref, so_ref, se_, so_, de_, do_, o_ref, *, inv_n):
    i = pl.program_id(0)
    tk = ce_ref.shape[0]
    pr = _dot(ce_ref[...], se_[...])
    qr = _dot(co_ref[...], so_[...])
    pi = _dot(se_ref[...], de_[...])
    qi = _dot(so_ref[...], do_[...])
    row0 = (i * tk + lax.broadcasted_iota(jnp.int32, (tk, 1), 0)) == 0
    w_re = jnp.where(row0, inv_n, 2.0 * inv_n)
    o_ref[0] = (pr + qr) * w_re
    o_ref[1] = (pr - qr) * w_re
    o_ref[2] = (pi + qi) * (2.0 * inv_n)
    o_ref[3] = (qi - pi) * (2.0 * inv_n)

    @pl.when(i == 0)
    def _():
        rows = 2 * SUBLANES
        mid_r = _dot(se_ref[0:rows, :], se_[...])
        o_ref[2, 0:1, :] = mid_r[0:1, :] * (2.0 * inv_n)
        o_ref[3, 0:1, :] = qi[0:1, :] * (2.0 * inv_n)


def _hyena_spectrum(dft, taps):
    n = taps.shape[1]
    cols = taps.shape[2]
    half = n // 2
    tk = _tile(half, 1024)
    tn = _tile(cols, 512)
    nr = half // tk
    mat = lambda par, part: pl.BlockSpec((None, tk, half), lambda i, j: (par, part * nr + i, 0))
    tap = lambda which, par: pl.BlockSpec((None, half, tn), lambda i, j: (which, par, j))
    return pl.pallas_call(
        functools.partial(_spectrum_body, inv_n=1.0 / (2 * n)),
        out_shape=jax.ShapeDtypeStruct((4, half, cols), F32),
        grid=(nr, cols // tn),
        in_specs=[mat(0, 0), mat(0, 1), mat(1, 0), mat(1, 1), tap(0, 0), tap(0, 1), tap(1, 0), tap(1, 1)],
        out_specs=pl.BlockSpec((4, tk, tn), lambda i, j: (0, i, j)),
        compiler_params=_params("parallel", "parallel"),
        name="hyena_filter_spectrum",
    )(dft, dft, dft, dft, taps, taps, taps, taps)


def _shortconv_body(p_ref, w_ref, b_ref, o_ref):
    pe = p_ref[0].astype(F32)
    po = p_ref[1].astype(F32)
    half = pe.shape[0]
    row = lax.broadcasted_iota(jnp.int32, (half, 1), 0)
    po_prev = jnp.where(row == 0, 0.0, pltpu.roll(po, 1, 0))
    pe_next = jnp.where(row == half - 1, 0.0, pltpu.roll(pe, half - 1, 0))
    w0, w1, w2 = w_ref[0:1, :], w_ref[1:2, :], w_ref[2:3, :]
    o_ref[0] = (((b_ref[...] + po_prev * w0) + pe * w1) + po * w2).astype(o_ref.dtype)
    o_ref[1] = (((b_ref[...] + pe * w0) + po * w1) + pe_next * w2).astype(o_ref.dtype)


def _shortconv(p, w, b, batch):
    _, rows, n3 = p.shape
    half = rows // batch
    tn = _tile(n3, 512)
    return pl.pallas_call(
        _shortconv_body,
        out_shape=jax.ShapeDtypeStruct(p.shape, BF16),
        grid=(batch, n3 // tn),
        in_specs=[
            pl.BlockSpec((2, half, tn), lambda b_, j: (0, b_, j)),
            pl.BlockSpec((HYENA_SHORT_K, tn), lambda b_, j: (0, j)),
            pl.BlockSpec((1, tn), lambda b_, j: (0, j)),
        ],
        out_specs=pl.BlockSpec((2, half, tn), lambda b_, j: (0, b_, j)),
        compiler_params=_params("parallel", "parallel"),
        name="hyena_short_conv",
    )(p, w, b.reshape(1, n3))


def _dft_fwd_body(ce_ref, se_ref, co_ref, so_ref, ze_ref, zo_ref, k_ref, yr_ref, yi_ref):
    i = pl.program_id(1)
    tk = ce_ref.shape[0]
    pr, pi, qr, qi = _butterfly(ce_ref, se_ref, co_ref, so_ref, ze_ref[...], zo_ref[...])
    kr_lo, kr_hi, ki_lo, ki_hi = k_ref[0], k_ref[1], k_ref[2], k_ref[3]
    row0 = (i * tk + lax.broadcasted_iota(jnp.int32, (tk, 1), 0)) == 0
    zr_lo, zr_hi, zi_lo, zi_hi = pr + qr, pr - qr, pi + qi, qi - pi
    yr_lo = zr_lo * kr_lo - jnp.where(row0, 0.0, zi_lo * ki_lo)
    yr_hi = zr_hi * kr_hi - jnp.where(row0, 0.0, zi_hi * ki_hi)
    yi_lo = zr_lo * ki_lo + zi_lo * kr_lo
    yi_hi = zr_hi * ki_hi + zi_hi * kr_hi
    mid_r = pi * ki_lo - qi * ki_hi
    mid_i = pi * ki_hi + qi * ki_lo
    yr_ref[0] = (yr_lo + yr_hi).astype(yr_ref.dtype)
    yr_ref[1] = (yr_lo - yr_hi).astype(yr_ref.dtype)
    yi_ref[0] = jnp.where(row0, mid_r, yi_lo - yi_hi).astype(yi_ref.dtype)
    yi_ref[1] = jnp.where(row0, mid_i, yi_lo + yi_hi).astype(yi_ref.dtype)


def _dft_fwd(dft, z_src, z_col, spec, order, batch, d):
    half = z_src.shape[1] // batch
    tk = _tile(half, 1024)
    tn = _tile(d, 512)
    nr, nd = half // tk, d // tn
    out = jax.ShapeDtypeStruct((2, batch * half, d), BF16)
    once = pl.Buffered(1)
    mat = lambda par, part: pl.BlockSpec((None, tk, half), lambda j, i, b: (par, part * nr + i, 0),
                                         pipeline_mode=once)
    zin = lambda par: pl.BlockSpec((None, half, tn), lambda j, i, b: (par, b, z_col * nd + j))
    res = pl.BlockSpec((2, tk, tn), lambda j, i, b: (0, b * nr + i, j))
    return pl.pallas_call(
        _dft_fwd_body,
        out_shape=(out, out),
        grid=(nd, nr, batch),
        in_specs=[
            mat(0, 0), mat(0, 1), mat(1, 0), mat(1, 1), zin(0), zin(1),
            pl.BlockSpec((4, tk, tn), lambda j, i, b: (0, i, order * nd + j), pipeline_mode=once),
        ],
        out_specs=(res, res),
        compiler_params=_params("parallel", "parallel", "parallel"),
        name="hyena_dft_forward",
    )(dft, dft, dft, dft, z_src, z_src, spec)


def _dft_inv_body(atr_ref, ati_ref, yr_ref, yi_ref, g_ref, zp_ref, bias_ref, o_ref):
    y = _dot(atr_ref[...], yr_ref[...]) + _dot(ati_ref[...], yi_ref[...])
    zp = zp_ref[...].astype(F32)
    o_ref[...] = (g_ref[...].astype(F32) * (y + zp * bias_ref[...])).astype(o_ref.dtype)


def _dft_inv(dft_t, yr, yi, gate_src, gate_col, z_src, z_col, bias, batch, d):
    half = yr.shape[1] // batch
    tt = _tile(half, 1024)
    tn = _tile(d, 1024)
    nr, nd = half // tt, d // tn
    row = lambda cols: pl.BlockSpec((None, tt, tn), lambda j, p, i, b: (p, b * nr + i, cols * nd + j))
    return pl.pallas_call(
        _dft_inv_body,
        out_shape=jax.ShapeDtypeStruct((2, batch * half, d), BF16),
        grid=(nd, 2, nr, batch),
        in_specs=[
            pl.BlockSpec((None, tt, half), lambda j, p, i, b: (p, i, 0)),
            pl.BlockSpec((None, tt, half), lambda j, p, i, b: (p, i, 1)),
            pl.BlockSpec((None, half, tn), lambda j, p, i, b: (p, b, j)),
            pl.BlockSpec((None, half, tn), lambda j, p, i, b: (p, b, j)),
            row(gate_col), row(z_col),
            pl.BlockSpec((1, tn), lambda j, p, i, b: (0, j)),
        ],
        out_specs=row(0),
        compiler_params=_params("parallel", "parallel", "parallel", "parallel"),
        name="hyena_dft_inverse",
    )(dft_t, dft_t, yr, yi, gate_src, z_src, bias.reshape(1, d))


def _rope_tables(n_tok):
    rows = n_tok // GRID_W
    r = jnp.repeat(jnp.arange(rows), GRID_W).astype(F32)
    col = jnp.tile(jnp.arange(GRID_W), rows).astype(F32)
    half = HEAD_DIM // 2
    inv = ROPE_THETA ** (-jnp.arange(0, half, 2, dtype=F32) / half)
    ang = jnp.concatenate([r[:, None] * inv, col[:, None] * inv], axis=-1)
    cos = jnp.repeat(jnp.cos(ang), 2, axis=-1)
    sin = jnp.repeat(jnp.sin(ang), 2, axis=-1)
    sign = jnp.where(jnp.arange(HEAD_DIM) % 2 == 0, -1.0, 1.0).astype(F32)
    return cos, sin * sign


def _dft_tables(n):
    big = 2 * n
    half = n // 2
    k = jnp.arange(half, dtype=jnp.int32)[:, None]
    m = jnp.arange(half, dtype=jnp.int32)[None, :]
    alt = jnp.where(m % 2 == 0, 1.0, -1.0).astype(F32)
    mats = []
    for par in range(2):
        ang = ((k * (2 * m + par)) % big).astype(F32) * (2.0 * math.pi / big)
        s = jnp.where(k == 0, alt if par == 0 else -alt, -jnp.sin(ang))
        mats.append(jnp.concatenate([jnp.cos(ang), s], axis=0))
    a = jnp.stack(mats).astype(BF16)
    return a, a.transpose(0, 2, 1)


def _filter_features(n, width):
    t = jnp.linspace(0.0, 1.0, n, dtype=F32)[:, None]
    w = (2.0 * math.pi / n) * jnp.arange(n, dtype=F32)[:, None]
    bands = jnp.linspace(1e-4, HYENA_BANDS - 1, HYENA_BANDS, dtype=F32)
    ang = w * bands[None, :]
    feats = jnp.concatenate([t, jnp.cos(ang), -jnp.sin(ang)], axis=-1)
    feats = jnp.concatenate([feats[0::2], feats[1::2]], axis=0)
    return jnp.pad(feats, ((0, 0), (0, width - feats.shape[1])))


def _pad_to(a, shape):
    return jnp.pad(a, [(0, s - d) for d, s in zip(a.shape, shape)])


def _attn_layer(x, xc, mod_l, mod_c, g, w_in, w_out, g_q, g_k, sink, rope, batch, seq, n_ctx, ctx_out):
    d = x.shape[1]
    ones = jnp.ones((HEAD_DIM,), F32)
    gamma = jnp.concatenate([jnp.tile(g_q[0], N_HEADS), jnp.tile(g_q[1], N_HEADS),
                             jnp.tile(g_k[0], N_KV), jnp.tile(ones, N_KV),
                             jnp.tile(g_k[1], N_KV), jnp.tile(ones, N_KV)])[None, :]
    qs = jnp.concatenate([jnp.full((2 * Q_COLS,), HEAD_DIM ** -0.5, F32), jnp.ones((4 * KV_COLS,), F32)])[None, :]
    zk, ok = jnp.zeros((KV_COLS,), F32), jnp.ones((KV_COLS,), F32)
    isv = jnp.concatenate([jnp.zeros((2 * Q_COLS,), F32), zk, ok, zk, ok])[None, :]
    tables = (gamma, qs, isv)
    tm = _row_tile(batch * n_ctx, batch * n_ctx)
    ident = (jnp.ones((tm, HEAD_DIM), F32), jnp.zeros((tm, HEAD_DIM), F32))
    w_in_b = _col_tiled(w_in, _tile(w_in.shape[1], 512))
    qkv = _qkv_proj(x, mod_l, g, w_in_b, seq, tables, rope, seq)
    qkv_c = _qkv_proj(xc, mod_c, g, w_in_b, batch * n_ctx, tables, ident, None)
    qa_col, qb_col = 0, N_KV
    ka_col = 2 * Q_COLS // HEAD_DIM
    va_col, kb_col, vb_col = ka_col + N_KV, ka_col + 2 * N_KV, ka_col + 3 * N_KV
    o_a = _attention(sink, qkv, qkv_c, qkv, batch=batch, q_rows=seq, q_col=qa_col, k_col=ka_col, v_col=va_col,
                     has_mask=False, has_sink=False)
    o_b = _attention(sink, qkv, qkv_c, qkv, batch=batch, q_rows=seq, q_col=qb_col, k_col=kb_col, v_col=vb_col,
                     has_mask=True, has_sink=True)
    w_out_b = w_out.astype(BF16)
    zero_b = jnp.zeros((d,), F32)
    x = _outproj((o_a, o_b), w_out_b, zero_b, x, mod_l, seq)
    if ctx_out:
        co_a = _attention(sink, qkv_c, qkv_c, None, batch=batch, q_rows=n_ctx, q_col=qa_col, k_col=ka_col,
                          v_col=va_col, has_mask=False, has_sink=False)
        co_b = _attention(sink, qkv_c, qkv_c, None, batch=batch, q_rows=n_ctx, q_col=qb_col, k_col=kb_col,
                          v_col=vb_col, has_mask=False, has_sink=True)
        xc = _outproj((co_a, co_b), w_out_b, zero_b, xc, mod_c, batch * n_ctx)
    return x, xc


def _hyena_layer(x, mod, g, rows_per_mod, batch, w_in_b, b_in, w_conv, b_conv, filt, hy_bias, w_out_b, b_out):
    d = x.shape[1]
    n = x.shape[0] // batch
    w1, b1, fr1, w2, b2, fr2, w3 = filt
    hp = LANES
    feats = _filter_features(n, LANES)
    max_decay = math.log(HYENA_TARGET) / HYENA_FAST_PCT
    min_decay = math.log(HYENA_TARGET) / HYENA_SLOW_PCT
    absdelta = jnp.abs(jnp.linspace(min_decay, max_decay, d, dtype=F32))[None, :]
    taps = _hyena_filter_taps(
        feats, _pad_to(w1, (LANES, hp)), _pad_to(b1[None, :], (1, hp)), _pad_to(fr1[None, :], (1, hp)),
        _pad_to(w2, (hp, hp)), _pad_to(b2[None, :], (1, hp)), _pad_to(fr2[None, :], (1, hp)),
        _pad_to(w3, (hp, w3.shape[1])), absdelta, d)
    dft, dft_t = _dft_tables(n)
    spec = _hyena_spectrum(dft, taps)
    p = _proj(x, mod, g, w_in_b, b_in, rows_per_mod, n)
    pc = _shortconv(p, w_conv, b_conv, batch)
    yr, yi = _dft_fwd(dft, pc, 0, spec, 0, batch, d)
    z = _dft_inv(dft_t, yr, yi, pc, 1, pc, 0, hy_bias[0], batch, d)
    yr, yi = _dft_fwd(dft, z, 0, spec, 1, batch, d)
    z = _dft_inv(dft_t, yr, yi, pc, 2, z, 0, hy_bias[1], batch, d)
    return _outproj_parity(z, w_out_b, b_out, x, mod, rows_per_mod, n)


def kernel(x, c, ctx, c_ctx, w_mod, b_mod, g_norm, w_ffn_in, w_ffn_out, w_attn_in, w_attn_out, g_q, g_k, sink, w_hy_in, b_hy_in, w_hy_conv, b_hy_conv, hf_w1, hf_b1, hf_freq1, hf_w2, hf_b2, hf_freq2, hf_w3, hy_bias, w_hy_out, b_hy_out):
    batch, seq, d = x.shape
    n_ctx = ctx.shape[1]
    depth = w_mod.shape[0]
    rows_c = batch * n_ctx
    rope = _rope_tables(seq)
    last_ctx = max(l for l in range(depth) if l % 2 == 0)

    r_pad = -(-(batch + 1) // SUBLANES) * SUBLANES
    c_all = _pad_to(jnp.concatenate([c, c_ctx[None, :]], axis=0), (r_pad, d))
    m_all = _mod_all(c_all, w_mod, b_mod).reshape(depth, r_pad, N_MOD, d)

    w_ffn_in_b = w_ffn_in.astype(BF16)
    w_ffn_out_b = w_ffn_out.astype(BF16)

    xl = x.reshape(batch * seq, d)
    xc = ctx.reshape(rows_c, d)
    for l in range(depth):
        i = l // 2
        ctx_live = l <= last_ctx
        ctx_full = l < last_ctx
        mods_l = [m_all[l, :batch, 3 * k:3 * k + 3] for k in range(3)]
        mods_c = [m_all[l, batch:batch + 1, 3 * k:3 * k + 3] for k in range(3)]
        xl = _ffn(xl, mods_l[0], g_norm[l, 0], w_ffn_in_b, w_ffn_out_b, l, 0, seq)
        if ctx_live:
            xc = _ffn(xc, mods_c[0], g_norm[l, 0], w_ffn_in_b, w_ffn_out_b, l, 0, rows_c)
        if l % 2 == 0:
            xl, xc = _attn_layer(xl, xc, mods_l[1], mods_c[1], g_norm[l, 1], w_attn_in[i], w_attn_out[i],
                                 g_q[i], g_k[i], sink[i], rope, batch, seq, n_ctx, ctx_full)
        else:
            w_in_b = _col_tiled(w_hy_in[i], _tile(w_hy_in.shape[2], 512))
            w_out_b = w_hy_out[i].astype(BF16)
            filt = (hf_w1[i], hf_b1[i], hf_freq1[i], hf_w2[i], hf_b2[i], hf_freq2[i], hf_w3[i])
            xl = _hyena_layer(xl, mods_l[1], g_norm[l, 1], seq, batch, w_in_b, b_hy_in[i], w_hy_conv[i],
                              b_hy_conv[i], filt, hy_bias[i], w_out_b, b_hy_out[i])
            if ctx_full:
                xc = _hyena_layer(xc, mods_c[1], g_norm[l, 1], rows_c, batch, w_in_b, b_hy_in[i], w_hy_conv[i],
                                  b_hy_conv[i], filt, hy_bias[i], w_out_b, b_hy_out[i])
        xl = _ffn(xl, mods_l[2], g_norm[l, 2], w_ffn_in_b, w_ffn_out_b, l, 1, seq)
        if ctx_full:
            xc = _ffn(xc, mods_c[2], g_norm[l, 2], w_ffn_in_b, w_ffn_out_b, l, 1, rows_c)
    return xl.reshape(batch, seq, d)
```

```python
import functools
import math

import jax
import jax.numpy as jnp
from jax import lax
from jax.experimental import pallas as pl
from jax.experimental.pallas import tpu as pltpu

HEAD_DIM = 128
N_HEADS = 8
N_KV = 2
GROUP = N_HEADS // N_KV
Q_COLS = N_HEADS * HEAD_DIM
KV_COLS = N_KV * HEAD_DIM
GRID_W = 64
WINDOW = 128
ROPE_THETA = 10000.0
N_MOD = 9
HYENA_ORDER = 2
HYENA_SHORT_K = 3
HYENA_BANDS = 16
HYENA_TARGET = 1e-2
HYENA_FAST_PCT = 0.3
HYENA_SLOW_PCT = 1.5
EPS = 1e-6
NEG_INF = -1e30

LANES = 128
SUBLANES = 8
VMEM_LIMIT_BYTES = 60 * 1024 * 1024

F32 = jnp.float32
BF16 = jnp.bfloat16
HIGHEST = lax.Precision.HIGHEST


def _params(*sem):
    return pltpu.CompilerParams(dimension_semantics=sem, vmem_limit_bytes=VMEM_LIMIT_BYTES)


def _tile(dim, pref):
    t = min(dim, pref)
    while dim % t:
        t //= 2
    return t


def _dot(a, b):
    return jnp.dot(a, b, preferred_element_type=F32)


def _dot_nt(a, b):
    return lax.dot_general(a, b, (((1,), (1,)), ((), ())), preferred_element_type=F32)


def _silu(v):
    return v * (1.0 / (1.0 + jnp.exp(-v)))


def _modulated(xf, g, shift, scale):
    ms = jnp.mean(xf * xf, axis=-1, keepdims=True)
    y = xf * lax.rsqrt(ms + EPS)
    return (y * g) * (1.0 + scale) + shift


def _mod_body(c_ref, w_ref, b_ref, o_ref):
    s = _silu(c_ref[...])
    o_ref[0] = jnp.dot(s, w_ref[0], preferred_element_type=F32, precision=HIGHEST) + b_ref[0]


def _mod_all(c_all, w_mod, b_mod):
    depth, d, nd = w_mod.shape
    r = c_all.shape[0]
    tn = _tile(nd, 1024)
    return pl.pallas_call(
        _mod_body,
        out_shape=jax.ShapeDtypeStruct((depth, r, nd), F32),
        grid=(depth, nd // tn),
        in_specs=[
            pl.BlockSpec((r, d), lambda l, j: (0, 0)),
            pl.BlockSpec((1, d, tn), lambda l, j: (l, 0, j)),
            pl.BlockSpec((1, 1, tn), lambda l, j: (l, 0, j)),
        ],
        out_specs=pl.BlockSpec((1, r, tn), lambda l, j: (l, 0, j)),
        compiler_params=_params("parallel", "parallel"),
        name="mod_vectors",
    )(c_all, w_mod, b_mod.reshape(depth, 1, nd))


def _ffn_body(x_ref, mod_ref, g_ref, wg_ref, wu_ref, wo_ref, o_ref, h_ref, acc_ref):
    j = pl.program_id(1)

    @pl.when(j == 0)
    def _():
        h = _modulated(x_ref[...], g_ref[...], mod_ref[0, 0:1, :], mod_ref[0, 1:2, :])
        h_ref[...] = h.astype(BF16)
        acc_ref[...] = jnp.zeros_like(acc_ref)

    h = h_ref[...]
    a = _dot(h, wg_ref[...])
    u = _dot(h, wu_ref[...])
    act = (_silu(a) * u).astype(BF16)
    acc_ref[...] += _dot(act, wo_ref[...])

    @pl.when(j == pl.num_programs(1) - 1)
    def _():
        o_ref[...] = x_ref[...] + (0.5 * mod_ref[0, 2:3, :]) * acc_ref[...]


def _ffn(x, mod, g, w_in, w_out, layer, half, rows_per_mod):
    rows, d = x.shape
    f = w_out.shape[2]
    tm = _tile(min(rows, rows_per_mod), 1024)
    tf = _tile(f, 256)
    nf = f // tf
    return pl.pallas_call(
        _ffn_body,
        out_shape=jax.ShapeDtypeStruct((rows, d), F32),
        grid=(rows // tm, nf),
        in_specs=[
            pl.BlockSpec((tm, d), lambda i, j: (i, 0)),
            pl.BlockSpec((1, 3, d), lambda i, j: (i * tm // rows_per_mod, 0, 0)),
            pl.BlockSpec((1, d), lambda i, j: (0, 0)),
            pl.BlockSpec((None, None, d, tf), lambda i, j: (layer, half, 0, j)),
            pl.BlockSpec((None, None, d, tf), lambda i, j: (layer, half, 0, nf + j)),
            pl.BlockSpec((None, None, tf, d), lambda i, j: (layer, half, j, 0)),
        ],
        out_specs=pl.BlockSpec((tm, d), lambda i, j: (i, 0)),
        scratch_shapes=[pltpu.VMEM((tm, d), BF16), pltpu.VMEM((tm, d), F32)],
        compiler_params=_params("parallel", "arbitrary"),
        name="ffn_half_step",
    )(x, mod, g.reshape(1, d), w_in, w_in, w_out)


def _proj_body(x_ref, mod_ref, g_ref, perm_ref, w_ref, b_ref, o_ref, h_ref):
    half = x_ref.shape[0] // 2

    @pl.when(pl.program_id(1) == 0)
    def _():
        h = _modulated(x_ref[...], g_ref[...], mod_ref[0, 0:1, :], mod_ref[0, 1:2, :]).astype(BF16)
        h_ref[...] = _dot(perm_ref[...], h).astype(BF16)

    p = (_dot(h_ref[...], w_ref[...]) + b_ref[...]).astype(o_ref.dtype)
    o_ref[0] = p[:half]
    o_ref[1] = p[half:]


def _qkv_body(x_ref, mod_ref, g_ref, w_ref, gamma_ref, qs_ref, isv_ref, cos_ref, sin_ref, o_ref, h_ref, p_ref):
    i = pl.program_id(0)
    j = pl.program_id(1)

    @pl.when(j == 0)
    def _():
        h = _modulated(x_ref[...], g_ref[...], mod_ref[0, 0:1, :], mod_ref[0, 1:2, :])
        h_ref[...] = h.astype(BF16)

    @pl.when(jnp.logical_and(i == 0, j == 0))
    def _():
        p_ref[...] = jnp.zeros_like(p_ref)

    cos = cos_ref[...]
    sin = sin_ref[...]
    even = (lax.broadcasted_iota(jnp.int32, (1, HEAD_DIM), 1) % 2) == 0
    for hh in range(p_ref.shape[1] // HEAD_DIM):
        sl = slice(hh * HEAD_DIM, (hh + 1) * HEAD_DIM)
        ph = p_ref[:, sl]
        ms = jnp.mean(ph * ph, axis=-1, keepdims=True)
        nh = (ph * lax.rsqrt(ms + EPS)) * gamma_ref[:, sl]
        partner = jnp.where(even, pltpu.roll(nh, HEAD_DIM - 1, 1), pltpu.roll(nh, 1, 1))
        r = (nh * cos + partner * sin) * qs_ref[:, sl]
        o_ref[:, sl] = jnp.where(isv_ref[:, sl] > 0.0, ph, r).astype(o_ref.dtype)
    p_ref[...] = _dot(h_ref[...], w_ref[...])


def _row_tile(rows, rows_per_mod):
    return _tile(min(rows, rows_per_mod), 1024)


def _parity_perm(tm):
    r = jnp.arange(tm, dtype=jnp.int32)[:, None]
    c = jnp.arange(tm, dtype=jnp.int32)[None, :]
    src = jnp.where(r < tm // 2, 2 * r, 2 * (r - tm // 2) + 1)
    return (c == src).astype(BF16)


def _col_tiled(w, tn):
    k, n = w.shape
    return w.reshape(k, n // tn, tn).transpose(1, 0, 2).astype(BF16)


def _proj(x, mod, g, w_t, b, rows_per_mod, seq):
    rows, d = x.shape
    nj, _, tn = w_t.shape
    n = nj * tn
    tm = _tile(min(rows, rows_per_mod), 1024)
    assert tm % 2 == 0 and (tm % seq == 0 or seq % tm == 0)
    return pl.pallas_call(
        _proj_body,
        out_shape=jax.ShapeDtypeStruct((2, rows // 2, n), BF16),
        grid=(rows // tm, n // tn),
        in_specs=[
            pl.BlockSpec((tm, d), lambda i, j: (i, 0)),
            pl.BlockSpec((1, 3, d), lambda i, j: (i * tm // rows_per_mod, 0, 0)),
            pl.BlockSpec((1, d), lambda i, j: (0, 0)),
            pl.BlockSpec((tm, tm), lambda i, j: (0, 0), pipeline_mode=pl.Buffered(1)),
            pl.BlockSpec((None, d, tn), lambda i, j: (j, 0, 0)),
            pl.BlockSpec((1, tn), lambda i, j: (0, j)),
        ],
        out_specs=pl.BlockSpec((2, tm // 2, tn), lambda i, j: (0, i, j)),
        scratch_shapes=[pltpu.VMEM((tm, d), BF16)],
        compiler_params=_params("parallel", "arbitrary"),
        name="modulate_project",
    )(x, mod, g.reshape(1, d), _parity_perm(tm), w_t, b.reshape(1, n))


def _qkv_proj(x, mod, g, w_t, rows_per_mod, tables, rope, seq):
    rows, d = x.shape
    nj, _, tn = w_t.shape
    n = nj * tn
    tm = _row_tile(rows, rows_per_mod)
    gamma, qs, isv = tables
    cos, sin = rope

    def prev(j):
        return jnp.maximum(j - 1, 0)

    col = pl.BlockSpec((1, tn), lambda i, j: (0, prev(j)))
    if seq is None:
        rope_spec = pl.BlockSpec((tm, HEAD_DIM), lambda i, j: (0, 0))
    else:
        rope_spec = pl.BlockSpec((tm, HEAD_DIM), lambda i, j: (i % (seq // tm), 0))
    return pl.pallas_call(
        _qkv_body,
        out_shape=jax.ShapeDtypeStruct((rows, n), BF16),
        grid=(rows // tm, nj + 1),
        in_specs=[
            pl.BlockSpec((tm, d), lambda i, j: (i, 0)),
            pl.BlockSpec((1, 3, d), lambda i, j: (i * tm // rows_per_mod, 0, 0)),
            pl.BlockSpec((1, d), lambda i, j: (0, 0)),
            pl.BlockSpec((None, d, tn), lambda i, j: (jnp.minimum(j, nj - 1), 0, 0)),
            col, col, col, rope_spec, rope_spec,
        ],
        out_specs=pl.BlockSpec((tm, tn), lambda i, j: (i, prev(j))),
        scratch_shapes=[pltpu.VMEM((tm, d), BF16), pltpu.VMEM((tm, tn), F32)],
        compiler_params=_params("arbitrary", "arbitrary"),
        name="qkv_project",
    )(x, mod, g.reshape(1, d), w_t, gamma, qs, isv, cos, sin)


def _attn_body(*refs, has_lat, has_mask, has_sink, tq):
    sink_ref, q_ref, kc_ref, vc_ref = refs[:4]
    if has_lat:
        k_ref, v_ref, o_ref = refs[4:7]
    else:
        o_ref = refs[4]
    g = pl.program_id(1)
    qi = pl.program_id(2)
    if has_lat:
        l_rows = k_ref.shape[0]
        if has_mask:
            kw = min(l_rows, tq + 2 * WINDOW)
            k0 = pl.multiple_of(jnp.clip(qi * tq - WINDOW, 0, l_rows - kw), WINDOW)
            keys = pl.ds(k0, kw)
        else:
            k0, keys = 0, slice(None)
    for hh in range(GROUP):
        sl = slice(hh * HEAD_DIM, (hh + 1) * HEAD_DIM)
        q = q_ref[:, sl]
        s1 = _dot_nt(q, kc_ref[...])
        m = jnp.max(s1, axis=-1, keepdims=True)
        if has_lat:
            s2 = _dot_nt(q, k_ref[keys, :])
            if has_mask:
                qpos = qi * tq + lax.broadcasted_iota(jnp.int32, (tq, 1), 0)
                kpos = k0 + lax.broadcasted_iota(jnp.int32, (1, s2.shape[1]), 1)
                s2 = jnp.where(jnp.abs(kpos - qpos) <= WINDOW, s2, NEG_INF)
            m = jnp.maximum(m, jnp.max(s2, axis=-1, keepdims=True))
        if has_sink:
            sk = sink_ref[g * GROUP + hh]
            m = jnp.maximum(m, sk)
        p1 = jnp.exp(s1 - m)
        l = jnp.sum(p1, axis=-1, keepdims=True)
        o = _dot(p1.astype(BF16), vc_ref[...])
        if has_lat:
            p2 = jnp.exp(s2 - m)
            l = l + jnp.sum(p2, axis=-1, keepdims=True)
            o = o + _dot(p2.astype(BF16), v_ref[keys, :])
        if has_sink:
            l = l + jnp.exp(sk - m)
        o_ref[:, sl] = (o / l).astype(o_ref.dtype)


def _attention(sink, q_src, ctx_src, lat_src, *, batch, q_rows, q_col, k_col, v_col, has_mask, has_sink):
    tq = _tile(q_rows, 256 if has_mask else 512)
    assert not has_mask or tq % WINDOW == 0
    c_rows = ctx_src.shape[0] // batch
    gw = GROUP * HEAD_DIM
    in_specs = [
        pl.BlockSpec(memory_space=pltpu.SMEM),
        pl.BlockSpec((tq, gw), lambda b, g, i: (b * (q_rows // tq) + i, q_col + g)),
        pl.BlockSpec((c_rows, HEAD_DIM), lambda b, g, i: (b, k_col + g)),
        pl.BlockSpec((c_rows, HEAD_DIM), lambda b, g, i: (b, v_col + g)),
    ]
    args = [sink, q_src, ctx_src, ctx_src]
    has_lat = lat_src is not None
    if has_lat:
        l_rows = lat_src.shape[0] // batch
        in_specs += [
            pl.BlockSpec((l_rows, HEAD_DIM), lambda b, g, i: (b, k_col + g)),
            pl.BlockSpec((l_rows, HEAD_DIM), lambda b, g, i: (b, v_col + g)),
        ]
        args += [lat_src, lat_src]
    return pl.pallas_call(
        functools.partial(_attn_body, has_lat=has_lat, has_mask=has_mask, has_sink=has_sink, tq=tq),
        out_shape=jax.ShapeDtypeStruct((batch * q_rows, Q_COLS), BF16),
        grid=(batch, N_KV, q_rows // tq),
        in_specs=in_specs,
        out_specs=pl.BlockSpec((tq, gw), lambda b, g, i: (b * (q_rows // tq) + i, g)),
        compiler_params=_params("parallel", "parallel", "parallel"),
        name="gqa_attention",
    )(*args)


def _outproj_body(*refs, n_in):
    a_refs, w_refs = refs[:n_in], refs[n_in:2 * n_in]
    b_ref, x_ref, mod_ref, o_ref = refs[2 * n_in:]
    y = b_ref[...] + _dot(a_refs[0][...], w_refs[0][...])
    for a_ref, w_ref in zip(a_refs[1:], w_refs[1:]):
        y = y + _dot(a_ref[...], w_ref[...])
    o_ref[...] = x_ref[...] + mod_ref[0, 2:3, :] * y


def _outproj(acts, weight, bias, x, mod, rows_per_mod):
    rows, d = x.shape
    tm = _tile(min(rows, rows_per_mod), 512)
    n_in = len(acts)
    k = acts[0].shape[1]
    assert all(a.shape[1] == k for a in acts) and weight.shape[0] == n_in * k
    in_specs = [pl.BlockSpec((tm, k), lambda i: (i, 0)) for _ in acts]
    in_specs += [pl.BlockSpec((k, d), lambda i, n=n: (n, 0)) for n in range(n_in)]
    in_specs += [
        pl.BlockSpec((1, d), lambda i: (0, 0)),
        pl.BlockSpec((tm, d), lambda i: (i, 0)),
        pl.BlockSpec((1, 3, d), lambda i: (i * tm // rows_per_mod, 0, 0)),
    ]
    return pl.pallas_call(
        functools.partial(_outproj_body, n_in=n_in),
        out_shape=jax.ShapeDtypeStruct((rows, d), F32),
        grid=(rows // tm,),
        in_specs=in_specs,
        out_specs=pl.BlockSpec((tm, d), lambda i: (i, 0)),
        compiler_params=_params("parallel"),
        name="outproj_residual",
    )(*acts, *([weight] * n_in), bias.reshape(1, d), x, mod)


def _outproj_parity_body(z_ref, unperm_ref, w_ref, b_ref, x_ref, mod_ref, o_ref):
    z = jnp.concatenate([z_ref[0], z_ref[1]], axis=0)
    z = _dot(unperm_ref[...], z).astype(BF16)
    y = b_ref[...] + _dot(z, w_ref[...])
    o_ref[...] = x_ref[...] + mod_ref[0, 2:3, :] * y


def _outproj_parity(z, weight, bias, x, mod, rows_per_mod, seq):
    rows, d = x.shape
    k = z.shape[2]
    tm = _tile(min(rows, rows_per_mod), 512)
    assert tm % 2 == 0 and (tm % seq == 0 or seq % tm == 0)
    return pl.pallas_call(
        _outproj_parity_body,
        out_shape=jax.ShapeDtypeStruct((rows, d), F32),
        grid=(rows // tm,),
        in_specs=[
            pl.BlockSpec((2, tm // 2, k), lambda i: (0, i, 0)),
            pl.BlockSpec((tm, tm), lambda i: (0, 0)),
            pl.BlockSpec((k, d), lambda i: (0, 0)),
            pl.BlockSpec((1, d), lambda i: (0, 0)),
            pl.BlockSpec((tm, d), lambda i: (i, 0)),
            pl.BlockSpec((1, 3, d), lambda i: (i * tm // rows_per_mod, 0, 0)),
        ],
        out_specs=pl.BlockSpec((tm, d), lambda i: (i, 0)),
        compiler_params=_params("parallel"),
        name="outproj_residual_parity",
    )(z, _parity_perm(tm).T, weight, bias.reshape(1, d), x, mod)


def _filter_body(feats_ref, w1_ref, b1_ref, fr1_ref, w2_ref, b2_ref, fr2_ref, w3f_ref, w3b_ref, dl_ref, o_ref):
    feats = feats_ref[...]
    h1 = jnp.sin(fr1_ref[...] * (jnp.dot(feats, w1_ref[...], preferred_element_type=F32, precision=HIGHEST)
                                 + b1_ref[...]))
    h2 = jnp.sin(fr2_ref[...] * (jnp.dot(h1, w2_ref[...], preferred_element_type=F32, precision=HIGHEST)
                                 + b2_ref[...]))
    decay = jnp.exp(-feats[:, 0:1] * dl_ref[...])
    fwd = jnp.dot(h2, w3f_ref[...], preferred_element_type=F32, precision=HIGHEST) * decay
    bwd = jnp.dot(h2, w3b_ref[...], preferred_element_type=F32, precision=HIGHEST) * decay
    row = lax.broadcasted_iota(jnp.int32, (fwd.shape[0], 1), 0)
    bwd = jnp.where(row == 0, 0.0, bwd)
    nrm = lax.rsqrt(jnp.sum(fwd * fwd + bwd * bwd, axis=0, keepdims=True) + EPS)
    o_ref[0] = ((fwd + bwd) * nrm).astype(o_ref.dtype)
    o_ref[1] = ((fwd - bwd) * nrm).astype(o_ref.dtype)


def _hyena_filter_taps(feats, w1, b1, fr1, w2, b2, fr2, w3, absdelta, d):
    n, fp = feats.shape
    hp = w1.shape[1]
    tn = _tile(d, 512)
    nd = d // tn
    vec = pl.BlockSpec((1, hp), lambda o, j: (0, 0))
    return pl.pallas_call(
        _filter_body,
        out_shape=jax.ShapeDtypeStruct((2, n, HYENA_ORDER * d), BF16),
        grid=(HYENA_ORDER, nd),
        in_specs=[
            pl.BlockSpec((n, fp), lambda o, j: (0, 0)),
            pl.BlockSpec((fp, hp), lambda o, j: (0, 0)), vec, vec,
            pl.BlockSpec((hp, hp), lambda o, j: (0, 0)), vec, vec,
            pl.BlockSpec((hp, tn), lambda o, j: (0, (2 * o) * nd + j)),
            pl.BlockSpec((hp, tn), lambda o, j: (0, (2 * o + 1) * nd + j)),
            pl.BlockSpec((1, tn), lambda o, j: (0, j)),
        ],
        out_specs=pl.BlockSpec((2, n, tn), lambda o, j: (0, 0, o * nd + j)),
        compiler_params=_params("parallel", "parallel"),
        name="hyena_filter_taps",
    )(feats, w1, b1, fr1, w2, b2, fr2, w3, w3, absdelta)


def _butterfly(ce_ref, se_ref, co_ref, so_ref, xe, xo):
    pr = _dot(ce_ref[...], xe)
    pi = _dot(se_ref[...], xe)
    qr = _dot(co_ref[...], xo)
    qi = _dot(so_ref[...], xo)
    return pr, pi, qr, qi


def _spectrum_body(ce_ref, se_ref, co_ref, so_ref, se_, so_, de_, do_, o_ref, *, inv_n):
    i = pl.program_id(0)
    tk = ce_ref.shape[0]
    pr = _dot(ce_ref[...], se_[...])
    qr = _dot(co_ref[...], so_[...])
    pi = _dot(se_ref[...], de_[...])
    qi = _dot(so_ref[...], do_[...])
    row0 = (i * tk + lax.broadcasted_iota(jnp.int32, (tk, 1), 0)) == 0
    w_re = jnp.where(row0, inv_n, 2.0 * inv_n)
    o_ref[0] = (pr + qr) * w_re
    o_ref[1] = (pr - qr) * w_re
    o_ref[2] = (pi + qi) * (2.0 * inv_n)
    o_ref[3] = (qi - pi) * (2.0 * inv_n)

    @pl.when(i == 0)
    def _():
        rows = 2 * SUBLANES
        mid_r = _dot(se_ref[0:rows, :], se_[...])
        o_ref[2, 0:1, :] = mid_r[0:1, :] * (2.0 * inv_n)
        o_ref[3, 0:1, :] = qi[0:1, :] * (2.0 * inv_n)


def _hyena_spectrum(dft, taps):
    n = taps.shape[1]
    cols = taps.shape[2]
    half = n // 2
    tk = _tile(half, 1024)
    tn = _tile(cols, 512)
    nr = half // tk
    mat = lambda par, part: pl.BlockSpec((None, tk, half), lambda i, j: (par, part * nr + i, 0))
    tap = lambda which, par: pl.BlockSpec((None, half, tn), lambda i, j: (which, par, j))
    return pl.pallas_call(
        functools.partial(_spectrum_body, inv_n=1.0 / (2 * n)),
        out_shape=jax.ShapeDtypeStruct((4, half, cols), F32),
        grid=(nr, cols // tn),
        in_specs=[mat(0, 0), mat(0, 1), mat(1, 0), mat(1, 1), tap(0, 0), tap(0, 1), tap(1, 0), tap(1, 1)],
        out_specs=pl.BlockSpec((4, tk, tn), lambda i, j: (0, i, j)),
        compiler_params=_params("parallel", "parallel"),
        name="hyena_filter_spectrum",
    )(dft, dft, dft, dft, taps, taps, taps, taps)


def _shortconv_body(p_ref, w_ref, b_ref, o_ref):
    pe = p_ref[0].astype(F32)
    po = p_ref[1].astype(F32)
    half = pe.shape[0]
    row = lax.broadcasted_iota(jnp.int32, (half, 1), 0)
    po_prev = jnp.where(row == 0, 0.0, pltpu.roll(po, 1, 0))
    pe_next = jnp.where(row == half - 1, 0.0, pltpu.roll(pe, half - 1, 0))
    w0, w1, w2 = w_ref[0:1, :], w_ref[1:2, :], w_ref[2:3, :]
    o_ref[0] = (((b_ref[...] + po_prev * w0) + pe * w1) + po * w2).astype(o_ref.dtype)
    o_ref[1] = (((b_ref[...] + pe * w0) + po * w1) + pe_next * w2).astype(o_ref.dtype)


def _shortconv(p, w, b, batch):
    _, rows, n3 = p.shape
    half = rows // batch
    tn = _tile(n3, 512)
    return pl.pallas_call(
        _shortconv_body,
        out_shape=jax.ShapeDtypeStruct(p.shape, BF16),
        grid=(batch, n3 // tn),
        in_specs=[
            pl.BlockSpec((2, half, tn), lambda b_, j: (0, b_, j)),
            pl.BlockSpec((HYENA_SHORT_K, tn), lambda b_, j: (0, j)),
            pl.BlockSpec((1, tn), lambda b_, j: (0, j)),
        ],
        out_specs=pl.BlockSpec((2, half, tn), lambda b_, j: (0, b_, j)),
        compiler_params=_params("parallel", "parallel"),
        name="hyena_short_conv",
    )(p, w, b.reshape(1, n3))


def _dft_fwd_body(ce_ref, se_ref, co_ref, so_ref, ze_ref, zo_ref, k_ref, yr_ref, yi_ref):
    i = pl.program_id(1)
    tk = ce_ref.shape[0]
    pr, pi, qr, qi = _butterfly(ce_ref, se_ref, co_ref, so_ref, ze_ref[...], zo_ref[...])
    kr_lo, kr_hi, ki_lo, ki_hi = k_ref[0], k_ref[1], k_ref[2], k_ref[3]
    row0 = (i * tk + lax.broadcasted_iota(jnp.int32, (tk, 1), 0)) == 0
    zr_lo, zr_hi, zi_lo, zi_hi = pr + qr, pr - qr, pi + qi, qi - pi
    yr_lo = zr_lo * kr_lo - jnp.where(row0, 0.0, zi_lo * ki_lo)
    yr_hi = zr_hi * kr_hi - jnp.where(row0, 0.0, zi_hi * ki_hi)
    yi_lo = zr_lo * ki_lo + zi_lo * kr_lo
    yi_hi = zr_hi * ki_hi + zi_hi * kr_hi
    mid_r = pi * ki_lo - qi * ki_hi
    mid_i = pi * ki_hi + qi * ki_lo
    yr_ref[0] = (yr_lo + yr_hi).astype(yr_ref.dtype)
    yr_ref[1] = (yr_lo - yr_hi).astype(yr_ref.dtype)
    yi_ref[0] = jnp.where(row0, mid_r, yi_lo - yi_hi).astype(yi_ref.dtype)
    yi_ref[1] = jnp.where(row0, mid_i, yi_lo + yi_hi).astype(yi_ref.dtype)


def _dft_fwd(dft, z_src, z_col, spec, order, batch, d):
    half = z_src.shape[1] // batch
    tk = _tile(half, 1024)
    tn = _tile(d, 512)
    nr, nd = half // tk, d // tn
    out = jax.ShapeDtypeStruct((2, batch * half, d), BF16)
    once = pl.Buffered(1)
    mat = lambda par, part: pl.BlockSpec((None, tk, half), lambda j, i, b: (par, part * nr + i, 0),
                                         pipeline_mode=once)
    zin = lambda par: pl.BlockSpec((None, half, tn), lambda j, i, b: (par, b, z_col * nd + j))
    res = pl.BlockSpec((2, tk, tn), lambda j, i, b: (0, b * nr + i, j))
    return pl.pallas_call(
        _dft_fwd_body,
        out_shape=(out, out),
        grid=(nd, nr, batch),
        in_specs=[
            mat(0, 0), mat(0, 1), mat(1, 0), mat(1, 1), zin(0), zin(1),
            pl.BlockSpec((4, tk, tn), lambda j, i, b: (0, i, order * nd + j), pipeline_mode=once),
        ],
        out_specs=(res, res),
        compiler_params=_params("parallel", "parallel", "parallel"),
        name="hyena_dft_forward",
    )(dft, dft, dft, dft, z_src, z_src, spec)


def _dft_inv_body(atr_ref, ati_ref, yr_ref, yi_ref, g_ref, zp_ref, bias_ref, o_ref):
    y = _dot(atr_ref[...], yr_ref[...]) + _dot(ati_ref[...], yi_ref[...])
    zp = zp_ref[...].astype(F32)
    o_ref[...] = (g_ref[...].astype(F32) * (y + zp * bias_ref[...])).astype(o_ref.dtype)


def _dft_inv(dft_t, yr, yi, gate_src, gate_col, z_src, z_col, bias, batch, d):
    half = yr.shape[1] // batch
    tt = _tile(half, 1024)
    tn = _tile(d, 1024)
    nr, nd = half // tt, d // tn
    row = lambda cols: pl.BlockSpec((None, tt, tn), lambda j, p, i, b: (p, b * nr + i, cols * nd + j))
    return pl.pallas_call(
        _dft_inv_body,
        out_shape=jax.ShapeDtypeStruct((2, batch * half, d), BF16),
        grid=(nd, 2, nr, batch),
        in_specs=[
            pl.BlockSpec((None, tt, half), lambda j, p, i, b: (p, i, 0)),
            pl.BlockSpec((None, tt, half), lambda j, p, i, b: (p, i, 1)),
            pl.BlockSpec((None, half, tn), lambda j, p, i, b: (p, b, j)),
            pl.BlockSpec((None, half, tn), lambda j, p, i, b: (p, b, j)),
            row(gate_col), row(z_col),
            pl.BlockSpec((1, tn), lambda j, p, i, b: (0, j)),
        ],
        out_specs=row(0),
        compiler_params=_params("parallel", "parallel", "parallel", "parallel"),
        name="hyena_dft_inverse",
    )(dft_t, dft_t, yr, yi, gate_src, z_src, bias.reshape(1, d))


def _rope_tables(n_tok):
    rows = n_tok // GRID_W
    r = jnp.repeat(jnp.arange(rows), GRID_W).astype(F32)
    col = jnp.tile(jnp.arange(GRID_W), rows).astype(F32)
    half = HEAD_DIM // 2
    inv = ROPE_THETA ** (-jnp.arange(0, half, 2, dtype=F32) / half)
    ang = jnp.concatenate([r[:, None] * inv, col[:, None] * inv], axis=-1)
    cos = jnp.repeat(jnp.cos(ang), 2, axis=-1)
    sin = jnp.repeat(jnp.sin(ang), 2, axis=-1)
    sign = jnp.where(jnp.arange(HEAD_DIM) % 2 == 0, -1.0, 1.0).astype(F32)
    return cos, sin * sign


def _dft_tables(n):
    big = 2 * n
    half = n // 2
    k = jnp.arange(half, dtype=jnp.int32)[:, None]
    m = jnp.arange(half, dtype=jnp.int32)[None, :]
    alt = jnp.where(m % 2 == 0, 1.0, -1.0).astype(F32)
    mats = []
    for par in range(2):
        ang = ((k * (2 * m + par)) % big).astype(F32) * (2.0 * math.pi / big)
        s = jnp.where(k == 0, alt if par == 0 else -alt, -jnp.sin(ang))
        mats.append(jnp.concatenate([jnp.cos(ang), s], axis=0))
    a = jnp.stack(mats).astype(BF16)
    return a, a.transpose(0, 2, 1)


def _filter_features(n, width):
    t = jnp.linspace(0.0, 1.0, n, dtype=F32)[:, None]
    w = (2.0 * math.pi / n) * jnp.arange(n, dtype=F32)[:, None]
    bands = jnp.linspace(1e-4, HYENA_BANDS - 1, HYENA_BANDS, dtype=F32)
    ang = w * bands[None, :]
    feats = jnp.concatenate([t, jnp.cos(ang), -jnp.sin(ang)], axis=-1)
    feats = jnp.concatenate([feats[0::2], feats[1::2]], axis=0)
    return jnp.pad(feats, ((0, 0), (0, width - feats.shape[1])))


def _pad_to(a, shape):
    return jnp.pad(a, [(0, s - d) for d, s in zip(a.shape, shape)])


def _attn_layer(x, xc, mod_l, mod_c, g, w_in, w_out, g_q, g_k, sink, rope, batch, seq, n_ctx, ctx_out):
    d = x.shape[1]
    ones = jnp.ones((HEAD_DIM,), F32)
    gamma = jnp.concatenate([jnp.tile(g_q[0], N_HEADS), jnp.tile(g_q[1], N_HEADS),
                             jnp.tile(g_k[0], N_KV), jnp.tile(ones, N_KV),
                             jnp.tile(g_k[1], N_KV), jnp.tile(ones, N_KV)])[None, :]
    qs = jnp.concatenate([jnp.full((2 * Q_COLS,), HEAD_DIM ** -0.5, F32), jnp.ones((4 * KV_COLS,), F32)])[None, :]
    zk, ok = jnp.zeros((KV_COLS,), F32), jnp.ones((KV_COLS,), F32)
    isv = jnp.concatenate([jnp.zeros((2 * Q_COLS,), F32), zk, ok, zk, ok])[None, :]
    tables = (gamma, qs, isv)
    tm = _row_tile(batch * n_ctx, batch * n_ctx)
    ident = (jnp.ones((tm, HEAD_DIM), F32), jnp.zeros((tm, HEAD_DIM), F32))
    w_in_b = _col_tiled(w_in, _tile(w_in.shape[1], 512))
    qkv = _qkv_proj(x, mod_l, g, w_in_b, seq, tables, rope, seq)
    qkv_c = _qkv_proj(xc, mod_c, g, w_in_b, batch * n_ctx, tables, ident, None)
    qa_col, qb_col = 0, N_KV
    ka_col = 2 * Q_COLS // HEAD_DIM
    va_col, kb_col, vb_col = ka_col + N_KV, ka_col + 2 * N_KV, ka_col + 3 * N_KV
    o_a = _attention(sink, qkv, qkv_c, qkv, batch=batch, q_rows=seq, q_col=qa_col, k_col=ka_col, v_col=va_col,
                     has_mask=False, has_sink=False)
    o_b = _attention(sink, qkv, qkv_c, qkv, batch=batch, q_rows=seq, q_col=qb_col, k_col=kb_col, v_col=vb_col,
                     has_mask=True, has_sink=True)
    w_out_b = w_out.astype(BF16)
    zero_b = jnp.zeros((d,), F32)
    x = _outproj((o_a, o_b), w_out_b, zero_b, x, mod_l, seq)
    if ctx_out:
        co_a = _attention(sink, qkv_c, qkv_c, None, batch=batch, q_rows=n_ctx, q_col=qa_col, k_col=ka_col,
                          v_col=va_col, has_mask=False, has_sink=False)
        co_b = _attention(sink, qkv_c, qkv_c, None, batch=batch, q_rows=n_ctx, q_col=qb_col, k_col=kb_col,
                          v_col=vb_col, has_mask=False, has_sink=True)
        xc = _outproj((co_a, co_b), w_out_b, zero_b, xc, mod_c, batch * n_ctx)
    return x, xc


def _hyena_layer(x, mod, g, rows_per_mod, batch, w_in_b, b_in, w_conv, b_conv, filt, hy_bias, w_out_b, b_out):
    d = x.shape[1]
    n = x.shape[0] // batch
    w1, b1, fr1, w2, b2, fr2, w3 = filt
    hp = LANES
    feats = _filter_features(n, LANES)
    max_decay = math.log(HYENA_TARGET) / HYENA_FAST_PCT
    min_decay = math.log(HYENA_TARGET) / HYENA_SLOW_PCT
    absdelta = jnp.abs(jnp.linspace(min_decay, max_decay, d, dtype=F32))[None, :]
    taps = _hyena_filter_taps(
        feats, _pad_to(w1, (LANES, hp)), _pad_to(b1[None, :], (1, hp)), _pad_to(fr1[None, :], (1, hp)),
        _pad_to(w2, (hp, hp)), _pad_to(b2[None, :], (1, hp)), _pad_to(fr2[None, :], (1, hp)),
        _pad_to(w3, (hp, w3.shape[1])), absdelta, d)
    dft, dft_t = _dft_tables(n)
    spec = _hyena_spectrum(dft, taps)
    p = _proj(x, mod, g, w_in_b, b_in, rows_per_mod, n)
    pc = _shortconv(p, w_conv, b_conv, batch)
    yr, yi = _dft_fwd(dft, pc, 0, spec, 0, batch, d)
    z = _dft_inv(dft_t, yr, yi, pc, 1, pc, 0, hy_bias[0], batch, d)
    yr, yi = _dft_fwd(dft, z, 0, spec, 1, batch, d)
    z = _dft_inv(dft_t, yr, yi, pc, 2, z, 0, hy_bias[1], batch, d)
    return _outproj_parity(z, w_out_b, b_out, x, mod, rows_per_mod, n)


def kernel(x, c, ctx, c_ctx, w_mod, b_mod, g_norm, w_ffn_in, w_ffn_out, w_attn_in, w_attn_out, g_q, g_k, sink, w_hy_in, b_hy_in, w_hy_conv, b_hy_conv, hf_w1, hf_b1, hf_freq1, hf_w2, hf_b2, hf_freq2, hf_w3, hy_bias, w_hy_out, b_hy_out):
    batch, seq, d = x.shape
    n_ctx = ctx.shape[1]
    depth = w_mod.shape[0]
    rows_c = batch * n_ctx
    rope = _rope_tables(seq)
    last_ctx = max(l for l in range(depth) if l % 2 == 0)

    r_pad = -(-(batch + 1) // SUBLANES) * SUBLANES
    c_all = _pad_to(jnp.concatenate([c, c_ctx[None, :]], axis=0), (r_pad, d))
    m_all = _mod_all(c_all, w_mod, b_mod).reshape(depth, r_pad, N_MOD, d)

    w_ffn_in_b = w_ffn_in.astype(BF16)
    w_ffn_out_b = w_ffn_out.astype(BF16)

    xl = x.reshape(batch * seq, d)
    xc = ctx.reshape(rows_c, d)
    for l in range(depth):
        i = l // 2
        ctx_live = l <= last_ctx
        ctx_full = l < last_ctx
        mods_l = [m_all[l, :batch, 3 * k:3 * k + 3] for k in range(3)]
        mods_c = [m_all[l, batch:batch + 1, 3 * k:3 * k + 3] for k in range(3)]
        xl = _ffn(xl, mods_l[0], g_norm[l, 0], w_ffn_in_b, w_ffn_out_b, l, 0, seq)
        if ctx_live:
            xc = _ffn(xc, mods_c[0], g_norm[l, 0], w_ffn_in_b, w_ffn_out_b, l, 0, rows_c)
        if l % 2 == 0:
            xl, xc = _attn_layer(xl, xc, mods_l[1], mods_c[1], g_norm[l, 1], w_attn_in[i], w_attn_out[i],
                                 g_q[i], g_k[i], sink[i], rope, batch, seq, n_ctx, ctx_full)
        else:
            w_in_b = _col_tiled(w_hy_in[i], _tile(w_hy_in.shape[2], 512))
            w_out_b = w_hy_out[i].astype(BF16)
            filt = (hf_w1[i], hf_b1[i], hf_freq1[i], hf_w2[i], hf_b2[i], hf_freq2[i], hf_w3[i])
            xl = _hyena_layer(xl, mods_l[1], g_norm[l, 1], seq, batch, w_in_b, b_hy_in[i], w_hy_conv[i],
                              b_hy_conv[i], filt, hy_bias[i], w_out_b, b_hy_out[i])
            if ctx_full:
                xc = _hyena_layer(xc, mods_c[1], g_norm[l, 1], rows_c, batch, w_in_b, b_hy_in[i], w_hy_conv[i],
                                  b_hy_conv[i], filt, hy_bias[i], w_out_b, b_hy_out[i])
        xl = _ffn(xl, mods_l[2], g_norm[l, 2], w_ffn_in_b, w_ffn_out_b, l, 1, seq)
        if ctx_full:
            xc = _ffn(xc, mods_c[2], g_norm[l, 2], w_ffn_in_b, w_ffn_out_b, l, 1, rows_c)
    return xl.reshape(batch, seq, d)
```

```python
import functools
import math

import jax
import jax.numpy as jnp
from jax import lax
from jax.experimental import pallas as pl
from jax.experimental.pallas import tpu as pltpu

HEAD_DIM = 128
N_HEADS = 8
N_KV = 2
GROUP = N_HEADS // N_KV
Q_COLS = N_HEADS * HEAD_DIM
KV_COLS = N_KV * HEAD_DIM
GRID_W = 64
WINDOW = 128
ROPE_THETA = 10000.0
N_MOD = 9
HYENA_ORDER = 2
HYENA_SHORT_K = 3
HYENA_BANDS = 16
HYENA_TARGET = 1e-2
HYENA_FAST_PCT = 0.3
HYENA_SLOW_PCT = 1.5
EPS = 1e-6
NEG_INF = -1e30

LANES = 128
SUBLANES = 8
VMEM_LIMIT_BYTES = 60 * 1024 * 1024

F32 = jnp.float32
BF16 = jnp.bfloat16
HIGHEST = lax.Precision.HIGHEST


def _params(*sem):
    return pltpu.CompilerParams(dimension_semantics=sem, vmem_limit_bytes=VMEM_LIMIT_BYTES)


def _tile(dim, pref):
    t = min(dim, pref)
    while dim % t:
        t //= 2
    return t


def _dot(a, b):
    return jnp.dot(a, b, preferred_element_type=F32)


def _dot_nt(a, b):
    return lax.dot_general(a, b, (((1,), (1,)), ((), ())), preferred_element_type=F32)


def _silu(v):
    return v * (1.0 / (1.0 + jnp.exp(-v)))


def _modulated(xf, g, shift, scale):
    ms = jnp.mean(xf * xf, axis=-1, keepdims=True)
    y = xf * lax.rsqrt(ms + EPS)
    return (y * g) * (1.0 + scale) + shift


def _mod_body(c_ref, w_ref, b_ref, o_ref):
    s = _silu(c_ref[...])
    o_ref[0] = jnp.dot(s, w_ref[0], preferred_element_type=F32, precision=HIGHEST) + b_ref[0]


def _mod_all(c_all, w_mod, b_mod):
    depth, d, nd = w_mod.shape
    r = c_all.shape[0]
    tn = _tile(nd, 1024)
    return pl.pallas_call(
        _mod_body,
        out_shape=jax.ShapeDtypeStruct((depth, r, nd), F32),
        grid=(depth, nd // tn),
        in_specs=[
            pl.BlockSpec((r, d), lambda l, j: (0, 0)),
            pl.BlockSpec((1, d, tn), lambda l, j: (l, 0, j)),
            pl.BlockSpec((1, 1, tn), lambda l, j: (l, 0, j)),
        ],
        out_specs=pl.BlockSpec((1, r, tn), lambda l, j: (l, 0, j)),
        compiler_params=_params("parallel", "parallel"),
        name="mod_vectors",
    )(c_all, w_mod, b_mod.reshape(depth, 1, nd))


def _ffn_body(x_ref, mod_ref, g_ref, wg_ref, wu_ref, wo_ref, o_ref, h_ref, acc_ref):
    j = pl.program_id(1)
    last = pl.num_programs(1) - 1

    def chunk(h):
        a = _dot(h, wg_ref[...])
        u = _dot(h, wu_ref[...])
        return _dot((_silu(a) * u).astype(BF16), wo_ref[...])

    @pl.when(j == 0)
    def _():
        h = _modulated(x_ref[...], g_ref[...], mod_ref[0, 0:1, :], mod_ref[0, 1:2, :]).astype(BF16)
        h_ref[...] = h
        acc_ref[...] = chunk(h)

    @pl.when(jnp.logical_and(j > 0, j < last))
    def _():
        acc_ref[...] += chunk(h_ref[...])

    @pl.when(jnp.logical_and(j > 0, j == last))
    def _():
        y = acc_ref[...] + chunk(h_ref[...])
        o_ref[...] = x_ref[...] + (0.5 * mod_ref[0, 2:3, :]) * y


def _ffn(x, mod, g, w_in, w_out, layer, half, rows_per_mod):
    rows, d = x.shape
    f = w_out.shape[2]
    tm = _tile(min(rows, rows_per_mod), 1024)
    tf = _tile(f, 256)
    nf = f // tf
    assert nf >= 2
    return pl.pallas_call(
        _ffn_body,
        out_shape=jax.ShapeDtypeStruct((rows, d), F32),
        grid=(rows // tm, nf),
        in_specs=[
            pl.BlockSpec((tm, d), lambda i, j: (i, 0)),
            pl.BlockSpec((1, 3, d), lambda i, j: (i * tm // rows_per_mod, 0, 0)),
            pl.BlockSpec((1, d), lambda i, j: (0, 0)),
            pl.BlockSpec((None, None, d, tf), lambda i, j: (layer, half, 0, j)),
            pl.BlockSpec((None, None, d, tf), lambda i, j: (layer, half, 0, nf + j)),
            pl.BlockSpec((None, None, tf, d), lambda i, j: (layer, half, j, 0)),
        ],
        out_specs=pl.BlockSpec((tm, d), lambda i, j: (i, 0)),
        scratch_shapes=[pltpu.VMEM((tm, d), BF16), pltpu.VMEM((tm, d), F32)],
        compiler_params=_params("parallel", "arbitrary"),
        name="ffn_half_step",
    )(x, mod, g.reshape(1, d), w_in, w_in, w_out)


def _proj_body(x_ref, mod_ref, g_ref, perm_ref, w_ref, b_ref, o_ref, h_ref):
    half = x_ref.shape[0] // 2

    @pl.when(pl.program_id(1) == 0)
    def _():
        h = _modulated(x_ref[...], g_ref[...], mod_ref[0, 0:1, :], mod_ref[0, 1:2, :]).astype(BF16)
        h_ref[...] = _dot(perm_ref[...], h).astype(BF16)

    p = (_dot(h_ref[...], w_ref[...]) + b_ref[...]).astype(o_ref.dtype)
    o_ref[0] = p[:half]
    o_ref[1] = p[half:]


def _qkv_body(x_ref, mod_ref, g_ref, w_ref, gamma_ref, qs_ref, isv_ref, cos_ref, sin_ref, o_ref, h_ref, p_ref):
    i = pl.program_id(0)
    j = pl.program_id(1)

    @pl.when(j == 0)
    def _():
        h = _modulated(x_ref[...], g_ref[...], mod_ref[0, 0:1, :], mod_ref[0, 1:2, :])
        h_ref[...] = h.astype(BF16)

    @pl.when(jnp.logical_and(i == 0, j == 0))
    def _():
        p_ref[...] = jnp.zeros_like(p_ref)

    cos = cos_ref[...]
    sin = sin_ref[...]
    even = (lax.broadcasted_iota(jnp.int32, (1, HEAD_DIM), 1) % 2) == 0
    for hh in range(p_ref.shape[1] // HEAD_DIM):
        sl = slice(hh * HEAD_DIM, (hh + 1) * HEAD_DIM)
        ph = p_ref[:, sl]
        ms = jnp.mean(ph * ph, axis=-1, keepdims=True)
        nh = (ph * lax.rsqrt(ms + EPS)) * gamma_ref[:, sl]
        partner = jnp.where(even, pltpu.roll(nh, HEAD_DIM - 1, 1), pltpu.roll(nh, 1, 1))
        r = (nh * cos + partner * sin) * qs_ref[:, sl]
        o_ref[:, sl] = jnp.where(isv_ref[:, sl] > 0.0, ph, r).astype(o_ref.dtype)
    p_ref[...] = _dot(h_ref[...], w_ref[...])


def _row_tile(rows, rows_per_mod):
    return _tile(min(rows, rows_per_mod), 1024)


def _parity_perm(tm):
    r = jnp.arange(tm, dtype=jnp.int32)[:, None]
    c = jnp.arange(tm, dtype=jnp.int32)[None, :]
    src = jnp.where(r < tm // 2, 2 * r, 2 * (r - tm // 2) + 1)
    return (c == src).astype(BF16)


def _col_tiled(w, tn):
    k, n = w.shape
    return w.reshape(k, n // tn, tn).transpose(1, 0, 2).astype(BF16)


def _proj(x, mod, g, w_t, b, rows_per_mod, seq):
    rows, d = x.shape
    nj, _, tn = w_t.shape
    n = nj * tn
    tm = _tile(min(rows, rows_per_mod), 1024)
    assert tm % 2 == 0 and (tm % seq == 0 or seq % tm == 0)
    return pl.pallas_call(
        _proj_body,
        out_shape=jax.ShapeDtypeStruct((2, rows // 2, n), BF16),
        grid=(rows // tm, n // tn),
        in_specs=[
            pl.BlockSpec((tm, d), lambda i, j: (i, 0)),
            pl.BlockSpec((1, 3, d), lambda i, j: (i * tm // rows_per_mod, 0, 0)),
            pl.BlockSpec((1, d), lambda i, j: (0, 0)),
            pl.BlockSpec((tm, tm), lambda i, j: (0, 0), pipeline_mode=pl.Buffered(1)),
            pl.BlockSpec((None, d, tn), lambda i, j: (j, 0, 0)),
            pl.BlockSpec((1, tn), lambda i, j: (0, j)),
        ],
        out_specs=pl.BlockSpec((2, tm // 2, tn), lambda i, j: (0, i, j)),
        scratch_shapes=[pltpu.VMEM((tm, d), BF16)],
        compiler_params=_params("parallel", "arbitrary"),
        name="modulate_project",
    )(x, mod, g.reshape(1, d), _parity_perm(tm), w_t, b.reshape(1, n))


def _qkv_proj(x, mod, g, w_t, rows_per_mod, tables, rope, seq):
    rows, d = x.shape
    nj, _, tn = w_t.shape
    n = nj * tn
    tm = _row_tile(rows, rows_per_mod)
    gamma, qs, isv = tables
    cos, sin = rope

    def prev(j):
        return jnp.maximum(j - 1, 0)

    col = pl.BlockSpec((1, tn), lambda i, j: (0, prev(j)))
    if seq is None:
        rope_spec = pl.BlockSpec((tm, HEAD_DIM), lambda i, j: (0, 0))
    else:
        rope_spec = pl.BlockSpec((tm, HEAD_DIM), lambda i, j: (i % (seq // tm), 0))
    return pl.pallas_call(
        _qkv_body,
        out_shape=jax.ShapeDtypeStruct((rows, n), BF16),
        grid=(rows // tm, nj + 1),
        in_specs=[
            pl.BlockSpec((tm, d), lambda i, j: (i, 0)),
            pl.BlockSpec((1, 3, d), lambda i, j: (i * tm // rows_per_mod, 0, 0)),
            pl.BlockSpec((1, d), lambda i, j: (0, 0)),
            pl.BlockSpec((None, d, tn), lambda i, j: (jnp.minimum(j, nj - 1), 0, 0)),
            col, col, col, rope_spec, rope_spec,
        ],
        out_specs=pl.BlockSpec((tm, tn), lambda i, j: (i, prev(j))),
        scratch_shapes=[pltpu.VMEM((tm, d), BF16), pltpu.VMEM((tm, tn), F32)],
        compiler_params=_params("arbitrary", "arbitrary"),
        name="qkv_project",
    )(x, mod, g.reshape(1, d), w_t, gamma, qs, isv, cos, sin)


def _attn_body(*refs, has_lat, has_mask, has_sink, tq):
    sink_ref, q_ref, kc_ref, vc_ref = refs[:4]
    if has_lat:
        k_ref, v_ref, o_ref = refs[4:7]
    else:
        o_ref = refs[4]
    g = pl.program_id(1)
    qi = pl.program_id(2)
    if has_lat:
        l_rows = k_ref.shape[0]
        if has_mask:
            kw = min(l_rows, tq + 2 * WINDOW)
            k0 = pl.multiple_of(jnp.clip(qi * tq - WINDOW, 0, l_rows - kw), WINDOW)
            keys = pl.ds(k0, kw)
        else:
            k0, keys = 0, slice(None)
    for hh in range(GROUP):
        sl = slice(hh * HEAD_DIM, (hh + 1) * HEAD_DIM)
        q = q_ref[:, sl]
        s1 = _dot_nt(q, kc_ref[...])
        m = jnp.max(s1, axis=-1, keepdims=True)
        if has_lat:
            s2 = _dot_nt(q, k_ref[keys, :])
            if has_mask:
                qpos = qi * tq + lax.broadcasted_iota(jnp.int32, (tq, 1), 0)
                kpos = k0 + lax.broadcasted_iota(jnp.int32, (1, s2.shape[1]), 1)
                s2 = jnp.where(jnp.abs(kpos - qpos) <= WINDOW, s2, NEG_INF)
            m = jnp.maximum(m, jnp.max(s2, axis=-1, keepdims=True))
        if has_sink:
            sk = sink_ref[g * GROUP + hh]
            m = jnp.maximum(m, sk)
        p1 = jnp.exp(s1 - m)
        l = jnp.sum(p1, axis=-1, keepdims=True)
        o = _dot(p1.astype(BF16), vc_ref[...])
        if has_lat:
            p2 = jnp.exp(s2 - m)
            l = l + jnp.sum(p2, axis=-1, keepdims=True)
            o = o + _dot(p2.astype(BF16), v_ref[keys, :])
        if has_sink:
            l = l + jnp.exp(sk - m)
        o_ref[:, sl] = (o / l).astype(o_ref.dtype)


def _attention(sink, q_src, ctx_src, lat_src, *, batch, q_rows, q_col, k_col, v_col, has_mask, has_sink):
    tq = _tile(q_rows, 256 if has_mask else 512)
    assert not has_mask or tq % WINDOW == 0
    c_rows = ctx_src.shape[0] // batch
    gw = GROUP * HEAD_DIM
    in_specs = [
        pl.BlockSpec(memory_space=pltpu.SMEM),
        pl.BlockSpec((tq, gw), lambda b, g, i: (b * (q_rows // tq) + i, q_col + g)),
        pl.BlockSpec((c_rows, HEAD_DIM), lambda b, g, i: (b, k_col + g)),
        pl.BlockSpec((c_rows, HEAD_DIM), lambda b, g, i: (b, v_col + g)),
    ]
    args = [sink, q_src, ctx_src, ctx_src]
    has_lat = lat_src is not None
    if has_lat:
        l_rows = lat_src.shape[0] // batch
        in_specs += [
            pl.BlockSpec((l_rows, HEAD_DIM), lambda b, g, i: (b, k_col + g)),
            pl.BlockSpec((l_rows, HEAD_DIM), lambda b, g, i: (b, v_col + g)),
        ]
        args += [lat_src, lat_src]
    return pl.pallas_call(
        functools.partial(_attn_body, has_lat=has_lat, has_mask=has_mask, has_sink=has_sink, tq=tq),
        out_shape=jax.ShapeDtypeStruct((batch * q_rows, Q_COLS), BF16),
        grid=(batch, N_KV, q_rows // tq),
        in_specs=in_specs,
        out_specs=pl.BlockSpec((tq, gw), lambda b, g, i: (b * (q_rows // tq) + i, g)),
        compiler_params=_params("parallel", "parallel", "parallel"),
        name="gqa_attention",
    )(*args)


def _outproj_body(*refs, n_in):
    a_refs, w_refs = refs[:n_in], refs[n_in:2 * n_in]
    b_ref, x_ref, mod_ref, o_ref = refs[2 * n_in:]
    y = b_ref[...] + _dot(a_refs[0][...], w_refs[0][...])
    for a_ref, w_ref in zip(a_refs[1:], w_refs[1:]):
        y = y + _dot(a_ref[...], w_ref[...])
    o_ref[...] = x_ref[...] + mod_ref[0, 2:3, :] * y


def _outproj(acts, weight, bias, x, mod, rows_per_mod):
    rows, d = x.shape
    tm = _tile(min(rows, rows_per_mod), 512)
    n_in = len(acts)
    k = acts[0].shape[1]
    assert all(a.shape[1] == k for a in acts) and weight.shape[0] == n_in * k
    in_specs = [pl.BlockSpec((tm, k), lambda i: (i, 0)) for _ in acts]
    in_specs += [pl.BlockSpec((k, d), lambda i, n=n: (n, 0)) for n in range(n_in)]
    in_specs += [
        pl.BlockSpec((1, d), lambda i: (0, 0)),
        pl.BlockSpec((tm, d), lambda i: (i, 0)),
        pl.BlockSpec((1, 3, d), lambda i: (i * tm // rows_per_mod, 0, 0)),
    ]
    return pl.pallas_call(
        functools.partial(_outproj_body, n_in=n_in),
        out_shape=jax.ShapeDtypeStruct((rows, d), F32),
        grid=(rows // tm,),
        in_specs=in_specs,
        out_specs=pl.BlockSpec((tm, d), lambda i: (i, 0)),
        compiler_params=_params("parallel"),
        name="outproj_residual",
    )(*acts, *([weight] * n_in), bias.reshape(1, d), x, mod)


def _outproj_parity_body(z_ref, unperm_ref, w_ref, b_ref, x_ref, mod_ref, o_ref):
    z = jnp.concatenate([z_ref[0], z_ref[1]], axis=0)
    z = _dot(unperm_ref[...], z).astype(BF16)
    y = b_ref[...] + _dot(z, w_ref[...])
    o_ref[...] = x_ref[...] + mod_ref[0, 2:3, :] * y


def _outproj_parity(z, weight, bias, x, mod, rows_per_mod, seq):
    rows, d = x.shape
    k = z.shape[2]
    tm = _tile(min(rows, rows_per_mod), 512)
    assert tm % 2 == 0 and (tm % seq == 0 or seq % tm == 0)
    return pl.pallas_call(
        _outproj_parity_body,
        out_shape=jax.ShapeDtypeStruct((rows, d), F32),
        grid=(rows // tm,),
        in_specs=[
            pl.BlockSpec((2, tm // 2, k), lambda i: (0, i, 0)),
            pl.BlockSpec((tm, tm), lambda i: (0, 0)),
            pl.BlockSpec((k, d), lambda i: (0, 0)),
            pl.BlockSpec((1, d), lambda i: (0, 0)),
            pl.BlockSpec((tm, d), lambda i: (i, 0)),
            pl.BlockSpec((1, 3, d), lambda i: (i * tm // rows_per_mod, 0, 0)),
        ],
        out_specs=pl.BlockSpec((tm, d), lambda i: (i, 0)),
        compiler_params=_params("parallel"),
        name="outproj_residual_parity",
    )(z, _parity_perm(tm).T, weight, bias.reshape(1, d), x, mod)


def _filter_body(feats_ref, w1_ref, b1_ref, fr1_ref, w2_ref, b2_ref, fr2_ref, w3f_ref, w3b_ref, dl_ref, o_ref):
    feats = feats_ref[...]
    h1 = jnp.sin(fr1_ref[...] * (jnp.dot(feats, w1_ref[...], preferred_element_type=F32, precision=HIGHEST)
                                 + b1_ref[...]))
    h2 = jnp.sin(fr2_ref[...] * (jnp.dot(h1, w2_ref[...], preferred_element_type=F32, precision=HIGHEST)
                                 + b2_ref[...]))
    decay = jnp.exp(-feats[:, 0:1] * dl_ref[...])
    fwd = jnp.dot(h2, w3f_ref[...], preferred_element_type=F32, precision=HIGHEST) * decay
    bwd = jnp.dot(h2, w3b_ref[...], preferred_element_type=F32, precision=HIGHEST) * decay
    row = lax.broadcasted_iota(jnp.int32, (fwd.shape[0], 1), 0)
    bwd = jnp.where(row == 0, 0.0, bwd)
    nrm = lax.rsqrt(jnp.sum(fwd * fwd + bwd * bwd, axis=0, keepdims=True) + EPS)
    o_ref[0] = ((fwd + bwd) * nrm).astype(o_ref.dtype)
    o_ref[1] = ((fwd - bwd) * nrm).astype(o_ref.dtype)


def _hyena_filter_taps(feats, w1, b1, fr1, w2, b2, fr2, w3, absdelta, d):
    n, fp = feats.shape
    hp = w1.shape[1]
    tn = _tile(d, 512)
    nd = d // tn
    vec = pl.BlockSpec((1, hp), lambda o, j: (0, 0))
    return pl.pallas_call(
        _filter_body,
        out_shape=jax.ShapeDtypeStruct((2, n, HYENA_ORDER * d), BF16),
        grid=(HYENA_ORDER, nd),
        in_specs=[
            pl.BlockSpec((n, fp), lambda o, j: (0, 0)),
            pl.BlockSpec((fp, hp), lambda o, j: (0, 0)), vec, vec,
            pl.BlockSpec((hp, hp), lambda o, j: (0, 0)), vec, vec,
            pl.BlockSpec((hp, tn), lambda o, j: (0, (2 * o) * nd + j)),
            pl.BlockSpec((hp, tn), lambda o, j: (0, (2 * o + 1) * nd + j)),
            pl.BlockSpec((1, tn), lambda o, j: (0, j)),
        ],
        out_specs=pl.BlockSpec((2, n, tn), lambda o, j: (0, 0, o * nd + j)),
        compiler_params=_params("parallel", "parallel"),
        name="hyena_filter_taps",
    )(feats, w1, b1, fr1, w2, b2, fr2, w3, w3, absdelta)


def _butterfly(ce_ref, se_ref, co_ref, so_ref, xe, xo):
    pr = _dot(ce_ref[...], xe)
    pi = _dot(se_ref[...], xe)
    qr = _dot(co_ref[...], xo)
    qi = _dot(so_ref[...], xo)
    return pr, pi, qr, qi


def _spectrum_body(ce_ref, se_ref, co_ref, so_ref, se_, so_, de_, do_, o_ref, *, inv_n):
    i = pl.program_id(0)
    tk = ce_ref.shape[0]
    pr = _dot(ce_ref[...], se_[...])
    qr = _dot(co_ref[...], so_[...])
    pi = _dot(se_ref[...], de_[...])
    qi = _dot(so_ref[...], do_[...])
    row0 = (i * tk + lax.broadcasted_iota(jnp.int32, (tk, 1), 0)) == 0
    w_re = jnp.where(row0, inv_n, 2.0 * inv_n)
    o_ref[0] = (pr + qr) * w_re
    o_ref[1] = (pr - qr) * w_re
    o_ref[2] = (pi + qi) * (2.0 * inv_n)
    o_ref[3] = (qi - pi) * (2.0 * inv_n)

    @pl.when(i == 0)
    def _():
        rows = 2 * SUBLANES
        mid_r = _dot(se_ref[0:rows, :], se_[...])
        o_ref[2, 0:1, :] = mid_r[0:1, :] * (2.0 * inv_n)
        o_ref[3, 0:1, :] = qi[0:1, :] * (2.0 * inv_n)


def _hyena_spectrum(dft, taps):
    n = taps.shape[1]
    cols = taps.shape[2]
    half = n // 2
    tk = _tile(half, 1024)
    tn = _tile(cols, 512)
    nr = half // tk
    mat = lambda par, part: pl.BlockSpec((None, tk, half), lambda i, j: (par, part * nr + i, 0))
    tap = lambda which, par: pl.BlockSpec((None, half, tn), lambda i, j: (which, par, j))
    return pl.pallas_call(
        functools.partial(_spectrum_body, inv_n=1.0 / (2 * n)),
        out_shape=jax.ShapeDtypeStruct((4, half, cols), F32),
        grid=(nr, cols // tn),
        in_specs=[mat(0, 0), mat(0, 1), mat(1, 0), mat(1, 1), tap(0, 0), tap(0, 1), tap(1, 0), tap(1, 1)],
        out_specs=pl.BlockSpec((4, tk, tn), lambda i, j: (0, i, j)),
        compiler_params=_params("parallel", "parallel"),
        name="hyena_filter_spectrum",
    )(dft, dft, dft, dft, taps, taps, taps, taps)


def _shortconv_body(p_ref, w_ref, b_ref, o_ref):
    pe = p_ref[0].astype(F32)
    po = p_ref[1].astype(F32)
    half = pe.shape[0]
    row = lax.broadcasted_iota(jnp.int32, (half, 1), 0)
    po_prev = jnp.where(row == 0, 0.0, pltpu.roll(po, 1, 0))
    pe_next = jnp.where(row == half - 1, 0.0, pltpu.roll(pe, half - 1, 0))
    w0, w1, w2 = w_ref[0:1, :], w_ref[1:2, :], w_ref[2:3, :]
    o_ref[0] = (((b_ref[...] + po_prev * w0) + pe * w1) + po * w2).astype(o_ref.dtype)
    o_ref[1] = (((b_ref[...] + pe * w0) + po * w1) + pe_next * w2).astype(o_ref.dtype)


def _shortconv(p, w, b, batch):
    _, rows, n3 = p.shape
    half = rows // batch
    tn = _tile(n3, 512)
    return pl.pallas_call(
        _shortconv_body,
        out_shape=jax.ShapeDtypeStruct(p.shape, BF16),
        grid=(batch, n3 // tn),
        in_specs=[
            pl.BlockSpec((2, half, tn), lambda b_, j: (0, b_, j)),
            pl.BlockSpec((HYENA_SHORT_K, tn), lambda b_, j: (0, j)),
            pl.BlockSpec((1, tn), lambda b_, j: (0, j)),
        ],
        out_specs=pl.BlockSpec((2, half, tn), lambda b_, j: (0, b_, j)),
        compiler_params=_params("parallel", "parallel"),
        name="hyena_short_conv",
    )(p, w, b.reshape(1, n3))


def _dft_fwd_body(ce_ref, se_ref, co_ref, so_ref, ze_ref, zo_ref, k_ref, yr_ref, yi_ref):
    i = pl.program_id(1)
    tk = ce_ref.shape[0]
    pr, pi, qr, qi = _butterfly(ce_ref, se_ref, co_ref, so_ref, ze_ref[...], zo_ref[...])
    kr_lo, kr_hi, ki_lo, ki_hi = k_ref[0], k_ref[1], k_ref[2], k_ref[3]
    row0 = (i * tk + lax.broadcasted_iota(jnp.int32, (tk, 1), 0)) == 0
    zr_lo, zr_hi, zi_lo, zi_hi = pr + qr, pr - qr, pi + qi, qi - pi
    yr_lo = zr_lo * kr_lo - jnp.where(row0, 0.0, zi_lo * ki_lo)
    yr_hi = zr_hi * kr_hi - jnp.where(row0, 0.0, zi_hi * ki_hi)
    yi_lo = zr_lo * ki_lo + zi_lo * kr_lo
    yi_hi = zr_hi * ki_hi + zi_hi * kr_hi
    mid_r = pi * ki_lo - qi * ki_hi
    mid_i = pi * ki_hi + qi * ki_lo
    yr_ref[0] = (yr_lo + yr_hi).astype(yr_ref.dtype)
    yr_ref[1] = (yr_lo - yr_hi).astype(yr_ref.dtype)
    yi_ref[0] = jnp.where(row0, mid_r, yi_lo - yi_hi).astype(yi_ref.dtype)
    yi_ref[1] = jnp.where(row0, mid_i, yi_lo + yi_hi).astype(yi_ref.dtype)


def _dft_fwd(dft, z_src, z_col, spec, order, batch, d):
    half = z_src.shape[1] // batch
    tk = _tile(half, 1024)
    tn = _tile(d, 512)
    nr, nd = half // tk, d // tn
    out = jax.ShapeDtypeStruct((2, batch * half, d), BF16)
    once = pl.Buffered(1)
    mat = lambda par, part: pl.BlockSpec((None, tk, half), lambda j, i, b: (par, part * nr + i, 0),
                                         pipeline_mode=once)
    zin = lambda par: pl.BlockSpec((None, half, tn), lambda j, i, b: (par, b, z_col * nd + j))
    res = pl.BlockSpec((2, tk, tn), lambda j, i, b: (0, b * nr + i, j))
    return pl.pallas_call(
        _dft_fwd_body,
        out_shape=(out, out),
        grid=(nd, nr, batch),
        in_specs=[
            mat(0, 0), mat(0, 1), mat(1, 0), mat(1, 1), zin(0), zin(1),
            pl.BlockSpec((4, tk, tn), lambda j, i, b: (0, i, order * nd + j), pipeline_mode=once),
        ],
        out_specs=(res, res),
        compiler_params=_params("parallel", "parallel", "parallel"),
        name="hyena_dft_forward",
    )(dft, dft, dft, dft, z_src, z_src, spec)


def _dft_inv_body(atr_ref, ati_ref, yr_ref, yi_ref, g_ref, zp_ref, bias_ref, o_ref):
    y = _dot(atr_ref[...], yr_ref[...]) + _dot(ati_ref[...], yi_ref[...])
    zp = zp_ref[...].astype(F32)
    o_ref[...] = (g_ref[...].astype(F32) * (y + zp * bias_ref[...])).astype(o_ref.dtype)


def _dft_inv(dft_t, yr, yi, gate_src, gate_col, z_src, z_col, bias, batch, d):
    half = yr.shape[1] // batch
    tt = _tile(half, 1024)
    tn = _tile(d, 1024)
    nr, nd = half // tt, d // tn
    row = lambda cols: pl.BlockSpec((None, tt, tn), lambda j, p, i, b: (p, b * nr + i, cols * nd + j))
    return pl.pallas_call(
        _dft_inv_body,
        out_shape=jax.ShapeDtypeStruct((2, batch * half, d), BF16),
        grid=(nd, 2, nr, batch),
        in_specs=[
            pl.BlockSpec((None, tt, half), lambda j, p, i, b: (p, i, 0)),
            pl.BlockSpec((None, tt, half), lambda j, p, i, b: (p, i, 1)),
            pl.BlockSpec((None, half, tn), lambda j, p, i, b: (p, b, j)),
            pl.BlockSpec((None, half, tn), lambda j, p, i, b: (p, b, j)),
            row(gate_col), row(z_col),
            pl.BlockSpec((1, tn), lambda j, p, i, b: (0, j)),
        ],
        out_specs=row(0),
        compiler_params=_params("parallel", "parallel", "parallel", "parallel"),
        name="hyena_dft_inverse",
    )(dft_t, dft_t, yr, yi, gate_src, z_src, bias.reshape(1, d))


def _rope_tables(n_tok):
    rows = n_tok // GRID_W
    r = jnp.repeat(jnp.arange(rows), GRID_W).astype(F32)
    col = jnp.tile(jnp.arange(GRID_W), rows).astype(F32)
    half = HEAD_DIM // 2
    inv = ROPE_THETA ** (-jnp.arange(0, half, 2, dtype=F32) / half)
    ang = jnp.concatenate([r[:, None] * inv, col[:, None] * inv], axis=-1)
    cos = jnp.repeat(jnp.cos(ang), 2, axis=-1)
    sin = jnp.repeat(jnp.sin(ang), 2, axis=-1)
    sign = jnp.where(jnp.arange(HEAD_DIM) % 2 == 0, -1.0, 1.0).astype(F32)
    return cos, sin * sign


def _dft_tables(n):
    big = 2 * n
    half = n // 2
    k = jnp.arange(half, dtype=jnp.int32)[:, None]
    m = jnp.arange(half, dtype=jnp.int32)[None, :]
    alt = jnp.where(m % 2 == 0, 1.0, -1.0).astype(F32)
    mats = []
    for par in range(2):
        ang = ((k * (2 * m + par)) % big).astype(F32) * (2.0 * math.pi / big)
        s = jnp.where(k == 0, alt if par == 0 else -alt, -jnp.sin(ang))
        mats.append(jnp.concatenate([jnp.cos(ang), s], axis=0))
    a = jnp.stack(mats).astype(BF16)
    return a, a.transpose(0, 2, 1)


def _filter_features(n, width):
    t = jnp.linspace(0.0, 1.0, n, dtype=F32)[:, None]
    w = (2.0 * math.pi / n) * jnp.arange(n, dtype=F32)[:, None]
    bands = jnp.linspace(1e-4, HYENA_BANDS - 1, HYENA_BANDS, dtype=F32)
    ang = w * bands[None, :]
    feats = jnp.concatenate([t, jnp.cos(ang), -jnp.sin(ang)], axis=-1)
    feats = jnp.concatenate([feats[0::2], feats[1::2]], axis=0)
    return jnp.pad(feats, ((0, 0), (0, width - feats.shape[1])))


def _pad_to(a, shape):
    return jnp.pad(a, [(0, s - d) for d, s in zip(a.shape, shape)])


def _attn_layer(x, xc, mod_l, mod_c, g, w_in, w_out, g_q, g_k, sink, rope, batch, seq, n_ctx, ctx_out):
    d = x.shape[1]
    ones = jnp.ones((HEAD_DIM,), F32)
    gamma = jnp.concatenate([jnp.tile(g_q[0], N_HEADS), jnp.tile(g_q[1], N_HEADS),
                             jnp.tile(g_k[0], N_KV), jnp.tile(ones, N_KV),
                             jnp.tile(g_k[1], N_KV), jnp.tile(ones, N_KV)])[None, :]
    qs = jnp.concatenate([jnp.full((2 * Q_COLS,), HEAD_DIM ** -0.5, F32), jnp.ones((4 * KV_COLS,), F32)])[None, :]
    zk, ok = jnp.zeros((KV_COLS,), F32), jnp.ones((KV_COLS,), F32)
    isv = jnp.concatenate([jnp.zeros((2 * Q_COLS,), F32), zk, ok, zk, ok])[None, :]
    tables = (gamma, qs, isv)
    tm = _row_tile(batch * n_ctx, batch * n_ctx)
    ident = (jnp.ones((tm, HEAD_DIM), F32), jnp.zeros((tm, HEAD_DIM), F32))
    w_in_b = _col_tiled(w_in, _tile(w_in.shape[1], 512))
    qkv = _qkv_proj(x, mod_l, g, w_in_b, seq, tables, rope, seq)
    qkv_c = _qkv_proj(xc, mod_c, g, w_in_b, batch * n_ctx, tables, ident, None)
    qa_col, qb_col = 0, N_KV
    ka_col = 2 * Q_COLS // HEAD_DIM
    va_col, kb_col, vb_col = ka_col + N_KV, ka_col + 2 * N_KV, ka_col + 3 * N_KV
    o_a = _attention(sink, qkv, qkv_c, qkv, batch=batch, q_rows=seq, q_col=qa_col, k_col=ka_col, v_col=va_col,
                     has_mask=False, has_sink=False)
    o_b = _attention(sink, qkv, qkv_c, qkv, batch=batch, q_rows=seq, q_col=qb_col, k_col=kb_col, v_col=vb_col,
                     has_mask=True, has_sink=True)
    w_out_b = w_out.astype(BF16)
    zero_b = jnp.zeros((d,), F32)
    x = _outproj((o_a, o_b), w_out_b, zero_b, x, mod_l, seq)
    if ctx_out:
        co_a = _attention(sink, qkv_c, qkv_c, None, batch=batch, q_rows=n_ctx, q_col=qa_col, k_col=ka_col,
                          v_col=va_col, has_mask=False, has_sink=False)
        co_b = _attention(sink, qkv_c, qkv_c, None, batch=batch, q_rows=n_ctx, q_col=qb_col, k_col=kb_col,
                          v_col=vb_col, has_mask=False, has_sink=True)
        xc = _outproj((co_a, co_b), w_out_b, zero_b, xc, mod_c, batch * n_ctx)
    return x, xc


def _hyena_layer(x, mod, g, rows_per_mod, batch, w_in_b, b_in, w_conv, b_conv, filt, hy_bias, w_out_b, b_out):
    d = x.shape[1]
    n = x.shape[0] // batch
    w1, b1, fr1, w2, b2, fr2, w3 = filt
    hp = LANES
    feats = _filter_features(n, LANES)
    max_decay = math.log(HYENA_TARGET) / HYENA_FAST_PCT
    min_decay = math.log(HYENA_TARGET) / HYENA_SLOW_PCT
    absdelta = jnp.abs(jnp.linspace(min_decay, max_decay, d, dtype=F32))[None, :]
    taps = _hyena_filter_taps(
        feats, _pad_to(w1, (LANES, hp)), _pad_to(b1[None, :], (1, hp)), _pad_to(fr1[None, :], (1, hp)),
        _pad_to(w2, (hp, hp)), _pad_to(b2[None, :], (1, hp)), _pad_to(fr2[None, :], (1, hp)),
        _pad_to(w3, (hp, w3.shape[1])), absdelta, d)
    dft, dft_t = _dft_tables(n)
    spec = _hyena_spectrum(dft, taps)
    p = _proj(x, mod, g, w_in_b, b_in, rows_per_mod, n)
    pc = _shortconv(p, w_conv, b_conv, batch)
    yr, yi = _dft_fwd(dft, pc, 0, spec, 0, batch, d)
    z = _dft_inv(dft_t, yr, yi, pc, 1, pc, 0, hy_bias[0], batch, d)
    yr, yi = _dft_fwd(dft, z, 0, spec, 1, batch, d)
    z = _dft_inv(dft_t, yr, yi, pc, 2, z, 0, hy_bias[1], batch, d)
    return _outproj_parity(z, w_out_b, b_out, x, mod, rows_per_mod, n)


def kernel(x, c, ctx, c_ctx, w_mod, b_mod, g_norm, w_ffn_in, w_ffn_out, w_attn_in, w_attn_out, g_q, g_k, sink, w_hy_in, b_hy_in, w_hy_conv, b_hy_conv, hf_w1, hf_b1, hf_freq1, hf_w2, hf_b2, hf_freq2, hf_w3, hy_bias, w_hy_out, b_hy_out):
    batch, seq, d = x.shape
    n_ctx = ctx.shape[1]
    depth = w_mod.shape[0]
    rows_c = batch * n_ctx
    rope = _rope_tables(seq)
    last_ctx = max(l for l in range(depth) if l % 2 == 0)

    r_pad = -(-(batch + 1) // SUBLANES) * SUBLANES
    c_all = _pad_to(jnp.concatenate([c, c_ctx[None, :]], axis=0), (r_pad, d))
    m_all = _mod_all(c_all, w_mod, b_mod).reshape(depth, r_pad, N_MOD, d)

    w_ffn_in_b = w_ffn_in.astype(BF16)
    w_ffn_out_b = w_ffn_out.astype(BF16)

    xl = x.reshape(batch * seq, d)
    xc = ctx.reshape(rows_c, d)
    for l in range(depth):
        i = l // 2
        ctx_live = l <= last_ctx
        ctx_full = l < last_ctx
        mods_l = [m_all[l, :batch, 3 * k:3 * k + 3] for k in range(3)]
        mods_c = [m_all[l, batch:batch + 1, 3 * k:3 * k + 3] for k in range(3)]
        xl = _ffn(xl, mods_l[0], g_norm[l, 0], w_ffn_in_b, w_ffn_out_b, l, 0, seq)
        if ctx_live:
            xc = _ffn(xc, mods_c[0], g_norm[l, 0], w_ffn_in_b, w_ffn_out_b, l, 0, rows_c)
        if l % 2 == 0:
            xl, xc = _attn_layer(xl, xc, mods_l[1], mods_c[1], g_norm[l, 1], w_attn_in[i], w_attn_out[i],
                                 g_q[i], g_k[i], sink[i], rope, batch, seq, n_ctx, ctx_full)
        else:
            w_in_b = _col_tiled(w_hy_in[i], _tile(w_hy_in.shape[2], 512))
            w_out_b = w_hy_out[i].astype(BF16)
            filt = (hf_w1[i], hf_b1[i], hf_freq1[i], hf_w2[i], hf_b2[i], hf_freq2[i], hf_w3[i])
            xl = _hyena_layer(xl, mods_l[1], g_norm[l, 1], seq, batch, w_in_b, b_hy_in[i], w_hy_conv[i],
                              b_hy_conv[i], filt, hy_bias[i], w_out_b, b_hy_out[i])
            if ctx_full:
                xc = _hyena_layer(xc, mods_c[1], g_norm[l, 1], rows_c, batch, w_in_b, b_hy_in[i], w_hy_conv[i],
                                  b_hy_conv[i], filt, hy_bias[i], w_out_b, b_hy_out[i])
        xl = _ffn(xl, mods_l[2], g_norm[l, 2], w_ffn_in_b, w_ffn_out_b, l, 1, seq)
        if ctx_full:
            xc = _ffn(xc, mods_c[2], g_norm[l, 2], w_ffn_in_b, w_ffn_out_b, l, 1, rows_c)
    return xl.reshape(batch, seq, d)
```

```python
import functools
import math

import jax
import jax.numpy as jnp
from jax import lax
from jax.experimental import pallas as pl
from jax.experimental.pallas import tpu as pltpu

HEAD_DIM = 128
N_HEADS = 8
N_KV = 2
GROUP = N_HEADS // N_KV
Q_COLS = N_HEADS * HEAD_DIM
KV_COLS = N_KV * HEAD_DIM
GRID_W = 64
WINDOW = 128
ROPE_THETA = 10000.0
N_MOD = 9
HYENA_ORDER = 2
HYENA_SHORT_K = 3
HYENA_BANDS = 16
HYENA_TARGET = 1e-2
HYENA_FAST_PCT = 0.3
HYENA_SLOW_PCT = 1.5
EPS = 1e-6
NEG_INF = -1e30

LANES = 128
SUBLANES = 8
VMEM_LIMIT_BYTES = 60 * 1024 * 1024

F32 = jnp.float32
BF16 = jnp.bfloat16
HIGHEST = lax.Precision.HIGHEST


def _params(*sem):
    return pltpu.CompilerParams(dimension_semantics=sem, vmem_limit_bytes=VMEM_LIMIT_BYTES)


def _tile(dim, pref):
    t = min(dim, pref)
    while dim % t:
        t //= 2
    return t


def _dot(a, b):
    return jnp.dot(a, b, preferred_element_type=F32)


def _dot_nt(a, b):
    return lax.dot_general(a, b, (((1,), (1,)), ((), ())), preferred_element_type=F32)


def _silu(v):
    return v * (1.0 / (1.0 + jnp.exp(-v)))


def _modulated(xf, g, shift, scale):
    ms = jnp.mean(xf * xf, axis=-1, keepdims=True)
    y = xf * lax.rsqrt(ms + EPS)
    return (y * g) * (1.0 + scale) + shift


def _mod_body(c_ref, w_ref, b_ref, o_ref):
    s = _silu(c_ref[...])
    o_ref[0] = jnp.dot(s, w_ref[0], preferred_element_type=F32, precision=HIGHEST) + b_ref[0]


def _mod_all(c_all, w_mod, b_mod):
    depth, d, nd = w_mod.shape
    r = c_all.shape[0]
    tn = _tile(nd, 1024)
    return pl.pallas_call(
        _mod_body,
        out_shape=jax.ShapeDtypeStruct((depth, r, nd), F32),
        grid=(depth, nd // tn),
        in_specs=[
            pl.BlockSpec((r, d), lambda l, j: (0, 0)),
            pl.BlockSpec((1, d, tn), lambda l, j: (l, 0, j)),
            pl.BlockSpec((1, 1, tn), lambda l, j: (l, 0, j)),
        ],
        out_specs=pl.BlockSpec((1, r, tn), lambda l, j: (l, 0, j)),
        compiler_params=_params("parallel", "parallel"),
        name="mod_vectors",
    )(c_all, w_mod, b_mod.reshape(depth, 1, nd))


def _ffn_body(x_ref, mod_ref, g_ref, wg_ref, wu_ref, wo_ref, o_ref, h_ref, acc_ref):
    j = pl.program_id(1)
    last = pl.num_programs(1) - 1

    def chunk(h):
        a = _dot(h, wg_ref[...])
        u = _dot(h, wu_ref[...])
        return _dot((_silu(a) * u).astype(BF16), wo_ref[...])

    @pl.when(j == 0)
    def _():
        h = _modulated(x_ref[...], g_ref[...], mod_ref[0, 0:1, :], mod_ref[0, 1:2, :]).astype(BF16)
        h_ref[...] = h
        acc_ref[...] = chunk(h)

    @pl.when(jnp.logical_and(j > 0, j < last))
    def _():
        acc_ref[...] += chunk(h_ref[...])

    @pl.when(jnp.logical_and(j > 0, j == last))
    def _():
        y = acc_ref[...] + chunk(h_ref[...])
        o_ref[...] = x_ref[...] + (0.5 * mod_ref[0, 2:3, :]) * y


def _ffn(x, mod, g, w_in, w_out, layer, half, rows_per_mod):
    rows, d = x.shape
    f = w_out.shape[2]
    tm = _tile(min(rows, rows_per_mod), 1024)
    tf = _tile(f, 256)
    nf = f // tf
    assert nf >= 2
    return pl.pallas_call(
        _ffn_body,
        out_shape=jax.ShapeDtypeStruct((rows, d), F32),
        grid=(rows // tm, nf),
        in_specs=[
            pl.BlockSpec((tm, d), lambda i, j: (i, 0)),
            pl.BlockSpec((1, 3, d), lambda i, j: (i * tm // rows_per_mod, 0, 0)),
            pl.BlockSpec((1, d), lambda i, j: (0, 0)),
            pl.BlockSpec((None, None, d, tf), lambda i, j: (layer, half, 0, j)),
            pl.BlockSpec((None, None, d, tf), lambda i, j: (layer, half, 0, nf + j)),
            pl.BlockSpec((None, None, tf, d), lambda i, j: (layer, half, j, 0)),
        ],
        out_specs=pl.BlockSpec((tm, d), lambda i, j: (i, 0)),
        scratch_shapes=[pltpu.VMEM((tm, d), BF16), pltpu.VMEM((tm, d), F32)],
        compiler_params=_params("parallel", "arbitrary"),
        name="ffn_half_step",
    )(x, mod, g.reshape(1, d), w_in, w_in, w_out)


def _proj_body(x_ref, mod_ref, g_ref, perm_ref, w_ref, b_ref, cw_ref, cb_ref, o_ref, h_ref, *, half_seq):
    tm = x_ref.shape[0]
    half = tm // 2
    grp = perm_ref.shape[0]

    @pl.when(pl.program_id(1) == 0)
    def _():
        for r in range(tm // grp):
            h = _modulated(x_ref[r * grp:(r + 1) * grp, :], g_ref[...], mod_ref[0, 0:1, :], mod_ref[0, 1:2, :])
            hp = _dot(perm_ref[...], h.astype(BF16)).astype(BF16)
            h_ref[r * (grp // 2):(r + 1) * (grp // 2), :] = hp[:grp // 2]
            h_ref[half + r * (grp // 2):half + (r + 1) * (grp // 2), :] = hp[grp // 2:]

    p = _dot(h_ref[...], w_ref[...]) + b_ref[...]
    pe, po = p[:half], p[half:]
    pos = lax.broadcasted_iota(jnp.int32, (half, 1), 0) % half_seq
    po_prev = jnp.where(pos == 0, 0.0, pltpu.roll(po, 1, 0))
    pe_next = jnp.where(pos == half_seq - 1, 0.0, pltpu.roll(pe, half - 1, 0))
    w0, w1, w2 = cw_ref[0:1, :], cw_ref[1:2, :], cw_ref[2:3, :]
    o_ref[0] = (((cb_ref[...] + po_prev * w0) + pe * w1) + po * w2).astype(o_ref.dtype)
    o_ref[1] = (((cb_ref[...] + pe * w0) + po * w1) + pe_next * w2).astype(o_ref.dtype)


def _qkv_body(x_ref, mod_ref, g_ref, w_ref, gamma_ref, qs_ref, isv_ref, cos_ref, sin_ref, o_ref, h_ref, p_ref):
    j = pl.program_id(1)
    last = pl.num_programs(1) - 1

    def finish():
        cos = cos_ref[...]
        sin = sin_ref[...]
        even = (lax.broadcasted_iota(jnp.int32, (1, HEAD_DIM), 1) % 2) == 0
        for hh in range(p_ref.shape[1] // HEAD_DIM):
            sl = slice(hh * HEAD_DIM, (hh + 1) * HEAD_DIM)
            ph = p_ref[:, sl]
            ms = jnp.mean(ph * ph, axis=-1, keepdims=True)
            nh = (ph * lax.rsqrt(ms + EPS)) * gamma_ref[:, sl]
            partner = jnp.where(even, pltpu.roll(nh, HEAD_DIM - 1, 1), pltpu.roll(nh, 1, 1))
            r = (nh * cos + partner * sin) * qs_ref[:, sl]
            o_ref[:, sl] = jnp.where(isv_ref[:, sl] > 0.0, ph, r).astype(o_ref.dtype)

    @pl.when(j == 0)
    def _():
        h = _modulated(x_ref[...], g_ref[...], mod_ref[0, 0:1, :], mod_ref[0, 1:2, :]).astype(BF16)
        h_ref[...] = h
        p_ref[...] = _dot(h, w_ref[...])

    @pl.when(jnp.logical_and(j > 0, j < last))
    def _():
        finish()
        p_ref[...] = _dot(h_ref[...], w_ref[...])

    @pl.when(j == last)
    def _():
        finish()


def _row_tile(rows, rows_per_mod):
    return _tile(min(rows, rows_per_mod), 1024)


def _parity_perm(tm):
    r = jnp.arange(tm, dtype=jnp.int32)[:, None]
    c = jnp.arange(tm, dtype=jnp.int32)[None, :]
    src = jnp.where(r < tm // 2, 2 * r, 2 * (r - tm // 2) + 1)
    return (c == src).astype(BF16)


def _col_tiled(w, tn):
    k, n = w.shape
    return w.reshape(k, n // tn, tn).transpose(1, 0, 2).astype(BF16)


PERM_GROUP = 256


def _proj(x, mod, g, w_t, b, conv_w, conv_b, rows_per_mod, seq):
    rows, d = x.shape
    nj, _, tn = w_t.shape
    n = nj * tn
    tm = _tile(min(rows, rows_per_mod), max(seq, 1024))
    grp = min(PERM_GROUP, tm)
    assert tm % seq == 0 and tm % grp == 0 and grp % (4 * SUBLANES) == 0
    col = pl.BlockSpec((1, tn), lambda i, j: (0, j))
    return pl.pallas_call(
        functools.partial(_proj_body, half_seq=seq // 2),
        out_shape=jax.ShapeDtypeStruct((2, rows // 2, n), BF16),
        grid=(rows // tm, n // tn),
        in_specs=[
            pl.BlockSpec((tm, d), lambda i, j: (i, 0)),
            pl.BlockSpec((1, 3, d), lambda i, j: (i * tm // rows_per_mod, 0, 0)),
            pl.BlockSpec((1, d), lambda i, j: (0, 0)),
            pl.BlockSpec((grp, grp), lambda i, j: (0, 0)),
            pl.BlockSpec((None, d, tn), lambda i, j: (j, 0, 0)),
            col,
            pl.BlockSpec((HYENA_SHORT_K, tn), lambda i, j: (0, j)),
            col,
        ],
        out_specs=pl.BlockSpec((2, tm // 2, tn), lambda i, j: (0, i, j)),
        scratch_shapes=[pltpu.VMEM((tm, d), BF16)],
        compiler_params=_params("parallel", "arbitrary"),
        name="hyena_project_conv",
    )(x, mod, g.reshape(1, d), _parity_perm(grp), w_t, b.reshape(1, n), conv_w, conv_b.reshape(1, n))


def _qkv_proj(x, mod, g, w_t, rows_per_mod, tables, rope, seq):
    rows, d = x.shape
    nj, _, tn = w_t.shape
    n = nj * tn
    tm = _row_tile(rows, rows_per_mod)
    gamma, qs, isv = tables
    cos, sin = rope

    def prev(j):
        return jnp.maximum(j - 1, 0)

    col = pl.BlockSpec((1, tn), lambda i, j: (0, prev(j)))
    if seq is None:
        rope_spec = pl.BlockSpec((tm, HEAD_DIM), lambda i, j: (0, 0))
    else:
        rope_spec = pl.BlockSpec((tm, HEAD_DIM), lambda i, j: (i % (seq // tm), 0))
    return pl.pallas_call(
        _qkv_body,
        out_shape=jax.ShapeDtypeStruct((rows, n), BF16),
        grid=(rows // tm, nj + 1),
        in_specs=[
            pl.BlockSpec((tm, d), lambda i, j: (i, 0)),
            pl.BlockSpec((1, 3, d), lambda i, j: (i * tm // rows_per_mod, 0, 0)),
            pl.BlockSpec((1, d), lambda i, j: (0, 0)),
            pl.BlockSpec((None, d, tn), lambda i, j: (jnp.minimum(j, nj - 1), 0, 0)),
            col, col, col, rope_spec, rope_spec,
        ],
        out_specs=pl.BlockSpec((tm, tn), lambda i, j: (i, prev(j))),
        scratch_shapes=[pltpu.VMEM((tm, d), BF16), pltpu.VMEM((tm, tn), F32)],
        compiler_params=_params("arbitrary", "arbitrary"),
        name="qkv_project",
    )(x, mod, g.reshape(1, d), w_t, gamma, qs, isv, cos, sin)


def _attn_body(*refs, has_lat, has_mask, has_sink, tq):
    sink_ref, q_ref, kc_ref, vc_ref = refs[:4]
    if has_lat:
        k_ref, v_ref, o_ref = refs[4:7]
    else:
        o_ref = refs[4]
    g = pl.program_id(1)
    qi = pl.program_id(2)
    if has_lat:
        l_rows = k_ref.shape[0]
        if has_mask:
            kw = min(l_rows, tq + 2 * WINDOW)
            k0 = pl.multiple_of(jnp.clip(qi * tq - WINDOW, 0, l_rows - kw), WINDOW)
            keys = pl.ds(k0, kw)
        else:
            k0, keys = 0, slice(None)
    for hh in range(GROUP):
        sl = slice(hh * HEAD_DIM, (hh + 1) * HEAD_DIM)
        q = q_ref[:, sl]
        s1 = _dot_nt(q, kc_ref[...])
        m = jnp.max(s1, axis=-1, keepdims=True)
        if has_lat:
            s2 = _dot_nt(q, k_ref[keys, :])
            if has_mask:
                qpos = qi * tq + lax.broadcasted_iota(jnp.int32, (tq, 1), 0)
                kpos = k0 + lax.broadcasted_iota(jnp.int32, (1, s2.shape[1]), 1)
                s2 = jnp.where(jnp.abs(kpos - qpos) <= WINDOW, s2, NEG_INF)
            m = jnp.maximum(m, jnp.max(s2, axis=-1, keepdims=True))
        if has_sink:
            sk = sink_ref[g * GROUP + hh]
            m = jnp.maximum(m, sk)
        p1 = jnp.exp(s1 - m)
        l = jnp.sum(p1, axis=-1, keepdims=True)
        o = _dot(p1.astype(BF16), vc_ref[...])
        if has_lat:
            p2 = jnp.exp(s2 - m)
            l = l + jnp.sum(p2, axis=-1, keepdims=True)
            o = o + _dot(p2.astype(BF16), v_ref[keys, :])
        if has_sink:
            l = l + jnp.exp(sk - m)
        o_ref[:, sl] = (o / l).astype(o_ref.dtype)


def _attention(sink, q_src, ctx_src, lat_src, *, batch, q_rows, q_col, k_col, v_col, has_mask, has_sink):
    tq = _tile(q_rows, 256 if has_mask else 512)
    assert not has_mask or tq % WINDOW == 0
    c_rows = ctx_src.shape[0] // batch
    gw = GROUP * HEAD_DIM
    in_specs = [
        pl.BlockSpec(memory_space=pltpu.SMEM),
        pl.BlockSpec((tq, gw), lambda b, g, i: (b * (q_rows // tq) + i, q_col + g)),
        pl.BlockSpec((c_rows, HEAD_DIM), lambda b, g, i: (b, k_col + g)),
        pl.BlockSpec((c_rows, HEAD_DIM), lambda b, g, i: (b, v_col + g)),
    ]
    args = [sink, q_src, ctx_src, ctx_src]
    has_lat = lat_src is not None
    if has_lat:
        l_rows = lat_src.shape[0] // batch
        in_specs += [
            pl.BlockSpec((l_rows, HEAD_DIM), lambda b, g, i: (b, k_col + g)),
            pl.BlockSpec((l_rows, HEAD_DIM), lambda b, g, i: (b, v_col + g)),
        ]
        args += [lat_src, lat_src]
    return pl.pallas_call(
        functools.partial(_attn_body, has_lat=has_lat, has_mask=has_mask, has_sink=has_sink, tq=tq),
        out_shape=jax.ShapeDtypeStruct((batch * q_rows, Q_COLS), BF16),
        grid=(batch, N_KV, q_rows // tq),
        in_specs=in_specs,
        out_specs=pl.BlockSpec((tq, gw), lambda b, g, i: (b * (q_rows // tq) + i, g)),
        compiler_params=_params("parallel", "parallel", "parallel"),
        name="gqa_attention",
    )(*args)


def _outproj_body(*refs, n_in):
    a_refs, w_refs = refs[:n_in], refs[n_in:2 * n_in]
    b_ref, x_ref, mod_ref, o_ref = refs[2 * n_in:]
    y = b_ref[...] + _dot(a_refs[0][...], w_refs[0][...])
    for a_ref, w_ref in zip(a_refs[1:], w_refs[1:]):
        y = y + _dot(a_ref[...], w_ref[...])
    o_ref[...] = x_ref[...] + mod_ref[0, 2:3, :] * y


def _outproj(acts, weight, bias, x, mod, rows_per_mod):
    rows, d = x.shape
    tm = _tile(min(rows, rows_per_mod), 512)
    n_in = len(acts)
    k = acts[0].shape[1]
    assert all(a.shape[1] == k for a in acts) and weight.shape[0] == n_in * k
    in_specs = [pl.BlockSpec((tm, k), lambda i: (i, 0)) for _ in acts]
    in_specs += [pl.BlockSpec((k, d), lambda i, n=n: (n, 0)) for n in range(n_in)]
    in_specs += [
        pl.BlockSpec((1, d), lambda i: (0, 0)),
        pl.BlockSpec((tm, d), lambda i: (i, 0)),
        pl.BlockSpec((1, 3, d), lambda i: (i * tm // rows_per_mod, 0, 0)),
    ]
    return pl.pallas_call(
        functools.partial(_outproj_body, n_in=n_in),
        out_shape=jax.ShapeDtypeStruct((rows, d), F32),
        grid=(rows // tm,),
        in_specs=in_specs,
        out_specs=pl.BlockSpec((tm, d), lambda i: (i, 0)),
        compiler_params=_params("parallel"),
        name="outproj_residual",
    )(*acts, *([weight] * n_in), bias.reshape(1, d), x, mod)


def _outproj_parity_body(z_ref, unperm_ref, w_ref, b_ref, x_ref, mod_ref, o_ref):
    grp = unperm_ref.shape[0]
    hg = grp // 2
    parts = []
    for r in range(x_ref.shape[0] // grp):
        zg = jnp.concatenate([z_ref[0, r * hg:(r + 1) * hg, :], z_ref[1, r * hg:(r + 1) * hg, :]], axis=0)
        parts.append(_dot(unperm_ref[...], zg).astype(BF16))
    y = b_ref[...] + _dot(jnp.concatenate(parts, axis=0), w_ref[...])
    o_ref[...] = x_ref[...] + mod_ref[0, 2:3, :] * y


def _outproj_parity(z, weight, bias, x, mod, rows_per_mod, seq):
    rows, d = x.shape
    k = z.shape[2]
    tm = _tile(min(rows, rows_per_mod), 512)
    grp = min(PERM_GROUP, tm)
    assert tm % grp == 0 and (tm % seq == 0 or seq % tm == 0)
    return pl.pallas_call(
        _outproj_parity_body,
        out_shape=jax.ShapeDtypeStruct((rows, d), F32),
        grid=(rows // tm,),
        in_specs=[
            pl.BlockSpec((2, tm // 2, k), lambda i: (0, i, 0)),
            pl.BlockSpec((grp, grp), lambda i: (0, 0)),
            pl.BlockSpec((k, d), lambda i: (0, 0)),
            pl.BlockSpec((1, d), lambda i: (0, 0)),
            pl.BlockSpec((tm, d), lambda i: (i, 0)),
            pl.BlockSpec((1, 3, d), lambda i: (i * tm // rows_per_mod, 0, 0)),
        ],
        out_specs=pl.BlockSpec((tm, d), lambda i: (i, 0)),
        compiler_params=_params("parallel"),
        name="outproj_residual_parity",
    )(z, _parity_perm(grp).T, weight, bias.reshape(1, d), x, mod)


def _filter_body(feats_ref, w1_ref, b1_ref, fr1_ref, w2_ref, b2_ref, fr2_ref, w3f_ref, w3b_ref, dl_ref, o_ref):
    feats = feats_ref[...]
    h1 = jnp.sin(fr1_ref[...] * (jnp.dot(feats, w1_ref[...], preferred_element_type=F32, precision=HIGHEST)
                                 + b1_ref[...]))
    h2 = jnp.sin(fr2_ref[...] * (jnp.dot(h1, w2_ref[...], preferred_element_type=F32, precision=HIGHEST)
                                 + b2_ref[...]))
    decay = jnp.exp(-feats[:, 0:1] * dl_ref[...])
    fwd = jnp.dot(h2, w3f_ref[...], preferred_element_type=F32, precision=HIGHEST) * decay
    bwd = jnp.dot(h2, w3b_ref[...], preferred_element_type=F32, precision=HIGHEST) * decay
    row = lax.broadcasted_iota(jnp.int32, (fwd.shape[0], 1), 0)
    bwd = jnp.where(row == 0, 0.0, bwd)
    nrm = lax.rsqrt(jnp.sum(fwd * fwd + bwd * bwd, axis=0, keepdims=True) + EPS)
    o_ref[0] = ((fwd + bwd) * nrm).astype(o_ref.dtype)
    o_ref[1] = ((fwd - bwd) * nrm).astype(o_ref.dtype)


def _hyena_filter_taps(feats, w1, b1, fr1, w2, b2, fr2, w3, absdelta, d):
    n, fp = feats.shape
    hp = w1.shape[1]
    tn = _tile(d, 512)
    nd = d // tn
    vec = pl.BlockSpec((1, hp), lambda o, j: (0, 0))
    return pl.pallas_call(
        _filter_body,
        out_shape=jax.ShapeDtypeStruct((2, n, HYENA_ORDER * d), BF16),
        grid=(HYENA_ORDER, nd),
        in_specs=[
            pl.BlockSpec((n, fp), lambda o, j: (0, 0)),
            pl.BlockSpec((fp, hp), lambda o, j: (0, 0)), vec, vec,
            pl.BlockSpec((hp, hp), lambda o, j: (0, 0)), vec, vec,
            pl.BlockSpec((hp, tn), lambda o, j: (0, (2 * o) * nd + j)),
            pl.BlockSpec((hp, tn), lambda o, j: (0, (2 * o + 1) * nd + j)),
            pl.BlockSpec((1, tn), lambda o, j: (0, j)),
        ],
        out_specs=pl.BlockSpec((2, n, tn), lambda o, j: (0, 0, o * nd + j)),
        compiler_params=_params("parallel", "parallel"),
        name="hyena_filter_taps",
    )(feats, w1, b1, fr1, w2, b2, fr2, w3, w3, absdelta)


def _butterfly(ce_ref, se_ref, co_ref, so_ref, xe, xo):
    pr = _dot(ce_ref[...], xe)
    pi = _dot(se_ref[...], xe)
    qr = _dot(co_ref[...], xo)
    qi = _dot(so_ref[...], xo)
    return pr, pi, qr, qi


def _spectrum_body(ce_ref, se_ref, co_ref, so_ref, se_, so_, de_, do_, o_ref, *, inv_n):
    i = pl.program_id(0)
    tk = ce_ref.shape[0]
    pr = _dot(ce_ref[...], se_[...])
    qr = _dot(co_ref[...], so_[...])
    pi = _dot(se_ref[...], de_[...])
    qi = _dot(so_ref[...], do_[...])
    row0 = (i * tk + lax.broadcasted_iota(jnp.int32, (tk, 1), 0)) == 0
    w_re = jnp.where(row0, inv_n, 2.0 * inv_n)
    o_ref[0] = (pr + qr) * w_re
    o_ref[1] = (pr - qr) * w_re
    o_ref[2] = (pi + qi) * (2.0 * inv_n)
    o_ref[3] = (qi - pi) * (2.0 * inv_n)

    @pl.when(i == 0)
    def _():
        rows = 2 * SUBLANES
        mid_r = _dot(se_ref[0:rows, :], se_[...])
        o_ref[2, 0:1, :] = mid_r[0:1, :] * (2.0 * inv_n)
        o_ref[3, 0:1, :] = qi[0:1, :] * (2.0 * inv_n)


def _hyena_spectrum(dft, taps):
    n = taps.shape[1]
    cols = taps.shape[2]
    half = n // 2
    tk = _tile(half, 1024)
    tn = _tile(cols, 512)
    nr = half // tk
    mat = lambda par, part: pl.BlockSpec((None, tk, half), lambda i, j: (par, part * nr + i, 0))
    tap = lambda which, par: pl.BlockSpec((None, half, tn), lambda i, j: (which, par, j))
    return pl.pallas_call(
        functools.partial(_spectrum_body, inv_n=1.0 / (2 * n)),
        out_shape=jax.ShapeDtypeStruct((4, half, cols), F32),
        grid=(nr, cols // tn),
        in_specs=[mat(0, 0), mat(0, 1), mat(1, 0), mat(1, 1), tap(0, 0), tap(0, 1), tap(1, 0), tap(1, 1)],
        out_specs=pl.BlockSpec((4, tk, tn), lambda i, j: (0, i, j)),
        compiler_params=_params("parallel", "parallel"),
        name="hyena_filter_spectrum",
    )(dft, dft, dft, dft, taps, taps, taps, taps)


def _dft_fwd_body(ce_ref, se_ref, co_ref, so_ref, ze_ref, zo_ref, k_ref, yr_ref, yi_ref):
    i = pl.program_id(1)
    tk = ce_ref.shape[0]
    pr, pi, qr, qi = _butterfly(ce_ref, se_ref, co_ref, so_ref, ze_ref[...], zo_ref[...])
    kr_lo, kr_hi, ki_lo, ki_hi = k_ref[0], k_ref[1], k_ref[2], k_ref[3]
    row0 = (i * tk + lax.broadcasted_iota(jnp.int32, (tk, 1), 0)) == 0
    zr_lo, zr_hi, zi_lo, zi_hi = pr + qr, pr - qr, pi + qi, qi - pi
    yr_lo = zr_lo * kr_lo - jnp.where(row0, 0.0, zi_lo * ki_lo)
    yr_hi = zr_hi * kr_hi - jnp.where(row0, 0.0, zi_hi * ki_hi)
    yi_lo = zr_lo * ki_lo + zi_lo * kr_lo
    yi_hi = zr_hi * ki_hi + zi_hi * kr_hi
    mid_r = pi * ki_lo - qi * ki_hi
    mid_i = pi * ki_hi + qi * ki_lo
    yr_ref[0] = (yr_lo + yr_hi).astype(yr_ref.dtype)
    yr_ref[1] = (yr_lo - yr_hi).astype(yr_ref.dtype)
    yi_ref[0] = jnp.where(row0, mid_r, yi_lo - yi_hi).astype(yi_ref.dtype)
    yi_ref[1] = jnp.where(row0, mid_i, yi_lo + yi_hi).astype(yi_ref.dtype)


def _dft_fwd(dft, z_src, z_col, spec, order, batch, d):
    half = z_src.shape[1] // batch
    tk = _tile(half, 1024)
    tn = _tile(d, 512)
    nr, nd = half // tk, d // tn
    out = jax.ShapeDtypeStruct((2, batch * half, d), BF16)
    once = pl.Buffered(1)
    mat = lambda par, part: pl.BlockSpec((None, tk, half), lambda j, i, b: (par, part * nr + i, 0),
                                         pipeline_mode=once)
    zin = lambda par: pl.BlockSpec((None, half, tn), lambda j, i, b: (par, b, z_col * nd + j))
    res = pl.BlockSpec((2, tk, tn), lambda j, i, b: (0, b * nr + i, j))
    return pl.pallas_call(
        _dft_fwd_body,
        out_shape=(out, out),
        grid=(nd, nr, batch),
        in_specs=[
            mat(0, 0), mat(0, 1), mat(1, 0), mat(1, 1), zin(0), zin(1),
            pl.BlockSpec((4, tk, tn), lambda j, i, b: (0, i, order * nd + j), pipeline_mode=once),
        ],
        out_specs=(res, res),
        compiler_params=_params("parallel", "parallel", "parallel"),
        name="hyena_dft_forward",
    )(dft, dft, dft, dft, z_src, z_src, spec)


def _dft_inv_body(atr_ref, ati_ref, yr_ref, yi_ref, g_ref, zp_ref, bias_ref, o_ref):
    y = _dot(atr_ref[...], yr_ref[...]) + _dot(ati_ref[...], yi_ref[...])
    zp = zp_ref[...].astype(F32)
    o_ref[...] = (g_ref[...].astype(F32) * (y + zp * bias_ref[...])).astype(o_ref.dtype)


def _dft_inv(dft_t, yr, yi, gate_src, gate_col, z_src, z_col, bias, batch, d):
    half = yr.shape[1] // batch
    tt = _tile(half, 1024)
    tn = _tile(d, 1024)
    nr, nd = half // tt, d // tn
    row = lambda cols: pl.BlockSpec((None, tt, tn), lambda j, p, i, b: (p, b * nr + i, cols * nd + j))
    return pl.pallas_call(
        _dft_inv_body,
        out_shape=jax.ShapeDtypeStruct((2, batch * half, d), BF16),
        grid=(nd, 2, nr, batch),
        in_specs=[
            pl.BlockSpec((None, tt, half), lambda j, p, i, b: (p, i, 0)),
            pl.BlockSpec((None, tt, half), lambda j, p, i, b: (p, i, 1)),
            pl.BlockSpec((None, half, tn), lambda j, p, i, b: (p, b, j)),
            pl.BlockSpec((None, half, tn), lambda j, p, i, b: (p, b, j)),
            row(gate_col), row(z_col),
            pl.BlockSpec((1, tn), lambda j, p, i, b: (0, j)),
        ],
        out_specs=row(0),
        compiler_params=_params("parallel", "parallel", "parallel", "parallel"),
        name="hyena_dft_inverse",
    )(dft_t, dft_t, yr, yi, gate_src, z_src, bias.reshape(1, d))


def _rope_tables(n_tok):
    rows = n_tok // GRID_W
    r = jnp.repeat(jnp.arange(rows), GRID_W).astype(F32)
    col = jnp.tile(jnp.arange(GRID_W), rows).astype(F32)
    half = HEAD_DIM // 2
    inv = ROPE_THETA ** (-jnp.arange(0, half, 2, dtype=F32) / half)
    ang = jnp.concatenate([r[:, None] * inv, col[:, None] * inv], axis=-1)
    cos = jnp.repeat(jnp.cos(ang), 2, axis=-1)
    sin = jnp.repeat(jnp.sin(ang), 2, axis=-1)
    sign = jnp.where(jnp.arange(HEAD_DIM) % 2 == 0, -1.0, 1.0).astype(F32)
    return cos, sin * sign


def _dft_tables(n):
    big = 2 * n
    half = n // 2
    k = jnp.arange(half, dtype=jnp.int32)[:, None]
    m = jnp.arange(half, dtype=jnp.int32)[None, :]
    alt = jnp.where(m % 2 == 0, 1.0, -1.0).astype(F32)
    mats = []
    for par in range(2):
        ang = ((k * (2 * m + par)) % big).astype(F32) * (2.0 * math.pi / big)
        s = jnp.where(k == 0, alt if par == 0 else -alt, -jnp.sin(ang))
        mats.append(jnp.concatenate([jnp.cos(ang), s], axis=0))
    a = jnp.stack(mats).astype(BF16)
    return a, a.transpose(0, 2, 1)


def _filter_features(n, width):
    t = jnp.linspace(0.0, 1.0, n, dtype=F32)[:, None]
    w = (2.0 * math.pi / n) * jnp.arange(n, dtype=F32)[:, None]
    bands = jnp.linspace(1e-4, HYENA_BANDS - 1, HYENA_BANDS, dtype=F32)
    ang = w * bands[None, :]
    feats = jnp.concatenate([t, jnp.cos(ang), -jnp.sin(ang)], axis=-1)
    feats = jnp.concatenate([feats[0::2], feats[1::2]], axis=0)
    return jnp.pad(feats, ((0, 0), (0, width - feats.shape[1])))


def _pad_to(a, shape):
    return jnp.pad(a, [(0, s - d) for d, s in zip(a.shape, shape)])


def _attn_layer(x, xc, mod_l, mod_c, g, w_in, w_out, g_q, g_k, sink, rope, batch, seq, n_ctx, ctx_out):
    d = x.shape[1]
    ones = jnp.ones((HEAD_DIM,), F32)
    gamma = jnp.concatenate([jnp.tile(g_q[0], N_HEADS), jnp.tile(g_q[1], N_HEADS),
                             jnp.tile(g_k[0], N_KV), jnp.tile(ones, N_KV),
                             jnp.tile(g_k[1], N_KV), jnp.tile(ones, N_KV)])[None, :]
    qs = jnp.concatenate([jnp.full((2 * Q_COLS,), HEAD_DIM ** -0.5, F32), jnp.ones((4 * KV_COLS,), F32)])[None, :]
    zk, ok = jnp.zeros((KV_COLS,), F32), jnp.ones((KV_COLS,), F32)
    isv = jnp.concatenate([jnp.zeros((2 * Q_COLS,), F32), zk, ok, zk, ok])[None, :]
    tables = (gamma, qs, isv)
    tm = _row_tile(batch * n_ctx, batch * n_ctx)
    ident = (jnp.ones((tm, HEAD_DIM), F32), jnp.zeros((tm, HEAD_DIM), F32))
    w_in_b = _col_tiled(w_in, _tile(w_in.shape[1], 512))
    qkv = _qkv_proj(x, mod_l, g, w_in_b, seq, tables, rope, seq)
    qkv_c = _qkv_proj(xc, mod_c, g, w_in_b, batch * n_ctx, tables, ident, None)
    qa_col, qb_col = 0, N_KV
    ka_col = 2 * Q_COLS // HEAD_DIM
    va_col, kb_col, vb_col = ka_col + N_KV, ka_col + 2 * N_KV, ka_col + 3 * N_KV
    o_a = _attention(sink, qkv, qkv_c, qkv, batch=batch, q_rows=seq, q_col=qa_col, k_col=ka_col, v_col=va_col,
                     has_mask=False, has_sink=False)
    o_b = _attention(sink, qkv, qkv_c, qkv, batch=batch, q_rows=seq, q_col=qb_col, k_col=kb_col, v_col=vb_col,
                     has_mask=True, has_sink=True)
    w_out_b = w_out.astype(BF16)
    zero_b = jnp.zeros((d,), F32)
    x = _outproj((o_a, o_b), w_out_b, zero_b, x, mod_l, seq)
    if ctx_out:
        co_a = _attention(sink, qkv_c, qkv_c, None, batch=batch, q_rows=n_ctx, q_col=qa_col, k_col=ka_col,
                          v_col=va_col, has_mask=False, has_sink=False)
        co_b = _attention(sink, qkv_c, qkv_c, None, batch=batch, q_rows=n_ctx, q_col=qb_col, k_col=kb_col,
                          v_col=vb_col, has_mask=False, has_sink=True)
        xc = _outproj((co_a, co_b), w_out_b, zero_b, xc, mod_c, batch * n_ctx)
    return x, xc


def _hyena_layer(x, mod, g, rows_per_mod, batch, w_in_b, b_in, w_conv, b_conv, filt, hy_bias, w_out_b, b_out):
    d = x.shape[1]
    n = x.shape[0] // batch
    w1, b1, fr1, w2, b2, fr2, w3 = filt
    hp = LANES
    feats = _filter_features(n, LANES)
    max_decay = math.log(HYENA_TARGET) / HYENA_FAST_PCT
    min_decay = math.log(HYENA_TARGET) / HYENA_SLOW_PCT
    absdelta = jnp.abs(jnp.linspace(min_decay, max_decay, d, dtype=F32))[None, :]
    taps = _hyena_filter_taps(
        feats, _pad_to(w1, (LANES, hp)), _pad_to(b1[None, :], (1, hp)), _pad_to(fr1[None, :], (1, hp)),
        _pad_to(w2, (hp, hp)), _pad_to(b2[None, :], (1, hp)), _pad_to(fr2[None, :], (1, hp)),
        _pad_to(w3, (hp, w3.shape[1])), absdelta, d)
    dft, dft_t = _dft_tables(n)
    spec = _hyena_spectrum(dft, taps)
    pc = _proj(x, mod, g, w_in_b, b_in, w_conv, b_conv, rows_per_mod, n)
    yr, yi = _dft_fwd(dft, pc, 0, spec, 0, batch, d)
    z = _dft_inv(dft_t, yr, yi, pc, 1, pc, 0, hy_bias[0], batch, d)
    yr, yi = _dft_fwd(dft, z, 0, spec, 1, batch, d)
    z = _dft_inv(dft_t, yr, yi, pc, 2, z, 0, hy_bias[1], batch, d)
    return _outproj_parity(z, w_out_b, b_out, x, mod, rows_per_mod, n)


def kernel(x, c, ctx, c_ctx, w_mod, b_mod, g_norm, w_ffn_in, w_ffn_out, w_attn_in, w_attn_out, g_q, g_k, sink, w_hy_in, b_hy_in, w_hy_conv, b_hy_conv, hf_w1, hf_b1, hf_freq1, hf_w2, hf_b2, hf_freq2, hf_w3, hy_bias, w_hy_out, b_hy_out):
    batch, seq, d = x.shape
    n_ctx = ctx.shape[1]
    depth = w_mod.shape[0]
    rows_c = batch * n_ctx
    rope = _rope_tables(seq)
    last_ctx = max(l for l in range(depth) if l % 2 == 0)

    r_pad = -(-(batch + 1) // SUBLANES) * SUBLANES
    c_all = _pad_to(jnp.concatenate([c, c_ctx[None, :]], axis=0), (r_pad, d))
    m_all = _mod_all(c_all, w_mod, b_mod).reshape(depth, r_pad, N_MOD, d)

    w_ffn_in_b = w_ffn_in.astype(BF16)
    w_ffn_out_b = w_ffn_out.astype(BF16)

    xl = x.reshape(batch * seq, d)
    xc = ctx.reshape(rows_c, d)
    for l in range(depth):
        i = l // 2
        ctx_live = l <= last_ctx
        ctx_full = l < last_ctx
        mods_l = [m_all[l, :batch, 3 * k:3 * k + 3] for k in range(3)]
        mods_c = [m_all[l, batch:batch + 1, 3 * k:3 * k + 3] for k in range(3)]
        xl = _ffn(xl, mods_l[0], g_norm[l, 0], w_ffn_in_b, w_ffn_out_b, l, 0, seq)
        if ctx_live:
            xc = _ffn(xc, mods_c[0], g_norm[l, 0], w_ffn_in_b, w_ffn_out_b, l, 0, rows_c)
        if l % 2 == 0:
            xl, xc = _attn_layer(xl, xc, mods_l[1], mods_c[1], g_norm[l, 1], w_attn_in[i], w_attn_out[i],
                                 g_q[i], g_k[i], sink[i], rope, batch, seq, n_ctx, ctx_full)
        else:
            w_in_b = _col_tiled(w_hy_in[i], _tile(w_hy_in.shape[2], 256))
            w_out_b = w_hy_out[i].astype(BF16)
            filt = (hf_w1[i], hf_b1[i], hf_freq1[i], hf_w2[i], hf_b2[i], hf_freq2[i], hf_w3[i])
            xl = _hyena_layer(xl, mods_l[1], g_norm[l, 1], seq, batch, w_in_b, b_hy_in[i], w_hy_conv[i],
                              b_hy_conv[i], filt, hy_bias[i], w_out_b, b_hy_out[i])
            if ctx_full:
                xc = _hyena_layer(xc, mods_c[1], g_norm[l, 1], rows_c, batch, w_in_b, b_hy_in[i], w_hy_conv[i],
                                  b_hy_conv[i], filt, hy_bias[i], w_out_b, b_hy_out[i])
        xl = _ffn(xl, mods_l[2], g_norm[l, 2], w_ffn_in_b, w_ffn_out_b, l, 1, seq)
        if ctx_full:
            xc = _ffn(xc, mods_c[2], g_norm[l, 2], w_ffn_in_b, w_ffn_out_b, l, 1, rows_c)
    return xl.reshape(batch, seq, d)
```

```python
import functools
import math

import jax
import jax.numpy as jnp
from jax import lax
from jax.experimental import pallas as pl
from jax.experimental.pallas import tpu as pltpu

HEAD_DIM = 128
N_HEADS = 8
N_KV = 2
GROUP = N_HEADS // N_KV
Q_COLS = N_HEADS * HEAD_DIM
KV_COLS = N_KV * HEAD_DIM
GRID_W = 64
WINDOW = 128
ROPE_THETA = 10000.0
N_MOD = 9
HYENA_ORDER = 2
HYENA_SHORT_K = 3
HYENA_BANDS = 16
HYENA_TARGET = 1e-2
HYENA_FAST_PCT = 0.3
HYENA_SLOW_PCT = 1.5
EPS = 1e-6
NEG_INF = -1e30

LANES = 128
SUBLANES = 8
VMEM_LIMIT_BYTES = 60 * 1024 * 1024

F32 = jnp.float32
BF16 = jnp.bfloat16
HIGHEST = lax.Precision.HIGHEST


def _params(*sem):
    return pltpu.CompilerParams(dimension_semantics=sem, vmem_limit_bytes=VMEM_LIMIT_BYTES)


def _tile(dim, pref):
    t = min(dim, pref)
    while dim % t:
        t //= 2
    return t


def _dot(a, b):
    return jnp.dot(a, b, preferred_element_type=F32)


def _dot_nt(a, b):
    return lax.dot_general(a, b, (((1,), (1,)), ((), ())), preferred_element_type=F32)


def _silu(v):
    return v * (1.0 / (1.0 + jnp.exp(-v)))


def _modulated(xf, g, shift, scale):
    ms = jnp.mean(xf * xf, axis=-1, keepdims=True)
    y = xf * lax.rsqrt(ms + EPS)
    return (y * g) * (1.0 + scale) + shift


def _mod_body(c_ref, w_ref, b_ref, o_ref):
    s = _silu(c_ref[...])
    o_ref[0] = jnp.dot(s, w_ref[0], preferred_element_type=F32, precision=HIGHEST) + b_ref[0]


def _mod_all(c_all, w_mod, b_mod):
    depth, d, nd = w_mod.shape
    r = c_all.shape[0]
    tn = _tile(nd, 2048)
    return pl.pallas_call(
        _mod_body,
        out_shape=jax.ShapeDtypeStruct((depth, r, nd), F32),
        grid=(depth, nd // tn),
        in_specs=[
            pl.BlockSpec((r, d), lambda l, j: (0, 0)),
            pl.BlockSpec((1, d, tn), lambda l, j: (l, 0, j)),
            pl.BlockSpec((1, 1, tn), lambda l, j: (l, 0, j)),
        ],
        out_specs=pl.BlockSpec((1, r, tn), lambda l, j: (l, 0, j)),
        compiler_params=_params("parallel", "parallel"),
        name="mod_vectors",
    )(c_all, w_mod, b_mod.reshape(depth, 1, nd))


def _ffn_body(x_ref, mod_ref, g_ref, wg_ref, wu_ref, wo_ref, o_ref, h_ref, acc_ref):
    j = pl.program_id(1)
    last = pl.num_programs(1) - 1

    def chunk(h):
        a = _dot(h, wg_ref[...])
        u = _dot(h, wu_ref[...])
        return _dot((_silu(a) * u).astype(BF16), wo_ref[...])

    @pl.when(j == 0)
    def _():
        h = _modulated(x_ref[...], g_ref[...], mod_ref[0, 0:1, :], mod_ref[0, 1:2, :]).astype(BF16)
        h_ref[...] = h
        acc_ref[...] = chunk(h)

    @pl.when(jnp.logical_and(j > 0, j < last))
    def _():
        acc_ref[...] += chunk(h_ref[...])

    @pl.when(jnp.logical_and(j > 0, j == last))
    def _():
        y = acc_ref[...] + chunk(h_ref[...])
        o_ref[...] = x_ref[...] + (0.5 * mod_ref[0, 2:3, :]) * y


def _ffn(x, mod, g, w_in, w_out, layer, half, rows_per_mod):
    rows, d = x.shape
    f = w_out.shape[2]
    tm = _tile(min(rows, rows_per_mod), 1024)
    tf = _tile(f, 512)
    nf = f // tf
    assert nf >= 2
    return pl.pallas_call(
        _ffn_body,
        out_shape=jax.ShapeDtypeStruct((rows, d), F32),
        grid=(rows // tm, nf),
        in_specs=[
            pl.BlockSpec((tm, d), lambda i, j: (i, 0)),
            pl.BlockSpec((1, 3, d), lambda i, j: (i * tm // rows_per_mod, 0, 0)),
            pl.BlockSpec((1, d), lambda i, j: (0, 0)),
            pl.BlockSpec((None, None, d, tf), lambda i, j: (layer, half, 0, j)),
            pl.BlockSpec((None, None, d, tf), lambda i, j: (layer, half, 0, nf + j)),
            pl.BlockSpec((None, None, tf, d), lambda i, j: (layer, half, j, 0)),
        ],
        out_specs=pl.BlockSpec((tm, d), lambda i, j: (i, 0), pipeline_mode=pl.Buffered(1)),
        scratch_shapes=[pltpu.VMEM((tm, d), BF16), pltpu.VMEM((tm, d), F32)],
        compiler_params=_params("parallel", "arbitrary"),
        name="ffn_half_step",
    )(x, mod, g.reshape(1, d), w_in, w_in, w_out)


def _proj_body(x_ref, mod_ref, g_ref, perm_ref, w_ref, b_ref, cw_ref, cb_ref, o_ref, h_ref, *, half_seq):
    tm = x_ref.shape[0]
    half = tm // 2
    grp = perm_ref.shape[0]

    @pl.when(pl.program_id(1) == 0)
    def _():
        for r in range(tm // grp):
            h = _modulated(x_ref[r * grp:(r + 1) * grp, :], g_ref[...], mod_ref[0, 0:1, :], mod_ref[0, 1:2, :])
            hp = _dot(perm_ref[...], h.astype(BF16)).astype(BF16)
            h_ref[r * (grp // 2):(r + 1) * (grp // 2), :] = hp[:grp // 2]
            h_ref[half + r * (grp // 2):half + (r + 1) * (grp // 2), :] = hp[grp // 2:]

    p = _dot(h_ref[...], w_ref[...]) + b_ref[...]
    pe, po = p[:half], p[half:]
    pos = lax.broadcasted_iota(jnp.int32, (half, 1), 0) % half_seq
    po_prev = jnp.where(pos == 0, 0.0, pltpu.roll(po, 1, 0))
    pe_next = jnp.where(pos == half_seq - 1, 0.0, pltpu.roll(pe, half - 1, 0))
    w0, w1, w2 = cw_ref[0:1, :], cw_ref[1:2, :], cw_ref[2:3, :]
    o_ref[0] = (((cb_ref[...] + po_prev * w0) + pe * w1) + po * w2).astype(o_ref.dtype)
    o_ref[1] = (((cb_ref[...] + pe * w0) + po * w1) + pe_next * w2).astype(o_ref.dtype)


def _qkv_body(x_ref, mod_ref, g_ref, w_ref, gamma_ref, qs_ref, isv_ref, cos_ref, sin_ref, o_ref, h_ref, p_ref):
    j = pl.program_id(1)
    last = pl.num_programs(1) - 1

    def finish():
        cos = cos_ref[...]
        sin = sin_ref[...]
        even = (lax.broadcasted_iota(jnp.int32, (1, HEAD_DIM), 1) % 2) == 0
        for hh in range(p_ref.shape[1] // HEAD_DIM):
            sl = slice(hh * HEAD_DIM, (hh + 1) * HEAD_DIM)
            ph = p_ref[:, sl]
            ms = jnp.mean(ph * ph, axis=-1, keepdims=True)
            nh = (ph * lax.rsqrt(ms + EPS)) * gamma_ref[:, sl]
            partner = jnp.where(even, pltpu.roll(nh, HEAD_DIM - 1, 1), pltpu.roll(nh, 1, 1))
            r = (nh * cos + partner * sin) * qs_ref[:, sl]
            o_ref[:, sl] = jnp.where(isv_ref[:, sl] > 0.0, ph, r).astype(o_ref.dtype)

    @pl.when(j == 0)
    def _():
        h = _modulated(x_ref[...], g_ref[...], mod_ref[0, 0:1, :], mod_ref[0, 1:2, :]).astype(BF16)
        h_ref[...] = h
        p_ref[...] = _dot(h, w_ref[...])

    @pl.when(jnp.logical_and(j > 0, j < last))
    def _():
        finish()
        p_ref[...] = _dot(h_ref[...], w_ref[...])

    @pl.when(j == last)
    def _():
        finish()


def _row_tile(rows, rows_per_mod):
    return _tile(min(rows, rows_per_mod), 1024)


def _parity_perm(tm):
    r = jnp.arange(tm, dtype=jnp.int32)[:, None]
    c = jnp.arange(tm, dtype=jnp.int32)[None, :]
    src = jnp.where(r < tm // 2, 2 * r, 2 * (r - tm // 2) + 1)
    return (c == src).astype(BF16)


def _col_tiled(w, tn):
    k, n = w.shape
    return w.reshape(k, n // tn, tn).transpose(1, 0, 2).astype(BF16)


PERM_GROUP = 256


def _proj(x, mod, g, w_t, b, conv_w, conv_b, rows_per_mod, seq):
    rows, d = x.shape
    nj, _, tn = w_t.shape
    n = nj * tn
    tm = _tile(min(rows, rows_per_mod), max(seq, 1024))
    grp = min(PERM_GROUP, tm)
    assert tm % seq == 0 and tm % grp == 0 and grp % (4 * SUBLANES) == 0
    col = pl.BlockSpec((1, tn), lambda i, j: (0, j))
    return pl.pallas_call(
        functools.partial(_proj_body, half_seq=seq // 2),
        out_shape=jax.ShapeDtypeStruct((2, rows // 2, n), BF16),
        grid=(rows // tm, n // tn),
        in_specs=[
            pl.BlockSpec((tm, d), lambda i, j: (i, 0)),
            pl.BlockSpec((1, 3, d), lambda i, j: (i * tm // rows_per_mod, 0, 0)),
            pl.BlockSpec((1, d), lambda i, j: (0, 0)),
            pl.BlockSpec((grp, grp), lambda i, j: (0, 0)),
            pl.BlockSpec((None, d, tn), lambda i, j: (j, 0, 0)),
            col,
            pl.BlockSpec((HYENA_SHORT_K, tn), lambda i, j: (0, j)),
            col,
        ],
        out_specs=pl.BlockSpec((2, tm // 2, tn), lambda i, j: (0, i, j)),
        scratch_shapes=[pltpu.VMEM((tm, d), BF16)],
        compiler_params=_params("parallel", "arbitrary"),
        name="hyena_project_conv",
    )(x, mod, g.reshape(1, d), _parity_perm(grp), w_t, b.reshape(1, n), conv_w, conv_b.reshape(1, n))


def _qkv_proj(x, mod, g, w_t, rows_per_mod, tables, rope, seq):
    rows, d = x.shape
    nj, _, tn = w_t.shape
    n = nj * tn
    tm = _row_tile(rows, rows_per_mod)
    gamma, qs, isv = tables
    cos, sin = rope

    def prev(j):
        return jnp.maximum(j - 1, 0)

    col = pl.BlockSpec((1, tn), lambda i, j: (0, prev(j)))
    if seq is None:
        rope_spec = pl.BlockSpec((tm, HEAD_DIM), lambda i, j: (0, 0))
    else:
        rope_spec = pl.BlockSpec((tm, HEAD_DIM), lambda i, j: (i % (seq // tm), 0))
    return pl.pallas_call(
        _qkv_body,
        out_shape=jax.ShapeDtypeStruct((rows, n), BF16),
        grid=(rows // tm, nj + 1),
        in_specs=[
            pl.BlockSpec((tm, d), lambda i, j: (i, 0)),
            pl.BlockSpec((1, 3, d), lambda i, j: (i * tm // rows_per_mod, 0, 0)),
            pl.BlockSpec((1, d), lambda i, j: (0, 0)),
            pl.BlockSpec((None, d, tn), lambda i, j: (jnp.minimum(j, nj - 1), 0, 0)),
            col, col, col, rope_spec, rope_spec,
        ],
        out_specs=pl.BlockSpec((tm, tn), lambda i, j: (i, prev(j))),
        scratch_shapes=[pltpu.VMEM((tm, d), BF16), pltpu.VMEM((tm, tn), F32)],
        compiler_params=_params("arbitrary", "arbitrary"),
        name="qkv_project",
    )(x, mod, g.reshape(1, d), w_t, gamma, qs, isv, cos, sin)


def _attn_body(*refs, has_lat, has_mask, has_sink, tq):
    sink_ref, q_ref, kc_ref, vc_ref = refs[:4]
    if has_lat:
        k_ref, v_ref, o_ref = refs[4:7]
    else:
        o_ref = refs[4]
    g = pl.program_id(1)
    qi = pl.program_id(2)
    if has_lat:
        l_rows = k_ref.shape[0]
        if has_mask:
            kw = min(l_rows, tq + 2 * WINDOW)
            k0 = pl.multiple_of(jnp.clip(qi * tq - WINDOW, 0, l_rows - kw), WINDOW)
            keys = pl.ds(k0, kw)
        else:
            k0, keys = 0, slice(None)
    for hh in range(GROUP):
        sl = slice(hh * HEAD_DIM, (hh + 1) * HEAD_DIM)
        q = q_ref[:, sl]
        s1 = _dot_nt(q, kc_ref[...])
        m = jnp.max(s1, axis=-1, keepdims=True)
        if has_lat:
            s2 = _dot_nt(q, k_ref[keys, :])
            if has_mask:
                qpos = qi * tq + lax.broadcasted_iota(jnp.int32, (tq, 1), 0)
                kpos = k0 + lax.broadcasted_iota(jnp.int32, (1, s2.shape[1]), 1)
                s2 = jnp.where(jnp.abs(kpos - qpos) <= WINDOW, s2, NEG_INF)
            m = jnp.maximum(m, jnp.max(s2, axis=-1, keepdims=True))
        if has_sink:
            sk = sink_ref[g * GROUP + hh]
            m = jnp.maximum(m, sk)
        p1 = jnp.exp(s1 - m)
        l = jnp.sum(p1, axis=-1, keepdims=True)
        o = _dot(p1.astype(BF16), vc_ref[...])
        if has_lat:
            p2 = jnp.exp(s2 - m)
            l = l + jnp.sum(p2, axis=-1, keepdims=True)
            o = o + _dot(p2.astype(BF16), v_ref[keys, :])
        if has_sink:
            l = l + jnp.exp(sk - m)
        o_ref[:, sl] = (o / l).astype(o_ref.dtype)


def _attention(sink, q_src, ctx_src, lat_src, *, batch, q_rows, q_col, k_col, v_col, has_mask, has_sink):
    tq = _tile(q_rows, 256 if has_mask else 512)
    assert not has_mask or tq % WINDOW == 0
    c_rows = ctx_src.shape[0] // batch
    gw = GROUP * HEAD_DIM
    in_specs = [
        pl.BlockSpec(memory_space=pltpu.SMEM),
        pl.BlockSpec((tq, gw), lambda b, g, i: (b * (q_rows // tq) + i, q_col + g)),
        pl.BlockSpec((c_rows, HEAD_DIM), lambda b, g, i: (b, k_col + g)),
        pl.BlockSpec((c_rows, HEAD_DIM), lambda b, g, i: (b, v_col + g)),
    ]
    args = [sink, q_src, ctx_src, ctx_src]
    has_lat = lat_src is not None
    if has_lat:
        l_rows = lat_src.shape[0] // batch
        in_specs += [
            pl.BlockSpec((l_rows, HEAD_DIM), lambda b, g, i: (b, k_col + g)),
            pl.BlockSpec((l_rows, HEAD_DIM), lambda b, g, i: (b, v_col + g)),
        ]
        args += [lat_src, lat_src]
    return pl.pallas_call(
        functools.partial(_attn_body, has_lat=has_lat, has_mask=has_mask, has_sink=has_sink, tq=tq),
        out_shape=jax.ShapeDtypeStruct((batch * q_rows, Q_COLS), BF16),
        grid=(batch, N_KV, q_rows // tq),
        in_specs=in_specs,
        out_specs=pl.BlockSpec((tq, gw), lambda b, g, i: (b * (q_rows // tq) + i, g)),
        compiler_params=_params("parallel", "parallel", "parallel"),
        name="gqa_attention",
    )(*args)


def _outproj_body(*refs, n_in):
    a_refs, w_refs = refs[:n_in], refs[n_in:2 * n_in]
    b_ref, x_ref, mod_ref, o_ref = refs[2 * n_in:]
    y = b_ref[...] + _dot(a_refs[0][...], w_refs[0][...])
    for a_ref, w_ref in zip(a_refs[1:], w_refs[1:]):
        y = y + _dot(a_ref[...], w_ref[...])
    o_ref[...] = x_ref[...] + mod_ref[0, 2:3, :] * y


def _outproj(acts, weight, bias, x, mod, rows_per_mod):
    rows, d = x.shape
    tm = _tile(min(rows, rows_per_mod), 512)
    n_in = len(acts)
    k = acts[0].shape[1]
    assert all(a.shape[1] == k for a in acts) and weight.shape[0] == n_in * k
    in_specs = [pl.BlockSpec((tm, k), lambda i: (i, 0)) for _ in acts]
    in_specs += [pl.BlockSpec((k, d), lambda i, n=n: (n, 0)) for n in range(n_in)]
    in_specs += [
        pl.BlockSpec((1, d), lambda i: (0, 0)),
        pl.BlockSpec((tm, d), lambda i: (i, 0)),
        pl.BlockSpec((1, 3, d), lambda i: (i * tm // rows_per_mod, 0, 0)),
    ]
    return pl.pallas_call(
        functools.partial(_outproj_body, n_in=n_in),
        out_shape=jax.ShapeDtypeStruct((rows, d), F32),
        grid=(rows // tm,),
        in_specs=in_specs,
        out_specs=pl.BlockSpec((tm, d), lambda i: (i, 0)),
        compiler_params=_params("parallel"),
        name="outproj_residual",
    )(*acts, *([weight] * n_in), bias.reshape(1, d), x, mod)


def _outproj_parity_body(z_ref, unperm_ref, w_ref, b_ref, x_ref, mod_ref, o_ref):
    grp = unperm_ref.shape[0]
    hg = grp // 2
    parts = []
    for r in range(x_ref.shape[0] // grp):
        zg = jnp.concatenate([z_ref[0, r * hg:(r + 1) * hg, :], z_ref[1, r * hg:(r + 1) * hg, :]], axis=0)
        parts.append(_dot(unperm_ref[...], zg).astype(BF16))
    y = b_ref[...] + _dot(jnp.concatenate(parts, axis=0), w_ref[...])
    o_ref[...] = x_ref[...] + mod_ref[0, 2:3, :] * y


def _outproj_parity(z, weight, bias, x, mod, rows_per_mod, seq):
    rows, d = x.shape
    k = z.shape[2]
    tm = _tile(min(rows, rows_per_mod), 512)
    grp = min(PERM_GROUP, tm)
    assert tm % grp == 0 and (tm % seq == 0 or seq % tm == 0)
    return pl.pallas_call(
        _outproj_parity_body,
        out_shape=jax.ShapeDtypeStruct((rows, d), F32),
        grid=(rows // tm,),
        in_specs=[
            pl.BlockSpec((2, tm // 2, k), lambda i: (0, i, 0)),
            pl.BlockSpec((grp, grp), lambda i: (0, 0)),
            pl.BlockSpec((k, d), lambda i: (0, 0)),
            pl.BlockSpec((1, d), lambda i: (0, 0)),
            pl.BlockSpec((tm, d), lambda i: (i, 0)),
            pl.BlockSpec((1, 3, d), lambda i: (i * tm // rows_per_mod, 0, 0)),
        ],
        out_specs=pl.BlockSpec((tm, d), lambda i: (i, 0)),
        compiler_params=_params("parallel"),
        name="outproj_residual_parity",
    )(z, _parity_perm(grp).T, weight, bias.reshape(1, d), x, mod)


def _filter_body(feats_ref, w1_ref, b1_ref, fr1_ref, w2_ref, b2_ref, fr2_ref, w3f_ref, w3b_ref, dl_ref, o_ref,
                 h2_ref):
    feats = feats_ref[...]

    @pl.when(jnp.logical_and(pl.program_id(0) == 0, pl.program_id(1) == 0))
    def _():
        h1 = jnp.sin(fr1_ref[...] * (jnp.dot(feats, w1_ref[...], preferred_element_type=F32, precision=HIGHEST)
                                     + b1_ref[...]))
        h2_ref[...] = jnp.sin(fr2_ref[...] * (jnp.dot(h1, w2_ref[...], preferred_element_type=F32,
                                                      precision=HIGHEST) + b2_ref[...]))

    h2 = h2_ref[...]
    decay = jnp.exp(-feats[:, 0:1] * dl_ref[...])
    fwd = jnp.dot(h2, w3f_ref[...], preferred_element_type=F32, precision=HIGHEST) * decay
    bwd = jnp.dot(h2, w3b_ref[...], preferred_element_type=F32, precision=HIGHEST) * decay
    row = lax.broadcasted_iota(jnp.int32, (fwd.shape[0], 1), 0)
    bwd = jnp.where(row == 0, 0.0, bwd)
    nrm = lax.rsqrt(jnp.sum(fwd * fwd + bwd * bwd, axis=0, keepdims=True) + EPS)
    o_ref[0] = ((fwd + bwd) * nrm).astype(o_ref.dtype)
    o_ref[1] = ((fwd - bwd) * nrm).astype(o_ref.dtype)


def _hyena_filter_taps(feats, w1, b1, fr1, w2, b2, fr2, w3, absdelta, d):
    n, fp = feats.shape
    hp = w1.shape[1]
    tn = _tile(d, 512)
    nd = d // tn
    vec = pl.BlockSpec((1, hp), lambda o, j: (0, 0))
    return pl.pallas_call(
        _filter_body,
        out_shape=jax.ShapeDtypeStruct((2, n, HYENA_ORDER * d), BF16),
        grid=(HYENA_ORDER, nd),
        in_specs=[
            pl.BlockSpec((n, fp), lambda o, j: (0, 0)),
            pl.BlockSpec((fp, hp), lambda o, j: (0, 0)), vec, vec,
            pl.BlockSpec((hp, hp), lambda o, j: (0, 0)), vec, vec,
            pl.BlockSpec((hp, tn), lambda o, j: (0, (2 * o) * nd + j)),
            pl.BlockSpec((hp, tn), lambda o, j: (0, (2 * o + 1) * nd + j)),
            pl.BlockSpec((1, tn), lambda o, j: (0, j)),
        ],
        out_specs=pl.BlockSpec((2, n, tn), lambda o, j: (0, 0, o * nd + j)),
        scratch_shapes=[pltpu.VMEM((n, hp), F32)],
        compiler_params=_params("arbitrary", "arbitrary"),
        name="hyena_filter_taps",
    )(feats, w1, b1, fr1, w2, b2, fr2, w3, w3, absdelta)


def _spectrum_body(ce_ref, se_ref, co_ref, so_ref, se_, so_, de_, do_, o_ref, *, inv_n):
    i = pl.program_id(0)
    tk = ce_ref.shape[0]
    pr = _dot(ce_ref[...], se_[...])
    qr = _dot(co_ref[...], so_[...])
    pi = _dot(se_ref[...], de_[...])
    qi = _dot(so_ref[...], do_[...])
    row0 = (i * tk + lax.broadcasted_iota(jnp.int32, (tk, 1), 0)) == 0
    w_re = jnp.where(row0, inv_n, 2.0 * inv_n)
    o_ref[0] = (pr + qr) * w_re
    o_ref[1] = (pr - qr) * w_re
    o_ref[2] = (pi + qi) * (2.0 * inv_n)
    o_ref[3] = (qi - pi) * (2.0 * inv_n)

    @pl.when(i == 0)
    def _():
        rows = 2 * SUBLANES
        mid_r = _dot(se_ref[0:rows, :], se_[...])
        o_ref[2, 0:1, :] = mid_r[0:1, :] * (2.0 * inv_n)
        o_ref[3, 0:1, :] = qi[0:1, :] * (2.0 * inv_n)


def _hyena_spectrum(dft, taps):
    n = taps.shape[1]
    cols = taps.shape[2]
    half = n // 2
    tk = _tile(half, 1024)
    tn = _tile(cols, 512)
    nr = half // tk
    mat = lambda par, part: pl.BlockSpec((None, tk, half), lambda i, j: (par, part * nr + i, 0))
    tap = lambda which, par: pl.BlockSpec((None, half, tn), lambda i, j: (which, par, j))
    return pl.pallas_call(
        functools.partial(_spectrum_body, inv_n=1.0 / (2 * n)),
        out_shape=jax.ShapeDtypeStruct((4, half, cols), F32),
        grid=(nr, cols // tn),
        in_specs=[mat(0, 0), mat(0, 1), mat(1, 0), mat(1, 1), tap(0, 0), tap(0, 1), tap(1, 0), tap(1, 1)],
        out_specs=pl.BlockSpec((4, tk, tn), lambda i, j: (0, i, j)),
        compiler_params=_params("parallel", "parallel"),
        name="hyena_filter_spectrum",
    )(dft, dft, dft, dft, taps, taps, taps, taps)


def _dft_fwd_body(ce_ref, se_ref, co_ref, so_ref, ze_ref, zo_ref, k_ref, yr_ref, yi_ref):
    i = pl.program_id(1)
    tk = ce_ref.shape[0]
    ze, zo = ze_ref[...], zo_ref[...]
    pr = _dot(ce_ref[...], ze)
    pi = _dot(se_ref[...], ze)
    qr = _dot(co_ref[...], zo)
    qi = _dot(so_ref[...], zo)
    kr_lo, kr_hi, ki_lo, ki_hi = k_ref[0], k_ref[1], k_ref[2], k_ref[3]
    row0 = (i * tk + lax.broadcasted_iota(jnp.int32, (tk, 1), 0)) == 0
    zr_lo, zr_hi, zi_lo, zi_hi = pr + qr, pr - qr, pi + qi, qi - pi
    yr_lo = zr_lo * kr_lo - jnp.where(row0, 0.0, zi_lo * ki_lo)
    yr_hi = zr_hi * kr_hi - jnp.where(row0, 0.0, zi_hi * ki_hi)
    yi_lo = zr_lo * ki_lo + zi_lo * kr_lo
    yi_hi = zr_hi * ki_hi + zi_hi * kr_hi
    mid_r = pi * ki_lo - qi * ki_hi
    mid_i = pi * ki_hi + qi * ki_lo
    yr_ref[0] = (yr_lo + yr_hi).astype(yr_ref.dtype)
    yr_ref[1] = (yr_lo - yr_hi).astype(yr_ref.dtype)
    yi_ref[0] = jnp.where(row0, mid_r, yi_lo - yi_hi).astype(yi_ref.dtype)
    yi_ref[1] = jnp.where(row0, mid_i, yi_lo + yi_hi).astype(yi_ref.dtype)


def _dft_fwd(dft, z_src, z_col, spec, order, batch, d):
    half = z_src.shape[1] // batch
    tk = _tile(half, 1024)
    tn = _tile(d, 512)
    nr, nd = half // tk, d // tn
    out = jax.ShapeDtypeStruct((2, batch * half, d), BF16)
    once = pl.Buffered(1)
    mat = lambda par, part: pl.BlockSpec((None, tk, half), lambda j, i, b: (par, part * nr + i, 0),
                                         pipeline_mode=once)
    zin = lambda par: pl.BlockSpec((None, half, tn), lambda j, i, b: (par, b, z_col * nd + j))
    res = pl.BlockSpec((2, tk, tn), lambda j, i, b: (0, b * nr + i, j))
    return pl.pallas_call(
        _dft_fwd_body,
        out_shape=(out, out),
        grid=(nd, nr, batch),
        in_specs=[
            mat(0, 0), mat(0, 1), mat(1, 0), mat(1, 1), zin(0), zin(1),
            pl.BlockSpec((4, tk, tn), lambda j, i, b: (0, i, order * nd + j), pipeline_mode=once),
        ],
        out_specs=(res, res),
        compiler_params=_params("parallel", "parallel", "parallel"),
        name="hyena_dft_forward",
    )(dft, dft, dft, dft, z_src, z_src, spec)


def _dft_inv_body(atr_ref, ati_ref, yr_ref, yi_ref, g_ref, zp_ref, bias_ref, o_ref):
    y = _dot(atr_ref[...], yr_ref[...]) + _dot(ati_ref[...], yi_ref[...])
    zp = zp_ref[...].astype(F32)
    o_ref[...] = (g_ref[...].astype(F32) * (y + zp * bias_ref[...])).astype(o_ref.dtype)


def _dft_inv(dft_t, yr, yi, gate_src, gate_col, z_src, z_col, bias, batch, d):
    half = yr.shape[1] // batch
    tt = _tile(half, 1024)
    tn = _tile(d, 1024)
    nr, nd = half // tt, d // tn
    row = lambda cols: pl.BlockSpec((None, tt, tn), lambda j, p, i, b: (p, b * nr + i, cols * nd + j))
    return pl.pallas_call(
        _dft_inv_body,
        out_shape=jax.ShapeDtypeStruct((2, batch * half, d), BF16),
        grid=(nd, 2, nr, batch),
        in_specs=[
            pl.BlockSpec((None, tt, half), lambda j, p, i, b: (p, i, 0)),
            pl.BlockSpec((None, tt, half), lambda j, p, i, b: (p, i, 1)),
            pl.BlockSpec((None, half, tn), lambda j, p, i, b: (p, b, j)),
            pl.BlockSpec((None, half, tn), lambda j, p, i, b: (p, b, j)),
            row(gate_col), row(z_col),
            pl.BlockSpec((1, tn), lambda j, p, i, b: (0, j)),
        ],
        out_specs=row(0),
        compiler_params=_params("parallel", "parallel", "parallel", "parallel"),
        name="hyena_dft_inverse",
    )(dft_t, dft_t, yr, yi, gate_src, z_src, bias.reshape(1, d))


def _rope_tables(n_tok):
    rows = n_tok // GRID_W
    r = jnp.repeat(jnp.arange(rows), GRID_W).astype(F32)
    col = jnp.tile(jnp.arange(GRID_W), rows).astype(F32)
    half = HEAD_DIM // 2
    inv = ROPE_THETA ** (-jnp.arange(0, half, 2, dtype=F32) / half)
    ang = jnp.concatenate([r[:, None] * inv, col[:, None] * inv], axis=-1)
    cos = jnp.repeat(jnp.cos(ang), 2, axis=-1)
    sin = jnp.repeat(jnp.sin(ang), 2, axis=-1)
    sign = jnp.where(jnp.arange(HEAD_DIM) % 2 == 0, -1.0, 1.0).astype(F32)
    return cos, sin * sign


def _dft_tables(n):
    big = 2 * n
    half = n // 2
    k = jnp.arange(half, dtype=jnp.int32)[:, None]
    m = jnp.arange(half, dtype=jnp.int32)[None, :]
    alt = jnp.where(m % 2 == 0, 1.0, -1.0).astype(F32)
    mats = []
    for par in range(2):
        ang = ((k * (2 * m + par)) % big).astype(F32) * (2.0 * math.pi / big)
        s = jnp.where(k == 0, alt if par == 0 else -alt, -jnp.sin(ang))
        mats.append(jnp.concatenate([jnp.cos(ang), s], axis=0))
    a = jnp.stack(mats).astype(BF16)
    return a, a.transpose(0, 2, 1)


def _filter_features(n, width):
    t = jnp.linspace(0.0, 1.0, n, dtype=F32)[:, None]
    w = (2.0 * math.pi / n) * jnp.arange(n, dtype=F32)[:, None]
    bands = jnp.linspace(1e-4, HYENA_BANDS - 1, HYENA_BANDS, dtype=F32)
    ang = w * bands[None, :]
    feats = jnp.concatenate([t, jnp.cos(ang), -jnp.sin(ang)], axis=-1)
    feats = jnp.concatenate([feats[0::2], feats[1::2]], axis=0)
    return jnp.pad(feats, ((0, 0), (0, width - feats.shape[1])))


def _pad_to(a, shape):
    return jnp.pad(a, [(0, s - d) for d, s in zip(a.shape, shape)])


def _attn_layer(x, xc, mod_l, mod_c, g, w_in, w_out, g_q, g_k, sink, rope, batch, seq, n_ctx, ctx_out):
    d = x.shape[1]
    ones = jnp.ones((HEAD_DIM,), F32)
    gamma = jnp.concatenate([jnp.tile(g_q[0], N_HEADS), jnp.tile(g_q[1], N_HEADS),
                             jnp.tile(g_k[0], N_KV), jnp.tile(ones, N_KV),
                             jnp.tile(g_k[1], N_KV), jnp.tile(ones, N_KV)])[None, :]
    qs = jnp.concatenate([jnp.full((2 * Q_COLS,), HEAD_DIM ** -0.5, F32), jnp.ones((4 * KV_COLS,), F32)])[None, :]
    zk, ok = jnp.zeros((KV_COLS,), F32), jnp.ones((KV_COLS,), F32)
    isv = jnp.concatenate([jnp.zeros((2 * Q_COLS,), F32), zk, ok, zk, ok])[None, :]
    tables = (gamma, qs, isv)
    tm = _row_tile(batch * n_ctx, batch * n_ctx)
    ident = (jnp.ones((tm, HEAD_DIM), F32), jnp.zeros((tm, HEAD_DIM), F32))
    w_in_b = _col_tiled(w_in, _tile(w_in.shape[1], 512))
    qkv = _qkv_proj(x, mod_l, g, w_in_b, seq, tables, rope, seq)
    qkv_c = _qkv_proj(xc, mod_c, g, w_in_b, batch * n_ctx, tables, ident, None)
    qa_col, qb_col = 0, N_KV
    ka_col = 2 * Q_COLS // HEAD_DIM
    va_col, kb_col, vb_col = ka_col + N_KV, ka_col + 2 * N_KV, ka_col + 3 * N_KV
    o_a = _attention(sink, qkv, qkv_c, qkv, batch=batch, q_rows=seq, q_col=qa_col, k_col=ka_col, v_col=va_col,
                     has_mask=False, has_sink=False)
    o_b = _attention(sink, qkv, qkv_c, qkv, batch=batch, q_rows=seq, q_col=qb_col, k_col=kb_col, v_col=vb_col,
                     has_mask=True, has_sink=True)
    w_out_b = w_out.astype(BF16)
    zero_b = jnp.zeros((d,), F32)
    x = _outproj((o_a, o_b), w_out_b, zero_b, x, mod_l, seq)
    if ctx_out:
        co_a = _attention(sink, qkv_c, qkv_c, None, batch=batch, q_rows=n_ctx, q_col=qa_col, k_col=ka_col,
                          v_col=va_col, has_mask=False, has_sink=False)
        co_b = _attention(sink, qkv_c, qkv_c, None, batch=batch, q_rows=n_ctx, q_col=qb_col, k_col=kb_col,
                          v_col=vb_col, has_mask=False, has_sink=True)
        xc = _outproj((co_a, co_b), w_out_b, zero_b, xc, mod_c, batch * n_ctx)
    return x, xc


def _hyena_layer(x, mod, g, rows_per_mod, batch, w_in_b, b_in, w_conv, b_conv, filt, hy_bias, w_out_b, b_out):
    d = x.shape[1]
    n = x.shape[0] // batch
    w1, b1, fr1, w2, b2, fr2, w3 = filt
    hp = LANES
    feats = _filter_features(n, LANES)
    max_decay = math.log(HYENA_TARGET) / HYENA_FAST_PCT
    min_decay = math.log(HYENA_TARGET) / HYENA_SLOW_PCT
    absdelta = jnp.abs(jnp.linspace(min_decay, max_decay, d, dtype=F32))[None, :]
    taps = _hyena_filter_taps(
        feats, _pad_to(w1, (LANES, hp)), _pad_to(b1[None, :], (1, hp)), _pad_to(fr1[None, :], (1, hp)),
        _pad_to(w2, (hp, hp)), _pad_to(b2[None, :], (1, hp)), _pad_to(fr2[None, :], (1, hp)),
        _pad_to(w3, (hp, w3.shape[1])), absdelta, d)
    dft, dft_t = _dft_tables(n)
    spec = _hyena_spectrum(dft, taps)
    pc = _proj(x, mod, g, w_in_b, b_in, w_conv, b_conv, rows_per_mod, n)
    yr, yi = _dft_fwd(dft, pc, 0, spec, 0, batch, d)
    z = _dft_inv(dft_t, yr, yi, pc, 1, pc, 0, hy_bias[0], batch, d)
    yr, yi = _dft_fwd(dft, z, 0, spec, 1, batch, d)
    z = _dft_inv(dft_t, yr, yi, pc, 2, z, 0, hy_bias[1], batch, d)
    return _outproj_parity(z, w_out_b, b_out, x, mod, rows_per_mod, n)


def kernel(x, c, ctx, c_ctx, w_mod, b_mod, g_norm, w_ffn_in, w_ffn_out, w_attn_in, w_attn_out, g_q, g_k, sink, w_hy_in, b_hy_in, w_hy_conv, b_hy_conv, hf_w1, hf_b1, hf_freq1, hf_w2, hf_b2, hf_freq2, hf_w3, hy_bias, w_hy_out, b_hy_out):
    batch, seq, d = x.shape
    n_ctx = ctx.shape[1]
    depth = w_mod.shape[0]
    rows_c = batch * n_ctx
    rope = _rope_tables(seq)
    last_ctx = max(l for l in range(depth) if l % 2 == 0)

    r_pad = -(-(batch + 1) // SUBLANES) * SUBLANES
    c_all = _pad_to(jnp.concatenate([c, c_ctx[None, :]], axis=0), (r_pad, d))
    m_all = _mod_all(c_all, w_mod, b_mod).reshape(depth, r_pad, N_MOD, d)

    w_ffn_in_b = w_ffn_in.astype(BF16)
    w_ffn_out_b = w_ffn_out.astype(BF16)

    xl = x.reshape(batch * seq, d)
    xc = ctx.reshape(rows_c, d)
    for l in range(depth):
        i = l // 2
        ctx_live = l <= last_ctx
        ctx_full = l < last_ctx
        mods_l = [m_all[l, :batch, 3 * k:3 * k + 3] for k in range(3)]
        mods_c = [m_all[l, batch:batch + 1, 3 * k:3 * k + 3] for k in range(3)]
        xl = _ffn(xl, mods_l[0], g_norm[l, 0], w_ffn_in_b, w_ffn_out_b, l, 0, seq)
        if ctx_live:
            xc = _ffn(xc, mods_c[0], g_norm[l, 0], w_ffn_in_b, w_ffn_out_b, l, 0, rows_c)
        if l % 2 == 0:
            xl, xc = _attn_layer(xl, xc, mods_l[1], mods_c[1], g_norm[l, 1], w_attn_in[i], w_attn_out[i],
                                 g_q[i], g_k[i], sink[i], rope, batch, seq, n_ctx, ctx_full)
        else:
            w_in_b = _col_tiled(w_hy_in[i], _tile(w_hy_in.shape[2], 256))
            w_out_b = w_hy_out[i].astype(BF16)
            filt = (hf_w1[i], hf_b1[i], hf_freq1[i], hf_w2[i], hf_b2[i], hf_freq2[i], hf_w3[i])
            xl = _hyena_layer(xl, mods_l[1], g_norm[l, 1], seq, batch, w_in_b, b_hy_in[i], w_hy_conv[i],
                              b_hy_conv[i], filt, hy_bias[i], w_out_b, b_hy_out[i])
            if ctx_full:
                xc = _hyena_layer(xc, mods_c[1], g_norm[l, 1], rows_c, batch, w_in_b, b_hy_in[i], w_hy_conv[i],
                                  b_hy_conv[i], filt, hy_bias[i], w_out_b, b_hy_out[i])
        xl = _ffn(xl, mods_l[2], g_norm[l, 2], w_ffn_in_b, w_ffn_out_b, l, 1, seq)
        if ctx_full:
            xc = _ffn(xc, mods_c[2], g_norm[l, 2], w_ffn_in_b, w_ffn_out_b, l, 1, rows_c)
    return xl.reshape(batch, seq, d)
```

```python
import functools
import math

import jax
import jax.numpy as jnp
from jax import lax
from jax.experimental import pallas as pl
from jax.experimental.pallas import tpu as pltpu

HEAD_DIM = 128
N_HEADS = 8
N_KV = 2
GROUP = N_HEADS // N_KV
Q_COLS = N_HEADS * HEAD_DIM
KV_COLS = N_KV * HEAD_DIM
GRID_W = 64
WINDOW = 128
ROPE_THETA = 10000.0
N_MOD = 9
HYENA_ORDER = 2
HYENA_SHORT_K = 3
HYENA_BANDS = 16
HYENA_TARGET = 1e-2
HYENA_FAST_PCT = 0.3
HYENA_SLOW_PCT = 1.5
EPS = 1e-6
NEG_INF = -1e30

LANES = 128
SUBLANES = 8
VMEM_LIMIT_BYTES = 60 * 1024 * 1024

F32 = jnp.float32
BF16 = jnp.bfloat16
HIGHEST = lax.Precision.HIGHEST


def _params(*sem):
    return pltpu.CompilerParams(dimension_semantics=sem, vmem_limit_bytes=VMEM_LIMIT_BYTES)


def _tile(dim, pref):
    t = min(dim, pref)
    while dim % t:
        t //= 2
    return t


def _dot(a, b):
    return jnp.dot(a, b, preferred_element_type=F32)


def _dot_nt(a, b):
    return lax.dot_general(a, b, (((1,), (1,)), ((), ())), preferred_element_type=F32)


def _silu(v):
    return v * (1.0 / (1.0 + jnp.exp(-v)))


def _modulated(xf, g, shift, scale):
    ms = jnp.mean(xf * xf, axis=-1, keepdims=True)
    y = xf * lax.rsqrt(ms + EPS)
    return (y * g) * (1.0 + scale) + shift


def _mod_body(c_ref, w_ref, b_ref, o_ref):
    s = _silu(c_ref[...])
    o_ref[0] = jnp.dot(s, w_ref[0], preferred_element_type=F32, precision=HIGHEST) + b_ref[0]


def _mod_all(c_all, w_mod, b_mod):
    depth, d, nd = w_mod.shape
    r = c_all.shape[0]
    tn = _tile(nd, 2048)
    return pl.pallas_call(
        _mod_body,
        out_shape=jax.ShapeDtypeStruct((depth, r, nd), F32),
        grid=(depth, nd // tn),
        in_specs=[
            pl.BlockSpec((r, d), lambda l, j: (0, 0)),
            pl.BlockSpec((1, d, tn), lambda l, j: (l, 0, j)),
            pl.BlockSpec((1, 1, tn), lambda l, j: (l, 0, j)),
        ],
        out_specs=pl.BlockSpec((1, r, tn), lambda l, j: (l, 0, j)),
        compiler_params=_params("parallel", "parallel"),
        name="mod_vectors",
    )(c_all, w_mod, b_mod.reshape(depth, 1, nd))


def _ffn_body(x_ref, mod_ref, g_ref, wg_ref, wu_ref, wo_ref, o_ref, h_ref, acc_ref):
    j = pl.program_id(1)
    last = pl.num_programs(1) - 1

    def chunk(h):
        a = _dot(h, wg_ref[...])
        u = _dot(h, wu_ref[...])
        return _dot((_silu(a) * u).astype(BF16), wo_ref[...])

    @pl.when(j == 0)
    def _():
        h = _modulated(x_ref[...], g_ref[...], mod_ref[0, 0:1, :], mod_ref[0, 1:2, :]).astype(BF16)
        h_ref[...] = h
        acc_ref[...] = chunk(h)

    @pl.when(jnp.logical_and(j > 0, j < last))
    def _():
        acc_ref[...] += chunk(h_ref[...])

    @pl.when(jnp.logical_and(j > 0, j == last))
    def _():
        y = acc_ref[...] + chunk(h_ref[...])
        o_ref[...] = x_ref[...] + (0.5 * mod_ref[0, 2:3, :]) * y


def _ffn(x, mod, g, w_in, w_out, layer, half, rows_per_mod):
    rows, d = x.shape
    f = w_out.shape[2]
    tm = _tile(min(rows, rows_per_mod), 1024)
    tf = _tile(f, 512)
    nf = f // tf
    assert nf >= 2
    return pl.pallas_call(
        _ffn_body,
        out_shape=jax.ShapeDtypeStruct((rows, d), F32),
        grid=(rows // tm, nf),
        in_specs=[
            pl.BlockSpec((tm, d), lambda i, j: (i, 0)),
            pl.BlockSpec((1, 3, d), lambda i, j: (i * tm // rows_per_mod, 0, 0)),
            pl.BlockSpec((1, d), lambda i, j: (0, 0)),
            pl.BlockSpec((None, None, d, tf), lambda i, j: (layer, half, 0, j)),
            pl.BlockSpec((None, None, d, tf), lambda i, j: (layer, half, 0, nf + j)),
            pl.BlockSpec((None, None, tf, d), lambda i, j: (layer, half, j, 0)),
        ],
        out_specs=pl.BlockSpec((tm, d), lambda i, j: (i, 0), pipeline_mode=pl.Buffered(1)),
        scratch_shapes=[pltpu.VMEM((tm, d), BF16), pltpu.VMEM((tm, d), F32)],
        compiler_params=_params("parallel", "arbitrary"),
        name="ffn_half_step",
    )(x, mod, g.reshape(1, d), w_in, w_in, w_out)


def _proj_body(x_ref, mod_ref, g_ref, perm_ref, w_ref, b_ref, cw_ref, cb_ref, o_ref, h_ref, *, half_seq):
    tm = x_ref.shape[0]
    half = tm // 2
    grp = perm_ref.shape[0]

    @pl.when(pl.program_id(1) == 0)
    def _():
        for r in range(tm // grp):
            h = _modulated(x_ref[r * grp:(r + 1) * grp, :], g_ref[...], mod_ref[0, 0:1, :], mod_ref[0, 1:2, :])
            hp = _dot(perm_ref[...], h.astype(BF16)).astype(BF16)
            h_ref[r * (grp // 2):(r + 1) * (grp // 2), :] = hp[:grp // 2]
            h_ref[half + r * (grp // 2):half + (r + 1) * (grp // 2), :] = hp[grp // 2:]

    p = _dot(h_ref[...], w_ref[...]) + b_ref[...]
    pe, po = p[:half], p[half:]
    pos = lax.broadcasted_iota(jnp.int32, (half, 1), 0) % half_seq
    po_prev = jnp.where(pos == 0, 0.0, pltpu.roll(po, 1, 0))
    pe_next = jnp.where(pos == half_seq - 1, 0.0, pltpu.roll(pe, half - 1, 0))
    w0, w1, w2 = cw_ref[0:1, :], cw_ref[1:2, :], cw_ref[2:3, :]
    o_ref[0] = (((cb_ref[...] + po_prev * w0) + pe * w1) + po * w2).astype(o_ref.dtype)
    o_ref[1] = (((cb_ref[...] + pe * w0) + po * w1) + pe_next * w2).astype(o_ref.dtype)


def _qkv_body(x_ref, mod_ref, g_ref, w_ref, gamma_ref, qs_ref, isv_ref, cos_ref, sin_ref, o_ref, h_ref, p_ref):
    j = pl.program_id(1)
    last = pl.num_programs(1) - 1

    def finish():
        cos = cos_ref[...]
        sin = sin_ref[...]
        even = (lax.broadcasted_iota(jnp.int32, (1, HEAD_DIM), 1) % 2) == 0
        for hh in range(p_ref.shape[1] // HEAD_DIM):
            sl = slice(hh * HEAD_DIM, (hh + 1) * HEAD_DIM)
            ph = p_ref[:, sl]
            ms = jnp.mean(ph * ph, axis=-1, keepdims=True)
            nh = (ph * lax.rsqrt(ms + EPS)) * gamma_ref[:, sl]
            partner = jnp.where(even, pltpu.roll(nh, HEAD_DIM - 1, 1), pltpu.roll(nh, 1, 1))
            r = (nh * cos + partner * sin) * qs_ref[:, sl]
            o_ref[:, sl] = jnp.where(isv_ref[:, sl] > 0.0, ph, r).astype(o_ref.dtype)

    @pl.when(j == 0)
    def _():
        h = _modulated(x_ref[...], g_ref[...], mod_ref[0, 0:1, :], mod_ref[0, 1:2, :]).astype(BF16)
        h_ref[...] = h
        p_ref[...] = _dot(h, w_ref[...])

    @pl.when(jnp.logical_and(j > 0, j < last))
    def _():
        finish()
        p_ref[...] = _dot(h_ref[...], w_ref[...])

    @pl.when(j == last)
    def _():
        finish()


def _row_tile(rows, rows_per_mod):
    return _tile(min(rows, rows_per_mod), 1024)


def _parity_perm(tm):
    r = jnp.arange(tm, dtype=jnp.int32)[:, None]
    c = jnp.arange(tm, dtype=jnp.int32)[None, :]
    src = jnp.where(r < tm // 2, 2 * r, 2 * (r - tm // 2) + 1)
    return (c == src).astype(BF16)


def _col_tiled(w, tn):
    k, n = w.shape
    return w.reshape(k, n // tn, tn).transpose(1, 0, 2).astype(BF16)


PERM_GROUP = 256


def _proj(x, mod, g, w_t, b, conv_w, conv_b, rows_per_mod, seq):
    rows, d = x.shape
    nj, _, tn = w_t.shape
    n = nj * tn
    tm = _tile(min(rows, rows_per_mod), max(seq, 1024))
    grp = min(PERM_GROUP, tm)
    assert tm % seq == 0 and tm % grp == 0 and grp % (4 * SUBLANES) == 0
    col = pl.BlockSpec((1, tn), lambda i, j: (0, j))
    return pl.pallas_call(
        functools.partial(_proj_body, half_seq=seq // 2),
        out_shape=jax.ShapeDtypeStruct((2, rows // 2, n), BF16),
        grid=(rows // tm, n // tn),
        in_specs=[
            pl.BlockSpec((tm, d), lambda i, j: (i, 0)),
            pl.BlockSpec((1, 3, d), lambda i, j: (i * tm // rows_per_mod, 0, 0)),
            pl.BlockSpec((1, d), lambda i, j: (0, 0)),
            pl.BlockSpec((grp, grp), lambda i, j: (0, 0)),
            pl.BlockSpec((None, d, tn), lambda i, j: (j, 0, 0)),
            col,
            pl.BlockSpec((HYENA_SHORT_K, tn), lambda i, j: (0, j)),
            col,
        ],
        out_specs=pl.BlockSpec((2, tm // 2, tn), lambda i, j: (0, i, j)),
        scratch_shapes=[pltpu.VMEM((tm, d), BF16)],
        compiler_params=_params("parallel", "arbitrary"),
        name="hyena_project_conv",
    )(x, mod, g.reshape(1, d), _parity_perm(grp), w_t, b.reshape(1, n), conv_w, conv_b.reshape(1, n))


def _qkv_proj(x, mod, g, w_t, rows_per_mod, tables, rope, seq):
    rows, d = x.shape
    nj, _, tn = w_t.shape
    n = nj * tn
    tm = _row_tile(rows, rows_per_mod)
    gamma, qs, isv = tables
    cos, sin = rope

    def prev(j):
        return jnp.maximum(j - 1, 0)

    col = pl.BlockSpec((1, tn), lambda i, j: (0, prev(j)))
    if seq is None:
        rope_spec = pl.BlockSpec((tm, HEAD_DIM), lambda i, j: (0, 0))
    else:
        rope_spec = pl.BlockSpec((tm, HEAD_DIM), lambda i, j: (i % (seq // tm), 0))
    return pl.pallas_call(
        _qkv_body,
        out_shape=jax.ShapeDtypeStruct((rows, n), BF16),
        grid=(rows // tm, nj + 1),
        in_specs=[
            pl.BlockSpec((tm, d), lambda i, j: (i, 0)),
            pl.BlockSpec((1, 3, d), lambda i, j: (i * tm // rows_per_mod, 0, 0)),
            pl.BlockSpec((1, d), lambda i, j: (0, 0)),
            pl.BlockSpec((None, d, tn), lambda i, j: (jnp.minimum(j, nj - 1), 0, 0)),
            col, col, col, rope_spec, rope_spec,
        ],
        out_specs=pl.BlockSpec((tm, tn), lambda i, j: (i, prev(j))),
        scratch_shapes=[pltpu.VMEM((tm, d), BF16), pltpu.VMEM((tm, tn), F32)],
        compiler_params=_params("arbitrary", "arbitrary"),
        name="qkv_project",
    )(x, mod, g.reshape(1, d), w_t, gamma, qs, isv, cos, sin)


def _attn_body(*refs, has_lat, has_mask, has_sink, tq):
    sink_ref, q_ref, kc_ref, vc_ref = refs[:4]
    if has_lat:
        k_ref, v_ref, o_ref = refs[4:7]
    else:
        o_ref = refs[4]
    g = pl.program_id(1)
    qi = pl.program_id(2)
    if has_lat:
        l_rows = k_ref.shape[0]
        if has_mask:
            kw = min(l_rows, tq + 2 * WINDOW)
            k0 = pl.multiple_of(jnp.clip(qi * tq - WINDOW, 0, l_rows - kw), WINDOW)
            keys = pl.ds(k0, kw)
        else:
            k0, keys = 0, slice(None)
    if has_mask:
        qpos = qi * tq + lax.broadcasted_iota(jnp.int32, (tq, 1), 0)
        kpos = k0 + lax.broadcasted_iota(jnp.int32, (1, kw), 1)
        band = jnp.abs(kpos - qpos) <= WINDOW
    heads = [slice(hh * HEAD_DIM, (hh + 1) * HEAD_DIM) for hh in range(GROUP)]
    s1 = [_dot_nt(q_ref[:, sl], kc_ref[...]) for sl in heads]
    m = [jnp.max(s, axis=-1, keepdims=True) for s in s1]
    if has_lat:
        s2 = [_dot_nt(q_ref[:, sl], k_ref[keys, :]) for sl in heads]
        if has_mask:
            s2 = [jnp.where(band, s, NEG_INF) for s in s2]
        m = [jnp.maximum(mh, jnp.max(s, axis=-1, keepdims=True)) for mh, s in zip(m, s2)]
    if has_sink:
        sk = [sink_ref[g * GROUP + hh] for hh in range(GROUP)]
        m = [jnp.maximum(mh, skh) for mh, skh in zip(m, sk)]
    for hh, sl in enumerate(heads):
        p1 = jnp.exp(s1[hh] - m[hh])
        l = jnp.sum(p1, axis=-1, keepdims=True)
        o = _dot(p1.astype(BF16), vc_ref[...])
        if has_lat:
            p2 = jnp.exp(s2[hh] - m[hh])
            l = l + jnp.sum(p2, axis=-1, keepdims=True)
            o = o + _dot(p2.astype(BF16), v_ref[keys, :])
        if has_sink:
            l = l + jnp.exp(sk[hh] - m[hh])
        o_ref[:, sl] = (o / l).astype(o_ref.dtype)


def _attention(sink, q_src, ctx_src, lat_src, *, batch, q_rows, q_col, k_col, v_col, has_mask, has_sink):
    tq = _tile(q_rows, 256 if has_mask else 512)
    assert not has_mask or tq % WINDOW == 0
    c_rows = ctx_src.shape[0] // batch
    gw = GROUP * HEAD_DIM
    in_specs = [
        pl.BlockSpec(memory_space=pltpu.SMEM),
        pl.BlockSpec((tq, gw), lambda b, g, i: (b * (q_rows // tq) + i, q_col + g)),
        pl.BlockSpec((c_rows, HEAD_DIM), lambda b, g, i: (b, k_col + g)),
        pl.BlockSpec((c_rows, HEAD_DIM), lambda b, g, i: (b, v_col + g)),
    ]
    args = [sink, q_src, ctx_src, ctx_src]
    has_lat = lat_src is not None
    if has_lat:
        l_rows = lat_src.shape[0] // batch
        in_specs += [
            pl.BlockSpec((l_rows, HEAD_DIM), lambda b, g, i: (b, k_col + g)),
            pl.BlockSpec((l_rows, HEAD_DIM), lambda b, g, i: (b, v_col + g)),
        ]
        args += [lat_src, lat_src]
    return pl.pallas_call(
        functools.partial(_attn_body, has_lat=has_lat, has_mask=has_mask, has_sink=has_sink, tq=tq),
        out_shape=jax.ShapeDtypeStruct((batch * q_rows, Q_COLS), BF16),
        grid=(batch, N_KV, q_rows // tq),
        in_specs=in_specs,
        out_specs=pl.BlockSpec((tq, gw), lambda b, g, i: (b * (q_rows // tq) + i, g)),
        compiler_params=_params("parallel", "parallel", "parallel"),
        name="gqa_attention",
    )(*args)


def _outproj_body(*refs, n_in):
    a_refs, w_refs = refs[:n_in], refs[n_in:2 * n_in]
    b_ref, x_ref, mod_ref, o_ref = refs[2 * n_in:]
    y = b_ref[...] + _dot(a_refs[0][...], w_refs[0][...])
    for a_ref, w_ref in zip(a_refs[1:], w_refs[1:]):
        y = y + _dot(a_ref[...], w_ref[...])
    o_ref[...] = x_ref[...] + mod_ref[0, 2:3, :] * y


def _outproj(acts, weight, bias, x, mod, rows_per_mod):
    rows, d = x.shape
    tm = _tile(min(rows, rows_per_mod), 512)
    n_in = len(acts)
    k = acts[0].shape[1]
    assert all(a.shape[1] == k for a in acts) and weight.shape[0] == n_in * k
    in_specs = [pl.BlockSpec((tm, k), lambda i: (i, 0)) for _ in acts]
    in_specs += [pl.BlockSpec((k, d), lambda i, n=n: (n, 0)) for n in range(n_in)]
    in_specs += [
        pl.BlockSpec((1, d), lambda i: (0, 0)),
        pl.BlockSpec((tm, d), lambda i: (i, 0)),
        pl.BlockSpec((1, 3, d), lambda i: (i * tm // rows_per_mod, 0, 0)),
    ]
    return pl.pallas_call(
        functools.partial(_outproj_body, n_in=n_in),
        out_shape=jax.ShapeDtypeStruct((rows, d), F32),
        grid=(rows // tm,),
        in_specs=in_specs,
        out_specs=pl.BlockSpec((tm, d), lambda i: (i, 0)),
        compiler_params=_params("parallel"),
        name="outproj_residual",
    )(*acts, *([weight] * n_in), bias.reshape(1, d), x, mod)


def _outproj_parity_body(z_ref, unperm_ref, w_ref, b_ref, x_ref, mod_ref, o_ref):
    grp = unperm_ref.shape[0]
    hg = grp // 2
    parts = []
    for r in range(x_ref.shape[0] // grp):
        zg = jnp.concatenate([z_ref[0, r * hg:(r + 1) * hg, :], z_ref[1, r * hg:(r + 1) * hg, :]], axis=0)
        parts.append(_dot(unperm_ref[...], zg).astype(BF16))
    y = b_ref[...] + _dot(jnp.concatenate(parts, axis=0), w_ref[...])
    o_ref[...] = x_ref[...] + mod_ref[0, 2:3, :] * y


def _outproj_parity(z, weight, bias, x, mod, rows_per_mod, seq):
    rows, d = x.shape
    k = z.shape[2]
    tm = _tile(min(rows, rows_per_mod), 512)
    grp = min(PERM_GROUP, tm)
    assert tm % grp == 0 and (tm % seq == 0 or seq % tm == 0)
    return pl.pallas_call(
        _outproj_parity_body,
        out_shape=jax.ShapeDtypeStruct((rows, d), F32),
        grid=(rows // tm,),
        in_specs=[
            pl.BlockSpec((2, tm // 2, k), lambda i: (0, i, 0)),
            pl.BlockSpec((grp, grp), lambda i: (0, 0)),
            pl.BlockSpec((k, d), lambda i: (0, 0)),
            pl.BlockSpec((1, d), lambda i: (0, 0)),
            pl.BlockSpec((tm, d), lambda i: (i, 0)),
            pl.BlockSpec((1, 3, d), lambda i: (i * tm // rows_per_mod, 0, 0)),
        ],
        out_specs=pl.BlockSpec((tm, d), lambda i: (i, 0)),
        compiler_params=_params("parallel"),
        name="outproj_residual_parity",
    )(z, _parity_perm(grp).T, weight, bias.reshape(1, d), x, mod)


def _filter_body(feats_ref, w1_ref, b1_ref, fr1_ref, w2_ref, b2_ref, fr2_ref, w3f_ref, w3b_ref, dl_ref, o_ref,
                 h2_ref):
    feats = feats_ref[...]

    @pl.when(jnp.logical_and(pl.program_id(0) == 0, pl.program_id(1) == 0))
    def _():
        h1 = jnp.sin(fr1_ref[...] * (jnp.dot(feats, w1_ref[...], preferred_element_type=F32, precision=HIGHEST)
                                     + b1_ref[...]))
        h2_ref[...] = jnp.sin(fr2_ref[...] * (jnp.dot(h1, w2_ref[...], preferred_element_type=F32,
                                                      precision=HIGHEST) + b2_ref[...]))

    h2 = h2_ref[...]
    decay = jnp.exp(-feats[:, 0:1] * dl_ref[...])
    fwd = jnp.dot(h2, w3f_ref[...], preferred_element_type=F32, precision=HIGHEST) * decay
    bwd = jnp.dot(h2, w3b_ref[...], preferred_element_type=F32, precision=HIGHEST) * decay
    row = lax.broadcasted_iota(jnp.int32, (fwd.shape[0], 1), 0)
    bwd = jnp.where(row == 0, 0.0, bwd)
    nrm = lax.rsqrt(jnp.sum(fwd * fwd + bwd * bwd, axis=0, keepdims=True) + EPS)
    o_ref[0] = ((fwd + bwd) * nrm).astype(o_ref.dtype)
    o_ref[1] = ((fwd - bwd) * nrm).astype(o_ref.dtype)


def _hyena_filter_taps(feats, w1, b1, fr1, w2, b2, fr2, w3, absdelta, d):
    n, fp = feats.shape
    hp = w1.shape[1]
    tn = _tile(d, 512)
    nd = d // tn
    vec = pl.BlockSpec((1, hp), lambda o, j: (0, 0))
    return pl.pallas_call(
        _filter_body,
        out_shape=jax.ShapeDtypeStruct((2, n, HYENA_ORDER * d), BF16),
        grid=(HYENA_ORDER, nd),
        in_specs=[
            pl.BlockSpec((n, fp), lambda o, j: (0, 0)),
            pl.BlockSpec((fp, hp), lambda o, j: (0, 0)), vec, vec,
            pl.BlockSpec((hp, hp), lambda o, j: (0, 0)), vec, vec,
            pl.BlockSpec((hp, tn), lambda o, j: (0, (2 * o) * nd + j)),
            pl.BlockSpec((hp, tn), lambda o, j: (0, (2 * o + 1) * nd + j)),
            pl.BlockSpec((1, tn), lambda o, j: (0, j)),
        ],
        out_specs=pl.BlockSpec((2, n, tn), lambda o, j: (0, 0, o * nd + j)),
        scratch_shapes=[pltpu.VMEM((n, hp), F32)],
        compiler_params=_params("arbitrary", "arbitrary"),
        name="hyena_filter_taps",
    )(feats, w1, b1, fr1, w2, b2, fr2, w3, w3, absdelta)


def _spectrum_body(ce_ref, se_ref, co_ref, so_ref, se_, so_, de_, do_, o_ref, *, inv_n):
    i = pl.program_id(0)
    tk = ce_ref.shape[0]
    pr = _dot(ce_ref[...], se_[...])
    qr = _dot(co_ref[...], so_[...])
    pi = _dot(se_ref[...], de_[...])
    qi = _dot(so_ref[...], do_[...])
    row0 = (i * tk + lax.broadcasted_iota(jnp.int32, (tk, 1), 0)) == 0
    w_re = jnp.where(row0, inv_n, 2.0 * inv_n)
    o_ref[0] = (pr + qr) * w_re
    o_ref[1] = (pr - qr) * w_re
    o_ref[2] = (pi + qi) * (2.0 * inv_n)
    o_ref[3] = (qi - pi) * (2.0 * inv_n)

    @pl.when(i == 0)
    def _():
        rows = 2 * SUBLANES
        mid_r = _dot(se_ref[0:rows, :], se_[...])
        o_ref[2, 0:1, :] = mid_r[0:1, :] * (2.0 * inv_n)
        o_ref[3, 0:1, :] = qi[0:1, :] * (2.0 * inv_n)


def _hyena_spectrum(dft, taps):
    n = taps.shape[1]
    cols = taps.shape[2]
    half = n // 2
    tk = _tile(half, 1024)
    tn = _tile(cols, 512)
    nr = half // tk
    mat = lambda par, part: pl.BlockSpec((None, tk, half), lambda i, j: (par, part * nr + i, 0))
    tap = lambda which, par: pl.BlockSpec((None, half, tn), lambda i, j: (which, par, j))
    return pl.pallas_call(
        functools.partial(_spectrum_body, inv_n=1.0 / (2 * n)),
        out_shape=jax.ShapeDtypeStruct((4, half, cols), F32),
        grid=(nr, cols // tn),
        in_specs=[mat(0, 0), mat(0, 1), mat(1, 0), mat(1, 1), tap(0, 0), tap(0, 1), tap(1, 0), tap(1, 1)],
        out_specs=pl.BlockSpec((4, tk, tn), lambda i, j: (0, i, j)),
        compiler_params=_params("parallel", "parallel"),
        name="hyena_filter_spectrum",
    )(dft, dft, dft, dft, taps, taps, taps, taps)


def _dft_fwd_body(ce_ref, se_ref, co_ref, so_ref, ze_ref, zo_ref, k_ref, yr_ref, yi_ref):
    i = pl.program_id(1)
    tk = ce_ref.shape[0]
    row0 = (i * tk + lax.broadcasted_iota(jnp.int32, (tk, 1), 0)) == 0
    tn = ze_ref.shape[1]
    sub = min(tn, 2 * LANES)
    for c in range(tn // sub):
        cols = slice(c * sub, (c + 1) * sub)
        ze, zo = ze_ref[:, cols], zo_ref[:, cols]
        pr = _dot(ce_ref[...], ze)
        pi = _dot(se_ref[...], ze)
        qr = _dot(co_ref[...], zo)
        qi = _dot(so_ref[...], zo)
        kr_lo, kr_hi, ki_lo, ki_hi = k_ref[0, :, cols], k_ref[1, :, cols], k_ref[2, :, cols], k_ref[3, :, cols]
        zr_lo, zr_hi, zi_lo, zi_hi = pr + qr, pr - qr, pi + qi, qi - pi
        yr_lo = zr_lo * kr_lo - jnp.where(row0, 0.0, zi_lo * ki_lo)
        yr_hi = zr_hi * kr_hi - jnp.where(row0, 0.0, zi_hi * ki_hi)
        yi_lo = zr_lo * ki_lo + zi_lo * kr_lo
        yi_hi = zr_hi * ki_hi + zi_hi * kr_hi
        mid_r = pi * ki_lo - qi * ki_hi
        mid_i = pi * ki_hi + qi * ki_lo
        yr_ref[0, :, cols] = (yr_lo + yr_hi).astype(yr_ref.dtype)
        yr_ref[1, :, cols] = (yr_lo - yr_hi).astype(yr_ref.dtype)
        yi_ref[0, :, cols] = jnp.where(row0, mid_r, yi_lo - yi_hi).astype(yi_ref.dtype)
        yi_ref[1, :, cols] = jnp.where(row0, mid_i, yi_lo + yi_hi).astype(yi_ref.dtype)


def _dft_fwd(dft, z_src, z_col, spec, order, batch, d):
    half = z_src.shape[1] // batch
    tk = _tile(half, 1024)
    tn = _tile(d, 512)
    nr, nd = half // tk, d // tn
    out = jax.ShapeDtypeStruct((2, batch * half, d), BF16)
    once = pl.Buffered(1)
    mat = lambda par, part: pl.BlockSpec((None, tk, half), lambda j, i, b: (par, part * nr + i, 0),
                                         pipeline_mode=once)
    zin = lambda par: pl.BlockSpec((None, half, tn), lambda j, i, b: (par, b, z_col * nd + j))
    res = pl.BlockSpec((2, tk, tn), lambda j, i, b: (0, b * nr + i, j))
    return pl.pallas_call(
        _dft_fwd_body,
        out_shape=(out, out),
        grid=(nd, nr, batch),
        in_specs=[
            mat(0, 0), mat(0, 1), mat(1, 0), mat(1, 1), zin(0), zin(1),
            pl.BlockSpec((4, tk, tn), lambda j, i, b: (0, i, order * nd + j), pipeline_mode=once),
        ],
        out_specs=(res, res),
        compiler_params=_params("parallel", "parallel", "parallel"),
        name="hyena_dft_forward",
    )(dft, dft, dft, dft, z_src, z_src, spec)


def _dft_inv_body(atr_ref, ati_ref, yr_ref, yi_ref, g_ref, zp_ref, bias_ref, o_ref):
    y = _dot(atr_ref[...], yr_ref[...]) + _dot(ati_ref[...], yi_ref[...])
    zp = zp_ref[...].astype(F32)
    o_ref[...] = (g_ref[...].astype(F32) * (y + zp * bias_ref[...])).astype(o_ref.dtype)


def _dft_inv(dft_t, yr, yi, gate_src, gate_col, z_src, z_col, bias, batch, d):
    half = yr.shape[1] // batch
    tt = _tile(half, 1024)
    tn = _tile(d, 1024)
    nr, nd = half // tt, d // tn
    row = lambda cols: pl.BlockSpec((None, tt, tn), lambda j, p, i, b: (p, b * nr + i, cols * nd + j))
    return pl.pallas_call(
        _dft_inv_body,
        out_shape=jax.ShapeDtypeStruct((2, batch * half, d), BF16),
        grid=(nd, 2, nr, batch),
        in_specs=[
            pl.BlockSpec((None, tt, half), lambda j, p, i, b: (p, i, 0)),
            pl.BlockSpec((None, tt, half), lambda j, p, i, b: (p, i, 1)),
            pl.BlockSpec((None, half, tn), lambda j, p, i, b: (p, b, j)),
            pl.BlockSpec((None, half, tn), lambda j, p, i, b: (p, b, j)),
            row(gate_col), row(z_col),
            pl.BlockSpec((1, tn), lambda j, p, i, b: (0, j)),
        ],
        out_specs=row(0),
        compiler_params=_params("parallel", "parallel", "parallel", "parallel"),
        name="hyena_dft_inverse",
    )(dft_t, dft_t, yr, yi, gate_src, z_src, bias.reshape(1, d))


def _rope_tables(n_tok):
    rows = n_tok // GRID_W
    r = jnp.repeat(jnp.arange(rows), GRID_W).astype(F32)
    col = jnp.tile(jnp.arange(GRID_W), rows).astype(F32)
    half = HEAD_DIM // 2
    inv = ROPE_THETA ** (-jnp.arange(0, half, 2, dtype=F32) / half)
    ang = jnp.concatenate([r[:, None] * inv, col[:, None] * inv], axis=-1)
    cos = jnp.repeat(jnp.cos(ang), 2, axis=-1)
    sin = jnp.repeat(jnp.sin(ang), 2, axis=-1)
    sign = jnp.where(jnp.arange(HEAD_DIM) % 2 == 0, -1.0, 1.0).astype(F32)
    return cos, sin * sign


def _dft_tables(n):
    big = 2 * n
    half = n // 2
    k = jnp.arange(half, dtype=jnp.int32)[:, None]
    m = jnp.arange(half, dtype=jnp.int32)[None, :]
    alt = jnp.where(m % 2 == 0, 1.0, -1.0).astype(F32)
    mats = []
    for par in range(2):
        ang = ((k * (2 * m + par)) % big).astype(F32) * (2.0 * math.pi / big)
        s = jnp.where(k == 0, alt if par == 0 else -alt, -jnp.sin(ang))
        mats.append(jnp.concatenate([jnp.cos(ang), s], axis=0))
    a = jnp.stack(mats).astype(BF16)
    return a, a.transpose(0, 2, 1)


def _filter_features(n, width):
    t = jnp.linspace(0.0, 1.0, n, dtype=F32)[:, None]
    w = (2.0 * math.pi / n) * jnp.arange(n, dtype=F32)[:, None]
    bands = jnp.linspace(1e-4, HYENA_BANDS - 1, HYENA_BANDS, dtype=F32)
    ang = w * bands[None, :]
    feats = jnp.concatenate([t, jnp.cos(ang), -jnp.sin(ang)], axis=-1)
    feats = jnp.concatenate([feats[0::2], feats[1::2]], axis=0)
    return jnp.pad(feats, ((0, 0), (0, width - feats.shape[1])))


def _pad_to(a, shape):
    return jnp.pad(a, [(0, s - d) for d, s in zip(a.shape, shape)])


def _attn_layer(x, xc, mod_l, mod_c, g, w_in, w_out, g_q, g_k, sink, rope, batch, seq, n_ctx, ctx_out):
    d = x.shape[1]
    ones = jnp.ones((HEAD_DIM,), F32)
    gamma = jnp.concatenate([jnp.tile(g_q[0], N_HEADS), jnp.tile(g_q[1], N_HEADS),
                             jnp.tile(g_k[0], N_KV), jnp.tile(ones, N_KV),
                             jnp.tile(g_k[1], N_KV), jnp.tile(ones, N_KV)])[None, :]
    qs = jnp.concatenate([jnp.full((2 * Q_COLS,), HEAD_DIM ** -0.5, F32), jnp.ones((4 * KV_COLS,), F32)])[None, :]
    zk, ok = jnp.zeros((KV_COLS,), F32), jnp.ones((KV_COLS,), F32)
    isv = jnp.concatenate([jnp.zeros((2 * Q_COLS,), F32), zk, ok, zk, ok])[None, :]
    tables = (gamma, qs, isv)
    tm = _row_tile(batch * n_ctx, batch * n_ctx)
    ident = (jnp.ones((tm, HEAD_DIM), F32), jnp.zeros((tm, HEAD_DIM), F32))
    w_in_b = _col_tiled(w_in, _tile(w_in.shape[1], 512))
    qkv = _qkv_proj(x, mod_l, g, w_in_b, seq, tables, rope, seq)
    qkv_c = _qkv_proj(xc, mod_c, g, w_in_b, batch * n_ctx, tables, ident, None)
    qa_col, qb_col = 0, N_KV
    ka_col = 2 * Q_COLS // HEAD_DIM
    va_col, kb_col, vb_col = ka_col + N_KV, ka_col + 2 * N_KV, ka_col + 3 * N_KV
    o_a = _attention(sink, qkv, qkv_c, qkv, batch=batch, q_rows=seq, q_col=qa_col, k_col=ka_col, v_col=va_col,
                     has_mask=False, has_sink=False)
    o_b = _attention(sink, qkv, qkv_c, qkv, batch=batch, q_rows=seq, q_col=qb_col, k_col=kb_col, v_col=vb_col,
                     has_mask=True, has_sink=True)
    w_out_b = w_out.astype(BF16)
    zero_b = jnp.zeros((d,), F32)
    x = _outproj((o_a, o_b), w_out_b, zero_b, x, mod_l, seq)
    if ctx_out:
        co_a = _attention(sink, qkv_c, qkv_c, None, batch=batch, q_rows=n_ctx, q_col=qa_col, k_col=ka_col,
                          v_col=va_col, has_mask=False, has_sink=False)
        co_b = _attention(sink, qkv_c, qkv_c, None, batch=batch, q_rows=n_ctx, q_col=qb_col, k_col=kb_col,
                          v_col=vb_col, has_mask=False, has_sink=True)
        xc = _outproj((co_a, co_b), w_out_b, zero_b, xc, mod_c, batch * n_ctx)
    return x, xc


def _hyena_layer(x, mod, g, rows_per_mod, batch, w_in_b, b_in, w_conv, b_conv, filt, hy_bias, w_out_b, b_out):
    d = x.shape[1]
    n = x.shape[0] // batch
    w1, b1, fr1, w2, b2, fr2, w3 = filt
    hp = LANES
    feats = _filter_features(n, LANES)
    max_decay = math.log(HYENA_TARGET) / HYENA_FAST_PCT
    min_decay = math.log(HYENA_TARGET) / HYENA_SLOW_PCT
    absdelta = jnp.abs(jnp.linspace(min_decay, max_decay, d, dtype=F32))[None, :]
    taps = _hyena_filter_taps(
        feats, _pad_to(w1, (LANES, hp)), _pad_to(b1[None, :], (1, hp)), _pad_to(fr1[None, :], (1, hp)),
        _pad_to(w2, (hp, hp)), _pad_to(b2[None, :], (1, hp)), _pad_to(fr2[None, :], (1, hp)),
        _pad_to(w3, (hp, w3.shape[1])), absdelta, d)
    dft, dft_t = _dft_tables(n)
    spec = _hyena_spectrum(dft, taps)
    pc = _proj(x, mod, g, w_in_b, b_in, w_conv, b_conv, rows_per_mod, n)
    yr, yi = _dft_fwd(dft, pc, 0, spec, 0, batch, d)
    z = _dft_inv(dft_t, yr, yi, pc, 1, pc, 0, hy_bias[0], batch, d)
    yr, yi = _dft_fwd(dft, z, 0, spec, 1, batch, d)
    z = _dft_inv(dft_t, yr, yi, pc, 2, z, 0, hy_bias[1], batch, d)
    return _outproj_parity(z, w_out_b, b_out, x, mod, rows_per_mod, n)


def kernel(x, c, ctx, c_ctx, w_mod, b_mod, g_norm, w_ffn_in, w_ffn_out, w_attn_in, w_attn_out, g_q, g_k, sink, w_hy_in, b_hy_in, w_hy_conv, b_hy_conv, hf_w1, hf_b1, hf_freq1, hf_w2, hf_b2, hf_freq2, hf_w3, hy_bias, w_hy_out, b_hy_out):
    batch, seq, d = x.shape
    n_ctx = ctx.shape[1]
    depth = w_mod.shape[0]
    rows_c = batch * n_ctx
    rope = _rope_tables(seq)
    last_ctx = max(l for l in range(depth) if l % 2 == 0)

    r_pad = -(-(batch + 1) // SUBLANES) * SUBLANES
    c_all = _pad_to(jnp.concatenate([c, c_ctx[None, :]], axis=0), (r_pad, d))
    m_all = _mod_all(c_all, w_mod, b_mod).reshape(depth, r_pad, N_MOD, d)

    w_ffn_in_b = w_ffn_in.astype(BF16)
    w_ffn_out_b = w_ffn_out.astype(BF16)

    xl = x.reshape(batch * seq, d)
    xc = ctx.reshape(rows_c, d)
    for l in range(depth):
        i = l // 2
        ctx_live = l <= last_ctx
        ctx_full = l < last_ctx
        mods_l = [m_all[l, :batch, 3 * k:3 * k + 3] for k in range(3)]
        mods_c = [m_all[l, batch:batch + 1, 3 * k:3 * k + 3] for k in range(3)]
        xl = _ffn(xl, mods_l[0], g_norm[l, 0], w_ffn_in_b, w_ffn_out_b, l, 0, seq)
        if ctx_live:
            xc = _ffn(xc, mods_c[0], g_norm[l, 0], w_ffn_in_b, w_ffn_out_b, l, 0, rows_c)
        if l % 2 == 0:
            xl, xc = _attn_layer(xl, xc, mods_l[1], mods_c[1], g_norm[l, 1], w_attn_in[i], w_attn_out[i],
                                 g_q[i], g_k[i], sink[i], rope, batch, seq, n_ctx, ctx_full)
        else:
            w_in_b = _col_tiled(w_hy_in[i], _tile(w_hy_in.shape[2], 256))
            w_out_b = w_hy_out[i].astype(BF16)
            filt = (hf_w1[i], hf_b1[i], hf_freq1[i], hf_w2[i], hf_b2[i], hf_freq2[i], hf_w3[i])
            xl = _hyena_layer(xl, mods_l[1], g_norm[l, 1], seq, batch, w_in_b, b_hy_in[i], w_hy_conv[i],
                              b_hy_conv[i], filt, hy_bias[i], w_out_b, b_hy_out[i])
            if ctx_full:
                xc = _hyena_layer(xc, mods_c[1], g_norm[l, 1], rows_c, batch, w_in_b, b_hy_in[i], w_hy_conv[i],
                                  b_hy_conv[i], filt, hy_bias[i], w_out_b, b_hy_out[i])
        xl = _ffn(xl, mods_l[2], g_norm[l, 2], w_ffn_in_b, w_ffn_out_b, l, 1, seq)
        if ctx_full:
            xc = _ffn(xc, mods_c[2], g_norm[l, 2], w_ffn_in_b, w_ffn_out_b, l, 1, rows_c)
    return xl.reshape(batch, seq, d)
```

```python
import functools
import math

import jax
import jax.numpy as jnp
from jax import lax
from jax.experimental import pallas as pl
from jax.experimental.pallas import tpu as pltpu

HEAD_DIM = 128
N_HEADS = 8
N_KV = 2
GROUP = N_HEADS // N_KV
Q_COLS = N_HEADS * HEAD_DIM
KV_COLS = N_KV * HEAD_DIM
GRID_W = 64
WINDOW = 128
ROPE_THETA = 10000.0
N_MOD = 9
HYENA_ORDER = 2
HYENA_SHORT_K = 3
HYENA_BANDS = 16
HYENA_TARGET = 1e-2
HYENA_FAST_PCT = 0.3
HYENA_SLOW_PCT = 1.5
EPS = 1e-6
NEG_INF = -1e30

LANES = 128
SUBLANES = 8
VMEM_LIMIT_BYTES = 60 * 1024 * 1024

F32 = jnp.float32
BF16 = jnp.bfloat16
HIGHEST = lax.Precision.HIGHEST


def _params(*sem):
    return pltpu.CompilerParams(dimension_semantics=sem, vmem_limit_bytes=VMEM_LIMIT_BYTES)


def _tile(dim, pref):
    t = min(dim, pref)
    while dim % t:
        t //= 2
    return t


def _dot(a, b):
    return jnp.dot(a, b, preferred_element_type=F32)


def _dot_nt(a, b):
    return lax.dot_general(a, b, (((1,), (1,)), ((), ())), preferred_element_type=F32)


def _silu(v):
    return v * (1.0 / (1.0 + jnp.exp(-v)))


def _modulated(xf, g, shift, scale):
    ms = jnp.mean(xf * xf, axis=-1, keepdims=True)
    y = xf * lax.rsqrt(ms + EPS)
    return (y * g) * (1.0 + scale) + shift


def _mod_body(c_ref, w_ref, b_ref, o_ref):
    s = _silu(c_ref[...])
    o_ref[0] = jnp.dot(s, w_ref[0], preferred_element_type=F32, precision=HIGHEST) + b_ref[0]


def _mod_all(c_all, w_mod, b_mod):
    depth, d, nd = w_mod.shape
    r = c_all.shape[0]
    tn = _tile(nd, 2048)
    return pl.pallas_call(
        _mod_body,
        out_shape=jax.ShapeDtypeStruct((depth, r, nd), F32),
        grid=(depth, nd // tn),
        in_specs=[
            pl.BlockSpec((r, d), lambda l, j: (0, 0)),
            pl.BlockSpec((1, d, tn), lambda l, j: (l, 0, j)),
            pl.BlockSpec((1, 1, tn), lambda l, j: (l, 0, j)),
        ],
        out_specs=pl.BlockSpec((1, r, tn), lambda l, j: (l, 0, j)),
        compiler_params=_params("parallel", "parallel"),
        name="mod_vectors",
    )(c_all, w_mod, b_mod.reshape(depth, 1, nd))


def _ffn_body(x_ref, mod_ref, g_ref, wg_ref, wu_ref, wo_ref, o_ref, h_ref, acc_ref):
    j = pl.program_id(1)
    last = pl.num_programs(1) - 1

    def chunk(h):
        a = _dot(h, wg_ref[...])
        u = _dot(h, wu_ref[...])
        return _dot((_silu(a) * u).astype(BF16), wo_ref[...])

    @pl.when(j == 0)
    def _():
        h = _modulated(x_ref[...], g_ref[...], mod_ref[0, 0:1, :], mod_ref[0, 1:2, :]).astype(BF16)
        h_ref[...] = h
        acc_ref[...] = chunk(h)

    @pl.when(jnp.logical_and(j > 0, j < last))
    def _():
        acc_ref[...] += chunk(h_ref[...])

    @pl.when(jnp.logical_and(j > 0, j == last))
    def _():
        y = acc_ref[...] + chunk(h_ref[...])
        o_ref[...] = x_ref[...] + (0.5 * mod_ref[0, 2:3, :]) * y


def _ffn(x, mod, g, w_in, w_out, layer, half, rows_per_mod):
    rows, d = x.shape
    f = w_out.shape[2]
    tm = _tile(min(rows, rows_per_mod), 1024)
    tf = _tile(f, 512)
    nf = f // tf
    assert nf >= 2
    return pl.pallas_call(
        _ffn_body,
        out_shape=jax.ShapeDtypeStruct((rows, d), F32),
        grid=(rows // tm, nf),
        in_specs=[
            pl.BlockSpec((tm, d), lambda i, j: (i, 0)),
            pl.BlockSpec((1, 3, d), lambda i, j: (i * tm // rows_per_mod, 0, 0)),
            pl.BlockSpec((1, d), lambda i, j: (0, 0)),
            pl.BlockSpec((None, None, d, tf), lambda i, j: (layer, half, 0, j)),
            pl.BlockSpec((None, None, d, tf), lambda i, j: (layer, half, 0, nf + j)),
            pl.BlockSpec((None, None, tf, d), lambda i, j: (layer, half, j, 0)),
        ],
        out_specs=pl.BlockSpec((tm, d), lambda i, j: (i, 0), pipeline_mode=pl.Buffered(1)),
        scratch_shapes=[pltpu.VMEM((tm, d), BF16), pltpu.VMEM((tm, d), F32)],
        compiler_params=_params("parallel", "arbitrary"),
        name="ffn_half_step",
    )(x, mod, g.reshape(1, d), w_in, w_in, w_out)


def _proj_body(x_ref, mod_ref, g_ref, perm_ref, w_ref, b_ref, cw_ref, cb_ref, o_ref, h_ref, *, half_seq):
    tm = x_ref.shape[0]
    half = tm // 2
    grp = perm_ref.shape[0]

    def project_conv():
        p = _dot(h_ref[...], w_ref[...]) + b_ref[...]
        pe, po = p[:half], p[half:]
        pos = lax.broadcasted_iota(jnp.int32, (half, 1), 0) % half_seq
        po_prev = jnp.where(pos == 0, 0.0, pltpu.roll(po, 1, 0))
        pe_next = jnp.where(pos == half_seq - 1, 0.0, pltpu.roll(pe, half - 1, 0))
        w0, w1, w2 = cw_ref[0:1, :], cw_ref[1:2, :], cw_ref[2:3, :]
        o_ref[0] = (((cb_ref[...] + po_prev * w0) + pe * w1) + po * w2).astype(o_ref.dtype)
        o_ref[1] = (((cb_ref[...] + pe * w0) + po * w1) + pe_next * w2).astype(o_ref.dtype)

    @pl.when(pl.program_id(1) == 0)
    def _():
        for r in range(tm // grp):
            h = _modulated(x_ref[r * grp:(r + 1) * grp, :], g_ref[...], mod_ref[0, 0:1, :], mod_ref[0, 1:2, :])
            hp = _dot(perm_ref[...], h.astype(BF16)).astype(BF16)
            h_ref[r * (grp // 2):(r + 1) * (grp // 2), :] = hp[:grp // 2]
            h_ref[half + r * (grp // 2):half + (r + 1) * (grp // 2), :] = hp[grp // 2:]
        project_conv()

    @pl.when(pl.program_id(1) > 0)
    def _():
        project_conv()


def _qkv_body(x_ref, mod_ref, g_ref, w_ref, gamma_ref, qs_ref, isv_ref, cos_ref, sin_ref, o_ref, h_ref, p_ref):
    j = pl.program_id(1)
    last = pl.num_programs(1) - 1

    def finish():
        cos = cos_ref[...]
        sin = sin_ref[...]
        even = (lax.broadcasted_iota(jnp.int32, (1, HEAD_DIM), 1) % 2) == 0
        for hh in range(p_ref.shape[1] // HEAD_DIM):
            sl = slice(hh * HEAD_DIM, (hh + 1) * HEAD_DIM)
            ph = p_ref[:, sl]
            ms = jnp.mean(ph * ph, axis=-1, keepdims=True)
            nh = (ph * lax.rsqrt(ms + EPS)) * gamma_ref[:, sl]
            partner = jnp.where(even, pltpu.roll(nh, HEAD_DIM - 1, 1), pltpu.roll(nh, 1, 1))
            r = (nh * cos + partner * sin) * qs_ref[:, sl]
            o_ref[:, sl] = jnp.where(isv_ref[:, sl] > 0.0, ph, r).astype(o_ref.dtype)

    @pl.when(j == 0)
    def _():
        h = _modulated(x_ref[...], g_ref[...], mod_ref[0, 0:1, :], mod_ref[0, 1:2, :]).astype(BF16)
        h_ref[...] = h
        p_ref[...] = _dot(h, w_ref[...])

    @pl.when(jnp.logical_and(j > 0, j < last))
    def _():
        finish()
        p_ref[...] = _dot(h_ref[...], w_ref[...])

    @pl.when(j == last)
    def _():
        finish()


def _row_tile(rows, rows_per_mod):
    return _tile(min(rows, rows_per_mod), 1024)


def _parity_perm(tm):
    r = jnp.arange(tm, dtype=jnp.int32)[:, None]
    c = jnp.arange(tm, dtype=jnp.int32)[None, :]
    src = jnp.where(r < tm // 2, 2 * r, 2 * (r - tm // 2) + 1)
    return (c == src).astype(BF16)


def _col_tiled(w, tn):
    k, n = w.shape
    return w.reshape(k, n // tn, tn).transpose(1, 0, 2).astype(BF16)


PERM_GROUP = 256


def _proj(x, mod, g, w_t, b, conv_w, conv_b, rows_per_mod, seq):
    rows, d = x.shape
    nj, _, tn = w_t.shape
    n = nj * tn
    tm = _tile(min(rows, rows_per_mod), max(seq, 1024))
    grp = min(PERM_GROUP, tm)
    assert tm % seq == 0 and tm % grp == 0 and grp % (4 * SUBLANES) == 0
    col = pl.BlockSpec((1, tn), lambda i, j: (0, j))
    return pl.pallas_call(
        functools.partial(_proj_body, half_seq=seq // 2),
        out_shape=jax.ShapeDtypeStruct((2, rows // 2, n), BF16),
        grid=(rows // tm, n // tn),
        in_specs=[
            pl.BlockSpec((tm, d), lambda i, j: (i, 0)),
            pl.BlockSpec((1, 3, d), lambda i, j: (i * tm // rows_per_mod, 0, 0)),
            pl.BlockSpec((1, d), lambda i, j: (0, 0)),
            pl.BlockSpec((grp, grp), lambda i, j: (0, 0)),
            pl.BlockSpec((None, d, tn), lambda i, j: (j, 0, 0)),
            col,
            pl.BlockSpec((HYENA_SHORT_K, tn), lambda i, j: (0, j)),
            col,
        ],
        out_specs=pl.BlockSpec((2, tm // 2, tn), lambda i, j: (0, i, j)),
        scratch_shapes=[pltpu.VMEM((tm, d), BF16)],
        compiler_params=_params("parallel", "arbitrary"),
        name="hyena_project_conv",
    )(x, mod, g.reshape(1, d), _parity_perm(grp), w_t, b.reshape(1, n), conv_w, conv_b.reshape(1, n))


def _qkv_proj(x, mod, g, w_t, rows_per_mod, tables, rope, seq):
    rows, d = x.shape
    nj, _, tn = w_t.shape
    n = nj * tn
    tm = _row_tile(rows, rows_per_mod)
    gamma, qs, isv = tables
    cos, sin = rope

    def prev(j):
        return jnp.maximum(j - 1, 0)

    col = pl.BlockSpec((1, tn), lambda i, j: (0, prev(j)))
    if seq is None:
        rope_spec = pl.BlockSpec((tm, HEAD_DIM), lambda i, j: (0, 0))
    else:
        rope_spec = pl.BlockSpec((tm, HEAD_DIM), lambda i, j: (i % (seq // tm), 0))
    return pl.pallas_call(
        _qkv_body,
        out_shape=jax.ShapeDtypeStruct((rows, n), BF16),
        grid=(rows // tm, nj + 1),
        in_specs=[
            pl.BlockSpec((tm, d), lambda i, j: (i, 0)),
            pl.BlockSpec((1, 3, d), lambda i, j: (i * tm // rows_per_mod, 0, 0)),
            pl.BlockSpec((1, d), lambda i, j: (0, 0)),
            pl.BlockSpec((None, d, tn), lambda i, j: (jnp.minimum(j, nj - 1), 0, 0)),
            col, col, col, rope_spec, rope_spec,
        ],
        out_specs=pl.BlockSpec((tm, tn), lambda i, j: (i, prev(j))),
        scratch_shapes=[pltpu.VMEM((tm, d), BF16), pltpu.VMEM((tm, tn), F32)],
        compiler_params=_params("arbitrary", "arbitrary"),
        name="qkv_project",
    )(x, mod, g.reshape(1, d), w_t, gamma, qs, isv, cos, sin)


def _attn_body(*refs, has_lat, has_mask, has_sink, tq):
    sink_ref, q_ref, kc_ref, vc_ref = refs[:4]
    if has_lat:
        k_ref, v_ref, o_ref = refs[4:7]
    else:
        o_ref = refs[4]
    g = pl.program_id(1)
    qi = pl.program_id(2)
    if has_lat:
        l_rows = k_ref.shape[0]
        if has_mask:
            kw = min(l_rows, tq + 2 * WINDOW)
            k0 = pl.multiple_of(jnp.clip(qi * tq - WINDOW, 0, l_rows - kw), WINDOW)
            keys = pl.ds(k0, kw)
        else:
            k0, keys = 0, slice(None)
    if has_mask:
        qpos = qi * tq + lax.broadcasted_iota(jnp.int32, (tq, 1), 0)
        kpos = k0 + lax.broadcasted_iota(jnp.int32, (1, kw), 1)
        band = jnp.abs(kpos - qpos) <= WINDOW
    heads = [slice(hh * HEAD_DIM, (hh + 1) * HEAD_DIM) for hh in range(GROUP)]
    s1 = [_dot_nt(q_ref[:, sl], kc_ref[...]) for sl in heads]
    m = [jnp.max(s, axis=-1, keepdims=True) for s in s1]
    if has_lat:
        s2 = [_dot_nt(q_ref[:, sl], k_ref[keys, :]) for sl in heads]
        if has_mask:
            s2 = [jnp.where(band, s, NEG_INF) for s in s2]
        m = [jnp.maximum(mh, jnp.max(s, axis=-1, keepdims=True)) for mh, s in zip(m, s2)]
    if has_sink:
        sk = [sink_ref[g * GROUP + hh] for hh in range(GROUP)]
        m = [jnp.maximum(mh, skh) for mh, skh in zip(m, sk)]
    for hh, sl in enumerate(heads):
        p1 = jnp.exp(s1[hh] - m[hh])
        l = jnp.sum(p1, axis=-1, keepdims=True)
        o = _dot(p1.astype(BF16), vc_ref[...])
        if has_lat:
            p2 = jnp.exp(s2[hh] - m[hh])
            l = l + jnp.sum(p2, axis=-1, keepdims=True)
            o = o + _dot(p2.astype(BF16), v_ref[keys, :])
        if has_sink:
            l = l + jnp.exp(sk[hh] - m[hh])
        o_ref[:, sl] = (o / l).astype(o_ref.dtype)


def _attention(sink, q_src, ctx_src, lat_src, *, batch, q_rows, q_col, k_col, v_col, has_mask, has_sink):
    tq = _tile(q_rows, 256 if has_mask else 512)
    assert not has_mask or tq % WINDOW == 0
    c_rows = ctx_src.shape[0] // batch
    gw = GROUP * HEAD_DIM
    in_specs = [
        pl.BlockSpec(memory_space=pltpu.SMEM),
        pl.BlockSpec((tq, gw), lambda b, g, i: (b * (q_rows // tq) + i, q_col + g)),
        pl.BlockSpec((c_rows, HEAD_DIM), lambda b, g, i: (b, k_col + g)),
        pl.BlockSpec((c_rows, HEAD_DIM), lambda b, g, i: (b, v_col + g)),
    ]
    args = [sink, q_src, ctx_src, ctx_src]
    has_lat = lat_src is not None
    if has_lat:
        l_rows = lat_src.shape[0] // batch
        in_specs += [
            pl.BlockSpec((l_rows, HEAD_DIM), lambda b, g, i: (b, k_col + g)),
            pl.BlockSpec((l_rows, HEAD_DIM), lambda b, g, i: (b, v_col + g)),
        ]
        args += [lat_src, lat_src]
    return pl.pallas_call(
        functools.partial(_attn_body, has_lat=has_lat, has_mask=has_mask, has_sink=has_sink, tq=tq),
        out_shape=jax.ShapeDtypeStruct((batch * q_rows, Q_COLS), BF16),
        grid=(batch, N_KV, q_rows // tq),
        in_specs=in_specs,
        out_specs=pl.BlockSpec((tq, gw), lambda b, g, i: (b * (q_rows // tq) + i, g)),
        compiler_params=_params("parallel", "parallel", "parallel"),
        name="gqa_attention",
    )(*args)


def _outproj_body(*refs, n_in):
    a_refs, w_refs = refs[:n_in], refs[n_in:2 * n_in]
    b_ref, x_ref, mod_ref, o_ref = refs[2 * n_in:]
    y = b_ref[...] + _dot(a_refs[0][...], w_refs[0][...])
    for a_ref, w_ref in zip(a_refs[1:], w_refs[1:]):
        y = y + _dot(a_ref[...], w_ref[...])
    o_ref[...] = x_ref[...] + mod_ref[0, 2:3, :] * y


def _outproj(acts, weight, bias, x, mod, rows_per_mod):
    rows, d = x.shape
    tm = _tile(min(rows, rows_per_mod), 512)
    n_in = len(acts)
    k = acts[0].shape[1]
    assert all(a.shape[1] == k for a in acts) and weight.shape[0] == n_in * k
    in_specs = [pl.BlockSpec((tm, k), lambda i: (i, 0)) for _ in acts]
    in_specs += [pl.BlockSpec((k, d), lambda i, n=n: (n, 0)) for n in range(n_in)]
    in_specs += [
        pl.BlockSpec((1, d), lambda i: (0, 0)),
        pl.BlockSpec((tm, d), lambda i: (i, 0)),
        pl.BlockSpec((1, 3, d), lambda i: (i * tm // rows_per_mod, 0, 0)),
    ]
    return pl.pallas_call(
        functools.partial(_outproj_body, n_in=n_in),
        out_shape=jax.ShapeDtypeStruct((rows, d), F32),
        grid=(rows // tm,),
        in_specs=in_specs,
        out_specs=pl.BlockSpec((tm, d), lambda i: (i, 0)),
        compiler_params=_params("parallel"),
        name="outproj_residual",
    )(*acts, *([weight] * n_in), bias.reshape(1, d), x, mod)


def _outproj_parity_body(z_ref, unperm_ref, w_ref, b_ref, x_ref, mod_ref, o_ref):
    grp = unperm_ref.shape[0]
    hg = grp // 2
    parts = []
    for r in range(x_ref.shape[0] // grp):
        zg = jnp.concatenate([z_ref[0, r * hg:(r + 1) * hg, :], z_ref[1, r * hg:(r + 1) * hg, :]], axis=0)
        parts.append(_dot(unperm_ref[...], zg).astype(BF16))
    y = b_ref[...] + _dot(jnp.concatenate(parts, axis=0), w_ref[...])
    o_ref[...] = x_ref[...] + mod_ref[0, 2:3, :] * y


def _outproj_parity(z, weight, bias, x, mod, rows_per_mod, seq):
    rows, d = x.shape
    k = z.shape[2]
    tm = _tile(min(rows, rows_per_mod), 512)
    grp = min(PERM_GROUP, tm)
    assert tm % grp == 0 and (tm % seq == 0 or seq % tm == 0)
    return pl.pallas_call(
        _outproj_parity_body,
        out_shape=jax.ShapeDtypeStruct((rows, d), F32),
        grid=(rows // tm,),
        in_specs=[
            pl.BlockSpec((2, tm // 2, k), lambda i: (0, i, 0)),
            pl.BlockSpec((grp, grp), lambda i: (0, 0)),
            pl.BlockSpec((k, d), lambda i: (0, 0)),
            pl.BlockSpec((1, d), lambda i: (0, 0)),
            pl.BlockSpec((tm, d), lambda i: (i, 0)),
            pl.BlockSpec((1, 3, d), lambda i: (i * tm // rows_per_mod, 0, 0)),
        ],
        out_specs=pl.BlockSpec((tm, d), lambda i: (i, 0)),
        compiler_params=_params("parallel"),
        name="outproj_residual_parity",
    )(z, _parity_perm(grp).T, weight, bias.reshape(1, d), x, mod)


def _filter_body(feats_ref, w1_ref, b1_ref, fr1_ref, w2_ref, b2_ref, fr2_ref, w3f_ref, w3b_ref, dl_ref, o_ref,
                 h2_ref):
    feats = feats_ref[...]

    @pl.when(jnp.logical_and(pl.program_id(0) == 0, pl.program_id(1) == 0))
    def _():
        h1 = jnp.sin(fr1_ref[...] * (jnp.dot(feats, w1_ref[...], preferred_element_type=F32, precision=HIGHEST)
                                     + b1_ref[...]))
        h2_ref[...] = jnp.sin(fr2_ref[...] * (jnp.dot(h1, w2_ref[...], preferred_element_type=F32,
                                                      precision=HIGHEST) + b2_ref[...]))

    h2 = h2_ref[...]
    decay = jnp.exp(-feats[:, 0:1] * dl_ref[...])
    fwd = jnp.dot(h2, w3f_ref[...], preferred_element_type=F32, precision=HIGHEST) * decay
    bwd = jnp.dot(h2, w3b_ref[...], preferred_element_type=F32, precision=HIGHEST) * decay
    row = lax.broadcasted_iota(jnp.int32, (fwd.shape[0], 1), 0)
    bwd = jnp.where(row == 0, 0.0, bwd)
    nrm = lax.rsqrt(jnp.sum(fwd * fwd + bwd * bwd, axis=0, keepdims=True) + EPS)
    o_ref[0] = ((fwd + bwd) * nrm).astype(o_ref.dtype)
    o_ref[1] = ((fwd - bwd) * nrm).astype(o_ref.dtype)


def _hyena_filter_taps(feats, w1, b1, fr1, w2, b2, fr2, w3, absdelta, d):
    n, fp = feats.shape
    hp = w1.shape[1]
    tn = _tile(d, 512)
    nd = d // tn
    vec = pl.BlockSpec((1, hp), lambda o, j: (0, 0))
    return pl.pallas_call(
        _filter_body,
        out_shape=jax.ShapeDtypeStruct((2, n, HYENA_ORDER * d), BF16),
        grid=(HYENA_ORDER, nd),
        in_specs=[
            pl.BlockSpec((n, fp), lambda o, j: (0, 0)),
            pl.BlockSpec((fp, hp), lambda o, j: (0, 0)), vec, vec,
            pl.BlockSpec((hp, hp), lambda o, j: (0, 0)), vec, vec,
            pl.BlockSpec((hp, tn), lambda o, j: (0, (2 * o) * nd + j)),
            pl.BlockSpec((hp, tn), lambda o, j: (0, (2 * o + 1) * nd + j)),
            pl.BlockSpec((1, tn), lambda o, j: (0, j)),
        ],
        out_specs=pl.BlockSpec((2, n, tn), lambda o, j: (0, 0, o * nd + j)),
        scratch_shapes=[pltpu.VMEM((n, hp), F32)],
        compiler_params=_params("arbitrary", "arbitrary"),
        name="hyena_filter_taps",
    )(feats, w1, b1, fr1, w2, b2, fr2, w3, w3, absdelta)


def _spectrum_body(ce_ref, se_ref, co_ref, so_ref, se_, so_, de_, do_, o_ref, *, inv_n):
    i = pl.program_id(0)
    tk = ce_ref.shape[0]
    pr = _dot(ce_ref[...], se_[...])
    qr = _dot(co_ref[...], so_[...])
    pi = _dot(se_ref[...], de_[...])
    qi = _dot(so_ref[...], do_[...])
    row0 = (i * tk + lax.broadcasted_iota(jnp.int32, (tk, 1), 0)) == 0
    w_re = jnp.where(row0, inv_n, 2.0 * inv_n)
    o_ref[0] = (pr + qr) * w_re
    o_ref[1] = (pr - qr) * w_re
    o_ref[2] = (pi + qi) * (2.0 * inv_n)
    o_ref[3] = (qi - pi) * (2.0 * inv_n)

    @pl.when(i == 0)
    def _():
        rows = 2 * SUBLANES
        mid_r = _dot(se_ref[0:rows, :], se_[...])
        o_ref[2, 0:1, :] = mid_r[0:1, :] * (2.0 * inv_n)
        o_ref[3, 0:1, :] = qi[0:1, :] * (2.0 * inv_n)


def _hyena_spectrum(dft, taps):
    n = taps.shape[1]
    cols = taps.shape[2]
    half = n // 2
    tk = _tile(half, 1024)
    tn = _tile(cols, 512)
    nr = half // tk
    mat = lambda par, part: pl.BlockSpec((None, tk, half), lambda i, j: (par, part * nr + i, 0))
    tap = lambda which, par: pl.BlockSpec((None, half, tn), lambda i, j: (which, par, j))
    return pl.pallas_call(
        functools.partial(_spectrum_body, inv_n=1.0 / (2 * n)),
        out_shape=jax.ShapeDtypeStruct((4, half, cols), F32),
        grid=(nr, cols // tn),
        in_specs=[mat(0, 0), mat(0, 1), mat(1, 0), mat(1, 1), tap(0, 0), tap(0, 1), tap(1, 0), tap(1, 1)],
        out_specs=pl.BlockSpec((4, tk, tn), lambda i, j: (0, i, j)),
        compiler_params=_params("parallel", "parallel"),
        name="hyena_filter_spectrum",
    )(dft, dft, dft, dft, taps, taps, taps, taps)


def _dft_fwd_body(ce_ref, se_ref, co_ref, so_ref, ze_ref, zo_ref, k_ref, yr_ref, yi_ref):
    i = pl.program_id(1)
    tk = ce_ref.shape[0]
    row0 = (i * tk + lax.broadcasted_iota(jnp.int32, (tk, 1), 0)) == 0
    tn = ze_ref.shape[1]
    sub = min(tn, 2 * LANES)
    for c in range(tn // sub):
        cols = slice(c * sub, (c + 1) * sub)
        ze, zo = ze_ref[:, cols], zo_ref[:, cols]
        pr = _dot(ce_ref[...], ze)
        pi = _dot(se_ref[...], ze)
        qr = _dot(co_ref[...], zo)
        qi = _dot(so_ref[...], zo)
        kr_lo, kr_hi, ki_lo, ki_hi = k_ref[0, :, cols], k_ref[1, :, cols], k_ref[2, :, cols], k_ref[3, :, cols]
        zr_lo, zr_hi, zi_lo, zi_hi = pr + qr, pr - qr, pi + qi, qi - pi
        yr_lo = zr_lo * kr_lo - jnp.where(row0, 0.0, zi_lo * ki_lo)
        yr_hi = zr_hi * kr_hi - jnp.where(row0, 0.0, zi_hi * ki_hi)
        yi_lo = zr_lo * ki_lo + zi_lo * kr_lo
        yi_hi = zr_hi * ki_hi + zi_hi * kr_hi
        mid_r = pi * ki_lo - qi * ki_hi
        mid_i = pi * ki_hi + qi * ki_lo
        yr_ref[0, :, cols] = (yr_lo + yr_hi).astype(yr_ref.dtype)
        yr_ref[1, :, cols] = (yr_lo - yr_hi).astype(yr_ref.dtype)
        yi_ref[0, :, cols] = jnp.where(row0, mid_r, yi_lo - yi_hi).astype(yi_ref.dtype)
        yi_ref[1, :, cols] = jnp.where(row0, mid_i, yi_lo + yi_hi).astype(yi_ref.dtype)


def _dft_fwd(dft, z_src, z_col, spec, order, batch, d):
    half = z_src.shape[1] // batch
    tk = _tile(half, 1024)
    tn = _tile(d, 512)
    nr, nd = half // tk, d // tn
    out = jax.ShapeDtypeStruct((2, batch * half, d), BF16)
    once = pl.Buffered(1)
    mat = lambda par, part: pl.BlockSpec((None, tk, half), lambda j, i, b: (par, part * nr + i, 0),
                                         pipeline_mode=once)
    zin = lambda par: pl.BlockSpec((None, half, tn), lambda j, i, b: (par, b, z_col * nd + j))
    res = pl.BlockSpec((2, tk, tn), lambda j, i, b: (0, b * nr + i, j))
    return pl.pallas_call(
        _dft_fwd_body,
        out_shape=(out, out),
        grid=(nd, nr, batch),
        in_specs=[
            mat(0, 0), mat(0, 1), mat(1, 0), mat(1, 1), zin(0), zin(1),
            pl.BlockSpec((4, tk, tn), lambda j, i, b: (0, i, order * nd + j), pipeline_mode=once),
        ],
        out_specs=(res, res),
        compiler_params=_params("parallel", "parallel", "parallel"),
        name="hyena_dft_forward",
    )(dft, dft, dft, dft, z_src, z_src, spec)


def _dft_inv_body(atr_ref, ati_ref, yr_ref, yi_ref, g_ref, zp_ref, bias_ref, o_ref):
    y = _dot(atr_ref[...], yr_ref[...]) + _dot(ati_ref[...], yi_ref[...])
    zp = zp_ref[...].astype(F32)
    o_ref[...] = (g_ref[...].astype(F32) * (y + zp * bias_ref[...])).astype(o_ref.dtype)


def _dft_inv(dft_t, yr, yi, gate_src, gate_col, z_src, z_col, bias, batch, d):
    half = yr.shape[1] // batch
    tt = _tile(half, 1024)
    tn = _tile(d, 1024)
    nr, nd = half // tt, d // tn
    row = lambda cols: pl.BlockSpec((None, tt, tn), lambda j, p, i, b: (p, b * nr + i, cols * nd + j))
    return pl.pallas_call(
        _dft_inv_body,
        out_shape=jax.ShapeDtypeStruct((2, batch * half, d), BF16),
        grid=(nd, 2, nr, batch),
        in_specs=[
            pl.BlockSpec((None, tt, half), lambda j, p, i, b: (p, i, 0)),
            pl.BlockSpec((None, tt, half), lambda j, p, i, b: (p, i, 1)),
            pl.BlockSpec((None, half, tn), lambda j, p, i, b: (p, b, j)),
            pl.BlockSpec((None, half, tn), lambda j, p, i, b: (p, b, j)),
            row(gate_col), row(z_col),
            pl.BlockSpec((1, tn), lambda j, p, i, b: (0, j)),
        ],
        out_specs=row(0),
        compiler_params=_params("parallel", "parallel", "parallel", "parallel"),
        name="hyena_dft_inverse",
    )(dft_t, dft_t, yr, yi, gate_src, z_src, bias.reshape(1, d))


def _rope_tables(n_tok):
    rows = n_tok // GRID_W
    r = jnp.repeat(jnp.arange(rows), GRID_W).astype(F32)
    col = jnp.tile(jnp.arange(GRID_W), rows).astype(F32)
    half = HEAD_DIM // 2
    inv = ROPE_THETA ** (-jnp.arange(0, half, 2, dtype=F32) / half)
    ang = jnp.concatenate([r[:, None] * inv, col[:, None] * inv], axis=-1)
    cos = jnp.repeat(jnp.cos(ang), 2, axis=-1)
    sin = jnp.repeat(jnp.sin(ang), 2, axis=-1)
    sign = jnp.where(jnp.arange(HEAD_DIM) % 2 == 0, -1.0, 1.0).astype(F32)
    return cos, sin * sign


def _dft_tables(n):
    big = 2 * n
    half = n // 2
    k = jnp.arange(half, dtype=jnp.int32)[:, None]
    m = jnp.arange(half, dtype=jnp.int32)[None, :]
    alt = jnp.where(m % 2 == 0, 1.0, -1.0).astype(F32)
    mats = []
    for par in range(2):
        ang = ((k * (2 * m + par)) % big).astype(F32) * (2.0 * math.pi / big)
        s = jnp.where(k == 0, alt if par == 0 else -alt, -jnp.sin(ang))
        mats.append(jnp.concatenate([jnp.cos(ang), s], axis=0))
    a = jnp.stack(mats).astype(BF16)
    return a, a.transpose(0, 2, 1)


def _filter_features(n, width):
    t = jnp.linspace(0.0, 1.0, n, dtype=F32)[:, None]
    w = (2.0 * math.pi / n) * jnp.arange(n, dtype=F32)[:, None]
    bands = jnp.linspace(1e-4, HYENA_BANDS - 1, HYENA_BANDS, dtype=F32)
    ang = w * bands[None, :]
    feats = jnp.concatenate([t, jnp.cos(ang), -jnp.sin(ang)], axis=-1)
    feats = jnp.concatenate([feats[0::2], feats[1::2]], axis=0)
    return jnp.pad(feats, ((0, 0), (0, width - feats.shape[1])))


def _pad_to(a, shape):
    return jnp.pad(a, [(0, s - d) for d, s in zip(a.shape, shape)])


def _attn_layer(x, xc, mod_l, mod_c, g, w_in, w_out, g_q, g_k, sink, rope, batch, seq, n_ctx, ctx_out):
    d = x.shape[1]
    ones = jnp.ones((HEAD_DIM,), F32)
    gamma = jnp.concatenate([jnp.tile(g_q[0], N_HEADS), jnp.tile(g_q[1], N_HEADS),
                             jnp.tile(g_k[0], N_KV), jnp.tile(ones, N_KV),
                             jnp.tile(g_k[1], N_KV), jnp.tile(ones, N_KV)])[None, :]
    qs = jnp.concatenate([jnp.full((2 * Q_COLS,), HEAD_DIM ** -0.5, F32), jnp.ones((4 * KV_COLS,), F32)])[None, :]
    zk, ok = jnp.zeros((KV_COLS,), F32), jnp.ones((KV_COLS,), F32)
    isv = jnp.concatenate([jnp.zeros((2 * Q_COLS,), F32), zk, ok, zk, ok])[None, :]
    tables = (gamma, qs, isv)
    tm = _row_tile(batch * n_ctx, batch * n_ctx)
    ident = (jnp.ones((tm, HEAD_DIM), F32), jnp.zeros((tm, HEAD_DIM), F32))
    w_in_b = _col_tiled(w_in, _tile(w_in.shape[1], 512))
    qkv = _qkv_proj(x, mod_l, g, w_in_b, seq, tables, rope, seq)
    qkv_c = _qkv_proj(xc, mod_c, g, w_in_b, batch * n_ctx, tables, ident, None)
    qa_col, qb_col = 0, N_KV
    ka_col = 2 * Q_COLS // HEAD_DIM
    va_col, kb_col, vb_col = ka_col + N_KV, ka_col + 2 * N_KV, ka_col + 3 * N_KV
    o_a = _attention(sink, qkv, qkv_c, qkv, batch=batch, q_rows=seq, q_col=qa_col, k_col=ka_col, v_col=va_col,
                     has_mask=False, has_sink=False)
    o_b = _attention(sink, qkv, qkv_c, qkv, batch=batch, q_rows=seq, q_col=qb_col, k_col=kb_col, v_col=vb_col,
                     has_mask=True, has_sink=True)
    w_out_b = w_out.astype(BF16)
    zero_b = jnp.zeros((d,), F32)
    x = _outproj((o_a, o_b), w_out_b, zero_b, x, mod_l, seq)
    if ctx_out:
        co_a = _attention(sink, qkv_c, qkv_c, None, batch=batch, q_rows=n_ctx, q_col=qa_col, k_col=ka_col,
                          v_col=va_col, has_mask=False, has_sink=False)
        co_b = _attention(sink, qkv_c, qkv_c, None, batch=batch, q_rows=n_ctx, q_col=qb_col, k_col=kb_col,
                          v_col=vb_col, has_mask=False, has_sink=True)
        xc = _outproj((co_a, co_b), w_out_b, zero_b, xc, mod_c, batch * n_ctx)
    return x, xc


def _hyena_layer(x, mod, g, rows_per_mod, batch, w_in_b, b_in, w_conv, b_conv, filt, hy_bias, w_out_b, b_out):
    d = x.shape[1]
    n = x.shape[0] // batch
    w1, b1, fr1, w2, b2, fr2, w3 = filt
    hp = LANES
    feats = _filter_features(n, LANES)
    max_decay = math.log(HYENA_TARGET) / HYENA_FAST_PCT
    min_decay = math.log(HYENA_TARGET) / HYENA_SLOW_PCT
    absdelta = jnp.abs(jnp.linspace(min_decay, max_decay, d, dtype=F32))[None, :]
    taps = _hyena_filter_taps(
        feats, _pad_to(w1, (LANES, hp)), _pad_to(b1[None, :], (1, hp)), _pad_to(fr1[None, :], (1, hp)),
        _pad_to(w2, (hp, hp)), _pad_to(b2[None, :], (1, hp)), _pad_to(fr2[None, :], (1, hp)),
        _pad_to(w3, (hp, w3.shape[1])), absdelta, d)
    dft, dft_t = _dft_tables(n)
    spec = _hyena_spectrum(dft, taps)
    pc = _proj(x, mod, g, w_in_b, b_in, w_conv, b_conv, rows_per_mod, n)
    yr, yi = _dft_fwd(dft, pc, 0, spec, 0, batch, d)
    z = _dft_inv(dft_t, yr, yi, pc, 1, pc, 0, hy_bias[0], batch, d)
    yr, yi = _dft_fwd(dft, z, 0, spec, 1, batch, d)
    z = _dft_inv(dft_t, yr, yi, pc, 2, z, 0, hy_bias[1], batch, d)
    return _outproj_parity(z, w_out_b, b_out, x, mod, rows_per_mod, n)


def kernel(x, c, ctx, c_ctx, w_mod, b_mod, g_norm, w_ffn_in, w_ffn_out, w_attn_in, w_attn_out, g_q, g_k, sink, w_hy_in, b_hy_in, w_hy_conv, b_hy_conv, hf_w1, hf_b1, hf_freq1, hf_w2, hf_b2, hf_freq2, hf_w3, hy_bias, w_hy_out, b_hy_out):
    batch, seq, d = x.shape
    n_ctx = ctx.shape[1]
    depth = w_mod.shape[0]
    rows_c = batch * n_ctx
    rope = _rope_tables(seq)
    last_ctx = max(l for l in range(depth) if l % 2 == 0)

    r_pad = -(-(batch + 1) // SUBLANES) * SUBLANES
    c_all = _pad_to(jnp.concatenate([c, c_ctx[None, :]], axis=0), (r_pad, d))
    m_all = _mod_all(c_all, w_mod, b_mod).reshape(depth, r_pad, N_MOD, d)

    w_ffn_in_b = w_ffn_in.astype(BF16)
    w_ffn_out_b = w_ffn_out.astype(BF16)

    xl = x.reshape(batch * seq, d)
    xc = ctx.reshape(rows_c, d)
    for l in range(depth):
        i = l // 2
        ctx_live = l <= last_ctx
        ctx_full = l < last_ctx
        mods_l = [m_all[l, :batch, 3 * k:3 * k + 3] for k in range(3)]
        mods_c = [m_all[l, batch:batch + 1, 3 * k:3 * k + 3] for k in range(3)]
        xl = _ffn(xl, mods_l[0], g_norm[l, 0], w_ffn_in_b, w_ffn_out_b, l, 0, seq)
        if ctx_live:
            xc = _ffn(xc, mods_c[0], g_norm[l, 0], w_ffn_in_b, w_ffn_out_b, l, 0, rows_c)
        if l % 2 == 0:
            xl, xc = _attn_layer(xl, xc, mods_l[1], mods_c[1], g_norm[l, 1], w_attn_in[i], w_attn_out[i],
                                 g_q[i], g_k[i], sink[i], rope, batch, seq, n_ctx, ctx_full)
        else:
            w_in_b = _col_tiled(w_hy_in[i], _tile(w_hy_in.shape[2], 256))
            w_out_b = w_hy_out[i].astype(BF16)
            filt = (hf_w1[i], hf_b1[i], hf_freq1[i], hf_w2[i], hf_b2[i], hf_freq2[i], hf_w3[i])
            xl = _hyena_layer(xl, mods_l[1], g_norm[l, 1], seq, batch, w_in_b, b_hy_in[i], w_hy_conv[i],
                              b_hy_conv[i], filt, hy_bias[i], w_out_b, b_hy_out[i])
            if ctx_full:
                xc = _hyena_layer(xc, mods_c[1], g_norm[l, 1], rows_c, batch, w_in_b, b_hy_in[i], w_hy_conv[i],
                                  b_hy_conv[i], filt, hy_bias[i], w_out_b, b_hy_out[i])
        xl = _ffn(xl, mods_l[2], g_norm[l, 2], w_ffn_in_b, w_ffn_out_b, l, 1, seq)
        if ctx_full:
            xc = _ffn(xc, mods_c[2], g_norm[l, 2], w_ffn_in_b, w_ffn_out_b, l, 1, rows_c)
    return xl.reshape(batch, seq, d)
```

```python
import functools
import math

import jax
import jax.numpy as jnp
from jax import lax
from jax.experimental import pallas as pl
from jax.experimental.pallas import tpu as pltpu

HEAD_DIM = 128
N_HEADS = 8
N_KV = 2
GROUP = N_HEADS // N_KV
Q_COLS = N_HEADS * HEAD_DIM
KV_COLS = N_KV * HEAD_DIM
GRID_W = 64
WINDOW = 128
ROPE_THETA = 10000.0
N_MOD = 9
HYENA_ORDER = 2
HYENA_SHORT_K = 3
HYENA_BANDS = 16
HYENA_TARGET = 1e-2
HYENA_FAST_PCT = 0.3
HYENA_SLOW_PCT = 1.5
EPS = 1e-6
NEG_INF = -1e30

LANES = 128
SUBLANES = 8
VMEM_LIMIT_BYTES = 60 * 1024 * 1024

F32 = jnp.float32
BF16 = jnp.bfloat16
HIGHEST = lax.Precision.HIGHEST


def _params(*sem):
    return pltpu.CompilerParams(dimension_semantics=sem, vmem_limit_bytes=VMEM_LIMIT_BYTES)


def _tile(dim, pref):
    t = min(dim, pref)
    while dim % t:
        t //= 2
    return t


def _dot(a, b):
    return jnp.dot(a, b, preferred_element_type=F32)


def _dot_nt(a, b):
    return lax.dot_general(a, b, (((1,), (1,)), ((), ())), preferred_element_type=F32)


def _silu(v):
    return v * (1.0 / (1.0 + jnp.exp(-v)))


def _modulated(xf, g, shift, scale):
    ms = jnp.mean(xf * xf, axis=-1, keepdims=True)
    y = xf * lax.rsqrt(ms + EPS)
    return (y * g) * (1.0 + scale) + shift


def _mod_body(c_ref, w_ref, b_ref, o_ref):
    s = _silu(c_ref[...])
    o_ref[0] = jnp.dot(s, w_ref[0], preferred_element_type=F32, precision=HIGHEST) + b_ref[0]


def _mod_all(c_all, w_mod, b_mod):
    depth, d, nd = w_mod.shape
    r = c_all.shape[0]
    tn = _tile(nd, 2048)
    return pl.pallas_call(
        _mod_body,
        out_shape=jax.ShapeDtypeStruct((depth, r, nd), F32),
        grid=(depth, nd // tn),
        in_specs=[
            pl.BlockSpec((r, d), lambda l, j: (0, 0)),
            pl.BlockSpec((1, d, tn), lambda l, j: (l, 0, j)),
            pl.BlockSpec((1, 1, tn), lambda l, j: (l, 0, j)),
        ],
        out_specs=pl.BlockSpec((1, r, tn), lambda l, j: (l, 0, j)),
        compiler_params=_params("parallel", "parallel"),
        name="mod_vectors",
    )(c_all, w_mod, b_mod.reshape(depth, 1, nd))


def _ffn_body(x_ref, mod_ref, g_ref, wg_ref, wu_ref, wo_ref, o_ref, h_ref, acc_ref):
    j = pl.program_id(1)
    last = pl.num_programs(1) - 1

    def chunk(h):
        a = _dot(h, wg_ref[...])
        u = _dot(h, wu_ref[...])
        return _dot((_silu(a) * u).astype(BF16), wo_ref[...])

    @pl.when(j == 0)
    def _():
        h = _modulated(x_ref[...], g_ref[...], mod_ref[0, 0:1, :], mod_ref[0, 1:2, :]).astype(BF16)
        h_ref[...] = h
        acc_ref[...] = chunk(h)

    @pl.when(jnp.logical_and(j > 0, j < last))
    def _():
        acc_ref[...] += chunk(h_ref[...])

    @pl.when(jnp.logical_and(j > 0, j == last))
    def _():
        y = acc_ref[...] + chunk(h_ref[...])
        o_ref[...] = x_ref[...] + (0.5 * mod_ref[0, 2:3, :]) * y


def _ffn(x, mod, g, w_in, w_out, layer, half, rows_per_mod):
    rows, d = x.shape
    f = w_out.shape[2]
    tm = _tile(min(rows, rows_per_mod), 1024)
    tf = _tile(f, 512)
    nf = f // tf
    assert nf >= 2
    return pl.pallas_call(
        _ffn_body,
        out_shape=jax.ShapeDtypeStruct((rows, d), F32),
        grid=(rows // tm, nf),
        in_specs=[
            pl.BlockSpec((tm, d), lambda i, j: (i, 0)),
            pl.BlockSpec((1, 3, d), lambda i, j: (i * tm // rows_per_mod, 0, 0)),
            pl.BlockSpec((1, d), lambda i, j: (0, 0)),
            pl.BlockSpec((None, None, d, tf), lambda i, j: (layer, half, 0, j)),
            pl.BlockSpec((None, None, d, tf), lambda i, j: (layer, half, 0, nf + j)),
            pl.BlockSpec((None, None, tf, d), lambda i, j: (layer, half, j, 0)),
        ],
        out_specs=pl.BlockSpec((tm, d), lambda i, j: (i, 0), pipeline_mode=pl.Buffered(1)),
        scratch_shapes=[pltpu.VMEM((tm, d), BF16), pltpu.VMEM((tm, d), F32)],
        compiler_params=_params("parallel", "arbitrary"),
        name="ffn_half_step",
    )(x, mod, g.reshape(1, d), w_in, w_in, w_out)


def _proj_body(x_ref, mod_ref, g_ref, perm_ref, w_ref, b_ref, cw_ref, cb_ref, o_ref, h_ref, *, half_seq):
    tm = x_ref.shape[0]
    half = tm // 2
    grp = perm_ref.shape[0]

    def project_conv():
        p = _dot(h_ref[...], w_ref[...]) + b_ref[...]
        pe, po = p[:half], p[half:]
        pos = lax.broadcasted_iota(jnp.int32, (half, 1), 0) % half_seq
        po_prev = jnp.where(pos == 0, 0.0, pltpu.roll(po, 1, 0))
        pe_next = jnp.where(pos == half_seq - 1, 0.0, pltpu.roll(pe, half - 1, 0))
        w0, w1, w2 = cw_ref[0:1, :], cw_ref[1:2, :], cw_ref[2:3, :]
        o_ref[0] = (((cb_ref[...] + po_prev * w0) + pe * w1) + po * w2).astype(o_ref.dtype)
        o_ref[1] = (((cb_ref[...] + pe * w0) + po * w1) + pe_next * w2).astype(o_ref.dtype)

    @pl.when(pl.program_id(1) == 0)
    def _():
        for r in range(tm // grp):
            h = _modulated(x_ref[r * grp:(r + 1) * grp, :], g_ref[...], mod_ref[0, 0:1, :], mod_ref[0, 1:2, :])
            hp = _dot(perm_ref[...], h.astype(BF16)).astype(BF16)
            h_ref[r * (grp // 2):(r + 1) * (grp // 2), :] = hp[:grp // 2]
            h_ref[half + r * (grp // 2):half + (r + 1) * (grp // 2), :] = hp[grp // 2:]
        project_conv()

    @pl.when(pl.program_id(1) > 0)
    def _():
        project_conv()


def _qkv_body(x_ref, mod_ref, g_ref, w_ref, gamma_ref, qs_ref, isv_ref, cos_ref, sin_ref, o_ref, h_ref, p_ref):
    j = pl.program_id(1)
    last = pl.num_programs(1) - 1

    def finish():
        cos = cos_ref[...]
        sin = sin_ref[...]
        even = (lax.broadcasted_iota(jnp.int32, (1, HEAD_DIM), 1) % 2) == 0
        for hh in range(p_ref.shape[1] // HEAD_DIM):
            sl = slice(hh * HEAD_DIM, (hh + 1) * HEAD_DIM)
            ph = p_ref[:, sl]
            ms = jnp.mean(ph * ph, axis=-1, keepdims=True)
            nh = (ph * lax.rsqrt(ms + EPS)) * gamma_ref[:, sl]
            partner = jnp.where(even, pltpu.roll(nh, HEAD_DIM - 1, 1), pltpu.roll(nh, 1, 1))
            r = (nh * cos + partner * sin) * qs_ref[:, sl]
            o_ref[:, sl] = jnp.where(isv_ref[:, sl] > 0.0, ph, r).astype(o_ref.dtype)

    @pl.when(j == 0)
    def _():
        h = _modulated(x_ref[...], g_ref[...], mod_ref[0, 0:1, :], mod_ref[0, 1:2, :]).astype(BF16)
        h_ref[...] = h
        p_ref[...] = _dot(h, w_ref[...])

    @pl.when(jnp.logical_and(j > 0, j < last))
    def _():
        finish()
        p_ref[...] = _dot(h_ref[...], w_ref[...])

    @pl.when(j == last)
    def _():
        finish()


def _row_tile(rows, rows_per_mod):
    return _tile(min(rows, rows_per_mod), 1024)


def _parity_perm(tm):
    r = jnp.arange(tm, dtype=jnp.int32)[:, None]
    c = jnp.arange(tm, dtype=jnp.int32)[None, :]
    src = jnp.where(r < tm // 2, 2 * r, 2 * (r - tm // 2) + 1)
    return (c == src).astype(BF16)


def _col_tiled(w, tn):
    k, n = w.shape
    return w.reshape(k, n // tn, tn).transpose(1, 0, 2).astype(BF16)


PERM_GROUP = 256


def _proj(x, mod, g, w_t, b, conv_w, conv_b, rows_per_mod, seq):
    rows, d = x.shape
    nj, _, tn = w_t.shape
    n = nj * tn
    tm = _tile(min(rows, rows_per_mod), max(seq, 1024))
    grp = min(PERM_GROUP, tm)
    assert tm % seq == 0 and tm % grp == 0 and grp % (4 * SUBLANES) == 0
    col = pl.BlockSpec((1, tn), lambda i, j: (0, j))
    return pl.pallas_call(
        functools.partial(_proj_body, half_seq=seq // 2),
        out_shape=jax.ShapeDtypeStruct((2, rows // 2, n), BF16),
        grid=(rows // tm, n // tn),
        in_specs=[
            pl.BlockSpec((tm, d), lambda i, j: (i, 0)),
            pl.BlockSpec((1, 3, d), lambda i, j: (i * tm // rows_per_mod, 0, 0)),
            pl.BlockSpec((1, d), lambda i, j: (0, 0)),
            pl.BlockSpec((grp, grp), lambda i, j: (0, 0)),
            pl.BlockSpec((None, d, tn), lambda i, j: (j, 0, 0)),
            col,
            pl.BlockSpec((HYENA_SHORT_K, tn), lambda i, j: (0, j)),
            col,
        ],
        out_specs=pl.BlockSpec((2, tm // 2, tn), lambda i, j: (0, i, j)),
        scratch_shapes=[pltpu.VMEM((tm, d), BF16)],
        compiler_params=_params("parallel", "arbitrary"),
        name="hyena_project_conv",
    )(x, mod, g.reshape(1, d), _parity_perm(grp), w_t, b.reshape(1, n), conv_w, conv_b.reshape(1, n))


def _qkv_proj(x, mod, g, w_t, rows_per_mod, tables, rope, seq):
    rows, d = x.shape
    nj, _, tn = w_t.shape
    n = nj * tn
    tm = _row_tile(rows, rows_per_mod)
    gamma, qs, isv = tables
    cos, sin = rope

    def prev(j):
        return jnp.maximum(j - 1, 0)

    col = pl.BlockSpec((1, tn), lambda i, j: (0, prev(j)))
    if seq is None:
        rope_spec = pl.BlockSpec((tm, HEAD_DIM), lambda i, j: (0, 0))
    else:
        rope_spec = pl.BlockSpec((tm, HEAD_DIM), lambda i, j: (i % (seq // tm), 0))
    return pl.pallas_call(
        _qkv_body,
        out_shape=jax.ShapeDtypeStruct((rows, n), BF16),
        grid=(rows // tm, nj + 1),
        in_specs=[
            pl.BlockSpec((tm, d), lambda i, j: (i, 0)),
            pl.BlockSpec((1, 3, d), lambda i, j: (i * tm // rows_per_mod, 0, 0)),
            pl.BlockSpec((1, d), lambda i, j: (0, 0)),
            pl.BlockSpec((None, d, tn), lambda i, j: (jnp.minimum(j, nj - 1), 0, 0)),
            col, col, col, rope_spec, rope_spec,
        ],
        out_specs=pl.BlockSpec((tm, tn), lambda i, j: (i, prev(j))),
        scratch_shapes=[pltpu.VMEM((tm, d), BF16), pltpu.VMEM((tm, tn), F32)],
        compiler_params=_params("arbitrary", "arbitrary"),
        name="qkv_project",
    )(x, mod, g.reshape(1, d), w_t, gamma, qs, isv, cos, sin)


def _attn_body(*refs, has_lat, has_mask, has_sink, tq):
    sink_ref, q_ref, kc_ref, vc_ref = refs[:4]
    if has_lat:
        k_ref, v_ref, o_ref = refs[4:7]
    else:
        o_ref = refs[4]
    g = pl.program_id(1)
    qi = pl.program_id(2)
    if has_lat:
        l_rows = k_ref.shape[0]
        if has_mask:
            kw = min(l_rows, tq + 2 * WINDOW)
            k0 = pl.multiple_of(jnp.clip(qi * tq - WINDOW, 0, l_rows - kw), WINDOW)
            keys = pl.ds(k0, kw)
        else:
            k0, keys = 0, slice(None)
    if has_mask:
        qpos = qi * tq + lax.broadcasted_iota(jnp.int32, (tq, 1), 0)
        kpos = k0 + lax.broadcasted_iota(jnp.int32, (1, kw), 1)
        band = jnp.abs(kpos - qpos) <= WINDOW
    heads = [slice(hh * HEAD_DIM, (hh + 1) * HEAD_DIM) for hh in range(GROUP)]
    s1 = [_dot_nt(q_ref[:, sl], kc_ref[...]) for sl in heads]
    m = [jnp.max(s, axis=-1, keepdims=True) for s in s1]
    if has_lat:
        s2 = [_dot_nt(q_ref[:, sl], k_ref[keys, :]) for sl in heads]
        if has_mask:
            s2 = [jnp.where(band, s, NEG_INF) for s in s2]
        m = [jnp.maximum(mh, jnp.max(s, axis=-1, keepdims=True)) for mh, s in zip(m, s2)]
    if has_sink:
        sk = [sink_ref[g * GROUP + hh] for hh in range(GROUP)]
        m = [jnp.maximum(mh, skh) for mh, skh in zip(m, sk)]
    for hh, sl in enumerate(heads):
        p1 = jnp.exp(s1[hh] - m[hh])
        l = jnp.sum(p1, axis=-1, keepdims=True)
        o = _dot(p1.astype(BF16), vc_ref[...])
        if has_lat:
            p2 = jnp.exp(s2[hh] - m[hh])
            l = l + jnp.sum(p2, axis=-1, keepdims=True)
            o = o + _dot(p2.astype(BF16), v_ref[keys, :])
        if has_sink:
            l = l + jnp.exp(sk[hh] - m[hh])
        o_ref[:, sl] = (o / l).astype(o_ref.dtype)


def _attention(sink, q_src, ctx_src, lat_src, *, batch, q_rows, q_col, k_col, v_col, has_mask, has_sink):
    tq = _tile(q_rows, 256 if has_mask else 512)
    assert not has_mask or tq % WINDOW == 0
    c_rows = ctx_src.shape[0] // batch
    gw = GROUP * HEAD_DIM
    in_specs = [
        pl.BlockSpec(memory_space=pltpu.SMEM),
        pl.BlockSpec((tq, gw), lambda b, g, i: (b * (q_rows // tq) + i, q_col + g)),
        pl.BlockSpec((c_rows, HEAD_DIM), lambda b, g, i: (b, k_col + g)),
        pl.BlockSpec((c_rows, HEAD_DIM), lambda b, g, i: (b, v_col + g)),
    ]
    args = [sink, q_src, ctx_src, ctx_src]
    has_lat = lat_src is not None
    if has_lat:
        l_rows = lat_src.shape[0] // batch
        in_specs += [
            pl.BlockSpec((l_rows, HEAD_DIM), lambda b, g, i: (b, k_col + g)),
            pl.BlockSpec((l_rows, HEAD_DIM), lambda b, g, i: (b, v_col + g)),
        ]
        args += [lat_src, lat_src]
    return pl.pallas_call(
        functools.partial(_attn_body, has_lat=has_lat, has_mask=has_mask, has_sink=has_sink, tq=tq),
        out_shape=jax.ShapeDtypeStruct((batch * q_rows, Q_COLS), BF16),
        grid=(batch, N_KV, q_rows // tq),
        in_specs=in_specs,
        out_specs=pl.BlockSpec((tq, gw), lambda b, g, i: (b * (q_rows // tq) + i, g)),
        compiler_params=_params("parallel", "parallel", "parallel"),
        name="gqa_attention",
    )(*args)


def _outproj_body(*refs, n_in):
    a_refs, w_refs = refs[:n_in], refs[n_in:2 * n_in]
    b_ref, x_ref, mod_ref, o_ref = refs[2 * n_in:]
    y = b_ref[...] + _dot(a_refs[0][...], w_refs[0][...])
    for a_ref, w_ref in zip(a_refs[1:], w_refs[1:]):
        y = y + _dot(a_ref[...], w_ref[...])
    o_ref[...] = x_ref[...] + mod_ref[0, 2:3, :] * y


def _outproj(acts, weight, bias, x, mod, rows_per_mod):
    rows, d = x.shape
    tm = _tile(min(rows, rows_per_mod), 512)
    n_in = len(acts)
    k = acts[0].shape[1]
    assert all(a.shape[1] == k for a in acts) and weight.shape[0] == n_in * k
    in_specs = [pl.BlockSpec((tm, k), lambda i: (i, 0)) for _ in acts]
    in_specs += [pl.BlockSpec((k, d), lambda i, n=n: (n, 0)) for n in range(n_in)]
    in_specs += [
        pl.BlockSpec((1, d), lambda i: (0, 0)),
        pl.BlockSpec((tm, d), lambda i: (i, 0)),
        pl.BlockSpec((1, 3, d), lambda i: (i * tm // rows_per_mod, 0, 0)),
    ]
    return pl.pallas_call(
        functools.partial(_outproj_body, n_in=n_in),
        out_shape=jax.ShapeDtypeStruct((rows, d), F32),
        grid=(rows // tm,),
        in_specs=in_specs,
        out_specs=pl.BlockSpec((tm, d), lambda i: (i, 0)),
        compiler_params=_params("parallel"),
        name="outproj_residual",
    )(*acts, *([weight] * n_in), bias.reshape(1, d), x, mod)


def _outproj_parity_body(z_ref, unperm_ref, w_ref, b_ref, x_ref, mod_ref, o_ref):
    grp = unperm_ref.shape[0]
    hg = grp // 2
    parts = []
    for r in range(x_ref.shape[0] // grp):
        zg = jnp.concatenate([z_ref[0, r * hg:(r + 1) * hg, :], z_ref[1, r * hg:(r + 1) * hg, :]], axis=0)
        parts.append(_dot(unperm_ref[...], zg).astype(BF16))
    y = b_ref[...] + _dot(jnp.concatenate(parts, axis=0), w_ref[...])
    o_ref[...] = x_ref[...] + mod_ref[0, 2:3, :] * y


def _outproj_parity(z, weight, bias, x, mod, rows_per_mod, seq):
    rows, d = x.shape
    k = z.shape[2]
    tm = _tile(min(rows, rows_per_mod), 512)
    grp = min(PERM_GROUP, tm)
    assert tm % grp == 0 and (tm % seq == 0 or seq % tm == 0)
    return pl.pallas_call(
        _outproj_parity_body,
        out_shape=jax.ShapeDtypeStruct((rows, d), F32),
        grid=(rows // tm,),
        in_specs=[
            pl.BlockSpec((2, tm // 2, k), lambda i: (0, i, 0)),
            pl.BlockSpec((grp, grp), lambda i: (0, 0)),
            pl.BlockSpec((k, d), lambda i: (0, 0)),
            pl.BlockSpec((1, d), lambda i: (0, 0)),
            pl.BlockSpec((tm, d), lambda i: (i, 0)),
            pl.BlockSpec((1, 3, d), lambda i: (i * tm // rows_per_mod, 0, 0)),
        ],
        out_specs=pl.BlockSpec((tm, d), lambda i: (i, 0)),
        compiler_params=_params("parallel"),
        name="outproj_residual_parity",
    )(z, _parity_perm(grp).T, weight, bias.reshape(1, d), x, mod)


def _filter_body(feats_ref, w1_ref, b1_ref, fr1_ref, w2_ref, b2_ref, fr2_ref, w3f_ref, w3b_ref, dl_ref, o_ref,
                 h2_ref):
    feats = feats_ref[...]

    @pl.when(jnp.logical_and(pl.program_id(0) == 0, pl.program_id(1) == 0))
    def _():
        h1 = jnp.sin(fr1_ref[...] * (jnp.dot(feats, w1_ref[...], preferred_element_type=F32, precision=HIGHEST)
                                     + b1_ref[...]))
        h2_ref[...] = jnp.sin(fr2_ref[...] * (jnp.dot(h1, w2_ref[...], preferred_element_type=F32,
                                                      precision=HIGHEST) + b2_ref[...]))

    h2 = h2_ref[...]
    decay = jnp.exp(-feats[:, 0:1] * dl_ref[...])
    fwd = jnp.dot(h2, w3f_ref[...], preferred_element_type=F32, precision=HIGHEST) * decay
    bwd = jnp.dot(h2, w3b_ref[...], preferred_element_type=F32, precision=HIGHEST) * decay
    row = lax.broadcasted_iota(jnp.int32, (fwd.shape[0], 1), 0)
    bwd = jnp.where(row == 0, 0.0, bwd)
    nrm = lax.rsqrt(jnp.sum(fwd * fwd + bwd * bwd, axis=0, keepdims=True) + EPS)
    o_ref[0] = ((fwd + bwd) * nrm).astype(o_ref.dtype)
    o_ref[1] = ((fwd - bwd) * nrm).astype(o_ref.dtype)


def _hyena_filter_taps(feats, w1, b1, fr1, w2, b2, fr2, w3, absdelta, d):
    n, fp = feats.shape
    hp = w1.shape[1]
    tn = _tile(d, 512)
    nd = d // tn
    vec = pl.BlockSpec((1, hp), lambda o, j: (0, 0))
    return pl.pallas_call(
        _filter_body,
        out_shape=jax.ShapeDtypeStruct((2, n, HYENA_ORDER * d), BF16),
        grid=(HYENA_ORDER, nd),
        in_specs=[
            pl.BlockSpec((n, fp), lambda o, j: (0, 0)),
            pl.BlockSpec((fp, hp), lambda o, j: (0, 0)), vec, vec,
            pl.BlockSpec((hp, hp), lambda o, j: (0, 0)), vec, vec,
            pl.BlockSpec((hp, tn), lambda o, j: (0, (2 * o) * nd + j)),
            pl.BlockSpec((hp, tn), lambda o, j: (0, (2 * o + 1) * nd + j)),
            pl.BlockSpec((1, tn), lambda o, j: (0, j)),
        ],
        out_specs=pl.BlockSpec((2, n, tn), lambda o, j: (0, 0, o * nd + j)),
        scratch_shapes=[pltpu.VMEM((n, hp), F32)],
        compiler_params=_params("arbitrary", "arbitrary"),
        name="hyena_filter_taps",
    )(feats, w1, b1, fr1, w2, b2, fr2, w3, w3, absdelta)


def _spectrum_body(ce_ref, se_ref, co_ref, so_ref, se_, so_, de_, do_, o_ref, *, inv_n):
    i = pl.program_id(0)
    tk = ce_ref.shape[0]
    pr = _dot(ce_ref[...], se_[...])
    qr = _dot(co_ref[...], so_[...])
    pi = _dot(se_ref[...], de_[...])
    qi = _dot(so_ref[...], do_[...])
    row0 = (i * tk + lax.broadcasted_iota(jnp.int32, (tk, 1), 0)) == 0
    w_re = jnp.where(row0, inv_n, 2.0 * inv_n)
    o_ref[0] = (pr + qr) * w_re
    o_ref[1] = (pr - qr) * w_re
    o_ref[2] = (pi + qi) * (2.0 * inv_n)
    o_ref[3] = (qi - pi) * (2.0 * inv_n)

    @pl.when(i == 0)
    def _():
        rows = 2 * SUBLANES
        mid_r = _dot(se_ref[0:rows, :], se_[...])
        o_ref[2, 0:1, :] = mid_r[0:1, :] * (2.0 * inv_n)
        o_ref[3, 0:1, :] = qi[0:1, :] * (2.0 * inv_n)


def _hyena_spectrum(dft, taps):
    n = taps.shape[1]
    cols = taps.shape[2]
    half = n // 2
    tk = _tile(half, 1024)
    tn = _tile(cols, 512)
    nr = half // tk
    mat = lambda par, part: pl.BlockSpec((None, tk, half), lambda i, j: (par, part * nr + i, 0))
    tap = lambda which, par: pl.BlockSpec((None, half, tn), lambda i, j: (which, par, j))
    return pl.pallas_call(
        functools.partial(_spectrum_body, inv_n=1.0 / (2 * n)),
        out_shape=jax.ShapeDtypeStruct((4, half, cols), F32),
        grid=(nr, cols // tn),
        in_specs=[mat(0, 0), mat(0, 1), mat(1, 0), mat(1, 1), tap(0, 0), tap(0, 1), tap(1, 0), tap(1, 1)],
        out_specs=pl.BlockSpec((4, tk, tn), lambda i, j: (0, i, j)),
        compiler_params=_params("parallel", "parallel"),
        name="hyena_filter_spectrum",
    )(dft, dft, dft, dft, taps, taps, taps, taps)


def _dft_fwd_body(ce_ref, se_ref, co_ref, so_ref, ze_ref, zo_ref, k_ref, yr_ref, yi_ref):
    i = pl.program_id(1)
    tk = ce_ref.shape[0]
    row0 = (i * tk + lax.broadcasted_iota(jnp.int32, (tk, 1), 0)) == 0
    tn = ze_ref.shape[1]
    sub = min(tn, 2 * LANES)
    for c in range(tn // sub):
        cols = slice(c * sub, (c + 1) * sub)
        ze, zo = ze_ref[:, cols], zo_ref[:, cols]
        pr = _dot(ce_ref[...], ze)
        pi = _dot(se_ref[...], ze)
        qr = _dot(co_ref[...], zo)
        qi = _dot(so_ref[...], zo)
        kr_lo, kr_hi, ki_lo, ki_hi = k_ref[0, :, cols], k_ref[1, :, cols], k_ref[2, :, cols], k_ref[3, :, cols]
        zr_lo, zr_hi, zi_lo, zi_hi = pr + qr, pr - qr, pi + qi, qi - pi
        yr_lo = zr_lo * kr_lo - jnp.where(row0, 0.0, zi_lo * ki_lo)
        yr_hi = zr_hi * kr_hi - jnp.where(row0, 0.0, zi_hi * ki_hi)
        yi_lo = zr_lo * ki_lo + zi_lo * kr_lo
        yi_hi = zr_hi * ki_hi + zi_hi * kr_hi
        mid_r = pi * ki_lo - qi * ki_hi
        mid_i = pi * ki_hi + qi * ki_lo
        yr_ref[0, :, cols] = (yr_lo + yr_hi).astype(yr_ref.dtype)
        yr_ref[1, :, cols] = (yr_lo - yr_hi).astype(yr_ref.dtype)
        yi_ref[0, :, cols] = jnp.where(row0, mid_r, yi_lo - yi_hi).astype(yi_ref.dtype)
        yi_ref[1, :, cols] = jnp.where(row0, mid_i, yi_lo + yi_hi).astype(yi_ref.dtype)


def _dft_fwd(dft, z_src, z_col, spec, order, batch, d):
    half = z_src.shape[1] // batch
    tk = _tile(half, 1024)
    tn = _tile(d, 512)
    nr, nd = half // tk, d // tn
    out = jax.ShapeDtypeStruct((2, batch * half, d), BF16)
    once = pl.Buffered(1)
    mat = lambda par, part: pl.BlockSpec((None, tk, half), lambda j, i, b: (par, part * nr + i, 0),
                                         pipeline_mode=once)
    zin = lambda par: pl.BlockSpec((None, half, tn), lambda j, i, b: (par, b, z_col * nd + j))
    res = pl.BlockSpec((2, tk, tn), lambda j, i, b: (0, b * nr + i, j))
    return pl.pallas_call(
        _dft_fwd_body,
        out_shape=(out, out),
        grid=(nd, nr, batch),
        in_specs=[
            mat(0, 0), mat(0, 1), mat(1, 0), mat(1, 1), zin(0), zin(1),
            pl.BlockSpec((4, tk, tn), lambda j, i, b: (0, i, order * nd + j), pipeline_mode=once),
        ],
        out_specs=(res, res),
        compiler_params=_params("parallel", "parallel", "parallel"),
        name="hyena_dft_forward",
    )(dft, dft, dft, dft, z_src, z_src, spec)


def _dft_inv_body(atr_ref, ati_ref, yr_ref, yi_ref, g_ref, zp_ref, bias_ref, o_ref):
    y = _dot(atr_ref[...], yr_ref[...]) + _dot(ati_ref[...], yi_ref[...])
    zp = zp_ref[...].astype(F32)
    o_ref[...] = (g_ref[...].astype(F32) * (y + zp * bias_ref[...])).astype(o_ref.dtype)


def _dft_inv(dft_t, yr, yi, gate_src, gate_col, z_src, z_col, bias, batch, d):
    half = yr.shape[1] // batch
    tt = _tile(half, 1024)
    tn = _tile(d, 1024)
    nr, nd = half // tt, d // tn
    row = lambda cols: pl.BlockSpec((None, tt, tn), lambda j, p, i, b: (p, b * nr + i, cols * nd + j))
    return pl.pallas_call(
        _dft_inv_body,
        out_shape=jax.ShapeDtypeStruct((2, batch * half, d), BF16),
        grid=(nd, 2, nr, batch),
        in_specs=[
            pl.BlockSpec((None, tt, half), lambda j, p, i, b: (p, i, 0)),
            pl.BlockSpec((None, tt, half), lambda j, p, i, b: (p, i, 1)),
            pl.BlockSpec((None, half, tn), lambda j, p, i, b: (p, b, j)),
            pl.BlockSpec((None, half, tn), lambda j, p, i, b: (p, b, j)),
            row(gate_col), row(z_col),
            pl.BlockSpec((1, tn), lambda j, p, i, b: (0, j)),
        ],
        out_specs=row(0),
        compiler_params=_params("parallel", "parallel", "parallel", "parallel"),
        name="hyena_dft_inverse",
    )(dft_t, dft_t, yr, yi, gate_src, z_src, bias.reshape(1, d))


def _rope_tables(n_tok):
    rows = n_tok // GRID_W
    r = jnp.repeat(jnp.arange(rows), GRID_W).astype(F32)
    col = jnp.tile(jnp.arange(GRID_W), rows).astype(F32)
    half = HEAD_DIM // 2
    inv = ROPE_THETA ** (-jnp.arange(0, half, 2, dtype=F32) / half)
    ang = jnp.concatenate([r[:, None] * inv, col[:, None] * inv], axis=-1)
    cos = jnp.repeat(jnp.cos(ang), 2, axis=-1)
    sin = jnp.repeat(jnp.sin(ang), 2, axis=-1)
    sign = jnp.where(jnp.arange(HEAD_DIM) % 2 == 0, -1.0, 1.0).astype(F32)
    return cos, sin * sign


def _dft_tables(n):
    big = 2 * n
    half = n // 2
    k = jnp.arange(half, dtype=jnp.int32)[:, None]
    m = jnp.arange(half, dtype=jnp.int32)[None, :]
    alt = jnp.where(m % 2 == 0, 1.0, -1.0).astype(F32)
    mats = []
    for par in range(2):
        ang = ((k * (2 * m + par)) % big).astype(F32) * (2.0 * math.pi / big)
        s = jnp.where(k == 0, alt if par == 0 else -alt, -jnp.sin(ang))
        mats.append(jnp.concatenate([jnp.cos(ang), s], axis=0))
    a = jnp.stack(mats).astype(BF16)
    return a, a.transpose(0, 2, 1)


def _filter_features(n, width):
    t = jnp.linspace(0.0, 1.0, n, dtype=F32)[:, None]
    w = (2.0 * math.pi / n) * jnp.arange(n, dtype=F32)[:, None]
    bands = jnp.linspace(1e-4, HYENA_BANDS - 1, HYENA_BANDS, dtype=F32)
    ang = w * bands[None, :]
    feats = jnp.concatenate([t, jnp.cos(ang), -jnp.sin(ang)], axis=-1)
    feats = jnp.concatenate([feats[0::2], feats[1::2]], axis=0)
    return jnp.pad(feats, ((0, 0), (0, width - feats.shape[1])))


def _pad_to(a, shape):
    return jnp.pad(a, [(0, s - d) for d, s in zip(a.shape, shape)])


def _attn_layer(x, xc, mod_l, mod_c, g, w_in, w_out, g_q, g_k, sink, rope, batch, seq, n_ctx, ctx_out):
    d = x.shape[1]
    ones = jnp.ones((HEAD_DIM,), F32)
    gamma = jnp.concatenate([jnp.tile(g_q[0], N_HEADS), jnp.tile(g_q[1], N_HEADS),
                             jnp.tile(g_k[0], N_KV), jnp.tile(ones, N_KV),
                             jnp.tile(g_k[1], N_KV), jnp.tile(ones, N_KV)])[None, :]
    qs = jnp.concatenate([jnp.full((2 * Q_COLS,), HEAD_DIM ** -0.5, F32), jnp.ones((4 * KV_COLS,), F32)])[None, :]
    zk, ok = jnp.zeros((KV_COLS,), F32), jnp.ones((KV_COLS,), F32)
    isv = jnp.concatenate([jnp.zeros((2 * Q_COLS,), F32), zk, ok, zk, ok])[None, :]
    tables = (gamma, qs, isv)
    tm = _row_tile(batch * n_ctx, batch * n_ctx)
    ident = (jnp.ones((tm, HEAD_DIM), F32), jnp.zeros((tm, HEAD_DIM), F32))
    w_in_b = _col_tiled(w_in, _tile(w_in.shape[1], 512))
    qkv = _qkv_proj(x, mod_l, g, w_in_b, seq, tables, rope, seq)
    qkv_c = _qkv_proj(xc, mod_c, g, w_in_b, batch * n_ctx, tables, ident, None)
    qa_col, qb_col = 0, N_KV
    ka_col = 2 * Q_COLS // HEAD_DIM
    va_col, kb_col, vb_col = ka_col + N_KV, ka_col + 2 * N_KV, ka_col + 3 * N_KV
    o_a = _attention(sink, qkv, qkv_c, qkv, batch=batch, q_rows=seq, q_col=qa_col, k_col=ka_col, v_col=va_col,
                     has_mask=False, has_sink=False)
    o_b = _attention(sink, qkv, qkv_c, qkv, batch=batch, q_rows=seq, q_col=qb_col, k_col=kb_col, v_col=vb_col,
                     has_mask=True, has_sink=True)
    w_out_b = w_out.astype(BF16)
    zero_b = jnp.zeros((d,), F32)
    x = _outproj((o_a, o_b), w_out_b, zero_b, x, mod_l, seq)
    if ctx_out:
        co_a = _attention(sink, qkv_c, qkv_c, None, batch=batch, q_rows=n_ctx, q_col=qa_col, k_col=ka_col,
                          v_col=va_col, has_mask=False, has_sink=False)
        co_b = _attention(sink, qkv_c, qkv_c, None, batch=batch, q_rows=n_ctx, q_col=qb_col, k_col=kb_col,
                          v_col=vb_col, has_mask=False, has_sink=True)
        xc = _outproj((co_a, co_b), w_out_b, zero_b, xc, mod_c, batch * n_ctx)
    return x, xc


def _hyena_layer(x, mod, g, rows_per_mod, batch, w_in_b, b_in, w_conv, b_conv, filt, hy_bias, w_out_b, b_out):
    d = x.shape[1]
    n = x.shape[0] // batch
    w1, b1, fr1, w2, b2, fr2, w3 = filt
    hp = LANES
    feats = _filter_features(n, LANES)
    max_decay = math.log(HYENA_TARGET) / HYENA_FAST_PCT
    min_decay = math.log(HYENA_TARGET) / HYENA_SLOW_PCT
    absdelta = jnp.abs(jnp.linspace(min_decay, max_decay, d, dtype=F32))[None, :]
    taps = _hyena_filter_taps(
        feats, _pad_to(w1, (LANES, hp)), _pad_to(b1[None, :], (1, hp)), _pad_to(fr1[None, :], (1, hp)),
        _pad_to(w2, (hp, hp)), _pad_to(b2[None, :], (1, hp)), _pad_to(fr2[None, :], (1, hp)),
        _pad_to(w3, (hp, w3.shape[1])), absdelta, d)
    dft, dft_t = _dft_tables(n)
    spec = _hyena_spectrum(dft, taps)
    pc = _proj(x, mod, g, w_in_b, b_in, w_conv, b_conv, rows_per_mod, n)
    yr, yi = _dft_fwd(dft, pc, 0, spec, 0, batch, d)
    z = _dft_inv(dft_t, yr, yi, pc, 1, pc, 0, hy_bias[0], batch, d)
    yr, yi = _dft_fwd(dft, z, 0, spec, 1, batch, d)
    z = _dft_inv(dft_t, yr, yi, pc, 2, z, 0, hy_bias[1], batch, d)
    return _outproj_parity(z, w_out_b, b_out, x, mod, rows_per_mod, n)


def kernel(x, c, ctx, c_ctx, w_mod, b_mod, g_norm, w_ffn_in, w_ffn_out, w_attn_in, w_attn_out, g_q, g_k, sink, w_hy_in, b_hy_in, w_hy_conv, b_hy_conv, hf_w1, hf_b1, hf_freq1, hf_w2, hf_b2, hf_freq2, hf_w3, hy_bias, w_hy_out, b_hy_out):
    batch, seq, d = x.shape
    n_ctx = ctx.shape[1]
    depth = w_mod.shape[0]
    rows_c = batch * n_ctx
    rope = _rope_tables(seq)
    last_ctx = max(l for l in range(depth) if l % 2 == 0)

    r_pad = -(-(batch + 1) // SUBLANES) * SUBLANES
    c_all = _pad_to(jnp.concatenate([c, c_ctx[None, :]], axis=0), (r_pad, d))
    m_all = _mod_all(c_all, w_mod, b_mod).reshape(depth, r_pad, N_MOD, d)

    w_ffn_in_b = w_ffn_in.astype(BF16)
    w_ffn_out_b = w_ffn_out.astype(BF16)

    xl = x.reshape(batch * seq, d)
    xc = ctx.reshape(rows_c, d)
    for l in range(depth):
        i = l // 2
        ctx_live = l <= last_ctx
        ctx_full = l < last_ctx
        mods_l = [m_all[l, :batch, 3 * k:3 * k + 3] for k in range(3)]
        mods_c = [m_all[l, batch:batch + 1, 3 * k:3 * k + 3] for k in range(3)]
        xl = _ffn(xl, mods_l[0], g_norm[l, 0], w_ffn_in_b, w_ffn_out_b, l, 0, seq)
        if ctx_live:
            xc = _ffn(xc, mods_c[0], g_norm[l, 0], w_ffn_in_b, w_ffn_out_b, l, 0, rows_c)
        if l % 2 == 0:
            xl, xc = _attn_layer(xl, xc, mods_l[1], mods_c[1], g_norm[l, 1], w_attn_in[i], w_attn_out[i],
                                 g_q[i], g_k[i], sink[i], rope, batch, seq, n_ctx, ctx_full)
        else:
            w_in_b = _col_tiled(w_hy_in[i], _tile(w_hy_in.shape[2], 512))
            w_out_b = w_hy_out[i].astype(BF16)
            filt = (hf_w1[i], hf_b1[i], hf_freq1[i], hf_w2[i], hf_b2[i], hf_freq2[i], hf_w3[i])
            xl = _hyena_layer(xl, mods_l[1], g_norm[l, 1], seq, batch, w_in_b, b_hy_in[i], w_hy_conv[i],
                              b_hy_conv[i], filt, hy_bias[i], w_out_b, b_hy_out[i])
            if ctx_full:
                xc = _hyena_layer(xc, mods_c[1], g_norm[l, 1], rows_c, batch, w_in_b, b_hy_in[i], w_hy_conv[i],
                                  b_hy_conv[i], filt, hy_bias[i], w_out_b, b_hy_out[i])
        xl = _ffn(xl, mods_l[2], g_norm[l, 2], w_ffn_in_b, w_ffn_out_b, l, 1, seq)
        if ctx_full:
            xc = _ffn(xc, mods_c[2], g_norm[l, 2], w_ffn_in_b, w_ffn_out_b, l, 1, rows_c)
    return xl.reshape(batch, seq, d)
```

```python
import functools
import math

import jax
import jax.numpy as jnp
from jax import lax
from jax.experimental import pallas as pl
from jax.experimental.pallas import tpu as pltpu

HEAD_DIM = 128
N_HEADS = 8
N_KV = 2
GROUP = N_HEADS // N_KV
Q_COLS = N_HEADS * HEAD_DIM
KV_COLS = N_KV * HEAD_DIM
GRID_W = 64
WINDOW = 128
ROPE_THETA = 10000.0
N_MOD = 9
HYENA_ORDER = 2
HYENA_SHORT_K = 3
HYENA_BANDS = 16
HYENA_TARGET = 1e-2
HYENA_FAST_PCT = 0.3
HYENA_SLOW_PCT = 1.5
EPS = 1e-6
NEG_INF = -1e30

LANES = 128
SUBLANES = 8
VMEM_LIMIT_BYTES = 60 * 1024 * 1024

F32 = jnp.float32
BF16 = jnp.bfloat16
HIGHEST = lax.Precision.HIGHEST


def _params(*sem):
    return pltpu.CompilerParams(dimension_semantics=sem, vmem_limit_bytes=VMEM_LIMIT_BYTES)


def _tile(dim, pref):
    t = min(dim, pref)
    while dim % t:
        t //= 2
    return t


def _dot(a, b):
    return jnp.dot(a, b, preferred_element_type=F32)


def _dot_nt(a, b):
    return lax.dot_general(a, b, (((1,), (1,)), ((), ())), preferred_element_type=F32)


def _silu(v):
    return v * (1.0 / (1.0 + jnp.exp(-v)))


def _modulated(xf, g, shift, scale):
    ms = jnp.mean(xf * xf, axis=-1, keepdims=True)
    y = xf * lax.rsqrt(ms + EPS)
    return (y * g) * (1.0 + scale) + shift


def _mod_body(c_ref, w_ref, b_ref, o_ref):
    s = _silu(c_ref[...])
    o_ref[0] = jnp.dot(s, w_ref[0], preferred_element_type=F32, precision=HIGHEST) + b_ref[0]


def _mod_all(c_all, w_mod, b_mod):
    depth, d, nd = w_mod.shape
    r = c_all.shape[0]
    tn = _tile(nd, 2048)
    return pl.pallas_call(
        _mod_body,
        out_shape=jax.ShapeDtypeStruct((depth, r, nd), F32),
        grid=(depth, nd // tn),
        in_specs=[
            pl.BlockSpec((r, d), lambda l, j: (0, 0)),
            pl.BlockSpec((1, d, tn), lambda l, j: (l, 0, j)),
            pl.BlockSpec((1, 1, tn), lambda l, j: (l, 0, j)),
        ],
        out_specs=pl.BlockSpec((1, r, tn), lambda l, j: (l, 0, j)),
        compiler_params=_params("parallel", "parallel"),
        name="mod_vectors",
    )(c_all, w_mod, b_mod.reshape(depth, 1, nd))


def _ffn_body(x_ref, mod_ref, g_ref, wg_ref, wu_ref, wo_ref, o_ref, h_ref, acc_ref):
    j = pl.program_id(1)
    last = pl.num_programs(1) - 1

    def chunk(h):
        a = _dot(h, wg_ref[...])
        u = _dot(h, wu_ref[...])
        return _dot((_silu(a) * u).astype(BF16), wo_ref[...])

    @pl.when(j == 0)
    def _():
        h = _modulated(x_ref[...], g_ref[...], mod_ref[0, 0:1, :], mod_ref[0, 1:2, :]).astype(BF16)
        h_ref[...] = h
        acc_ref[...] = chunk(h)

    @pl.when(jnp.logical_and(j > 0, j < last))
    def _():
        acc_ref[...] += chunk(h_ref[...])

    @pl.when(jnp.logical_and(j > 0, j == last))
    def _():
        y = acc_ref[...] + chunk(h_ref[...])
        o_ref[...] = x_ref[...] + (0.5 * mod_ref[0, 2:3, :]) * y


def _ffn(x, mod, g, w_in, w_out, layer, half, rows_per_mod):
    rows, d = x.shape
    f = w_out.shape[2]
    tm = _tile(min(rows, rows_per_mod), 1024)
    tf = _tile(f, 512)
    nf = f // tf
    assert nf >= 2
    return pl.pallas_call(
        _ffn_body,
        out_shape=jax.ShapeDtypeStruct((rows, d), F32),
        grid=(rows // tm, nf),
        in_specs=[
            pl.BlockSpec((tm, d), lambda i, j: (i, 0)),
            pl.BlockSpec((1, 3, d), lambda i, j: (i * tm // rows_per_mod, 0, 0)),
            pl.BlockSpec((1, d), lambda i, j: (0, 0)),
            pl.BlockSpec((None, None, d, tf), lambda i, j: (layer, half, 0, j)),
            pl.BlockSpec((None, None, d, tf), lambda i, j: (layer, half, 0, nf + j)),
            pl.BlockSpec((None, None, tf, d), lambda i, j: (layer, half, j, 0)),
        ],
        out_specs=pl.BlockSpec((tm, d), lambda i, j: (i, 0), pipeline_mode=pl.Buffered(1)),
        scratch_shapes=[pltpu.VMEM((tm, d), BF16), pltpu.VMEM((tm, d), F32)],
        compiler_params=_params("parallel", "arbitrary"),
        name="ffn_half_step",
    )(x, mod, g.reshape(1, d), w_in, w_in, w_out)


def _proj_body(x_ref, mod_ref, g_ref, perm_ref, w_ref, b_ref, cw_ref, cb_ref, o_ref, h_ref, *, half_seq):
    tm = x_ref.shape[0]
    half = tm // 2
    grp = perm_ref.shape[0]

    def project_conv():
        p = _dot(h_ref[...], w_ref[...]) + b_ref[...]
        pe, po = p[:half], p[half:]
        pos = lax.broadcasted_iota(jnp.int32, (half, 1), 0) % half_seq
        po_prev = jnp.where(pos == 0, 0.0, pltpu.roll(po, 1, 0))
        pe_next = jnp.where(pos == half_seq - 1, 0.0, pltpu.roll(pe, half - 1, 0))
        w0, w1, w2 = cw_ref[0:1, :], cw_ref[1:2, :], cw_ref[2:3, :]
        o_ref[0] = (((cb_ref[...] + po_prev * w0) + pe * w1) + po * w2).astype(o_ref.dtype)
        o_ref[1] = (((cb_ref[...] + pe * w0) + po * w1) + pe_next * w2).astype(o_ref.dtype)

    @pl.when(pl.program_id(1) == 0)
    def _():
        for r in range(tm // grp):
            h = _modulated(x_ref[r * grp:(r + 1) * grp, :], g_ref[...], mod_ref[0, 0:1, :], mod_ref[0, 1:2, :])
            hp = _dot(perm_ref[...], h.astype(BF16)).astype(BF16)
            h_ref[r * (grp // 2):(r + 1) * (grp // 2), :] = hp[:grp // 2]
            h_ref[half + r * (grp // 2):half + (r + 1) * (grp // 2), :] = hp[grp // 2:]
        project_conv()

    @pl.when(pl.program_id(1) > 0)
    def _():
        project_conv()


def _qkv_body(x_ref, mod_ref, g_ref, w_ref, gamma_ref, qs_ref, isv_ref, cos_ref, sin_ref, o_ref, h_ref, p_ref):
    j = pl.program_id(1)
    last = pl.num_programs(1) - 1

    def finish():
        cos = cos_ref[...]
        sin = sin_ref[...]
        even = (lax.broadcasted_iota(jnp.int32, (1, HEAD_DIM), 1) % 2) == 0
        for hh in range(p_ref.shape[1] // HEAD_DIM):
            sl = slice(hh * HEAD_DIM, (hh + 1) * HEAD_DIM)
            ph = p_ref[:, sl]
            ms = jnp.mean(ph * ph, axis=-1, keepdims=True)
            nh = (ph * lax.rsqrt(ms + EPS)) * gamma_ref[:, sl]
            partner = jnp.where(even, pltpu.roll(nh, HEAD_DIM - 1, 1), pltpu.roll(nh, 1, 1))
            r = (nh * cos + partner * sin) * qs_ref[:, sl]
            o_ref[:, sl] = jnp.where(isv_ref[:, sl] > 0.0, ph, r).astype(o_ref.dtype)

    @pl.when(j == 0)
    def _():
        h = _modulated(x_ref[...], g_ref[...], mod_ref[0, 0:1, :], mod_ref[0, 1:2, :]).astype(BF16)
        h_ref[...] = h
        p_ref[...] = _dot(h, w_ref[...])

    @pl.when(jnp.logical_and(j > 0, j < last))
    def _():
        finish()
        p_ref[...] = _dot(h_ref[...], w_ref[...])

    @pl.when(j == last)
    def _():
        finish()


def _row_tile(rows, rows_per_mod):
    return _tile(min(rows, rows_per_mod), 1024)


def _parity_perm(tm):
    r = jnp.arange(tm, dtype=jnp.int32)[:, None]
    c = jnp.arange(tm, dtype=jnp.int32)[None, :]
    src = jnp.where(r < tm // 2, 2 * r, 2 * (r - tm // 2) + 1)
    return (c == src).astype(BF16)


def _col_tiled(w, tn):
    k, n = w.shape
    return w.reshape(k, n // tn, tn).transpose(1, 0, 2).astype(BF16)


PERM_GROUP = 256


def _proj(x, mod, g, w_t, b, conv_w, conv_b, rows_per_mod, seq):
    rows, d = x.shape
    nj, _, tn = w_t.shape
    n = nj * tn
    tm = _tile(min(rows, rows_per_mod), max(seq, 1024))
    grp = min(PERM_GROUP, tm)
    assert tm % seq == 0 and tm % grp == 0 and grp % (4 * SUBLANES) == 0
    col = pl.BlockSpec((1, tn), lambda i, j: (0, j))
    return pl.pallas_call(
        functools.partial(_proj_body, half_seq=seq // 2),
        out_shape=jax.ShapeDtypeStruct((2, rows // 2, n), BF16),
        grid=(rows // tm, n // tn),
        in_specs=[
            pl.BlockSpec((tm, d), lambda i, j: (i, 0)),
            pl.BlockSpec((1, 3, d), lambda i, j: (i * tm // rows_per_mod, 0, 0)),
            pl.BlockSpec((1, d), lambda i, j: (0, 0)),
            pl.BlockSpec((grp, grp), lambda i, j: (0, 0)),
            pl.BlockSpec((None, d, tn), lambda i, j: (j, 0, 0)),
            col,
            pl.BlockSpec((HYENA_SHORT_K, tn), lambda i, j: (0, j)),
            col,
        ],
        out_specs=pl.BlockSpec((2, tm // 2, tn), lambda i, j: (0, i, j)),
        scratch_shapes=[pltpu.VMEM((tm, d), BF16)],
        compiler_params=_params("parallel", "arbitrary"),
        name="hyena_project_conv",
    )(x, mod, g.reshape(1, d), _parity_perm(grp), w_t, b.reshape(1, n), conv_w, conv_b.reshape(1, n))


def _qkv_proj(x, mod, g, w_t, rows_per_mod, tables, rope, seq):
    rows, d = x.shape
    nj, _, tn = w_t.shape
    n = nj * tn
    tm = _row_tile(rows, rows_per_mod)
    gamma, qs, isv = tables
    cos, sin = rope

    def prev(j):
        return jnp.maximum(j - 1, 0)

    col = pl.BlockSpec((1, tn), lambda i, j: (0, prev(j)))
    if seq is None:
        rope_spec = pl.BlockSpec((tm, HEAD_DIM), lambda i, j: (0, 0))
    else:
        rope_spec = pl.BlockSpec((tm, HEAD_DIM), lambda i, j: (i % (seq // tm), 0))
    return pl.pallas_call(
        _qkv_body,
        out_shape=jax.ShapeDtypeStruct((rows, n), BF16),
        grid=(rows // tm, nj + 1),
        in_specs=[
            pl.BlockSpec((tm, d), lambda i, j: (i, 0)),
            pl.BlockSpec((1, 3, d), lambda i, j: (i * tm // rows_per_mod, 0, 0)),
            pl.BlockSpec((1, d), lambda i, j: (0, 0)),
            pl.BlockSpec((None, d, tn), lambda i, j: (jnp.minimum(j, nj - 1), 0, 0)),
            col, col, col, rope_spec, rope_spec,
        ],
        out_specs=pl.BlockSpec((tm, tn), lambda i, j: (i, prev(j))),
        scratch_shapes=[pltpu.VMEM((tm, d), BF16), pltpu.VMEM((tm, tn), F32)],
        compiler_params=_params("arbitrary", "arbitrary"),
        name="qkv_project",
    )(x, mod, g.reshape(1, d), w_t, gamma, qs, isv, cos, sin)


def _attn_body(*refs, has_lat, has_mask, has_sink, tq):
    sink_ref, q_ref, kc_ref, vc_ref = refs[:4]
    if has_lat:
        k_ref, v_ref, o_ref = refs[4:7]
    else:
        o_ref = refs[4]
    g = pl.program_id(1)
    qi = pl.program_id(2)
    if has_lat:
        l_rows = k_ref.shape[0]
        if has_mask:
            kw = min(l_rows, tq + 2 * WINDOW)
            k0 = pl.multiple_of(jnp.clip(qi * tq - WINDOW, 0, l_rows - kw), WINDOW)
            keys = pl.ds(k0, kw)
        else:
            k0, keys = 0, slice(None)
    if has_mask:
        qpos = qi * tq + lax.broadcasted_iota(jnp.int32, (tq, 1), 0)
        kpos = k0 + lax.broadcasted_iota(jnp.int32, (1, kw), 1)
        band = jnp.abs(kpos - qpos) <= WINDOW
    heads = [slice(hh * HEAD_DIM, (hh + 1) * HEAD_DIM) for hh in range(GROUP)]
    s1 = [_dot_nt(q_ref[:, sl], kc_ref[...]) for sl in heads]
    m = [jnp.max(s, axis=-1, keepdims=True) for s in s1]
    if has_lat:
        s2 = [_dot_nt(q_ref[:, sl], k_ref[keys, :]) for sl in heads]
        if has_mask:
            s2 = [jnp.where(band, s, NEG_INF) for s in s2]
        m = [jnp.maximum(mh, jnp.max(s, axis=-1, keepdims=True)) for mh, s in zip(m, s2)]
    if has_sink:
        sk = [sink_ref[g * GROUP + hh] for hh in range(GROUP)]
        m = [jnp.maximum(mh, skh) for mh, skh in zip(m, sk)]
    for hh, sl in enumerate(heads):
        p1 = jnp.exp(s1[hh] - m[hh])
        l = jnp.sum(p1, axis=-1, keepdims=True)
        o = _dot(p1.astype(BF16), vc_ref[...])
        if has_lat:
            p2 = jnp.exp(s2[hh] - m[hh])
            l = l + jnp.sum(p2, axis=-1, keepdims=True)
            o = o + _dot(p2.astype(BF16), v_ref[keys, :])
        if has_sink:
            l = l + jnp.exp(sk[hh] - m[hh])
        o_ref[:, sl] = (o / l).astype(o_ref.dtype)


def _attention(sink, q_src, ctx_src, lat_src, *, batch, q_rows, q_col, k_col, v_col, has_mask, has_sink):
    tq = _tile(q_rows, 256 if has_mask else 512)
    assert not has_mask or tq % WINDOW == 0
    c_rows = ctx_src.shape[0] // batch
    gw = GROUP * HEAD_DIM
    in_specs = [
        pl.BlockSpec(memory_space=pltpu.SMEM),
        pl.BlockSpec((tq, gw), lambda b, g, i: (b * (q_rows // tq) + i, q_col + g)),
        pl.BlockSpec((c_rows, HEAD_DIM), lambda b, g, i: (b, k_col + g)),
        pl.BlockSpec((c_rows, HEAD_DIM), lambda b, g, i: (b, v_col + g)),
    ]
    args = [sink, q_src, ctx_src, ctx_src]
    has_lat = lat_src is not None
    if has_lat:
        l_rows = lat_src.shape[0] // batch
        in_specs += [
            pl.BlockSpec((l_rows, HEAD_DIM), lambda b, g, i: (b, k_col + g)),
            pl.BlockSpec((l_rows, HEAD_DIM), lambda b, g, i: (b, v_col + g)),
        ]
        args += [lat_src, lat_src]
    return pl.pallas_call(
        functools.partial(_attn_body, has_lat=has_lat, has_mask=has_mask, has_sink=has_sink, tq=tq),
        out_shape=jax.ShapeDtypeStruct((batch * q_rows, Q_COLS), BF16),
        grid=(batch, N_KV, q_rows // tq),
        in_specs=in_specs,
        out_specs=pl.BlockSpec((tq, gw), lambda b, g, i: (b * (q_rows // tq) + i, g)),
        compiler_params=_params("parallel", "parallel", "parallel"),
        name="gqa_attention",
    )(*args)


def _outproj_body(*refs, n_in):
    a_refs, w_refs = refs[:n_in], refs[n_in:2 * n_in]
    b_ref, x_ref, mod_ref, o_ref = refs[2 * n_in:]
    y = b_ref[...] + _dot(a_refs[0][...], w_refs[0][...])
    for a_ref, w_ref in zip(a_refs[1:], w_refs[1:]):
        y = y + _dot(a_ref[...], w_ref[...])
    o_ref[...] = x_ref[...] + mod_ref[0, 2:3, :] * y


def _outproj(acts, weight, bias, x, mod, rows_per_mod):
    rows, d = x.shape
    tm = _tile(min(rows, rows_per_mod), 512)
    n_in = len(acts)
    k = acts[0].shape[1]
    assert all(a.shape[1] == k for a in acts) and weight.shape[0] == n_in * k
    in_specs = [pl.BlockSpec((tm, k), lambda i: (i, 0)) for _ in acts]
    in_specs += [pl.BlockSpec((k, d), lambda i, n=n: (n, 0)) for n in range(n_in)]
    in_specs += [
        pl.BlockSpec((1, d), lambda i: (0, 0)),
        pl.BlockSpec((tm, d), lambda i: (i, 0)),
        pl.BlockSpec((1, 3, d), lambda i: (i * tm // rows_per_mod, 0, 0)),
    ]
    return pl.pallas_call(
        functools.partial(_outproj_body, n_in=n_in),
        out_shape=jax.ShapeDtypeStruct((rows, d), F32),
        grid=(rows // tm,),
        in_specs=in_specs,
        out_specs=pl.BlockSpec((tm, d), lambda i: (i, 0)),
        compiler_params=_params("parallel"),
        name="outproj_residual",
    )(*acts, *([weight] * n_in), bias.reshape(1, d), x, mod)


def _outproj_parity_body(z_ref, unperm_ref, w_ref, b_ref, x_ref, mod_ref, o_ref):
    grp = unperm_ref.shape[0]
    hg = grp // 2
    parts = []
    for r in range(x_ref.shape[0] // grp):
        zg = jnp.concatenate([z_ref[0, r * hg:(r + 1) * hg, :], z_ref[1, r * hg:(r + 1) * hg, :]], axis=0)
        parts.append(_dot(unperm_ref[...], zg).astype(BF16))
    y = b_ref[...] + _dot(jnp.concatenate(parts, axis=0), w_ref[...])
    o_ref[...] = x_ref[...] + mod_ref[0, 2:3, :] * y


def _outproj_parity(z, weight, bias, x, mod, rows_per_mod, seq):
    rows, d = x.shape
    k = z.shape[2]
    tm = _tile(min(rows, rows_per_mod), 512)
    grp = min(PERM_GROUP, tm)
    assert tm % grp == 0 and (tm % seq == 0 or seq % tm == 0)
    return pl.pallas_call(
        _outproj_parity_body,
        out_shape=jax.ShapeDtypeStruct((rows, d), F32),
        grid=(rows // tm,),
        in_specs=[
            pl.BlockSpec((2, tm // 2, k), lambda i: (0, i, 0)),
            pl.BlockSpec((grp, grp), lambda i: (0, 0)),
            pl.BlockSpec((k, d), lambda i: (0, 0)),
            pl.BlockSpec((1, d), lambda i: (0, 0)),
            pl.BlockSpec((tm, d), lambda i: (i, 0)),
            pl.BlockSpec((1, 3, d), lambda i: (i * tm // rows_per_mod, 0, 0)),
        ],
        out_specs=pl.BlockSpec((tm, d), lambda i: (i, 0)),
        compiler_params=_params("parallel"),
        name="outproj_residual_parity",
    )(z, _parity_perm(grp).T, weight, bias.reshape(1, d), x, mod)


def _filter_body(feats_ref, w1_ref, b1_ref, fr1_ref, w2_ref, b2_ref, fr2_ref, w3f_ref, w3b_ref, dl_ref, o_ref,
                 h2_ref):
    feats = feats_ref[...]

    @pl.when(jnp.logical_and(pl.program_id(0) == 0, pl.program_id(1) == 0))
    def _():
        h1 = jnp.sin(fr1_ref[...] * (jnp.dot(feats, w1_ref[...], preferred_element_type=F32, precision=HIGHEST)
                                     + b1_ref[...]))
        h2_ref[...] = jnp.sin(fr2_ref[...] * (jnp.dot(h1, w2_ref[...], preferred_element_type=F32,
                                                      precision=HIGHEST) + b2_ref[...]))

    h2 = h2_ref[...]
    decay = jnp.exp(-feats[:, 0:1] * dl_ref[...])
    fwd = jnp.dot(h2, w3f_ref[...], preferred_element_type=F32, precision=HIGHEST) * decay
    bwd = jnp.dot(h2, w3b_ref[...], preferred_element_type=F32, precision=HIGHEST) * decay
    row = lax.broadcasted_iota(jnp.int32, (fwd.shape[0], 1), 0)
    bwd = jnp.where(row == 0, 0.0, bwd)
    nrm = lax.rsqrt(jnp.sum(fwd * fwd + bwd * bwd, axis=0, keepdims=True) + EPS)
    o_ref[0] = ((fwd + bwd) * nrm).astype(o_ref.dtype)
    o_ref[1] = ((fwd - bwd) * nrm).astype(o_ref.dtype)


def _hyena_filter_taps(feats, w1, b1, fr1, w2, b2, fr2, w3, absdelta, d):
    n, fp = feats.shape
    hp = w1.shape[1]
    tn = _tile(d, 512)
    nd = d // tn
    vec = pl.BlockSpec((1, hp), lambda o, j: (0, 0))
    return pl.pallas_call(
        _filter_body,
        out_shape=jax.ShapeDtypeStruct((2, n, HYENA_ORDER * d), BF16),
        grid=(HYENA_ORDER, nd),
        in_specs=[
            pl.BlockSpec((n, fp), lambda o, j: (0, 0)),
            pl.BlockSpec((fp, hp), lambda o, j: (0, 0)), vec, vec,
            pl.BlockSpec((hp, hp), lambda o, j: (0, 0)), vec, vec,
            pl.BlockSpec((hp, tn), lambda o, j: (0, (2 * o) * nd + j)),
            pl.BlockSpec((hp, tn), lambda o, j: (0, (2 * o + 1) * nd + j)),
            pl.BlockSpec((1, tn), lambda o, j: (0, j)),
        ],
        out_specs=pl.BlockSpec((2, n, tn), lambda o, j: (0, 0, o * nd + j)),
        scratch_shapes=[pltpu.VMEM((n, hp), F32)],
        compiler_params=_params("arbitrary", "arbitrary"),
        name="hyena_filter_taps",
    )(feats, w1, b1, fr1, w2, b2, fr2, w3, w3, absdelta)


def _spectrum_body(ce_ref, se_ref, co_ref, so_ref, se_, so_, de_, do_, o_ref, *, inv_n):
    i = pl.program_id(0)
    tk = ce_ref.shape[0]
    pr = _dot(ce_ref[...], se_[...])
    qr = _dot(co_ref[...], so_[...])
    pi = _dot(se_ref[...], de_[...])
    qi = _dot(so_ref[...], do_[...])
    row0 = (i * tk + lax.broadcasted_iota(jnp.int32, (tk, 1), 0)) == 0
    w_re = jnp.where(row0, inv_n, 2.0 * inv_n)
    o_ref[0] = (pr + qr) * w_re
    o_ref[1] = (pr - qr) * w_re
    o_ref[2] = (pi + qi) * (2.0 * inv_n)
    o_ref[3] = (qi - pi) * (2.0 * inv_n)

    @pl.when(i == 0)
    def _():
        rows = 2 * SUBLANES
        mid_r = _dot(se_ref[0:rows, :], se_[...])
        o_ref[2, 0:1, :] = mid_r[0:1, :] * (2.0 * inv_n)
        o_ref[3, 0:1, :] = qi[0:1, :] * (2.0 * inv_n)


def _hyena_spectrum(dft, taps):
    n = taps.shape[1]
    cols = taps.shape[2]
    half = n // 2
    tk = _tile(half, 1024)
    tn = _tile(cols, 512)
    nr = half // tk
    mat = lambda par, part: pl.BlockSpec((None, tk, half), lambda i, j: (par, part * nr + i, 0))
    tap = lambda which, par: pl.BlockSpec((None, half, tn), lambda i, j: (which, par, j))
    return pl.pallas_call(
        functools.partial(_spectrum_body, inv_n=1.0 / (2 * n)),
        out_shape=jax.ShapeDtypeStruct((4, half, cols), F32),
        grid=(nr, cols // tn),
        in_specs=[mat(0, 0), mat(0, 1), mat(1, 0), mat(1, 1), tap(0, 0), tap(0, 1), tap(1, 0), tap(1, 1)],
        out_specs=pl.BlockSpec((4, tk, tn), lambda i, j: (0, i, j)),
        compiler_params=_params("parallel", "parallel"),
        name="hyena_filter_spectrum",
    )(dft, dft, dft, dft, taps, taps, taps, taps)


def _dft_fwd_body(ce_ref, se_ref, co_ref, so_ref, ze_ref, zo_ref, k_ref, yr_ref, yi_ref):
    i = pl.program_id(1)
    tk = ce_ref.shape[0]
    row0 = (i * tk + lax.broadcasted_iota(jnp.int32, (tk, 1), 0)) == 0
    tn = ze_ref.shape[1]
    sub = min(tn, 2 * LANES)
    for c in range(tn // sub):
        cols = slice(c * sub, (c + 1) * sub)
        ze, zo = ze_ref[:, cols], zo_ref[:, cols]
        pr = _dot(ce_ref[...], ze)
        pi = _dot(se_ref[...], ze)
        qr = _dot(co_ref[...], zo)
        qi = _dot(so_ref[...], zo)
        kr_lo, kr_hi, ki_lo, ki_hi = k_ref[0, :, cols], k_ref[1, :, cols], k_ref[2, :, cols], k_ref[3, :, cols]
        zr_lo, zr_hi, zi_lo, zi_hi = pr + qr, pr - qr, pi + qi, qi - pi
        yr_lo = zr_lo * kr_lo - jnp.where(row0, 0.0, zi_lo * ki_lo)
        yr_hi = zr_hi * kr_hi - jnp.where(row0, 0.0, zi_hi * ki_hi)
        yi_lo = zr_lo * ki_lo + zi_lo * kr_lo
        yi_hi = zr_hi * ki_hi + zi_hi * kr_hi
        mid_r = pi * ki_lo - qi * ki_hi
        mid_i = pi * ki_hi + qi * ki_lo
        yr_ref[0, :, cols] = (yr_lo + yr_hi).astype(yr_ref.dtype)
        yr_ref[1, :, cols] = (yr_lo - yr_hi).astype(yr_ref.dtype)
        yi_ref[0, :, cols] = jnp.where(row0, mid_r, yi_lo - yi_hi).astype(yi_ref.dtype)
        yi_ref[1, :, cols] = jnp.where(row0, mid_i, yi_lo + yi_hi).astype(yi_ref.dtype)


def _dft_fwd(dft, z_src, z_col, spec, order, batch, d):
    half = z_src.shape[1] // batch
    tk = _tile(half, 1024)
    tn = _tile(d, 512)
    nr, nd = half // tk, d // tn
    out = jax.ShapeDtypeStruct((2, batch * half, d), BF16)
    once = pl.Buffered(1)
    mat = lambda par, part: pl.BlockSpec((None, tk, half), lambda j, i, b: (par, part * nr + i, 0),
                                         pipeline_mode=once)
    zin = lambda par: pl.BlockSpec((None, half, tn), lambda j, i, b: (par, b, z_col * nd + j))
    res = pl.BlockSpec((2, tk, tn), lambda j, i, b: (0, b * nr + i, j))
    return pl.pallas_call(
        _dft_fwd_body,
        out_shape=(out, out),
        grid=(nd, nr, batch),
        in_specs=[
            mat(0, 0), mat(0, 1), mat(1, 0), mat(1, 1), zin(0), zin(1),
            pl.BlockSpec((4, tk, tn), lambda j, i, b: (0, i, order * nd + j), pipeline_mode=once),
        ],
        out_specs=(res, res),
        compiler_params=_params("parallel", "parallel", "parallel"),
        name="hyena_dft_forward",
    )(dft, dft, dft, dft, z_src, z_src, spec)


def _dft_inv_body(atr_ref, ati_ref, yr_ref, yi_ref, g_ref, zp_ref, bias_ref, o_ref):
    y = _dot(atr_ref[...], yr_ref[...]) + _dot(ati_ref[...], yi_ref[...])
    zp = zp_ref[...].astype(F32)
    o_ref[...] = (g_ref[...].astype(F32) * (y + zp * bias_ref[...])).astype(o_ref.dtype)


def _dft_inv(dft_t, yr, yi, gate_src, gate_col, z_src, z_col, bias, batch, d):
    half = yr.shape[1] // batch
    tt = _tile(half, 1024)
    tn = _tile(d, 1024)
    nr, nd = half // tt, d // tn
    row = lambda cols: pl.BlockSpec((None, tt, tn), lambda j, p, i, b: (p, b * nr + i, cols * nd + j))
    return pl.pallas_call(
        _dft_inv_body,
        out_shape=jax.ShapeDtypeStruct((2, batch * half, d), BF16),
        grid=(nd, 2, nr, batch),
        in_specs=[
            pl.BlockSpec((None, tt, half), lambda j, p, i, b: (p, i, 0)),
            pl.BlockSpec((None, tt, half), lambda j, p, i, b: (p, i, 1)),
            pl.BlockSpec((None, half, tn), lambda j, p, i, b: (p, b, j)),
            pl.BlockSpec((None, half, tn), lambda j, p, i, b: (p, b, j)),
            row(gate_col), row(z_col),
            pl.BlockSpec((1, tn), lambda j, p, i, b: (0, j)),
        ],
        out_specs=row(0),
        compiler_params=_params("parallel", "parallel", "parallel", "parallel"),
        name="hyena_dft_inverse",
    )(dft_t, dft_t, yr, yi, gate_src, z_src, bias.reshape(1, d))


def _rope_tables(n_tok):
    rows = n_tok // GRID_W
    r = jnp.repeat(jnp.arange(rows), GRID_W).astype(F32)
    col = jnp.tile(jnp.arange(GRID_W), rows).astype(F32)
    half = HEAD_DIM // 2
    inv = ROPE_THETA ** (-jnp.arange(0, half, 2, dtype=F32) / half)
    ang = jnp.concatenate([r[:, None] * inv, col[:, None] * inv], axis=-1)
    cos = jnp.repeat(jnp.cos(ang), 2, axis=-1)
    sin = jnp.repeat(jnp.sin(ang), 2, axis=-1)
    sign = jnp.where(jnp.arange(HEAD_DIM) % 2 == 0, -1.0, 1.0).astype(F32)
    return cos, sin * sign


def _dft_tables(n):
    big = 2 * n
    half = n // 2
    k = jnp.arange(half, dtype=jnp.int32)[:, None]
    m = jnp.arange(half, dtype=jnp.int32)[None, :]
    alt = jnp.where(m % 2 == 0, 1.0, -1.0).astype(F32)
    mats = []
    for par in range(2):
        ang = ((k * (2 * m + par)) % big).astype(F32) * (2.0 * math.pi / big)
        s = jnp.where(k == 0, alt if par == 0 else -alt, -jnp.sin(ang))
        mats.append(jnp.concatenate([jnp.cos(ang), s], axis=0))
    a = jnp.stack(mats).astype(BF16)
    return a, a.transpose(0, 2, 1)


def _filter_features(n, width):
    t = jnp.linspace(0.0, 1.0, n, dtype=F32)[:, None]
    w = (2.0 * math.pi / n) * jnp.arange(n, dtype=F32)[:, None]
    bands = jnp.linspace(1e-4, HYENA_BANDS - 1, HYENA_BANDS, dtype=F32)
    ang = w * bands[None, :]
    feats = jnp.concatenate([t, jnp.cos(ang), -jnp.sin(ang)], axis=-1)
    feats = jnp.concatenate([feats[0::2], feats[1::2]], axis=0)
    return jnp.pad(feats, ((0, 0), (0, width - feats.shape[1])))


def _pad_to(a, shape):
    return jnp.pad(a, [(0, s - d) for d, s in zip(a.shape, shape)])


def _attn_layer(x, xc, mod_l, mod_c, g, w_in, w_out, g_q, g_k, sink, rope, batch, seq, n_ctx, ctx_out):
    d = x.shape[1]
    ones = jnp.ones((HEAD_DIM,), F32)
    gamma = jnp.concatenate([jnp.tile(g_q[0], N_HEADS), jnp.tile(g_q[1], N_HEADS),
                             jnp.tile(g_k[0], N_KV), jnp.tile(ones, N_KV),
                             jnp.tile(g_k[1], N_KV), jnp.tile(ones, N_KV)])[None, :]
    qs = jnp.concatenate([jnp.full((2 * Q_COLS,), HEAD_DIM ** -0.5, F32), jnp.ones((4 * KV_COLS,), F32)])[None, :]
    zk, ok = jnp.zeros((KV_COLS,), F32), jnp.ones((KV_COLS,), F32)
    isv = jnp.concatenate([jnp.zeros((2 * Q_COLS,), F32), zk, ok, zk, ok])[None, :]
    tables = (gamma, qs, isv)
    tm = _row_tile(batch * n_ctx, batch * n_ctx)
    ident = (jnp.ones((tm, HEAD_DIM), F32), jnp.zeros((tm, HEAD_DIM), F32))
    w_in_b = _col_tiled(w_in, _tile(w_in.shape[1], 1024))
    qkv = _qkv_proj(x, mod_l, g, w_in_b, seq, tables, rope, seq)
    qkv_c = _qkv_proj(xc, mod_c, g, w_in_b, batch * n_ctx, tables, ident, None)
    qa_col, qb_col = 0, N_KV
    ka_col = 2 * Q_COLS // HEAD_DIM
    va_col, kb_col, vb_col = ka_col + N_KV, ka_col + 2 * N_KV, ka_col + 3 * N_KV
    o_a = _attention(sink, qkv, qkv_c, qkv, batch=batch, q_rows=seq, q_col=qa_col, k_col=ka_col, v_col=va_col,
                     has_mask=False, has_sink=False)
    o_b = _attention(sink, qkv, qkv_c, qkv, batch=batch, q_rows=seq, q_col=qb_col, k_col=kb_col, v_col=vb_col,
                     has_mask=True, has_sink=True)
    w_out_b = w_out.astype(BF16)
    zero_b = jnp.zeros((d,), F32)
    x = _outproj((o_a, o_b), w_out_b, zero_b, x, mod_l, seq)
    if ctx_out:
        co_a = _attention(sink, qkv_c, qkv_c, None, batch=batch, q_rows=n_ctx, q_col=qa_col, k_col=ka_col,
                          v_col=va_col, has_mask=False, has_sink=False)
        co_b = _attention(sink, qkv_c, qkv_c, None, batch=batch, q_rows=n_ctx, q_col=qb_col, k_col=kb_col,
                          v_col=vb_col, has_mask=False, has_sink=True)
        xc = _outproj((co_a, co_b), w_out_b, zero_b, xc, mod_c, batch * n_ctx)
    return x, xc


def _hyena_layer(x, mod, g, rows_per_mod, batch, w_in_b, b_in, w_conv, b_conv, filt, hy_bias, w_out_b, b_out):
    d = x.shape[1]
    n = x.shape[0] // batch
    w1, b1, fr1, w2, b2, fr2, w3 = filt
    hp = LANES
    feats = _filter_features(n, LANES)
    max_decay = math.log(HYENA_TARGET) / HYENA_FAST_PCT
    min_decay = math.log(HYENA_TARGET) / HYENA_SLOW_PCT
    absdelta = jnp.abs(jnp.linspace(min_decay, max_decay, d, dtype=F32))[None, :]
    taps = _hyena_filter_taps(
        feats, _pad_to(w1, (LANES, hp)), _pad_to(b1[None, :], (1, hp)), _pad_to(fr1[None, :], (1, hp)),
        _pad_to(w2, (hp, hp)), _pad_to(b2[None, :], (1, hp)), _pad_to(fr2[None, :], (1, hp)),
        _pad_to(w3, (hp, w3.shape[1])), absdelta, d)
    dft, dft_t = _dft_tables(n)
    spec = _hyena_spectrum(dft, taps)
    pc = _proj(x, mod, g, w_in_b, b_in, w_conv, b_conv, rows_per_mod, n)
    yr, yi = _dft_fwd(dft, pc, 0, spec, 0, batch, d)
    z = _dft_inv(dft_t, yr, yi, pc, 1, pc, 0, hy_bias[0], batch, d)
    yr, yi = _dft_fwd(dft, z, 0, spec, 1, batch, d)
    z = _dft_inv(dft_t, yr, yi, pc, 2, z, 0, hy_bias[1], batch, d)
    return _outproj_parity(z, w_out_b, b_out, x, mod, rows_per_mod, n)


def kernel(x, c, ctx, c_ctx, w_mod, b_mod, g_norm, w_ffn_in, w_ffn_out, w_attn_in, w_attn_out, g_q, g_k, sink, w_hy_in, b_hy_in, w_hy_conv, b_hy_conv, hf_w1, hf_b1, hf_freq1, hf_w2, hf_b2, hf_freq2, hf_w3, hy_bias, w_hy_out, b_hy_out):
    batch, seq, d = x.shape
    n_ctx = ctx.shape[1]
    depth = w_mod.shape[0]
    rows_c = batch * n_ctx
    rope = _rope_tables(seq)
    last_ctx = max(l for l in range(depth) if l % 2 == 0)

    r_pad = -(-(batch + 1) // SUBLANES) * SUBLANES
    c_all = _pad_to(jnp.concatenate([c, c_ctx[None, :]], axis=0), (r_pad, d))
    m_all = _mod_all(c_all, w_mod, b_mod).reshape(depth, r_pad, N_MOD, d)

    w_ffn_in_b = w_ffn_in.astype(BF16)
    w_ffn_out_b = w_ffn_out.astype(BF16)

    xl = x.reshape(batch * seq, d)
    xc = ctx.reshape(rows_c, d)
    for l in range(depth):
        i = l // 2
        ctx_live = l <= last_ctx
        ctx_full = l < last_ctx
        mods_l = [m_all[l, :batch, 3 * k:3 * k + 3] for k in range(3)]
        mods_c = [m_all[l, batch:batch + 1, 3 * k:3 * k + 3] for k in range(3)]
        xl = _ffn(xl, mods_l[0], g_norm[l, 0], w_ffn_in_b, w_ffn_out_b, l, 0, seq)
        if ctx_live:
            xc = _ffn(xc, mods_c[0], g_norm[l, 0], w_ffn_in_b, w_ffn_out_b, l, 0, rows_c)
        if l % 2 == 0:
            xl, xc = _attn_layer(xl, xc, mods_l[1], mods_c[1], g_norm[l, 1], w_attn_in[i], w_attn_out[i],
                                 g_q[i], g_k[i], sink[i], rope, batch, seq, n_ctx, ctx_full)
        else:
            w_in_b = _col_tiled(w_hy_in[i], _tile(w_hy_in.shape[2], 512))
            w_out_b = w_hy_out[i].astype(BF16)
            filt = (hf_w1[i], hf_b1[i], hf_freq1[i], hf_w2[i], hf_b2[i], hf_freq2[i], hf_w3[i])
            xl = _hyena_layer(xl, mods_l[1], g_norm[l, 1], seq, batch, w_in_b, b_hy_in[i], w_hy_conv[i],
                              b_hy_conv[i], filt, hy_bias[i], w_out_b, b_hy_out[i])
            if ctx_full:
                xc = _hyena_layer(xc, mods_c[1], g_norm[l, 1], rows_c, batch, w_in_b, b_hy_in[i], w_hy_conv[i],
                                  b_hy_conv[i], filt, hy_bias[i], w_out_b, b_hy_out[i])
        xl = _ffn(xl, mods_l[2], g_norm[l, 2], w_ffn_in_b, w_ffn_out_b, l, 1, seq)
        if ctx_full:
            xc = _ffn(xc, mods_c[2], g_norm[l, 2], w_ffn_in_b, w_ffn_out_b, l, 1, rows_c)
    return xl.reshape(batch, seq, d)
```

```python
import functools
import math

import jax
import jax.numpy as jnp
from jax import lax
from jax.experimental import pallas as pl
from jax.experimental.pallas import tpu as pltpu

HEAD_DIM = 128
N_HEADS = 8
N_KV = 2
GROUP = N_HEADS // N_KV
Q_COLS = N_HEADS * HEAD_DIM
KV_COLS = N_KV * HEAD_DIM
GRID_W = 64
WINDOW = 128
ROPE_THETA = 10000.0
N_MOD = 9
HYENA_ORDER = 2
HYENA_SHORT_K = 3
HYENA_BANDS = 16
HYENA_TARGET = 1e-2
HYENA_FAST_PCT = 0.3
HYENA_SLOW_PCT = 1.5
EPS = 1e-6
NEG_INF = -1e30

LANES = 128
SUBLANES = 8
VMEM_LIMIT_BYTES = 60 * 1024 * 1024

F32 = jnp.float32
BF16 = jnp.bfloat16
HIGHEST = lax.Precision.HIGHEST


def _params(*sem):
    return pltpu.CompilerParams(dimension_semantics=sem, vmem_limit_bytes=VMEM_LIMIT_BYTES)


def _tile(dim, pref):
    t = min(dim, pref)
    while dim % t:
        t //= 2
    return t


def _dot(a, b):
    return jnp.dot(a, b, preferred_element_type=F32)


def _dot_nt(a, b):
    return lax.dot_general(a, b, (((1,), (1,)), ((), ())), preferred_element_type=F32)


def _silu(v):
    return v * (1.0 / (1.0 + jnp.exp(-v)))


def _modulated(xf, g, shift, scale):
    ms = jnp.mean(xf * xf, axis=-1, keepdims=True)
    y = xf * lax.rsqrt(ms + EPS)
    return (y * g) * (1.0 + scale) + shift


def _mod_body(c_ref, w_ref, b_ref, o_ref):
    s = _silu(c_ref[...])
    o_ref[0] = jnp.dot(s, w_ref[0], preferred_element_type=F32, precision=HIGHEST) + b_ref[0]


def _mod_all(c_all, w_mod, b_mod):
    depth, d, nd = w_mod.shape
    r = c_all.shape[0]
    tn = _tile(nd, 2048)
    return pl.pallas_call(
        _mod_body,
        out_shape=jax.ShapeDtypeStruct((depth, r, nd), F32),
        grid=(depth, nd // tn),
        in_specs=[
            pl.BlockSpec((r, d), lambda l, j: (0, 0)),
            pl.BlockSpec((1, d, tn), lambda l, j: (l, 0, j)),
            pl.BlockSpec((1, 1, tn), lambda l, j: (l, 0, j)),
        ],
        out_specs=pl.BlockSpec((1, r, tn), lambda l, j: (l, 0, j)),
        compiler_params=_params("parallel", "parallel"),
        name="mod_vectors",
    )(c_all, w_mod, b_mod.reshape(depth, 1, nd))


def _ffn_body(x_ref, mod_ref, g_ref, wg_ref, wu_ref, wo_ref, o_ref, h_ref, acc_ref):
    j = pl.program_id(1)
    last = pl.num_programs(1) - 1

    def chunk(h):
        a = _dot(h, wg_ref[...])
        u = _dot(h, wu_ref[...])
        return _dot((_silu(a) * u).astype(BF16), wo_ref[...])

    @pl.when(j == 0)
    def _():
        h = _modulated(x_ref[...], g_ref[...], mod_ref[0, 0:1, :], mod_ref[0, 1:2, :]).astype(BF16)
        h_ref[...] = h
        acc_ref[...] = chunk(h)

    @pl.when(jnp.logical_and(j > 0, j < last))
    def _():
        acc_ref[...] += chunk(h_ref[...])

    @pl.when(jnp.logical_and(j > 0, j == last))
    def _():
        y = acc_ref[...] + chunk(h_ref[...])
        o_ref[...] = x_ref[...] + (0.5 * mod_ref[0, 2:3, :]) * y


def _ffn(x, mod, g, w_in, w_out, layer, half, rows_per_mod):
    rows, d = x.shape
    f = w_out.shape[2]
    tm = _tile(min(rows, rows_per_mod), 1024)
    tf = _tile(f, 512)
    nf = f // tf
    assert nf >= 2
    return pl.pallas_call(
        _ffn_body,
        out_shape=jax.ShapeDtypeStruct((rows, d), F32),
        grid=(rows // tm, nf),
        in_specs=[
            pl.BlockSpec((tm, d), lambda i, j: (i, 0)),
            pl.BlockSpec((1, 3, d), lambda i, j: (i * tm // rows_per_mod, 0, 0)),
            pl.BlockSpec((1, d), lambda i, j: (0, 0)),
            pl.BlockSpec((None, None, d, tf), lambda i, j: (layer, half, 0, j)),
            pl.BlockSpec((None, None, d, tf), lambda i, j: (layer, half, 0, nf + j)),
            pl.BlockSpec((None, None, tf, d), lambda i, j: (layer, half, j, 0)),
        ],
        out_specs=pl.BlockSpec((tm, d), lambda i, j: (i, 0), pipeline_mode=pl.Buffered(1)),
        scratch_shapes=[pltpu.VMEM((tm, d), BF16), pltpu.VMEM((tm, d), F32)],
        compiler_params=_params("parallel", "arbitrary"),
        name="ffn_half_step",
    )(x, mod, g.reshape(1, d), w_in, w_in, w_out)


def _proj_body(x_ref, mod_ref, g_ref, perm_ref, w_ref, b_ref, cw_ref, cb_ref, o_ref, h_ref, *, half_seq):
    tm = x_ref.shape[0]
    half = tm // 2
    grp = perm_ref.shape[0]

    def project_conv():
        p = _dot(h_ref[...], w_ref[...]) + b_ref[...]
        pe, po = p[:half], p[half:]
        pos = lax.broadcasted_iota(jnp.int32, (half, 1), 0) % half_seq
        po_prev = jnp.where(pos == 0, 0.0, pltpu.roll(po, 1, 0))
        pe_next = jnp.where(pos == half_seq - 1, 0.0, pltpu.roll(pe, half - 1, 0))
        w0, w1, w2 = cw_ref[0:1, :], cw_ref[1:2, :], cw_ref[2:3, :]
        o_ref[0] = (((cb_ref[...] + po_prev * w0) + pe * w1) + po * w2).astype(o_ref.dtype)
        o_ref[1] = (((cb_ref[...] + pe * w0) + po * w1) + pe_next * w2).astype(o_ref.dtype)

    @pl.when(pl.program_id(1) == 0)
    def _():
        for r in range(tm // grp):
            h = _modulated(x_ref[r * grp:(r + 1) * grp, :], g_ref[...], mod_ref[0, 0:1, :], mod_ref[0, 1:2, :])
            hp = _dot(perm_ref[...], h.astype(BF16)).astype(BF16)
            h_ref[r * (grp // 2):(r + 1) * (grp // 2), :] = hp[:grp // 2]
            h_ref[half + r * (grp // 2):half + (r + 1) * (grp // 2), :] = hp[grp // 2:]
        project_conv()

    @pl.when(pl.program_id(1) > 0)
    def _():
        project_conv()


def _qkv_body(x_ref, mod_ref, g_ref, w_ref, gamma_ref, qs_ref, isv_ref, cos_ref, sin_ref, o_ref, h_ref, p_ref):
    j = pl.program_id(1)
    last = pl.num_programs(1) - 1

    def finish():
        cos = cos_ref[...]
        sin = sin_ref[...]
        even = (lax.broadcasted_iota(jnp.int32, (1, HEAD_DIM), 1) % 2) == 0
        for hh in range(p_ref.shape[1] // HEAD_DIM):
            sl = slice(hh * HEAD_DIM, (hh + 1) * HEAD_DIM)
            ph = p_ref[:, sl]
            ms = jnp.mean(ph * ph, axis=-1, keepdims=True)
            nh = (ph * lax.rsqrt(ms + EPS)) * gamma_ref[:, sl]
            partner = jnp.where(even, pltpu.roll(nh, HEAD_DIM - 1, 1), pltpu.roll(nh, 1, 1))
            r = (nh * cos + partner * sin) * qs_ref[:, sl]
            o_ref[:, sl] = jnp.where(isv_ref[:, sl] > 0.0, ph, r).astype(o_ref.dtype)

    @pl.when(j == 0)
    def _():
        h = _modulated(x_ref[...], g_ref[...], mod_ref[0, 0:1, :], mod_ref[0, 1:2, :]).astype(BF16)
        h_ref[...] = h
        p_ref[...] = _dot(h, w_ref[...])

    @pl.when(jnp.logical_and(j > 0, j < last))
    def _():
        finish()
        p_ref[...] = _dot(h_ref[...], w_ref[...])

    @pl.when(j == last)
    def _():
        finish()


def _read_in_first_step(n_tiles):
    return lambda i, j: (jnp.minimum(i + jnp.minimum(j, 1), n_tiles - 1), 0)


def _row_tile(rows, rows_per_mod):
    return _tile(min(rows, rows_per_mod), 1024)


def _parity_perm(tm):
    r = jnp.arange(tm, dtype=jnp.int32)[:, None]
    c = jnp.arange(tm, dtype=jnp.int32)[None, :]
    src = jnp.where(r < tm // 2, 2 * r, 2 * (r - tm // 2) + 1)
    return (c == src).astype(BF16)


def _col_tiled(w, tn):
    k, n = w.shape
    return w.reshape(k, n // tn, tn).transpose(1, 0, 2).astype(BF16)


PERM_GROUP = 256


def _proj(x, mod, g, w_t, b, conv_w, conv_b, rows_per_mod, seq):
    rows, d = x.shape
    nj, _, tn = w_t.shape
    n = nj * tn
    tm = _tile(min(rows, rows_per_mod), max(seq, 1024))
    grp = min(PERM_GROUP, tm)
    assert tm % seq == 0 and tm % grp == 0 and grp % (4 * SUBLANES) == 0
    col = pl.BlockSpec((1, tn), lambda i, j: (0, j))
    return pl.pallas_call(
        functools.partial(_proj_body, half_seq=seq // 2),
        out_shape=jax.ShapeDtypeStruct((2, rows // 2, n), BF16),
        grid=(rows // tm, n // tn),
        in_specs=[
            pl.BlockSpec((tm, d), _read_in_first_step(rows // tm)),
            pl.BlockSpec((1, 3, d), lambda i, j: (i * tm // rows_per_mod, 0, 0)),
            pl.BlockSpec((1, d), lambda i, j: (0, 0)),
            pl.BlockSpec((grp, grp), lambda i, j: (0, 0)),
            pl.BlockSpec((None, d, tn), lambda i, j: (j, 0, 0)),
            col,
            pl.BlockSpec((HYENA_SHORT_K, tn), lambda i, j: (0, j)),
            col,
        ],
        out_specs=pl.BlockSpec((2, tm // 2, tn), lambda i, j: (0, i, j)),
        scratch_shapes=[pltpu.VMEM((tm, d), BF16)],
        compiler_params=_params("parallel", "arbitrary"),
        name="hyena_project_conv",
    )(x, mod, g.reshape(1, d), _parity_perm(grp), w_t, b.reshape(1, n), conv_w, conv_b.reshape(1, n))


def _qkv_proj(x, mod, g, w_t, rows_per_mod, tables, rope, seq):
    rows, d = x.shape
    nj, _, tn = w_t.shape
    n = nj * tn
    tm = _row_tile(rows, rows_per_mod)
    gamma, qs, isv = tables
    cos, sin = rope

    def prev(j):
        return jnp.maximum(j - 1, 0)

    col = pl.BlockSpec((1, tn), lambda i, j: (0, prev(j)))
    if seq is None:
        rope_spec = pl.BlockSpec((tm, HEAD_DIM), lambda i, j: (0, 0))
    else:
        rope_spec = pl.BlockSpec((tm, HEAD_DIM), lambda i, j: (i % (seq // tm), 0))
    return pl.pallas_call(
        _qkv_body,
        out_shape=jax.ShapeDtypeStruct((rows, n), BF16),
        grid=(rows // tm, nj + 1),
        in_specs=[
            pl.BlockSpec((tm, d), _read_in_first_step(rows // tm)),
            pl.BlockSpec((1, 3, d), lambda i, j: (i * tm // rows_per_mod, 0, 0)),
            pl.BlockSpec((1, d), lambda i, j: (0, 0)),
            pl.BlockSpec((None, d, tn), lambda i, j: (jnp.minimum(j, nj - 1), 0, 0)),
            col, col, col, rope_spec, rope_spec,
        ],
        out_specs=pl.BlockSpec((tm, tn), lambda i, j: (i, prev(j))),
        scratch_shapes=[pltpu.VMEM((tm, d), BF16), pltpu.VMEM((tm, tn), F32)],
        compiler_params=_params("arbitrary", "arbitrary"),
        name="qkv_project",
    )(x, mod, g.reshape(1, d), w_t, gamma, qs, isv, cos, sin)


def _attn_body(*refs, has_lat, has_mask, has_sink, tq):
    sink_ref, q_ref, kc_ref, vc_ref = refs[:4]
    if has_lat:
        k_ref, v_ref, o_ref = refs[4:7]
    else:
        o_ref = refs[4]
    g = pl.program_id(1)
    qi = pl.program_id(2)
    if has_lat:
        l_rows = k_ref.shape[0]
        if has_mask:
            kw = min(l_rows, tq + 2 * WINDOW)
            k0 = pl.multiple_of(jnp.clip(qi * tq - WINDOW, 0, l_rows - kw), WINDOW)
            keys = pl.ds(k0, kw)
        else:
            k0, keys = 0, slice(None)
    if has_mask:
        qpos = qi * tq + lax.broadcasted_iota(jnp.int32, (tq, 1), 0)
        kpos = k0 + lax.broadcasted_iota(jnp.int32, (1, kw), 1)
        band = jnp.abs(kpos - qpos) <= WINDOW
    heads = [slice(hh * HEAD_DIM, (hh + 1) * HEAD_DIM) for hh in range(GROUP)]
    s1 = [_dot_nt(q_ref[:, sl], kc_ref[...]) for sl in heads]
    m = [jnp.max(s, axis=-1, keepdims=True) for s in s1]
    if has_lat:
        s2 = [_dot_nt(q_ref[:, sl], k_ref[keys, :]) for sl in heads]
        if has_mask:
            s2 = [jnp.where(band, s, NEG_INF) for s in s2]
        m = [jnp.maximum(mh, jnp.max(s, axis=-1, keepdims=True)) for mh, s in zip(m, s2)]
    if has_sink:
        sk = [sink_ref[g * GROUP + hh] for hh in range(GROUP)]
        m = [jnp.maximum(mh, skh) for mh, skh in zip(m, sk)]
    for hh, sl in enumerate(heads):
        p1 = jnp.exp(s1[hh] - m[hh])
        l = jnp.sum(p1, axis=-1, keepdims=True)
        o = _dot(p1.astype(BF16), vc_ref[...])
        if has_lat:
            p2 = jnp.exp(s2[hh] - m[hh])
            l = l + jnp.sum(p2, axis=-1, keepdims=True)
            o = o + _dot(p2.astype(BF16), v_ref[keys, :])
        if has_sink:
            l = l + jnp.exp(sk[hh] - m[hh])
        o_ref[:, sl] = (o / l).astype(o_ref.dtype)


def _attention(sink, q_src, ctx_src, lat_src, *, batch, q_rows, q_col, k_col, v_col, has_mask, has_sink):
    tq = _tile(q_rows, 256 if has_mask else 512)
    assert not has_mask or tq % WINDOW == 0
    c_rows = ctx_src.shape[0] // batch
    gw = GROUP * HEAD_DIM
    in_specs = [
        pl.BlockSpec(memory_space=pltpu.SMEM),
        pl.BlockSpec((tq, gw), lambda b, g, i: (b * (q_rows // tq) + i, q_col + g)),
        pl.BlockSpec((c_rows, HEAD_DIM), lambda b, g, i: (b, k_col + g)),
        pl.BlockSpec((c_rows, HEAD_DIM), lambda b, g, i: (b, v_col + g)),
    ]
    args = [sink, q_src, ctx_src, ctx_src]
    has_lat = lat_src is not None
    if has_lat:
        l_rows = lat_src.shape[0] // batch
        in_specs += [
            pl.BlockSpec((l_rows, HEAD_DIM), lambda b, g, i: (b, k_col + g)),
            pl.BlockSpec((l_rows, HEAD_DIM), lambda b, g, i: (b, v_col + g)),
        ]
        args += [lat_src, lat_src]
    return pl.pallas_call(
        functools.partial(_attn_body, has_lat=has_lat, has_mask=has_mask, has_sink=has_sink, tq=tq),
        out_shape=jax.ShapeDtypeStruct((batch * q_rows, Q_COLS), BF16),
        grid=(batch, N_KV, q_rows // tq),
        in_specs=in_specs,
        out_specs=pl.BlockSpec((tq, gw), lambda b, g, i: (b * (q_rows // tq) + i, g)),
        compiler_params=_params("parallel", "parallel", "parallel"),
        name="gqa_attention",
    )(*args)


def _outproj_body(*refs, n_in):
    a_refs, w_refs = refs[:n_in], refs[n_in:2 * n_in]
    b_ref, x_ref, mod_ref, o_ref = refs[2 * n_in:]
    y = b_ref[...] + _dot(a_refs[0][...], w_refs[0][...])
    for a_ref, w_ref in zip(a_refs[1:], w_refs[1:]):
        y = y + _dot(a_ref[...], w_ref[...])
    o_ref[...] = x_ref[...] + mod_ref[0, 2:3, :] * y


def _outproj(acts, weight, bias, x, mod, rows_per_mod):
    rows, d = x.shape
    tm = _tile(min(rows, rows_per_mod), 512)
    n_in = len(acts)
    k = acts[0].shape[1]
    assert all(a.shape[1] == k for a in acts) and weight.shape[0] == n_in * k
    in_specs = [pl.BlockSpec((tm, k), lambda i: (i, 0)) for _ in acts]
    in_specs += [pl.BlockSpec((k, d), lambda i, n=n: (n, 0)) for n in range(n_in)]
    in_specs += [
        pl.BlockSpec((1, d), lambda i: (0, 0)),
        pl.BlockSpec((tm, d), lambda i: (i, 0)),
        pl.BlockSpec((1, 3, d), lambda i: (i * tm // rows_per_mod, 0, 0)),
    ]
    return pl.pallas_call(
        functools.partial(_outproj_body, n_in=n_in),
        out_shape=jax.ShapeDtypeStruct((rows, d), F32),
        grid=(rows // tm,),
        in_specs=in_specs,
        out_specs=pl.BlockSpec((tm, d), lambda i: (i, 0)),
        compiler_params=_params("parallel"),
        name="outproj_residual",
    )(*acts, *([weight] * n_in), bias.reshape(1, d), x, mod)


def _outproj_parity_body(z_ref, unperm_ref, w_ref, b_ref, x_ref, mod_ref, o_ref):
    grp = unperm_ref.shape[0]
    hg = grp // 2
    parts = []
    for r in range(x_ref.shape[0] // grp):
        zg = jnp.concatenate([z_ref[0, r * hg:(r + 1) * hg, :], z_ref[1, r * hg:(r + 1) * hg, :]], axis=0)
        parts.append(_dot(unperm_ref[...], zg).astype(BF16))
    y = b_ref[...] + _dot(jnp.concatenate(parts, axis=0), w_ref[...])
    o_ref[...] = x_ref[...] + mod_ref[0, 2:3, :] * y


def _outproj_parity(z, weight, bias, x, mod, rows_per_mod, seq):
    rows, d = x.shape
    k = z.shape[2]
    tm = _tile(min(rows, rows_per_mod), 512)
    grp = min(PERM_GROUP, tm)
    assert tm % grp == 0 and (tm % seq == 0 or seq % tm == 0)
    return pl.pallas_call(
        _outproj_parity_body,
        out_shape=jax.ShapeDtypeStruct((rows, d), F32),
        grid=(rows // tm,),
        in_specs=[
            pl.BlockSpec((2, tm // 2, k), lambda i: (0, i, 0)),
            pl.BlockSpec((grp, grp), lambda i: (0, 0)),
            pl.BlockSpec((k, d), lambda i: (0, 0)),
            pl.BlockSpec((1, d), lambda i: (0, 0)),
            pl.BlockSpec((tm, d), lambda i: (i, 0)),
            pl.BlockSpec((1, 3, d), lambda i: (i * tm // rows_per_mod, 0, 0)),
        ],
        out_specs=pl.BlockSpec((tm, d), lambda i: (i, 0)),
        compiler_params=_params("parallel"),
        name="outproj_residual_parity",
    )(z, _parity_perm(grp).T, weight, bias.reshape(1, d), x, mod)


def _filter_body(feats_ref, w1_ref, b1_ref, fr1_ref, w2_ref, b2_ref, fr2_ref, w3f_ref, w3b_ref, dl_ref, o_ref,
                 h2_ref):
    feats = feats_ref[...]

    @pl.when(jnp.logical_and(pl.program_id(0) == 0, pl.program_id(1) == 0))
    def _():
        h1 = jnp.sin(fr1_ref[...] * (jnp.dot(feats, w1_ref[...], preferred_element_type=F32, precision=HIGHEST)
                                     + b1_ref[...]))
        h2_ref[...] = jnp.sin(fr2_ref[...] * (jnp.dot(h1, w2_ref[...], preferred_element_type=F32,
                                                      precision=HIGHEST) + b2_ref[...]))

    h2 = h2_ref[...]
    decay = jnp.exp(-feats[:, 0:1] * dl_ref[...])
    fwd = jnp.dot(h2, w3f_ref[...], preferred_element_type=F32, precision=HIGHEST) * decay
    bwd = jnp.dot(h2, w3b_ref[...], preferred_element_type=F32, precision=HIGHEST) * decay
    row = lax.broadcasted_iota(jnp.int32, (fwd.shape[0], 1), 0)
    bwd = jnp.where(row == 0, 0.0, bwd)
    nrm = lax.rsqrt(jnp.sum(fwd * fwd + bwd * bwd, axis=0, keepdims=True) + EPS)
    o_ref[0] = ((fwd + bwd) * nrm).astype(o_ref.dtype)
    o_ref[1] = ((fwd - bwd) * nrm).astype(o_ref.dtype)


def _hyena_filter_taps(feats, w1, b1, fr1, w2, b2, fr2, w3, absdelta, d):
    n, fp = feats.shape
    hp = w1.shape[1]
    tn = _tile(d, 512)
    nd = d // tn
    vec = pl.BlockSpec((1, hp), lambda o, j: (0, 0))
    return pl.pallas_call(
        _filter_body,
        out_shape=jax.ShapeDtypeStruct((2, n, HYENA_ORDER * d), BF16),
        grid=(HYENA_ORDER, nd),
        in_specs=[
            pl.BlockSpec((n, fp), lambda o, j: (0, 0)),
            pl.BlockSpec((fp, hp), lambda o, j: (0, 0)), vec, vec,
            pl.BlockSpec((hp, hp), lambda o, j: (0, 0)), vec, vec,
            pl.BlockSpec((hp, tn), lambda o, j: (0, (2 * o) * nd + j)),
            pl.BlockSpec((hp, tn), lambda o, j: (0, (2 * o + 1) * nd + j)),
            pl.BlockSpec((1, tn), lambda o, j: (0, j)),
        ],
        out_specs=pl.BlockSpec((2, n, tn), lambda o, j: (0, 0, o * nd + j)),
        scratch_shapes=[pltpu.VMEM((n, hp), F32)],
        compiler_params=_params("arbitrary", "arbitrary"),
        name="hyena_filter_taps",
    )(feats, w1, b1, fr1, w2, b2, fr2, w3, w3, absdelta)


def _spectrum_body(ce_ref, se_ref, co_ref, so_ref, se_, so_, de_, do_, o_ref, *, inv_n):
    i = pl.program_id(0)
    tk = ce_ref.shape[0]
    pr = _dot(ce_ref[...], se_[...])
    qr = _dot(co_ref[...], so_[...])
    pi = _dot(se_ref[...], de_[...])
    qi = _dot(so_ref[...], do_[...])
    row0 = (i * tk + lax.broadcasted_iota(jnp.int32, (tk, 1), 0)) == 0
    w_re = jnp.where(row0, inv_n, 2.0 * inv_n)
    o_ref[0] = (pr + qr) * w_re
    o_ref[1] = (pr - qr) * w_re
    o_ref[2] = (pi + qi) * (2.0 * inv_n)
    o_ref[3] = (qi - pi) * (2.0 * inv_n)

    @pl.when(i == 0)
    def _():
        rows = 2 * SUBLANES
        mid_r = _dot(se_ref[0:rows, :], se_[...])
        o_ref[2, 0:1, :] = mid_r[0:1, :] * (2.0 * inv_n)
        o_ref[3, 0:1, :] = qi[0:1, :] * (2.0 * inv_n)


def _hyena_spectrum(dft, taps):
    n = taps.shape[1]
    cols = taps.shape[2]
    half = n // 2
    tk = _tile(half, 1024)
    tn = _tile(cols, 512)
    nr = half // tk
    mat = lambda par, part: pl.BlockSpec((None, tk, half), lambda i, j: (par, part * nr + i, 0))
    tap = lambda which, par: pl.BlockSpec((None, half, tn), lambda i, j: (which, par, j))
    return pl.pallas_call(
        functools.partial(_spectrum_body, inv_n=1.0 / (2 * n)),
        out_shape=jax.ShapeDtypeStruct((4, half, cols), F32),
        grid=(nr, cols // tn),
        in_specs=[mat(0, 0), mat(0, 1), mat(1, 0), mat(1, 1), tap(0, 0), tap(0, 1), tap(1, 0), tap(1, 1)],
        out_specs=pl.BlockSpec((4, tk, tn), lambda i, j: (0, i, j)),
        compiler_params=_params("parallel", "parallel"),
        name="hyena_filter_spectrum",
    )(dft, dft, dft, dft, taps, taps, taps, taps)


def _dft_fwd_body(ce_ref, se_ref, co_ref, so_ref, ze_ref, zo_ref, k_ref, yr_ref, yi_ref):
    i = pl.program_id(1)
    tk = ce_ref.shape[0]
    row0 = (i * tk + lax.broadcasted_iota(jnp.int32, (tk, 1), 0)) == 0
    tn = ze_ref.shape[1]
    sub = min(tn, 2 * LANES)
    for c in range(tn // sub):
        cols = slice(c * sub, (c + 1) * sub)
        ze, zo = ze_ref[:, cols], zo_ref[:, cols]
        pr = _dot(ce_ref[...], ze)
        pi = _dot(se_ref[...], ze)
        qr = _dot(co_ref[...], zo)
        qi = _dot(so_ref[...], zo)
        kr_lo, kr_hi, ki_lo, ki_hi = k_ref[0, :, cols], k_ref[1, :, cols], k_ref[2, :, cols], k_ref[3, :, cols]
        zr_lo, zr_hi, zi_lo, zi_hi = pr + qr, pr - qr, pi + qi, qi - pi
        yr_lo = zr_lo * kr_lo - jnp.where(row0, 0.0, zi_lo * ki_lo)
        yr_hi = zr_hi * kr_hi - jnp.where(row0, 0.0, zi_hi * ki_hi)
        yi_lo = zr_lo * ki_lo + zi_lo * kr_lo
        yi_hi = zr_hi * ki_hi + zi_hi * kr_hi
        mid_r = pi * ki_lo - qi * ki_hi
        mid_i = pi * ki_hi + qi * ki_lo
        yr_ref[0, :, cols] = (yr_lo + yr_hi).astype(yr_ref.dtype)
        yr_ref[1, :, cols] = (yr_lo - yr_hi).astype(yr_ref.dtype)
        yi_ref[0, :, cols] = jnp.where(row0, mid_r, yi_lo - yi_hi).astype(yi_ref.dtype)
        yi_ref[1, :, cols] = jnp.where(row0, mid_i, yi_lo + yi_hi).astype(yi_ref.dtype)


def _dft_fwd(dft, z_src, z_col, spec, order, batch, d):
    half = z_src.shape[1] // batch
    tk = _tile(half, 1024)
    tn = _tile(d, 512)
    nr, nd = half // tk, d // tn
    out = jax.ShapeDtypeStruct((2, batch * half, d), BF16)
    once = pl.Buffered(1)
    mat = lambda par, part: pl.BlockSpec((None, tk, half), lambda j, i, b: (par, part * nr + i, 0),
                                         pipeline_mode=once)
    zin = lambda par: pl.BlockSpec((None, half, tn), lambda j, i, b: (par, b, z_col * nd + j))
    res = pl.BlockSpec((2, tk, tn), lambda j, i, b: (0, b * nr + i, j))
    return pl.pallas_call(
        _dft_fwd_body,
        out_shape=(out, out),
        grid=(nd, nr, batch),
        in_specs=[
            mat(0, 0), mat(0, 1), mat(1, 0), mat(1, 1), zin(0), zin(1),
            pl.BlockSpec((4, tk, tn), lambda j, i, b: (0, i, order * nd + j), pipeline_mode=once),
        ],
        out_specs=(res, res),
        compiler_params=_params("parallel", "parallel", "parallel"),
        name="hyena_dft_forward",
    )(dft, dft, dft, dft, z_src, z_src, spec)


def _dft_inv_body(atr_ref, ati_ref, yr_ref, yi_ref, g_ref, zp_ref, bias_ref, o_ref):
    y = _dot(atr_ref[...], yr_ref[...]) + _dot(ati_ref[...], yi_ref[...])
    zp = zp_ref[...].astype(F32)
    o_ref[...] = (g_ref[...].astype(F32) * (y + zp * bias_ref[...])).astype(o_ref.dtype)


def _dft_inv(dft_t, yr, yi, gate_src, gate_col, z_src, z_col, bias, batch, d):
    half = yr.shape[1] // batch
    tt = _tile(half, 1024)
    tn = _tile(d, 1024)
    nr, nd = half // tt, d // tn
    row = lambda cols: pl.BlockSpec((None, tt, tn), lambda j, p, i, b: (p, b * nr + i, cols * nd + j))
    return pl.pallas_call(
        _dft_inv_body,
        out_shape=jax.ShapeDtypeStruct((2, batch * half, d), BF16),
        grid=(nd, 2, nr, batch),
        in_specs=[
            pl.BlockSpec((None, tt, half), lambda j, p, i, b: (p, i, 0)),
            pl.BlockSpec((None, tt, half), lambda j, p, i, b: (p, i, 1)),
            pl.BlockSpec((None, half, tn), lambda j, p, i, b: (p, b, j)),
            pl.BlockSpec((None, half, tn), lambda j, p, i, b: (p, b, j)),
            row(gate_col), row(z_col),
            pl.BlockSpec((1, tn), lambda j, p, i, b: (0, j)),
        ],
        out_specs=row(0),
        compiler_params=_params("parallel", "parallel", "parallel", "parallel"),
        name="hyena_dft_inverse",
    )(dft_t, dft_t, yr, yi, gate_src, z_src, bias.reshape(1, d))


def _rope_tables(n_tok):
    rows = n_tok // GRID_W
    r = jnp.repeat(jnp.arange(rows), GRID_W).astype(F32)
    col = jnp.tile(jnp.arange(GRID_W), rows).astype(F32)
    half = HEAD_DIM // 2
    inv = ROPE_THETA ** (-jnp.arange(0, half, 2, dtype=F32) / half)
    ang = jnp.concatenate([r[:, None] * inv, col[:, None] * inv], axis=-1)
    cos = jnp.repeat(jnp.cos(ang), 2, axis=-1)
    sin = jnp.repeat(jnp.sin(ang), 2, axis=-1)
    sign = jnp.where(jnp.arange(HEAD_DIM) % 2 == 0, -1.0, 1.0).astype(F32)
    return cos, sin * sign


def _dft_tables(n):
    big = 2 * n
    half = n // 2
    k = jnp.arange(half, dtype=jnp.int32)[:, None]
    m = jnp.arange(half, dtype=jnp.int32)[None, :]
    alt = jnp.where(m % 2 == 0, 1.0, -1.0).astype(F32)
    mats = []
    for par in range(2):
        ang = ((k * (2 * m + par)) % big).astype(F32) * (2.0 * math.pi / big)
        s = jnp.where(k == 0, alt if par == 0 else -alt, -jnp.sin(ang))
        mats.append(jnp.concatenate([jnp.cos(ang), s], axis=0))
    a = jnp.stack(mats).astype(BF16)
    return a, a.transpose(0, 2, 1)


def _filter_features(n, width):
    t = jnp.linspace(0.0, 1.0, n, dtype=F32)[:, None]
    w = (2.0 * math.pi / n) * jnp.arange(n, dtype=F32)[:, None]
    bands = jnp.linspace(1e-4, HYENA_BANDS - 1, HYENA_BANDS, dtype=F32)
    ang = w * bands[None, :]
    feats = jnp.concatenate([t, jnp.cos(ang), -jnp.sin(ang)], axis=-1)
    feats = jnp.concatenate([feats[0::2], feats[1::2]], axis=0)
    return jnp.pad(feats, ((0, 0), (0, width - feats.shape[1])))


def _pad_to(a, shape):
    return jnp.pad(a, [(0, s - d) for d, s in zip(a.shape, shape)])


def _attn_layer(x, xc, mod_l, mod_c, g, w_in, w_out, g_q, g_k, sink, rope, batch, seq, n_ctx, ctx_out):
    d = x.shape[1]
    ones = jnp.ones((HEAD_DIM,), F32)
    gamma = jnp.concatenate([jnp.tile(g_q[0], N_HEADS), jnp.tile(g_q[1], N_HEADS),
                             jnp.tile(g_k[0], N_KV), jnp.tile(ones, N_KV),
                             jnp.tile(g_k[1], N_KV), jnp.tile(ones, N_KV)])[None, :]
    qs = jnp.concatenate([jnp.full((2 * Q_COLS,), HEAD_DIM ** -0.5, F32), jnp.ones((4 * KV_COLS,), F32)])[None, :]
    zk, ok = jnp.zeros((KV_COLS,), F32), jnp.ones((KV_COLS,), F32)
    isv = jnp.concatenate([jnp.zeros((2 * Q_COLS,), F32), zk, ok, zk, ok])[None, :]
    tables = (gamma, qs, isv)
    tm = _row_tile(batch * n_ctx, batch * n_ctx)
    ident = (jnp.ones((tm, HEAD_DIM), F32), jnp.zeros((tm, HEAD_DIM), F32))
    w_in_b = _col_tiled(w_in, _tile(w_in.shape[1], 1024))
    qkv = _qkv_proj(x, mod_l, g, w_in_b, seq, tables, rope, seq)
    qkv_c = _qkv_proj(xc, mod_c, g, w_in_b, batch * n_ctx, tables, ident, None)
    qa_col, qb_col = 0, N_KV
    ka_col = 2 * Q_COLS // HEAD_DIM
    va_col, kb_col, vb_col = ka_col + N_KV, ka_col + 2 * N_KV, ka_col + 3 * N_KV
    o_a = _attention(sink, qkv, qkv_c, qkv, batch=batch, q_rows=seq, q_col=qa_col, k_col=ka_col, v_col=va_col,
                     has_mask=False, has_sink=False)
    o_b = _attention(sink, qkv, qkv_c, qkv, batch=batch, q_rows=seq, q_col=qb_col, k_col=kb_col, v_col=vb_col,
                     has_mask=True, has_sink=True)
    w_out_b = w_out.astype(BF16)
    zero_b = jnp.zeros((d,), F32)
    x = _outproj((o_a, o_b), w_out_b, zero_b, x, mod_l, seq)
    if ctx_out:
        co_a = _attention(sink, qkv_c, qkv_c, None, batch=batch, q_rows=n_ctx, q_col=qa_col, k_col=ka_col,
                          v_col=va_col, has_mask=False, has_sink=False)
        co_b = _attention(sink, qkv_c, qkv_c, None, batch=batch, q_rows=n_ctx, q_col=qb_col, k_col=kb_col,
                          v_col=vb_col, has_mask=False, has_sink=True)
        xc = _outproj((co_a, co_b), w_out_b, zero_b, xc, mod_c, batch * n_ctx)
    return x, xc


def _hyena_layer(x, mod, g, rows_per_mod, batch, w_in_b, b_in, w_conv, b_conv, filt, hy_bias, w_out_b, b_out):
    d = x.shape[1]
    n = x.shape[0] // batch
    w1, b1, fr1, w2, b2, fr2, w3 = filt
    hp = LANES
    feats = _filter_features(n, LANES)
    max_decay = math.log(HYENA_TARGET) / HYENA_FAST_PCT
    min_decay = math.log(HYENA_TARGET) / HYENA_SLOW_PCT
    absdelta = jnp.abs(jnp.linspace(min_decay, max_decay, d, dtype=F32))[None, :]
    taps = _hyena_filter_taps(
        feats, _pad_to(w1, (LANES, hp)), _pad_to(b1[None, :], (1, hp)), _pad_to(fr1[None, :], (1, hp)),
        _pad_to(w2, (hp, hp)), _pad_to(b2[None, :], (1, hp)), _pad_to(fr2[None, :], (1, hp)),
        _pad_to(w3, (hp, w3.shape[1])), absdelta, d)
    dft, dft_t = _dft_tables(n)
    spec = _hyena_spectrum(dft, taps)
    pc = _proj(x, mod, g, w_in_b, b_in, w_conv, b_conv, rows_per_mod, n)
    yr, yi = _dft_fwd(dft, pc, 0, spec, 0, batch, d)
    z = _dft_inv(dft_t, yr, yi, pc, 1, pc, 0, hy_bias[0], batch, d)
    yr, yi = _dft_fwd(dft, z, 0, spec, 1, batch, d)
    z = _dft_inv(dft_t, yr, yi, pc, 2, z, 0, hy_bias[1], batch, d)
    return _outproj_parity(z, w_out_b, b_out, x, mod, rows_per_mod, n)


def kernel(x, c, ctx, c_ctx, w_mod, b_mod, g_norm, w_ffn_in, w_ffn_out, w_attn_in, w_attn_out, g_q, g_k, sink, w_hy_in, b_hy_in, w_hy_conv, b_hy_conv, hf_w1, hf_b1, hf_freq1, hf_w2, hf_b2, hf_freq2, hf_w3, hy_bias, w_hy_out, b_hy_out):
    batch, seq, d = x.shape
    n_ctx = ctx.shape[1]
    depth = w_mod.shape[0]
    rows_c = batch * n_ctx
    rope = _rope_tables(seq)
    last_ctx = max(l for l in range(depth) if l % 2 == 0)

    r_pad = -(-(batch + 1) // SUBLANES) * SUBLANES
    c_all = _pad_to(jnp.concatenate([c, c_ctx[None, :]], axis=0), (r_pad, d))
    m_all = _mod_all(c_all, w_mod, b_mod).reshape(depth, r_pad, N_MOD, d)

    w_ffn_in_b = w_ffn_in.astype(BF16)
    w_ffn_out_b = w_ffn_out.astype(BF16)

    xl = x.reshape(batch * seq, d)
    xc = ctx.reshape(rows_c, d)
    for l in range(depth):
        i = l // 2
        ctx_live = l <= last_ctx
        ctx_full = l < last_ctx
        mods_l = [m_all[l, :batch, 3 * k:3 * k + 3] for k in range(3)]
        mods_c = [m_all[l, batch:batch + 1, 3 * k:3 * k + 3] for k in range(3)]
        xl = _ffn(xl, mods_l[0], g_norm[l, 0], w_ffn_in_b, w_ffn_out_b, l, 0, seq)
        if ctx_live:
            xc = _ffn(xc, mods_c[0], g_norm[l, 0], w_ffn_in_b, w_ffn_out_b, l, 0, rows_c)
        if l % 2 == 0:
            xl, xc = _attn_layer(xl, xc, mods_l[1], mods_c[1], g_norm[l, 1], w_attn_in[i], w_attn_out[i],
                                 g_q[i], g_k[i], sink[i], rope, batch, seq, n_ctx, ctx_full)
        else:
            w_in_b = _col_tiled(w_hy_in[i], _tile(w_hy_in.shape[2], 512))
            w_out_b = w_hy_out[i].astype(BF16)
            filt = (hf_w1[i], hf_b1[i], hf_freq1[i], hf_w2[i], hf_b2[i], hf_freq2[i], hf_w3[i])
            xl = _hyena_layer(xl, mods_l[1], g_norm[l, 1], seq, batch, w_in_b, b_hy_in[i], w_hy_conv[i],
                              b_hy_conv[i], filt, hy_bias[i], w_out_b, b_hy_out[i])
            if ctx_full:
                xc = _hyena_layer(xc, mods_c[1], g_norm[l, 1], rows_c, batch, w_in_b, b_hy_in[i], w_hy_conv[i],
                                  b_hy_conv[i], filt, hy_bias[i], w_out_b, b_hy_out[i])
        xl = _ffn(xl, mods_l[2], g_norm[l, 2], w_ffn_in_b, w_ffn_out_b, l, 1, seq)
        if ctx_full:
            xc = _ffn(xc, mods_c[2], g_norm[l, 2], w_ffn_in_b, w_ffn_out_b, l, 1, rows_c)
    return xl.reshape(batch, seq, d)
```

```python
import functools
import math

import jax
import jax.numpy as jnp
from jax import lax
from jax.experimental import pallas as pl
from jax.experimental.pallas import tpu as pltpu

HEAD_DIM = 128
N_HEADS = 8
N_KV = 2
GROUP = N_HEADS // N_KV
Q_COLS = N_HEADS * HEAD_DIM
KV_COLS = N_KV * HEAD_DIM
GRID_W = 64
WINDOW = 128
ROPE_THETA = 10000.0
N_MOD = 9
HYENA_ORDER = 2
HYENA_SHORT_K = 3
HYENA_BANDS = 16
HYENA_TARGET = 1e-2
HYENA_FAST_PCT = 0.3
HYENA_SLOW_PCT = 1.5
EPS = 1e-6
NEG_INF = -1e30

LANES = 128
SUBLANES = 8
VMEM_LIMIT_BYTES = 60 * 1024 * 1024

F32 = jnp.float32
BF16 = jnp.bfloat16
HIGHEST = lax.Precision.HIGHEST


def _params(*sem):
    return pltpu.CompilerParams(dimension_semantics=sem, vmem_limit_bytes=VMEM_LIMIT_BYTES)


def _tile(dim, pref):
    t = min(dim, pref)
    while dim % t:
        t //= 2
    return t


def _dot(a, b):
    return jnp.dot(a, b, preferred_element_type=F32)


def _dot_nt(a, b):
    return lax.dot_general(a, b, (((1,), (1,)), ((), ())), preferred_element_type=F32)


def _silu(v):
    return v * (1.0 / (1.0 + jnp.exp(-v)))


def _modulated(xf, g, shift, scale):
    ms = jnp.mean(xf * xf, axis=-1, keepdims=True)
    y = xf * lax.rsqrt(ms + EPS)
    return (y * g) * (1.0 + scale) + shift


def _mod_body(c_ref, w_ref, b_ref, o_ref):
    s = _silu(c_ref[...])
    o_ref[0] = jnp.dot(s, w_ref[0], preferred_element_type=F32, precision=HIGHEST) + b_ref[0]


def _mod_all(c_all, w_mod, b_mod):
    depth, d, nd = w_mod.shape
    r = c_all.shape[0]
    tn = _tile(nd, 2048)
    return pl.pallas_call(
        _mod_body,
        out_shape=jax.ShapeDtypeStruct((depth, r, nd), F32),
        grid=(depth, nd // tn),
        in_specs=[
            pl.BlockSpec((r, d), lambda l, j: (0, 0)),
            pl.BlockSpec((1, d, tn), lambda l, j: (l, 0, j)),
            pl.BlockSpec((1, 1, tn), lambda l, j: (l, 0, j)),
        ],
        out_specs=pl.BlockSpec((1, r, tn), lambda l, j: (l, 0, j)),
        compiler_params=_params("parallel", "parallel"),
        name="mod_vectors",
    )(c_all, w_mod, b_mod.reshape(depth, 1, nd))


def _ffn_body(x_ref, mod_ref, g_ref, wg_ref, wu_ref, wo_ref, o_ref, h_ref, acc_ref):
    j = pl.program_id(1)
    last = pl.num_programs(1) - 1

    def chunk(h):
        a = _dot(h, wg_ref[...])
        u = _dot(h, wu_ref[...])
        return _dot((_silu(a) * u).astype(BF16), wo_ref[...])

    gate = 0.5 * mod_ref[0, 2:3, :]

    @pl.when(j == 0)
    def _():
        x = x_ref[...]
        h = _modulated(x, g_ref[...], mod_ref[0, 0:1, :], mod_ref[0, 1:2, :]).astype(BF16)
        h_ref[...] = h
        acc_ref[...] = x + gate * chunk(h)

    @pl.when(jnp.logical_and(j > 0, j < last))
    def _():
        acc_ref[...] += gate * chunk(h_ref[...])

    @pl.when(jnp.logical_and(j > 0, j == last))
    def _():
        o_ref[...] = acc_ref[...] + gate * chunk(h_ref[...])


def _ffn(x, mod, g, w_in, w_out, layer, half, rows_per_mod):
    rows, d = x.shape
    f = w_out.shape[2]
    tm = _tile(min(rows, rows_per_mod), 1024)
    tf = _tile(f, 512)
    nf = f // tf
    assert nf >= 2
    return pl.pallas_call(
        _ffn_body,
        out_shape=jax.ShapeDtypeStruct((rows, d), F32),
        grid=(rows // tm, nf),
        in_specs=[
            pl.BlockSpec((tm, d), _read_in_first_step(rows // tm)),
            pl.BlockSpec((1, 3, d), lambda i, j: (i * tm // rows_per_mod, 0, 0)),
            pl.BlockSpec((1, d), lambda i, j: (0, 0)),
            pl.BlockSpec((None, None, d, tf), lambda i, j: (layer, half, 0, j)),
            pl.BlockSpec((None, None, d, tf), lambda i, j: (layer, half, 0, nf + j)),
            pl.BlockSpec((None, None, tf, d), lambda i, j: (layer, half, j, 0)),
        ],
        out_specs=pl.BlockSpec((tm, d), lambda i, j: (i, 0), pipeline_mode=pl.Buffered(1)),
        scratch_shapes=[pltpu.VMEM((tm, d), BF16), pltpu.VMEM((tm, d), F32)],
        compiler_params=_params("parallel", "arbitrary"),
        name="ffn_half_step",
    )(x, mod, g.reshape(1, d), w_in, w_in, w_out)


def _proj_body(x_ref, mod_ref, g_ref, perm_ref, w_ref, b_ref, cw_ref, cb_ref, o_ref, h_ref, *, half_seq):
    tm = x_ref.shape[0]
    half = tm // 2
    grp = perm_ref.shape[0]

    def project_conv():
        p = _dot(h_ref[...], w_ref[...]) + b_ref[...]
        pe, po = p[:half], p[half:]
        pos = lax.broadcasted_iota(jnp.int32, (half, 1), 0) % half_seq
        po_prev = jnp.where(pos == 0, 0.0, pltpu.roll(po, 1, 0))
        pe_next = jnp.where(pos == half_seq - 1, 0.0, pltpu.roll(pe, half - 1, 0))
        w0, w1, w2 = cw_ref[0:1, :], cw_ref[1:2, :], cw_ref[2:3, :]
        o_ref[0] = (((cb_ref[...] + po_prev * w0) + pe * w1) + po * w2).astype(o_ref.dtype)
        o_ref[1] = (((cb_ref[...] + pe * w0) + po * w1) + pe_next * w2).astype(o_ref.dtype)

    @pl.when(pl.program_id(1) == 0)
    def _():
        for r in range(tm // grp):
            h = _modulated(x_ref[r * grp:(r + 1) * grp, :], g_ref[...], mod_ref[0, 0:1, :], mod_ref[0, 1:2, :])
            hp = _dot(perm_ref[...], h.astype(BF16)).astype(BF16)
            h_ref[r * (grp // 2):(r + 1) * (grp // 2), :] = hp[:grp // 2]
            h_ref[half + r * (grp // 2):half + (r + 1) * (grp // 2), :] = hp[grp // 2:]
        project_conv()

    @pl.when(pl.program_id(1) > 0)
    def _():
        project_conv()


def _qkv_body(x_ref, mod_ref, g_ref, w_ref, gamma_ref, qs_ref, isv_ref, cos_ref, sin_ref, o_ref, h_ref, p_ref):
    j = pl.program_id(1)
    last = pl.num_programs(1) - 1

    def finish():
        cos = cos_ref[...]
        sin = sin_ref[...]
        even = (lax.broadcasted_iota(jnp.int32, (1, HEAD_DIM), 1) % 2) == 0
        for hh in range(p_ref.shape[1] // HEAD_DIM):
            sl = slice(hh * HEAD_DIM, (hh + 1) * HEAD_DIM)
            ph = p_ref[:, sl]
            ms = jnp.mean(ph * ph, axis=-1, keepdims=True)
            nh = (ph * lax.rsqrt(ms + EPS)) * gamma_ref[:, sl]
            partner = jnp.where(even, pltpu.roll(nh, HEAD_DIM - 1, 1), pltpu.roll(nh, 1, 1))
            r = (nh * cos + partner * sin) * qs_ref[:, sl]
            o_ref[:, sl] = jnp.where(isv_ref[:, sl] > 0.0, ph, r).astype(o_ref.dtype)

    @pl.when(j == 0)
    def _():
        h = _modulated(x_ref[...], g_ref[...], mod_ref[0, 0:1, :], mod_ref[0, 1:2, :]).astype(BF16)
        h_ref[...] = h
        p_ref[...] = _dot(h, w_ref[...])

    @pl.when(jnp.logical_and(j > 0, j < last))
    def _():
        finish()
        p_ref[...] = _dot(h_ref[...], w_ref[...])

    @pl.when(j == last)
    def _():
        finish()


def _read_in_first_step(n_tiles):
    return lambda i, j: (jnp.minimum(i + jnp.minimum(j, 1), n_tiles - 1), 0)


def _row_tile(rows, rows_per_mod):
    return _tile(min(rows, rows_per_mod), 1024)


def _parity_perm(tm):
    r = jnp.arange(tm, dtype=jnp.int32)[:, None]
    c = jnp.arange(tm, dtype=jnp.int32)[None, :]
    src = jnp.where(r < tm // 2, 2 * r, 2 * (r - tm // 2) + 1)
    return (c == src).astype(BF16)


def _col_tiled(w, tn):
    k, n = w.shape
    return w.reshape(k, n // tn, tn).transpose(1, 0, 2).astype(BF16)


PERM_GROUP = 256


def _proj(x, mod, g, w_t, b, conv_w, conv_b, rows_per_mod, seq):
    rows, d = x.shape
    nj, _, tn = w_t.shape
    n = nj * tn
    tm = _tile(min(rows, rows_per_mod), max(seq, 1024))
    grp = min(PERM_GROUP, tm)
    assert tm % seq == 0 and tm % grp == 0 and grp % (4 * SUBLANES) == 0
    col = pl.BlockSpec((1, tn), lambda i, j: (0, j))
    return pl.pallas_call(
        functools.partial(_proj_body, half_seq=seq // 2),
        out_shape=jax.ShapeDtypeStruct((2, rows // 2, n), BF16),
        grid=(rows // tm, n // tn),
        in_specs=[
            pl.BlockSpec((tm, d), _read_in_first_step(rows // tm)),
            pl.BlockSpec((1, 3, d), lambda i, j: (i * tm // rows_per_mod, 0, 0)),
            pl.BlockSpec((1, d), lambda i, j: (0, 0)),
            pl.BlockSpec((grp, grp), lambda i, j: (0, 0)),
            pl.BlockSpec((None, d, tn), lambda i, j: (j, 0, 0)),
            col,
            pl.BlockSpec((HYENA_SHORT_K, tn), lambda i, j: (0, j)),
            col,
        ],
        out_specs=pl.BlockSpec((2, tm // 2, tn), lambda i, j: (0, i, j)),
        scratch_shapes=[pltpu.VMEM((tm, d), BF16)],
        compiler_params=_params("parallel", "arbitrary"),
        name="hyena_project_conv",
    )(x, mod, g.reshape(1, d), _parity_perm(grp), w_t, b.reshape(1, n), conv_w, conv_b.reshape(1, n))


def _qkv_proj(x, mod, g, w_t, rows_per_mod, tables, rope, seq):
    rows, d = x.shape
    nj, _, tn = w_t.shape
    n = nj * tn
    tm = _row_tile(rows, rows_per_mod)
    gamma, qs, isv = tables
    cos, sin = rope

    def prev(j):
        return jnp.maximum(j - 1, 0)

    col = pl.BlockSpec((1, tn), lambda i, j: (0, prev(j)))
    if seq is None:
        rope_spec = pl.BlockSpec((tm, HEAD_DIM), lambda i, j: (0, 0))
    else:
        rope_spec = pl.BlockSpec((tm, HEAD_DIM), lambda i, j: (i % (seq // tm), 0))
    return pl.pallas_call(
        _qkv_body,
        out_shape=jax.ShapeDtypeStruct((rows, n), BF16),
        grid=(rows // tm, nj + 1),
        in_specs=[
            pl.BlockSpec((tm, d), _read_in_first_step(rows // tm)),
            pl.BlockSpec((1, 3, d), lambda i, j: (i * tm // rows_per_mod, 0, 0)),
            pl.BlockSpec((1, d), lambda i, j: (0, 0)),
            pl.BlockSpec((None, d, tn), lambda i, j: (jnp.minimum(j, nj - 1), 0, 0)),
            col, col, col, rope_spec, rope_spec,
        ],
        out_specs=pl.BlockSpec((tm, tn), lambda i, j: (i, prev(j))),
        scratch_shapes=[pltpu.VMEM((tm, d), BF16), pltpu.VMEM((tm, tn), F32)],
        compiler_params=_params("arbitrary", "arbitrary"),
        name="qkv_project",
    )(x, mod, g.reshape(1, d), w_t, gamma, qs, isv, cos, sin)


def _attn_body(*refs, has_lat, has_mask, has_sink, tq):
    sink_ref, q_ref, kc_ref, vc_ref = refs[:4]
    if has_lat:
        k_ref, v_ref, o_ref = refs[4:7]
    else:
        o_ref = refs[4]
    g = pl.program_id(1)
    qi = pl.program_id(2)
    if has_lat:
        l_rows = k_ref.shape[0]
        if has_mask:
            kw = min(l_rows, tq + 2 * WINDOW)
            k0 = pl.multiple_of(jnp.clip(qi * tq - WINDOW, 0, l_rows - kw), WINDOW)
            keys = pl.ds(k0, kw)
        else:
            k0, keys = 0, slice(None)
    if has_mask:
        qpos = qi * tq + lax.broadcasted_iota(jnp.int32, (tq, 1), 0)
        kpos = k0 + lax.broadcasted_iota(jnp.int32, (1, kw), 1)
        band = jnp.abs(kpos - qpos) <= WINDOW
    heads = [slice(hh * HEAD_DIM, (hh + 1) * HEAD_DIM) for hh in range(GROUP)]
    s1 = [_dot_nt(q_ref[:, sl], kc_ref[...]) for sl in heads]
    m = [jnp.max(s, axis=-1, keepdims=True) for s in s1]
    if has_lat:
        s2 = [_dot_nt(q_ref[:, sl], k_ref[keys, :]) for sl in heads]
        if has_mask:
            s2 = [jnp.where(band, s, NEG_INF) for s in s2]
        m = [jnp.maximum(mh, jnp.max(s, axis=-1, keepdims=True)) for mh, s in zip(m, s2)]
    if has_sink:
        sk = [sink_ref[g * GROUP + hh] for hh in range(GROUP)]
        m = [jnp.maximum(mh, skh) for mh, skh in zip(m, sk)]
    for hh, sl in enumerate(heads):
        p1 = jnp.exp(s1[hh] - m[hh])
        l = jnp.sum(p1, axis=-1, keepdims=True)
        o = _dot(p1.astype(BF16), vc_ref[...])
        if has_lat:
            p2 = jnp.exp(s2[hh] - m[hh])
            l = l + jnp.sum(p2, axis=-1, keepdims=True)
            o = o + _dot(p2.astype(BF16), v_ref[keys, :])
        if has_sink:
            l = l + jnp.exp(sk[hh] - m[hh])
        o_ref[:, sl] = (o / l).astype(o_ref.dtype)


def _attention(sink, q_src, ctx_src, lat_src, *, batch, q_rows, q_col, k_col, v_col, has_mask, has_sink):
    tq = _tile(q_rows, 256 if has_mask else 512)
    assert not has_mask or tq % WINDOW == 0
    c_rows = ctx_src.shape[0] // batch
    gw = GROUP * HEAD_DIM
    in_specs = [
        pl.BlockSpec(memory_space=pltpu.SMEM),
        pl.BlockSpec((tq, gw), lambda b, g, i: (b * (q_rows // tq) + i, q_col + g)),
        pl.BlockSpec((c_rows, HEAD_DIM), lambda b, g, i: (b, k_col + g)),
        pl.BlockSpec((c_rows, HEAD_DIM), lambda b, g, i: (b, v_col + g)),
    ]
    args = [sink, q_src, ctx_src, ctx_src]
    has_lat = lat_src is not None
    if has_lat:
        l_rows = lat_src.shape[0] // batch
        in_specs += [
            pl.BlockSpec((l_rows, HEAD_DIM), lambda b, g, i: (b, k_col + g)),
            pl.BlockSpec((l_rows, HEAD_DIM), lambda b, g, i: (b, v_col + g)),
        ]
        args += [lat_src, lat_src]
    return pl.pallas_call(
        functools.partial(_attn_body, has_lat=has_lat, has_mask=has_mask, has_sink=has_sink, tq=tq),
        out_shape=jax.ShapeDtypeStruct((batch * q_rows, Q_COLS), BF16),
        grid=(batch, N_KV, q_rows // tq),
        in_specs=in_specs,
        out_specs=pl.BlockSpec((tq, gw), lambda b, g, i: (b * (q_rows // tq) + i, g)),
        compiler_params=_params("parallel", "parallel", "parallel"),
        name="gqa_attention",
    )(*args)


def _outproj_body(*refs, n_in):
    a_refs, w_refs = refs[:n_in], refs[n_in:2 * n_in]
    b_ref, x_ref, mod_ref, o_ref = refs[2 * n_in:]
    y = b_ref[...] + _dot(a_refs[0][...], w_refs[0][...])
    for a_ref, w_ref in zip(a_refs[1:], w_refs[1:]):
        y = y + _dot(a_ref[...], w_ref[...])
    o_ref[...] = x_ref[...] + mod_ref[0, 2:3, :] * y


def _outproj(acts, weight, bias, x, mod, rows_per_mod):
    rows, d = x.shape
    tm = _tile(min(rows, rows_per_mod), 512)
    n_in = len(acts)
    k = acts[0].shape[1]
    assert all(a.shape[1] == k for a in acts) and weight.shape[0] == n_in * k
    in_specs = [pl.BlockSpec((tm, k), lambda i: (i, 0)) for _ in acts]
    in_specs += [pl.BlockSpec((k, d), lambda i, n=n: (n, 0)) for n in range(n_in)]
    in_specs += [
        pl.BlockSpec((1, d), lambda i: (0, 0)),
        pl.BlockSpec((tm, d), lambda i: (i, 0)),
        pl.BlockSpec((1, 3, d), lambda i: (i * tm // rows_per_mod, 0, 0)),
    ]
    return pl.pallas_call(
        functools.partial(_outproj_body, n_in=n_in),
        out_shape=jax.ShapeDtypeStruct((rows, d), F32),
        grid=(rows // tm,),
        in_specs=in_specs,
        out_specs=pl.BlockSpec((tm, d), lambda i: (i, 0)),
        compiler_params=_params("parallel"),
        name="outproj_residual",
    )(*acts, *([weight] * n_in), bias.reshape(1, d), x, mod)


def _outproj_parity_body(z_ref, unperm_ref, w_ref, b_ref, x_ref, mod_ref, o_ref):
    grp = unperm_ref.shape[0]
    hg = grp // 2
    parts = []
    for r in range(x_ref.shape[0] // grp):
        zg = jnp.concatenate([z_ref[0, r * hg:(r + 1) * hg, :], z_ref[1, r * hg:(r + 1) * hg, :]], axis=0)
        parts.append(_dot(unperm_ref[...], zg).astype(BF16))
    y = b_ref[...] + _dot(jnp.concatenate(parts, axis=0), w_ref[...])
    o_ref[...] = x_ref[...] + mod_ref[0, 2:3, :] * y


def _outproj_parity(z, weight, bias, x, mod, rows_per_mod, seq):
    rows, d = x.shape
    k = z.shape[2]
    tm = _tile(min(rows, rows_per_mod), 512)
    grp = min(PERM_GROUP, tm)
    assert tm % grp == 0 and (tm % seq == 0 or seq % tm == 0)
    return pl.pallas_call(
        _outproj_parity_body,
        out_shape=jax.ShapeDtypeStruct((rows, d), F32),
        grid=(rows // tm,),
        in_specs=[
            pl.BlockSpec((2, tm // 2, k), lambda i: (0, i, 0)),
            pl.BlockSpec((grp, grp), lambda i: (0, 0)),
            pl.BlockSpec((k, d), lambda i: (0, 0)),
            pl.BlockSpec((1, d), lambda i: (0, 0)),
            pl.BlockSpec((tm, d), lambda i: (i, 0)),
            pl.BlockSpec((1, 3, d), lambda i: (i * tm // rows_per_mod, 0, 0)),
        ],
        out_specs=pl.BlockSpec((tm, d), lambda i: (i, 0)),
        compiler_params=_params("parallel"),
        name="outproj_residual_parity",
    )(z, _parity_perm(grp).T, weight, bias.reshape(1, d), x, mod)


def _filter_body(feats_ref, w1_ref, b1_ref, fr1_ref, w2_ref, b2_ref, fr2_ref, w3f_ref, w3b_ref, dl_ref, o_ref,
                 h2_ref):
    feats = feats_ref[...]

    @pl.when(jnp.logical_and(pl.program_id(0) == 0, pl.program_id(1) == 0))
    def _():
        h1 = jnp.sin(fr1_ref[...] * (jnp.dot(feats, w1_ref[...], preferred_element_type=F32, precision=HIGHEST)
                                     + b1_ref[...]))
        h2_ref[...] = jnp.sin(fr2_ref[...] * (jnp.dot(h1, w2_ref[...], preferred_element_type=F32,
                                                      precision=HIGHEST) + b2_ref[...]))

    h2 = h2_ref[...]
    decay = jnp.exp(-feats[:, 0:1] * dl_ref[...])
    fwd = jnp.dot(h2, w3f_ref[...], preferred_element_type=F32, precision=HIGHEST) * decay
    bwd = jnp.dot(h2, w3b_ref[...], preferred_element_type=F32, precision=HIGHEST) * decay
    row = lax.broadcasted_iota(jnp.int32, (fwd.shape[0], 1), 0)
    bwd = jnp.where(row == 0, 0.0, bwd)
    nrm = lax.rsqrt(jnp.sum(fwd * fwd + bwd * bwd, axis=0, keepdims=True) + EPS)
    o_ref[0] = ((fwd + bwd) * nrm).astype(o_ref.dtype)
    o_ref[1] = ((fwd - bwd) * nrm).astype(o_ref.dtype)


def _hyena_filter_taps(feats, w1, b1, fr1, w2, b2, fr2, w3, absdelta, d):
    n, fp = feats.shape
    hp = w1.shape[1]
    tn = _tile(d, 512)
    nd = d // tn
    vec = pl.BlockSpec((1, hp), lambda o, j: (0, 0))
    return pl.pallas_call(
        _filter_body,
        out_shape=jax.ShapeDtypeStruct((2, n, HYENA_ORDER * d), BF16),
        grid=(HYENA_ORDER, nd),
        in_specs=[
            pl.BlockSpec((n, fp), lambda o, j: (0, 0)),
            pl.BlockSpec((fp, hp), lambda o, j: (0, 0)), vec, vec,
            pl.BlockSpec((hp, hp), lambda o, j: (0, 0)), vec, vec,
            pl.BlockSpec((hp, tn), lambda o, j: (0, (2 * o) * nd + j)),
            pl.BlockSpec((hp, tn), lambda o, j: (0, (2 * o + 1) * nd + j)),
            pl.BlockSpec((1, tn), lambda o, j: (0, j)),
        ],
        out_specs=pl.BlockSpec((2, n, tn), lambda o, j: (0, 0, o * nd + j)),
        scratch_shapes=[pltpu.VMEM((n, hp), F32)],
        compiler_params=_params("arbitrary", "arbitrary"),
        name="hyena_filter_taps",
    )(feats, w1, b1, fr1, w2, b2, fr2, w3, w3, absdelta)


def _spectrum_body(ce_ref, se_ref, co_ref, so_ref, se_, so_, de_, do_, o_ref, *, inv_n):
    i = pl.program_id(0)
    tk = ce_ref.shape[0]
    pr = _dot(ce_ref[...], se_[...])
    qr = _dot(co_ref[...], so_[...])
    pi = _dot(se_ref[...], de_[...])
    qi = _dot(so_ref[...], do_[...])
    row0 = (i * tk + lax.broadcasted_iota(jnp.int32, (tk, 1), 0)) == 0
    w_re = jnp.where(row0, inv_n, 2.0 * inv_n)
    o_ref[0] = (pr + qr) * w_re
    o_ref[1] = (pr - qr) * w_re
    o_ref[2] = (pi + qi) * (2.0 * inv_n)
    o_ref[3] = (qi - pi) * (2.0 * inv_n)

    @pl.when(i == 0)
    def _():
        rows = 2 * SUBLANES
        mid_r = _dot(se_ref[0:rows, :], se_[...])
        o_ref[2, 0:1, :] = mid_r[0:1, :] * (2.0 * inv_n)
        o_ref[3, 0:1, :] = qi[0:1, :] * (2.0 * inv_n)


def _hyena_spectrum(dft, taps):
    n = taps.shape[1]
    cols = taps.shape[2]
    half = n // 2
    tk = _tile(half, 1024)
    tn = _tile(cols, 512)
    nr = half // tk
    mat = lambda par, part: pl.BlockSpec((None, tk, half), lambda i, j: (par, part * nr + i, 0))
    tap = lambda which, par: pl.BlockSpec((None, half, tn), lambda i, j: (which, par, j))
    return pl.pallas_call(
        functools.partial(_spectrum_body, inv_n=1.0 / (2 * n)),
        out_shape=jax.ShapeDtypeStruct((4, half, cols), F32),
        grid=(nr, cols // tn),
        in_specs=[mat(0, 0), mat(0, 1), mat(1, 0), mat(1, 1), tap(0, 0), tap(0, 1), tap(1, 0), tap(1, 1)],
        out_specs=pl.BlockSpec((4, tk, tn), lambda i, j: (0, i, j)),
        compiler_params=_params("parallel", "parallel"),
        name="hyena_filter_spectrum",
    )(dft, dft, dft, dft, taps, taps, taps, taps)


def _dft_fwd_body(ce_ref, se_ref, co_ref, so_ref, ze_ref, zo_ref, k_ref, yr_ref, yi_ref):
    i = pl.program_id(1)
    tk = ce_ref.shape[0]
    row0 = (i * tk + lax.broadcasted_iota(jnp.int32, (tk, 1), 0)) == 0
    tn = ze_ref.shape[1]
    sub = min(tn, 2 * LANES)
    for c in range(tn // sub):
        cols = slice(c * sub, (c + 1) * sub)
        ze, zo = ze_ref[:, cols], zo_ref[:, cols]
        pr = _dot(ce_ref[...], ze)
        pi = _dot(se_ref[...], ze)
        qr = _dot(co_ref[...], zo)
        qi = _dot(so_ref[...], zo)
        kr_lo, kr_hi, ki_lo, ki_hi = k_ref[0, :, cols], k_ref[1, :, cols], k_ref[2, :, cols], k_ref[3, :, cols]
        zr_lo, zr_hi, zi_lo, zi_hi = pr + qr, pr - qr, pi + qi, qi - pi
        yr_lo = zr_lo * kr_lo - jnp.where(row0, 0.0, zi_lo * ki_lo)
        yr_hi = zr_hi * kr_hi - jnp.where(row0, 0.0, zi_hi * ki_hi)
        yi_lo = zr_lo * ki_lo + zi_lo * kr_lo
        yi_hi = zr_hi * ki_hi + zi_hi * kr_hi
        mid_r = pi * ki_lo - qi * ki_hi
        mid_i = pi * ki_hi + qi * ki_lo
        yr_ref[0, :, cols] = (yr_lo + yr_hi).astype(yr_ref.dtype)
        yr_ref[1, :, cols] = (yr_lo - yr_hi).astype(yr_ref.dtype)
        yi_ref[0, :, cols] = jnp.where(row0, mid_r, yi_lo - yi_hi).astype(yi_ref.dtype)
        yi_ref[1, :, cols] = jnp.where(row0, mid_i, yi_lo + yi_hi).astype(yi_ref.dtype)


def _dft_fwd(dft, z_src, z_col, spec, order, batch, d):
    half = z_src.shape[1] // batch
    tk = _tile(half, 1024)
    tn = _tile(d, 512)
    nr, nd = half // tk, d // tn
    out = jax.ShapeDtypeStruct((2, batch * half, d), BF16)
    once = pl.Buffered(1)
    mat = lambda par, part: pl.BlockSpec((None, tk, half), lambda j, i, b: (par, part * nr + i, 0),
                                         pipeline_mode=once)
    zin = lambda par: pl.BlockSpec((None, half, tn), lambda j, i, b: (par, b, z_col * nd + j))
    res = pl.BlockSpec((2, tk, tn), lambda j, i, b: (0, b * nr + i, j))
    return pl.pallas_call(
        _dft_fwd_body,
        out_shape=(out, out),
        grid=(nd, nr, batch),
        in_specs=[
            mat(0, 0), mat(0, 1), mat(1, 0), mat(1, 1), zin(0), zin(1),
            pl.BlockSpec((4, tk, tn), lambda j, i, b: (0, i, order * nd + j), pipeline_mode=once),
        ],
        out_specs=(res, res),
        compiler_params=_params("parallel", "parallel", "parallel"),
        name="hyena_dft_forward",
    )(dft, dft, dft, dft, z_src, z_src, spec)


def _dft_inv_body(atr_ref, ati_ref, yr_ref, yi_ref, g_ref, zp_ref, bias_ref, o_ref):
    y = _dot(atr_ref[...], yr_ref[...]) + _dot(ati_ref[...], yi_ref[...])
    zp = zp_ref[...].astype(F32)
    o_ref[...] = (g_ref[...].astype(F32) * (y + zp * bias_ref[...])).astype(o_ref.dtype)


def _dft_inv(dft_t, yr, yi, gate_src, gate_col, z_src, z_col, bias, batch, d):
    half = yr.shape[1] // batch
    tt = _tile(half, 1024)
    tn = _tile(d, 1024)
    nr, nd = half // tt, d // tn
    row = lambda cols: pl.BlockSpec((None, tt, tn), lambda j, p, i, b: (p, b * nr + i, cols * nd + j))
    return pl.pallas_call(
        _dft_inv_body,
        out_shape=jax.ShapeDtypeStruct((2, batch * half, d), BF16),
        grid=(nd, 2, nr, batch),
        in_specs=[
            pl.BlockSpec((None, tt, half), lambda j, p, i, b: (p, i, 0)),
            pl.BlockSpec((None, tt, half), lambda j, p, i, b: (p, i, 1)),
            pl.BlockSpec((None, half, tn), lambda j, p, i, b: (p, b, j)),
            pl.BlockSpec((None, half, tn), lambda j, p, i, b: (p, b, j)),
            row(gate_col), row(z_col),
            pl.BlockSpec((1, tn), lambda j, p, i, b: (0, j)),
        ],
        out_specs=row(0),
        compiler_params=_params("parallel", "parallel", "parallel", "parallel"),
        name="hyena_dft_inverse",
    )(dft_t, dft_t, yr, yi, gate_src, z_src, bias.reshape(1, d))


def _rope_tables(n_tok):
    rows = n_tok // GRID_W
    r = jnp.repeat(jnp.arange(rows), GRID_W).astype(F32)
    col = jnp.tile(jnp.arange(GRID_W), rows).astype(F32)
    half = HEAD_DIM // 2
    inv = ROPE_THETA ** (-jnp.arange(0, half, 2, dtype=F32) / half)
    ang = jnp.concatenate([r[:, None] * inv, col[:, None] * inv], axis=-1)
    cos = jnp.repeat(jnp.cos(ang), 2, axis=-1)
    sin = jnp.repeat(jnp.sin(ang), 2, axis=-1)
    sign = jnp.where(jnp.arange(HEAD_DIM) % 2 == 0, -1.0, 1.0).astype(F32)
    return cos, sin * sign


def _dft_tables(n):
    big = 2 * n
    half = n // 2
    k = jnp.arange(half, dtype=jnp.int32)[:, None]
    m = jnp.arange(half, dtype=jnp.int32)[None, :]
    alt = jnp.where(m % 2 == 0, 1.0, -1.0).astype(F32)
    mats = []
    for par in range(2):
        ang = ((k * (2 * m + par)) % big).astype(F32) * (2.0 * math.pi / big)
        s = jnp.where(k == 0, alt if par == 0 else -alt, -jnp.sin(ang))
        mats.append(jnp.concatenate([jnp.cos(ang), s], axis=0))
    a = jnp.stack(mats).astype(BF16)
    return a, a.transpose(0, 2, 1)


def _filter_features(n, width):
    t = jnp.linspace(0.0, 1.0, n, dtype=F32)[:, None]
    w = (2.0 * math.pi / n) * jnp.arange(n, dtype=F32)[:, None]
    bands = jnp.linspace(1e-4, HYENA_BANDS - 1, HYENA_BANDS, dtype=F32)
    ang = w * bands[None, :]
    feats = jnp.concatenate([t, jnp.cos(ang), -jnp.sin(ang)], axis=-1)
    feats = jnp.concatenate([feats[0::2], feats[1::2]], axis=0)
    return jnp.pad(feats, ((0, 0), (0, width - feats.shape[1])))


def _pad_to(a, shape):
    return jnp.pad(a, [(0, s - d) for d, s in zip(a.shape, shape)])


def _attn_layer(x, xc, mod_l, mod_c, g, w_in, w_out, g_q, g_k, sink, rope, batch, seq, n_ctx, ctx_out):
    d = x.shape[1]
    ones = jnp.ones((HEAD_DIM,), F32)
    gamma = jnp.concatenate([jnp.tile(g_q[0], N_HEADS), jnp.tile(g_q[1], N_HEADS),
                             jnp.tile(g_k[0], N_KV), jnp.tile(ones, N_KV),
                             jnp.tile(g_k[1], N_KV), jnp.tile(ones, N_KV)])[None, :]
    qs = jnp.concatenate([jnp.full((2 * Q_COLS,), HEAD_DIM ** -0.5, F32), jnp.ones((4 * KV_COLS,), F32)])[None, :]
    zk, ok = jnp.zeros((KV_COLS,), F32), jnp.ones((KV_COLS,), F32)
    isv = jnp.concatenate([jnp.zeros((2 * Q_COLS,), F32), zk, ok, zk, ok])[None, :]
    tables = (gamma, qs, isv)
    tm = _row_tile(batch * n_ctx, batch * n_ctx)
    ident = (jnp.ones((tm, HEAD_DIM), F32), jnp.zeros((tm, HEAD_DIM), F32))
    w_in_b = _col_tiled(w_in, _tile(w_in.shape[1], 1024))
    qkv = _qkv_proj(x, mod_l, g, w_in_b, seq, tables, rope, seq)
    qkv_c = _qkv_proj(xc, mod_c, g, w_in_b, batch * n_ctx, tables, ident, None)
    qa_col, qb_col = 0, N_KV
    ka_col = 2 * Q_COLS // HEAD_DIM
    va_col, kb_col, vb_col = ka_col + N_KV, ka_col + 2 * N_KV, ka_col + 3 * N_KV
    o_a = _attention(sink, qkv, qkv_c, qkv, batch=batch, q_rows=seq, q_col=qa_col, k_col=ka_col, v_col=va_col,
                     has_mask=False, has_sink=False)
    o_b = _attention(sink, qkv, qkv_c, qkv, batch=batch, q_rows=seq, q_col=qb_col, k_col=kb_col, v_col=vb_col,
                     has_mask=True, has_sink=True)
    w_out_b = w_out.astype(BF16)
    zero_b = jnp.zeros((d,), F32)
    x = _outproj((o_a, o_b), w_out_b, zero_b, x, mod_l, seq)
    if ctx_out:
        co_a = _attention(sink, qkv_c, qkv_c, None, batch=batch, q_rows=n_ctx, q_col=qa_col, k_col=ka_col,
                          v_col=va_col, has_mask=False, has_sink=False)
        co_b = _attention(sink, qkv_c, qkv_c, None, batch=batch, q_rows=n_ctx, q_col=qb_col, k_col=kb_col,
                          v_col=vb_col, has_mask=False, has_sink=True)
        xc = _outproj((co_a, co_b), w_out_b, zero_b, xc, mod_c, batch * n_ctx)
    return x, xc


def _hyena_layer(x, mod, g, rows_per_mod, batch, w_in_b, b_in, w_conv, b_conv, filt, hy_bias, w_out_b, b_out):
    d = x.shape[1]
    n = x.shape[0] // batch
    w1, b1, fr1, w2, b2, fr2, w3 = filt
    hp = LANES
    feats = _filter_features(n, LANES)
    max_decay = math.log(HYENA_TARGET) / HYENA_FAST_PCT
    min_decay = math.log(HYENA_TARGET) / HYENA_SLOW_PCT
    absdelta = jnp.abs(jnp.linspace(min_decay, max_decay, d, dtype=F32))[None, :]
    taps = _hyena_filter_taps(
        feats, _pad_to(w1, (LANES, hp)), _pad_to(b1[None, :], (1, hp)), _pad_to(fr1[None, :], (1, hp)),
        _pad_to(w2, (hp, hp)), _pad_to(b2[None, :], (1, hp)), _pad_to(fr2[None, :], (1, hp)),
        _pad_to(w3, (hp, w3.shape[1])), absdelta, d)
    dft, dft_t = _dft_tables(n)
    spec = _hyena_spectrum(dft, taps)
    pc = _proj(x, mod, g, w_in_b, b_in, w_conv, b_conv, rows_per_mod, n)
    yr, yi = _dft_fwd(dft, pc, 0, spec, 0, batch, d)
    z = _dft_inv(dft_t, yr, yi, pc, 1, pc, 0, hy_bias[0], batch, d)
    yr, yi = _dft_fwd(dft, z, 0, spec, 1, batch, d)
    z = _dft_inv(dft_t, yr, yi, pc, 2, z, 0, hy_bias[1], batch, d)
    return _outproj_parity(z, w_out_b, b_out, x, mod, rows_per_mod, n)


def kernel(x, c, ctx, c_ctx, w_mod, b_mod, g_norm, w_ffn_in, w_ffn_out, w_attn_in, w_attn_out, g_q, g_k, sink, w_hy_in, b_hy_in, w_hy_conv, b_hy_conv, hf_w1, hf_b1, hf_freq1, hf_w2, hf_b2, hf_freq2, hf_w3, hy_bias, w_hy_out, b_hy_out):
    batch, seq, d = x.shape
    n_ctx = ctx.shape[1]
    depth = w_mod.shape[0]
    rows_c = batch * n_ctx
    rope = _rope_tables(seq)
    last_ctx = max(l for l in range(depth) if l % 2 == 0)

    r_pad = -(-(batch + 1) // SUBLANES) * SUBLANES
    c_all = _pad_to(jnp.concatenate([c, c_ctx[None, :]], axis=0), (r_pad, d))
    m_all = _mod_all(c_all, w_mod, b_mod).reshape(depth, r_pad, N_MOD, d)

    w_ffn_in_b = w_ffn_in.astype(BF16)
    w_ffn_out_b = w_ffn_out.astype(BF16)

    xl = x.reshape(batch * seq, d)
    xc = ctx.reshape(rows_c, d)
    for l in range(depth):
        i = l // 2
        ctx_live = l <= last_ctx
        ctx_full = l < last_ctx
        mods_l = [m_all[l, :batch, 3 * k:3 * k + 3] for k in range(3)]
        mods_c = [m_all[l, batch:batch + 1, 3 * k:3 * k + 3] for k in range(3)]
        xl = _ffn(xl, mods_l[0], g_norm[l, 0], w_ffn_in_b, w_ffn_out_b, l, 0, seq)
        if ctx_live:
            xc = _ffn(xc, mods_c[0], g_norm[l, 0], w_ffn_in_b, w_ffn_out_b, l, 0, rows_c)
        if l % 2 == 0:
            xl, xc = _attn_layer(xl, xc, mods_l[1], mods_c[1], g_norm[l, 1], w_attn_in[i], w_attn_out[i],
                                 g_q[i], g_k[i], sink[i], rope, batch, seq, n_ctx, ctx_full)
        else:
            w_in_b = _col_tiled(w_hy_in[i], _tile(w_hy_in.shape[2], 512))
            w_out_b = w_hy_out[i].astype(BF16)
            filt = (hf_w1[i], hf_b1[i], hf_freq1[i], hf_w2[i], hf_b2[i], hf_freq2[i], hf_w3[i])
            xl = _hyena_layer(xl, mods_l[1], g_norm[l, 1], seq, batch, w_in_b, b_hy_in[i], w_hy_conv[i],
                              b_hy_conv[i], filt, hy_bias[i], w_out_b, b_hy_out[i])
            if ctx_full:
                xc = _hyena_layer(xc, mods_c[1], g_norm[l, 1], rows_c, batch, w_in_b, b_hy_in[i], w_hy_conv[i],
                                  b_hy_conv[i], filt, hy_bias[i], w_out_b, b_hy_out[i])
        xl = _ffn(xl, mods_l[2], g_norm[l, 2], w_ffn_in_b, w_ffn_out_b, l, 1, seq)
        if ctx_full:
            xc = _ffn(xc, mods_c[2], g_norm[l, 2], w_ffn_in_b, w_ffn_out_b, l, 1, rows_c)
    return xl.reshape(batch, seq, d)
```
